```python
import jax, jax.numpy as jnp
from jax import lax
import numpy as np

D_MODEL = 1024
BATCH = 1
SEQ = 16384
DEPTH = 1

CHUNK = 64
Q_BLOCK = 128
N_MEM = 256
EPS = 1e-6

GLA_HEADS = 4
GLA_DK = D_MODEL // 16
GLA_DV = D_MODEL // 8
GLA_LOWRANK = 16
GLA_TAU = 16.0
FOX_HEADS = 8
FOX_DH = D_MODEL // 16
MEM_HEADS = 4
MEM_DH = D_MODEL // 8
D_FF = 4 * D_MODEL
N_BRANCH = 3

GLA_K = GLA_HEADS * GLA_DK
GLA_V = GLA_HEADS * GLA_DV
FOX_W = FOX_HEADS * FOX_DH
MEM_W = MEM_HEADS * MEM_DH
PROJ_SIZES = (GLA_K, GLA_K, GLA_V, GLA_V, GLA_LOWRANK, FOX_W, FOX_W, FOX_W, FOX_HEADS, MEM_W, N_BRANCH * D_MODEL)
D_IN = 2 * GLA_K + 2 * GLA_V + GLA_LOWRANK + 3 * FOX_W + FOX_HEADS + MEM_W + N_BRANCH * D_MODEL

kernel_name = "hybrid_gla_fox_memory_gated_block"


def rmsnorm(x, g):
    xf = x.astype(jnp.float32)
    r = lax.rsqrt(jnp.mean(xf * xf, axis=-1, keepdims=True) + EPS)
    return (xf * r).astype(x.dtype) * g


def split_cols(t, sizes):
    offs = np.cumsum(np.array(sizes))[:-1].tolist()
    return jnp.split(t, offs, axis=-1)


def gla_chunk_causal(q, k, v, log_a):
    B, S, H, DK = q.shape
    DV = v.shape[-1]
    N = S // CHUNK
    f32 = jnp.float32
    qc = q.reshape(B, N, CHUNK, H, DK).astype(f32)
    kc = k.reshape(B, N, CHUNK, H, DK).astype(f32)
    vc = v.reshape(B, N, CHUNK, H, DV).astype(f32)
    b = jnp.cumsum(log_a.reshape(B, N, CHUNK, H, DK).astype(f32), axis=2)
    b_last = b[:, :, -1:]
    e_pos = jnp.exp(b)
    e_neg = jnp.exp(-b)
    q_pos = qc * e_pos
    a_causal = jnp.einsum('bnthd,bnshd->bnhts', q_pos, kc * e_neg)
    a_anti = jnp.einsum('bnthd,bnshd->bnhts', qc * e_neg, kc * e_pos)
    t_idx = jnp.arange(CHUNK)
    lower = t_idx[:, None] >= t_idx[None, :]
    attn = jnp.where(lower, a_causal, a_anti)
    o_intra = jnp.einsum('bnhts,bnshv->bnthv', attn, vc)
    chunk_kv = jnp.einsum('bnshd,bnshv->bnhdv', kc * jnp.exp(b_last - b), vc)
    chunk_decay = jnp.exp(b_last[:, :, 0])

    def step(state, inp):
        kv, dec = inp
        return state * dec[..., None] + kv, state

    init = jnp.zeros((B, H, DK, DV), f32)
    _, prev = lax.scan(step, init, (jnp.moveaxis(chunk_kv, 1, 0), jnp.moveaxis(chunk_decay, 1, 0)))
    prev = jnp.moveaxis(prev, 0, 1)
    o_inter = jnp.einsum('bnthd,bnhdv->bnthv', q_pos, prev)
    return (o_intra + o_inter).reshape(B, S, H, DV).astype(v.dtype)


def forgetting_attention(q, k, v, log_f):
    B, S, H, Dh = q.shape
    nb = S // Q_BLOCK
    scale = Dh ** -0.5
    F = jnp.cumsum(log_f.astype(jnp.float32), axis=1).transpose(0, 2, 1)
    kh = k.transpose(0, 2, 1, 3)
    vh = v.transpose(0, 2, 1, 3)
    qb = q.reshape(B, nb, Q_BLOCK, H, Dh).transpose(1, 0, 3, 2, 4)
    Fq = F.reshape(B, H, nb, Q_BLOCK).transpose(2, 0, 1, 3)
    k_pos = jnp.arange(S)

    def block(args):
        qi, Fi, i = args
        s = jnp.einsum('bhqd,bhkd->bhqk', qi, kh).astype(jnp.float32) * scale
        s = s + Fi[..., None] - F[:, :, None, :]
        q_pos = i * Q_BLOCK + jnp.arange(Q_BLOCK)
        mask = k_pos[None, :] <= q_pos[:, None]
        p = jax.nn.softmax(jnp.where(mask, s, -jnp.inf), axis=-1)
        return jnp.einsum('bhqk,bhkd->bhqd', p.astype(vh.dtype), vh)

    out = lax.map(block, (qb, Fq, jnp.arange(nb)))
    return out.transpose(1, 0, 3, 2, 4).reshape(B, S, H, Dh)


def memory_attention(q, mk, mv):
    scale = q.shape[-1] ** -0.5
    s = jnp.einsum('bshd,bmhd->bhsm', q, mk).astype(jnp.float32) * scale
    p = jax.nn.softmax(s, axis=-1)
    return jnp.einsum('bhsm,bmhd->bshd', p.astype(mv.dtype), mv)


def setup_inputs(seed: int = 0) -> dict:
    key = jax.random.key(seed)
    ks = jax.random.split(key, 20)
    f32 = jnp.float32

    def nrm(k, shape, fan_in):
        return jax.random.normal(k, shape, f32) * (fan_in ** -0.5)

    def gain(k, shape):
        return 1.0 + 0.02 * jax.random.normal(k, shape, f32)

    L = DEPTH
    return {
        "x": jax.random.normal(ks[0], (BATCH, SEQ, D_MODEL), f32),
        "mem": jax.random.normal(ks[1], (BATCH, N_MEM, D_MODEL), f32),
        "g_mix": gain(ks[2], (L, D_MODEL)),
        "w_in": nrm(ks[3], (L, D_MODEL, D_IN), D_MODEL),
        "w_alpha_up": nrm(ks[4], (L, GLA_LOWRANK, GLA_K), GLA_LOWRANK),
        "b_alpha": 0.02 * jax.random.normal(ks[5], (L, GLA_K), f32),
        "b_forget": jax.random.uniform(ks[6], (L, FOX_HEADS), f32, 1.0, 5.0),
        "g_gla_head": gain(ks[7], (L, GLA_HEADS, GLA_DV)),
        "g_mem": gain(ks[8], (L, D_MODEL)),
        "w_mem_kv": nrm(ks[9], (L, D_MODEL, 2 * MEM_W), D_MODEL),
        "w_gla_o": nrm(ks[10], (L, GLA_V, D_MODEL), GLA_V),
        "w_fox_o": nrm(ks[11], (L, FOX_W, D_MODEL), FOX_W),
        "w_mem_o": nrm(ks[12], (L, MEM_W, D_MODEL), MEM_W),
        "w_out": nrm(ks[13], (L, D_MODEL, D_MODEL), D_MODEL),
        "g_ffn": gain(ks[14], (L, D_MODEL)),
        "w_ff1": nrm(ks[15], (L, D_MODEL, D_FF), D_MODEL),
        "w_ff2": nrm(ks[16], (L, D_FF, D_MODEL), D_FF),
        "g_final": gain(ks[17], (D_MODEL,)),
    }


def reference(x, mem, g_mix, w_in, w_alpha_up, b_alpha, b_forget, g_gla_head, g_mem, w_mem_kv,
              w_gla_o, w_fox_o, w_mem_o, w_out, g_ffn, w_ff1, w_ff2, g_final):
    B, S, D = x.shape
    M = mem.shape[1]
    h = x
    for l in range(DEPTH):
        u = rmsnorm(h, g_mix[l])
        proj = u @ w_in[l]
        (gq, gk, gv, gg, ga, fq, fk, fv, ff, mq, gates) = split_cols(proj, PROJ_SIZES)

        log_a = jax.nn.log_sigmoid(ga @ w_alpha_up[l] + b_alpha[l]) / GLA_TAU
        o_gla = gla_chunk_causal(
            gq.reshape(B, S, GLA_HEADS, GLA_DK) * (GLA_DK ** -0.5),
            gk.reshape(B, S, GLA_HEADS, GLA_DK),
            gv.reshape(B, S, GLA_HEADS, GLA_DV),
            log_a.reshape(B, S, GLA_HEADS, GLA_DK))
        o_gla = rmsnorm(o_gla, g_gla_head[l]) * jax.nn.silu(gg.reshape(B, S, GLA_HEADS, GLA_DV))
        y_gla = o_gla.reshape(B, S, GLA_V) @ w_gla_o[l]

        log_f = jax.nn.log_sigmoid(ff + b_forget[l])
        o_fox = forgetting_attention(
            fq.reshape(B, S, FOX_HEADS, FOX_DH),
            fk.reshape(B, S, FOX_HEADS, FOX_DH),
            fv.reshape(B, S, FOX_HEADS, FOX_DH),
            log_f)
        y_fox = o_fox.reshape(B, S, FOX_W) @ w_fox_o[l]

        mkv = rmsnorm(mem, g_mem[l]) @ w_mem_kv[l]
        mk, mv = jnp.split(mkv, 2, axis=-1)
        o_mem = memory_attention(
            mq.reshape(B, S, MEM_HEADS, MEM_DH),
            mk.reshape(B, M, MEM_HEADS, MEM_DH),
            mv.reshape(B, M, MEM_HEADS, MEM_DH))
        y_mem = o_mem.reshape(B, S, MEM_W) @ w_mem_o[l]

        gt = jax.nn.sigmoid(gates.reshape(B, S, N_BRANCH, D))
        merged = gt[:, :, 0] * y_gla + gt[:, :, 1] * y_fox + gt[:, :, 2] * y_mem
        h = h + merged @ w_out[l]

        u2 = rmsnorm(h, g_ffn[l])
        h = h + jnp.square(jax.nn.relu(u2 @ w_ff1[l])) @ w_ff2[l]
    return rmsnorm(h, g_final)
```

```python
import functools

import jax
import jax.numpy as jnp
from jax import lax
from jax.experimental import pallas as pl
from jax.experimental.pallas import tpu as pltpu

D_MODEL = 1024
CHUNK = 64
EPS = 1e-6
GLA_HEADS = 4
GLA_DK = 64
GLA_DV = 128
GLA_LOWRANK = 16
GLA_TAU = 16.0
FOX_HEADS = 8
FOX_DH = 64
MEM_HEADS = 4
MEM_DH = 128
D_FF = 4 * D_MODEL
GLA_K = GLA_HEADS * GLA_DK
GLA_V = GLA_HEADS * GLA_DV
FOX_W = FOX_HEADS * FOX_DH
MEM_W = MEM_HEADS * MEM_DH

LANES = 128
ROW_TILE = 256
TAIL_TILE = 256
FF_CHUNK = 1024
VMEM_LIMIT = 56 * 1024 * 1024
NEG_BIG = -1e30
PRUNE_LOGIT_GAP = 104.0

F32 = jnp.float32
BF16 = jnp.bfloat16
HIGHEST = lax.Precision.HIGHEST


def _rms(xf, g):
    r = lax.rsqrt(jnp.mean(xf * xf, axis=-1, keepdims=True) + EPS)
    return (xf * r) * g


def _log_sigmoid(x):
    return jnp.minimum(x, 0.0) - jnp.log1p(jnp.exp(-jnp.abs(x)))


def _sigmoid(x):
    return 1.0 / (1.0 + jnp.exp(-x))


def _dot(a, b):
    return jnp.dot(a, b, preferred_element_type=F32)


def _dot_nt(a, b):
    return lax.dot_general(a, b, (((1,), (1,)), ((), ())), preferred_element_type=F32)


def _dot_tn(a, b):
    return lax.dot_general(a, b, (((0,), (0,)), ((), ())), preferred_element_type=F32)


def _memkv_kernel(mem_ref, g_ref, w_ref, mk_ref, mv_ref):
    mn = _rms(mem_ref[...], g_ref[...]).astype(BF16)
    kv = _dot(mn, w_ref[...])
    mk_ref[...] = kv[:, :MEM_W].astype(BF16)
    mv_ref[...] = kv[:, MEM_W:].astype(BF16)


def _memkv(mem, g_mem, w_mem_kv):
    m = mem.shape[0]
    return pl.pallas_call(
        _memkv_kernel,
        out_shape=(jax.ShapeDtypeStruct((m, MEM_W), BF16), jax.ShapeDtypeStruct((m, MEM_W), BF16)),
        name="memkv",
    )(mem, g_mem, w_mem_kv)


def _proj_kernel(x_ref, gmix_ref, wgla_ref, wfox_ref, wmq_ref, wgate_ref, wsmall_ref, wup_ref,
                 balpha_ref, bforget_ref, mk_ref, mv_ref,
                 gla_ref, loga_ref, fox_ref, fcum_ref, omem_ref, gate_ref, stat_ref,
                 carry_ref):
    tm = x_ref.shape[0]

    @pl.when(pl.program_id(0) == 0)
    def _():
        carry_ref[...] = jnp.zeros_like(carry_ref)

    u = _rms(x_ref[...], gmix_ref[...])
    ub = u.astype(BF16)

    gla_ref[...] = _dot(ub, wgla_ref[...]).astype(BF16)
    foxb = _dot(ub, wfox_ref[...]).astype(BF16)
    fox_ref[...] = foxb
    gate_ref[...] = _sigmoid(_dot(ub, wgate_ref[...])).astype(BF16)

    small = jnp.dot(u, wsmall_ref[...], precision=HIGHEST, preferred_element_type=F32)
    ga = small[:, :GLA_LOWRANK]
    alpha_pre = jnp.dot(ga, wup_ref[...], precision=HIGHEST, preferred_element_type=F32) + balpha_ref[...]
    loga_ref[...] = _log_sigmoid(alpha_pre) * (1.0 / GLA_TAU)

    lane = lax.broadcasted_iota(jnp.int32, (tm, LANES), 1)
    ff_valid = (lane >= GLA_LOWRANK) & (lane < GLA_LOWRANK + FOX_HEADS)
    logf = jnp.where(ff_valid, _log_sigmoid(small + bforget_ref[...]), 0.0)
    logf = pltpu.roll(logf, LANES - GLA_LOWRANK, 1)
    r = lax.broadcasted_iota(jnp.int32, (tm, tm), 0)
    c = lax.broadcasted_iota(jnp.int32, (tm, tm), 1)
    tri = (r >= c).astype(F32)
    fcum = jnp.dot(tri, logf, precision=HIGHEST, preferred_element_type=F32) + carry_ref[...]
    fcum_ref[...] = fcum
    carry_ref[...] = fcum[tm - 1:tm, :]

    gi = lax.broadcasted_iota(jnp.int32, (FOX_W, LANES), 0) // FOX_DH
    gj = lax.broadcasted_iota(jnp.int32, (FOX_W, LANES), 1)
    group = (gi == gj).astype(F32)
    fq = foxb[:, :FOX_W].astype(F32)
    fk = foxb[:, FOX_W:2 * FOX_W].astype(F32)
    nq2 = jnp.max(jnp.dot(fq * fq, group, precision=HIGHEST, preferred_element_type=F32), axis=0, keepdims=True)
    nk2 = jnp.max(jnp.dot(fk * fk, group, precision=HIGHEST, preferred_element_type=F32), axis=0, keepdims=True)
    fmax = jnp.max(fcum, axis=0, keepdims=True)
    fmin = jnp.min(fcum, axis=0, keepdims=True)
    stat_ref[0] = jnp.concatenate([nq2, nk2, fmax, fmin, jnp.zeros((4, LANES), F32)], axis=0)

    mq = _dot(ub, wmq_ref[...]).astype(BF16)
    scale = MEM_DH ** -0.5
    outs = []
    for h in range(MEM_HEADS):
        sl = slice(h * MEM_DH, (h + 1) * MEM_DH)
        s = _dot_nt(mq[:, sl], mk_ref[:, sl]) * scale
        m = jnp.max(s, axis=-1, keepdims=True)
        p = jnp.exp(s - m)
        l = jnp.sum(p, axis=-1, keepdims=True)
        p = p / l
        outs.append(_dot(p.astype(BF16), mv_ref[:, sl]))
    omem_ref[...] = jnp.concatenate(outs, axis=-1).astype(BF16)


def _proj(x, g_mix, w_gla, w_fox, w_mq, w_gate, w_small, w_up, b_alpha, b_forget, mk, mv):
    s = x.shape[0]
    tm = ROW_TILE
    nt = s // tm
    full = lambda shape: pl.BlockSpec(shape, lambda i: (0,) * len(shape))
    row = lambda w: pl.BlockSpec((tm, w), lambda i: (i, 0))
    return pl.pallas_call(
        _proj_kernel,
        grid=(nt,),
        in_specs=[row(D_MODEL), full((1, D_MODEL)), full(w_gla.shape), full(w_fox.shape), full(w_mq.shape),
                  full(w_gate.shape), full(w_small.shape), full(w_up.shape), full(b_alpha.shape),
                  full(b_forget.shape), full(mk.shape), full(mv.shape)],
        out_specs=[row(w_gla.shape[1]), row(GLA_K), row(w_fox.shape[1]), row(LANES), row(MEM_W),
                   row(w_gate.shape[1]), pl.BlockSpec((1, 8, LANES), lambda i: (i, 0, 0))],
        out_shape=[jax.ShapeDtypeStruct((s, w_gla.shape[1]), BF16),
                   jax.ShapeDtypeStruct((s, GLA_K), F32),
                   jax.ShapeDtypeStruct((s, w_fox.shape[1]), BF16),
                   jax.ShapeDtypeStruct((s, LANES), F32),
                   jax.ShapeDtypeStruct((s, MEM_W), BF16),
                   jax.ShapeDtypeStruct((s, w_gate.shape[1]), BF16),
                   jax.ShapeDtypeStruct((nt, 8, LANES), F32)],
        scratch_shapes=[pltpu.VMEM((1, LANES), F32)],
        compiler_params=pltpu.CompilerParams(dimension_semantics=("arbitrary",), vmem_limit_bytes=VMEM_LIMIT),
        name="proj",
    )(x, g_mix, w_gla, w_fox, w_mq, w_gate, w_small, w_up, b_alpha, b_forget, mk, mv)


def _gla_kernel(qkvg_ref, loga_ref, ghead_ref, o_ref, state_ref):
    t = qkvg_ref.shape[0]
    nc = t // CHUNK

    @pl.when(pl.program_id(0) == 0)
    def _():
        state_ref[...] = jnp.zeros_like(state_ref)

    r = lax.broadcasted_iota(jnp.int32, (t, t), 0)
    c = lax.broadcasted_iota(jnp.int32, (t, t), 1)
    same = (r // CHUNK) == (c // CHUNK)
    lower = r >= c
    causal = same & lower
    anti = same & (r < c)

    b = jnp.dot(causal.astype(F32), loga_ref[...], precision=HIGHEST, preferred_element_type=F32)
    b_last = jnp.concatenate(
        [jnp.broadcast_to(b[(ci + 1) * CHUNK - 1:(ci + 1) * CHUNK, :], (CHUNK, GLA_K)) for ci in range(nc)], axis=0)
    e_pos = jnp.exp(b)
    e_neg = jnp.exp(-b)
    q = qkvg_ref[:, 0:GLA_K].astype(F32) * (GLA_DK ** -0.5)
    k = qkvg_ref[:, GLA_K:2 * GLA_K].astype(F32)
    q_pos = (q * e_pos).astype(BF16)
    q_neg = (q * e_neg).astype(BF16)
    k_pos = (k * e_pos).astype(BF16)
    k_neg = (k * e_neg).astype(BF16)
    k_dec = (k * jnp.exp(b_last - b)).astype(BF16)

    lane = lax.broadcasted_iota(jnp.int32, (1, LANES), 1)
    for h in range(GLA_HEADS):
        pair = slice((h // 2) * LANES, (h // 2 + 1) * LANES)
        in_head = (lane // GLA_DK) == (h % 2)
        vsl = slice(2 * GLA_K + h * GLA_DV, 2 * GLA_K + (h + 1) * GLA_DV)
        gsl = slice(2 * GLA_K + GLA_V + h * GLA_DV, 2 * GLA_K + GLA_V + (h + 1) * GLA_DV)
        v = qkvg_ref[:, vsl]
        qp = jnp.where(in_head, q_pos[:, pair], jnp.zeros((), BF16))
        qn = jnp.where(in_head, q_neg[:, pair], jnp.zeros((), BF16))
        a_c = _dot_nt(qp, k_neg[:, pair])
        a_a = _dot_nt(qn, k_pos[:, pair])
        attn = jnp.where(causal, a_c, jnp.where(anti, a_a, 0.0)).astype(BF16)
        o_intra = _dot(attn, v)

        st = state_ref[h]
        inter = []
        for ci in range(nc):
            rows = slice(ci * CHUNK, (ci + 1) * CHUNK)
            inter.append(_dot_nt(qp[rows], st.astype(BF16)))
            kv_t = _dot_tn(v[rows], k_dec[rows, pair])
            dec = e_pos[(ci + 1) * CHUNK - 1:(ci + 1) * CHUNK, pair]
            st = st * dec + jnp.where(in_head, kv_t, 0.0)
        state_ref[h] = st
        o = o_intra + jnp.concatenate(inter, axis=0)

        g = ghead_ref[:, h * GLA_DV:(h + 1) * GLA_DV]
        on = _rms(o, g)
        gg = qkvg_ref[:, gsl].astype(F32)
        o_ref[:, h * GLA_DV:(h + 1) * GLA_DV] = (on * (gg * _sigmoid(gg))).astype(BF16)


def _gla(qkvg, loga, ghead):
    s = qkvg.shape[0]
    t = ROW_TILE
    return pl.pallas_call(
        _gla_kernel,
        grid=(s // t,),
        in_specs=[pl.BlockSpec((t, qkvg.shape[1]), lambda i: (i, 0)),
                  pl.BlockSpec((t, GLA_K), lambda i: (i, 0)),
                  pl.BlockSpec((1, GLA_V), lambda i: (0, 0))],
        out_specs=pl.BlockSpec((t, GLA_V), lambda i: (i, 0)),
        out_shape=jax.ShapeDtypeStruct((s, GLA_V), BF16),
        scratch_shapes=[pltpu.VMEM((GLA_HEADS, GLA_DV, LANES), F32)],
        compiler_params=pltpu.CompilerParams(dimension_semantics=("arbitrary",), vmem_limit_bytes=VMEM_LIMIT),
        name="gla",
    )(qkvg, loga, ghead)


def _fox_kernel(thr_ref, fmin_ref, q_ref, k_ref, v_ref, fq_ref, fk_ref, o_ref, m_ref, l_ref, acc_ref):
    p = pl.program_id(0)
    i = pl.program_id(1)
    tq = q_ref.shape[0]
    scale = FOX_DH ** -0.5
    q = q_ref[...] * jnp.asarray(scale, BF16)
    lane = lax.broadcasted_iota(jnp.int32, (1, LANES), 1)
    row = lax.broadcasted_iota(jnp.int32, (tq, tq), 0)
    col = lax.broadcasted_iota(jnp.int32, (tq, tq), 1)
    q0 = pl.multiple_of(i * tq, tq)

    for hi in range(2):
        h = 2 * p + hi
        qm = jnp.where((lane // FOX_DH) == hi, q, jnp.zeros((), BF16))
        fq = fq_ref[:, hi:hi + 1]

        s = _dot_nt(qm, k_ref[pl.ds(q0, tq), :]) + fq - fk_ref[hi:hi + 1, pl.ds(q0, tq)]
        s = jnp.where(row >= col, s, NEG_BIG)
        m0 = jnp.max(s, axis=-1, keepdims=True)
        pr = jnp.exp(s - m0)
        m_ref[hi] = m0
        l_ref[hi] = jnp.sum(pr, axis=-1, keepdims=True)
        acc_ref[hi] = _dot(pr.astype(BF16), v_ref[pl.ds(q0, tq), :])

        thr = thr_ref[h, i]

        def cond(j):
            jc = jnp.maximum(j, 0)
            return (j >= 0) & (fmin_ref[h, jc] <= thr)

        def body(j):
            k0 = pl.multiple_of(j * tq, tq)
            sj = _dot_nt(qm, k_ref[pl.ds(k0, tq), :]) + fq - fk_ref[hi:hi + 1, pl.ds(k0, tq)]
            m_old = m_ref[hi]
            m_new = jnp.maximum(m_old, jnp.max(sj, axis=-1, keepdims=True))
            alpha = jnp.exp(m_old - m_new)
            pj = jnp.exp(sj - m_new)
            m_ref[hi] = m_new
            l_ref[hi] = alpha * l_ref[hi] + jnp.sum(pj, axis=-1, keepdims=True)
            acc_ref[hi] = alpha * acc_ref[hi] + _dot(pj.astype(BF16), v_ref[pl.ds(k0, tq), :])
            return j - 1

        lax.while_loop(cond, body, i - 1)

    out0 = acc_ref[0] / l_ref[0]
    out1 = acc_ref[1] / l_ref[1]
    o_ref[...] = jnp.where((lane // FOX_DH) == 0, out0, out1).astype(BF16)


def _fox(thr, fmin_blk, fox_qkv, fq_cols, fk_rows):
    s = fox_qkv.shape[0]
    tq = ROW_TILE
    nq = s // tq
    npair = FOX_HEADS // 2
    grid_spec = pltpu.PrefetchScalarGridSpec(
        num_scalar_prefetch=2,
        grid=(npair, nq),
        in_specs=[pl.BlockSpec((tq, LANES), lambda p, i, *_: (i, p)),
                  pl.BlockSpec((s, LANES), lambda p, i, *_: (0, npair + p)),
                  pl.BlockSpec((s, LANES), lambda p, i, *_: (0, 2 * npair + p)),
                  pl.BlockSpec((None, tq, 2), lambda p, i, *_: (p, i, 0)),
                  pl.BlockSpec((None, 2, s), lambda p, i, *_: (p, 0, 0))],
        out_specs=pl.BlockSpec((tq, LANES), lambda p, i, *_: (i, p)),
        scratch_shapes=[pltpu.VMEM((2, tq, 1), F32), pltpu.VMEM((2, tq, 1), F32), pltpu.VMEM((2, tq, LANES), F32)],
    )
    return pl.pallas_call(
        _fox_kernel,
        grid_spec=grid_spec,
        out_shape=jax.ShapeDtypeStruct((s, FOX_W), BF16),
        compiler_params=pltpu.CompilerParams(dimension_semantics=("arbitrary", "arbitrary"),
                                             vmem_limit_bytes=VMEM_LIMIT),
        name="fox",
    )(thr, fmin_blk, fox_qkv, fox_qkv, fox_qkv, fq_cols, fk_rows)


def _tail_kernel(x_ref, ogla_ref, ofox_ref, omem_ref, gate_ref, wg_ref, wf_ref, wm_ref, wo_ref,
                 gffn_ref, w1_ref, w2_ref, gfin_ref, out_ref):
    gate = gate_ref[...]
    merged = (gate[:, 0:D_MODEL].astype(F32) * _dot(ogla_ref[...], wg_ref[...])
              + gate[:, D_MODEL:2 * D_MODEL].astype(F32) * _dot(ofox_ref[...], wf_ref[...])
              + gate[:, 2 * D_MODEL:3 * D_MODEL].astype(F32) * _dot(omem_ref[...], wm_ref[...]))
    h = x_ref[...] + _dot(merged.astype(BF16), wo_ref[...])
    u2 = _rms(h, gffn_ref[...]).astype(BF16)
    acc = jnp.zeros_like(h)
    for cidx in range(D_FF // FF_CHUNK):
        cs = slice(cidx * FF_CHUNK, (cidx + 1) * FF_CHUNK)
        a = jnp.maximum(_dot(u2, w1_ref[:, cs]), 0.0)
        acc = acc + _dot((a * a).astype(BF16), w2_ref[cs, :])
    out_ref[...] = _rms(h + acc, gfin_ref[...])


def _tail(x, ogla, ofox, omem, gate, wg, wf, wm, wo, gffn, w1, w2, gfin):
    s = x.shape[0]
    tm = TAIL_TILE
    full = lambda a: pl.BlockSpec(a.shape, lambda i: (0,) * a.ndim, pipeline_mode=pl.Buffered(1))
    row = lambda w: pl.BlockSpec((tm, w), lambda i: (i, 0))
    return pl.pallas_call(
        _tail_kernel,
        grid=(s // tm,),
        in_specs=[row(D_MODEL), row(GLA_V), row(FOX_W), row(MEM_W), row(3 * D_MODEL),
                  full(wg), full(wf), full(wm), full(wo), full(gffn), full(w1), full(w2), full(gfin)],
        out_specs=row(D_MODEL),
        out_shape=jax.ShapeDtypeStruct((s, D_MODEL), F32),
        compiler_params=pltpu.CompilerParams(dimension_semantics=("arbitrary",), vmem_limit_bytes=VMEM_LIMIT),
        name="tail",
    )(x, ogla, ofox, omem, gate, wg, wf, wm, wo, gffn, w1, w2, gfin)


def kernel(x, mem, g_mix, w_in, w_alpha_up, b_alpha, b_forget, g_gla_head, g_mem, w_mem_kv,
           w_gla_o, w_fox_o, w_mem_o, w_out, g_ffn, w_ff1, w_ff2, g_final):
    assert x.shape[0] == 1 and g_mix.shape[0] == 1, "single batch, single layer"
    s = x.shape[1]
    assert s % ROW_TILE == 0 and s % TAIL_TILE == 0
    xs = x[0]
    w = w_in[0]

    o_ga = 2 * GLA_K + 2 * GLA_V
    o_fox = o_ga + GLA_LOWRANK
    o_ff = o_fox + 3 * FOX_W
    o_mq = o_ff + FOX_HEADS
    o_gate = o_mq + MEM_W
    w_gla = w[:, :o_ga].astype(BF16)
    w_fox = w[:, o_fox:o_ff].astype(BF16)
    w_mq = w[:, o_mq:o_gate].astype(BF16)
    w_gate = w[:, o_gate:].astype(BF16)
    w_small = jnp.concatenate(
        [w[:, o_ga:o_fox], w[:, o_ff:o_mq], jnp.zeros((D_MODEL, LANES - GLA_LOWRANK - FOX_HEADS), F32)], axis=1)
    b_f = jnp.zeros((1, LANES), F32).at[0, GLA_LOWRANK:GLA_LOWRANK + FOX_HEADS].set(b_forget[0])

    mk, mv = _memkv(mem[0], g_mem, w_mem_kv[0].astype(BF16))
    gla_qkvg, loga, fox_qkv, fcum, omem, gate, stats = _proj(
        xs, g_mix, w_gla, w_fox, w_mq, w_gate, w_small, w_alpha_up[0], b_alpha, b_f, mk, mv)

    ogla = _gla(gla_qkvg, loga, g_gla_head.reshape(1, GLA_V))

    f8 = fcum[:, :FOX_HEADS]
    fq_cols = f8.reshape(s, FOX_HEADS // 2, 2).transpose(1, 0, 2)
    fk_rows = f8.T.reshape(FOX_HEADS // 2, 2, s)
    qn = jnp.sqrt(stats[:, 0, :FOX_HEADS])
    kn = jnp.sqrt(jnp.max(stats[:, 1, :FOX_HEADS], axis=0))
    thr = (2.0 * 1.02 * (FOX_DH ** -0.5)) * qn * kn[None, :] + stats[:, 2, :FOX_HEADS] + PRUNE_LOGIT_GAP
    ofox = _fox(thr.T, stats[:, 3, :FOX_HEADS].T, fox_qkv, fq_cols, fk_rows)

    out = _tail(xs, ogla, ofox, omem, gate, w_gla_o[0].astype(BF16), w_fox_o[0].astype(BF16),
                w_mem_o[0].astype(BF16), w_out[0].astype(BF16), g_ffn, w_ff1[0].astype(BF16),
                w_ff2[0].astype(BF16), g_final.reshape(1, D_MODEL))
    return out[None]
```

```python
import functools

import jax
import jax.numpy as jnp
import numpy as np
from jax import lax
from jax.experimental import pallas as pl
from jax.experimental.pallas import tpu as pltpu

D_MODEL = 1024
CHUNK = 64
EPS = 1e-6
GLA_HEADS = 4
GLA_DK = 64
GLA_DV = 128
GLA_LOWRANK = 16
GLA_TAU = 16.0
FOX_HEADS = 8
FOX_DH = 64
MEM_HEADS = 4
MEM_DH = 128
D_FF = 4 * D_MODEL
GLA_K = GLA_HEADS * GLA_DK
GLA_V = GLA_HEADS * GLA_DV
FOX_W = FOX_HEADS * FOX_DH
MEM_W = MEM_HEADS * MEM_DH

LANES = 128
FOX_ONES_ROWS = 16
FOX_AUG_PER_HEAD = 6
ROW_TILE = 256
TAIL_TILE = 256
FF_CHUNK = 1024
VMEM_LIMIT = 56 * 1024 * 1024
NEG_BIG = -1e30
PRUNE_LOGIT_GAP = 104.0

F32 = jnp.float32
BF16 = jnp.bfloat16
HIGHEST = lax.Precision.HIGHEST


def _rms(xf, g):
    r = lax.rsqrt(jnp.mean(xf * xf, axis=-1, keepdims=True) + EPS)
    return (xf * r) * g


def _log_sigmoid(x):
    return jnp.minimum(x, 0.0) - jnp.log1p(jnp.exp(-jnp.abs(x)))


def _sigmoid(x):
    return 1.0 / (1.0 + jnp.exp(-x))


def _dot(a, b):
    return jnp.dot(a, b, preferred_element_type=F32)


def _dot_nt(a, b):
    return lax.dot_general(a, b, (((1,), (1,)), ((), ())), preferred_element_type=F32)


def _dot_tn(a, b):
    return lax.dot_general(a, b, (((0,), (0,)), ((), ())), preferred_element_type=F32)


def _memkv_kernel(mem_ref, g_ref, w_ref, mk_ref, mv_ref):
    mn = _rms(mem_ref[...], g_ref[...]).astype(BF16)
    kv = _dot(mn, w_ref[...])
    mk_ref[...] = kv[:, :MEM_W].astype(BF16)
    mv_ref[...] = kv[:, MEM_W:].astype(BF16)


def _memkv(mem, g_mem, w_mem_kv):
    m = mem.shape[0]
    return pl.pallas_call(
        _memkv_kernel,
        out_shape=(jax.ShapeDtypeStruct((m, MEM_W), BF16), jax.ShapeDtypeStruct((m, MEM_W), BF16)),
        name="memkv",
    )(mem, g_mem, w_mem_kv)


def _proj_kernel(x_ref, gmix_ref, wgla_ref, wfox_ref, wfvt_ref, wmq_ref, wgate_ref, wsmall_ref, wup_ref,
                 balpha_ref, bforget_ref, mk_ref, mv_ref, pq_ref, pk_ref,
                 gla_ref, loga_ref, qaug_ref, kaug_ref, vt_ref, omem_ref, gate_ref, stat_ref,
                 carry_ref):
    tm = x_ref.shape[0]

    @pl.when(pl.program_id(0) == 0)
    def _():
        carry_ref[...] = jnp.zeros_like(carry_ref)

    u = _rms(x_ref[...], gmix_ref[...])
    ub = u.astype(BF16)

    gla_ref[...] = _dot(ub, wgla_ref[...]).astype(BF16)
    foxb = _dot(ub, wfox_ref[...]).astype(BF16)
    gate_ref[...] = _sigmoid(_dot(ub, wgate_ref[...])).astype(BF16)

    vt = _dot_nt(wfvt_ref[...], ub).astype(BF16)
    ones = jnp.ones((FOX_ONES_ROWS, tm), BF16)
    vt_parts = []
    for h in range(FOX_HEADS):
        vt_parts += [vt[h * FOX_DH:(h + 1) * FOX_DH, :], ones]
    vt_ref[...] = jnp.concatenate(vt_parts, axis=0)

    small = jnp.dot(u, wsmall_ref[...], precision=HIGHEST, preferred_element_type=F32)
    ga = small[:, :GLA_LOWRANK]
    alpha_pre = jnp.dot(ga, wup_ref[...], precision=HIGHEST, preferred_element_type=F32) + balpha_ref[...]
    loga_ref[...] = _log_sigmoid(alpha_pre) * (1.0 / GLA_TAU)

    lane = lax.broadcasted_iota(jnp.int32, (tm, LANES), 1)
    ff_valid = (lane >= GLA_LOWRANK) & (lane < GLA_LOWRANK + FOX_HEADS)
    logf = jnp.where(ff_valid, _log_sigmoid(small + bforget_ref[...]), 0.0)
    logf = pltpu.roll(logf, LANES - GLA_LOWRANK, 1)
    r = lax.broadcasted_iota(jnp.int32, (tm, tm), 0)
    c = lax.broadcasted_iota(jnp.int32, (tm, tm), 1)
    tri = (r >= c).astype(F32)
    fcum = jnp.dot(tri, logf, precision=HIGHEST, preferred_element_type=F32) + carry_ref[...]
    carry_ref[...] = fcum[tm - 1:tm, :]

    f_hi = fcum.astype(BF16)
    rem = fcum - f_hi.astype(F32)
    f_mid = rem.astype(BF16)
    f_lo = (rem - f_mid.astype(F32)).astype(BF16)
    f3 = jnp.concatenate([f_hi, f_mid, f_lo], axis=1)
    aug_lane = lax.broadcasted_iota(jnp.int32, (1, FOX_W), 1) % LANES
    slot = aug_lane % FOX_AUG_PER_HEAD
    in_aug = aug_lane < 2 * FOX_AUG_PER_HEAD
    augq = _dot(f3, pq_ref[...]) + jnp.where(in_aug & (slot < 3), -1.0, 0.0)
    augk = _dot(f3, pk_ref[...]) + jnp.where(in_aug & (slot >= 3), 1.0, 0.0)
    for p in range(FOX_HEADS // 2):
        src = slice(p * LANES, (p + 1) * LANES)
        qaug_ref[:, 2 * p * LANES:(2 * p + 1) * LANES] = foxb[:, src] * jnp.asarray(FOX_DH ** -0.5, BF16)
        qaug_ref[:, (2 * p + 1) * LANES:(2 * p + 2) * LANES] = augq[:, src].astype(BF16)
        kaug_ref[:, 2 * p * LANES:(2 * p + 1) * LANES] = foxb[:, FOX_W + p * LANES:FOX_W + (p + 1) * LANES]
        kaug_ref[:, (2 * p + 1) * LANES:(2 * p + 2) * LANES] = augk[:, src].astype(BF16)

    gi = lax.broadcasted_iota(jnp.int32, (FOX_W, LANES), 0) // FOX_DH
    gj = lax.broadcasted_iota(jnp.int32, (FOX_W, LANES), 1)
    group = (gi == gj).astype(F32)
    fq = foxb[:, :FOX_W].astype(F32)
    fk = foxb[:, FOX_W:2 * FOX_W].astype(F32)
    nq2 = jnp.max(jnp.dot(fq * fq, group, precision=HIGHEST, preferred_element_type=F32), axis=0, keepdims=True)
    nk2 = jnp.max(jnp.dot(fk * fk, group, precision=HIGHEST, preferred_element_type=F32), axis=0, keepdims=True)
    fmax = jnp.max(fcum, axis=0, keepdims=True)
    fmin = jnp.min(fcum, axis=0, keepdims=True)
    stat_ref[0] = jnp.concatenate([nq2, nk2, fmax, fmin, jnp.zeros((4, LANES), F32)], axis=0)

    mq = _dot(ub, wmq_ref[...]).astype(BF16)
    scale = MEM_DH ** -0.5
    outs = []
    for h in range(MEM_HEADS):
        sl = slice(h * MEM_DH, (h + 1) * MEM_DH)
        s = _dot_nt(mq[:, sl], mk_ref[:, sl]) * scale
        m = jnp.max(s, axis=-1, keepdims=True)
        p = jnp.exp(s - m)
        l = jnp.sum(p, axis=-1, keepdims=True)
        p = p / l
        outs.append(_dot(p.astype(BF16), mv_ref[:, sl]))
    omem_ref[...] = jnp.concatenate(outs, axis=-1).astype(BF16)


def _aug_placement():
    pq = np.zeros((3 * LANES, FOX_W), np.float32)
    pk = np.zeros((3 * LANES, FOX_W), np.float32)
    for h in range(FOX_HEADS):
        base = (h // 2) * LANES + (h % 2) * FOX_AUG_PER_HEAD
        for c in range(3):
            pq[c * LANES + h, base + 3 + c] = 1.0
            pk[c * LANES + h, base + c] = 1.0
    return jnp.asarray(pq, BF16), jnp.asarray(pk, BF16)


def _proj(x, g_mix, w_gla, w_fox, w_fvt, w_mq, w_gate, w_small, w_up, b_alpha, b_forget, mk, mv):
    s = x.shape[0]
    tm = ROW_TILE
    nt = s // tm
    pq, pk = _aug_placement()
    full = lambda shape: pl.BlockSpec(shape, lambda i: (0,) * len(shape))
    row = lambda w: pl.BlockSpec((tm, w), lambda i: (i, 0))
    vt_rows = FOX_HEADS * (FOX_DH + FOX_ONES_ROWS)
    return pl.pallas_call(
        _proj_kernel,
        grid=(nt,),
        in_specs=[row(D_MODEL), full((1, D_MODEL)), full(w_gla.shape), full(w_fox.shape), full(w_fvt.shape),
                  full(w_mq.shape), full(w_gate.shape), full(w_small.shape), full(w_up.shape),
                  full(b_alpha.shape), full(b_forget.shape), full(mk.shape), full(mv.shape),
                  full(pq.shape), full(pk.shape)],
        out_specs=[row(w_gla.shape[1]), row(GLA_K), row(2 * FOX_W), row(2 * FOX_W),
                   pl.BlockSpec((vt_rows, tm), lambda i: (0, i)), row(MEM_W),
                   row(w_gate.shape[1]), pl.BlockSpec((1, 8, LANES), lambda i: (i, 0, 0))],
        out_shape=[jax.ShapeDtypeStruct((s, w_gla.shape[1]), BF16),
                   jax.ShapeDtypeStruct((s, GLA_K), F32),
                   jax.ShapeDtypeStruct((s, 2 * FOX_W), BF16),
                   jax.ShapeDtypeStruct((s, 2 * FOX_W), BF16),
                   jax.ShapeDtypeStruct((vt_rows, s), BF16),
                   jax.ShapeDtypeStruct((s, MEM_W), BF16),
                   jax.ShapeDtypeStruct((s, w_gate.shape[1]), BF16),
                   jax.ShapeDtypeStruct((nt, 8, LANES), F32)],
        scratch_shapes=[pltpu.VMEM((1, LANES), F32)],
        compiler_params=pltpu.CompilerParams(dimension_semantics=("arbitrary",), vmem_limit_bytes=VMEM_LIMIT),
        name="proj",
    )(x, g_mix, w_gla, w_fox, w_fvt, w_mq, w_gate, w_small, w_up, b_alpha, b_forget, mk, mv, pq, pk)


def _gla_kernel(qkvg_ref, loga_ref, ghead_ref, o_ref, state_ref):
    t = qkvg_ref.shape[0]
    nc = t // CHUNK

    @pl.when(pl.program_id(0) == 0)
    def _():
        state_ref[...] = jnp.zeros_like(state_ref)

    r = lax.broadcasted_iota(jnp.int32, (t, t), 0)
    c = lax.broadcasted_iota(jnp.int32, (t, t), 1)
    same = (r // CHUNK) == (c // CHUNK)
    lower = r >= c
    causal = same & lower
    anti = same & (r < c)

    b = jnp.dot(causal.astype(F32), loga_ref[...], precision=HIGHEST, preferred_element_type=F32)
    b_last = jnp.concatenate(
        [jnp.broadcast_to(b[(ci + 1) * CHUNK - 1:(ci + 1) * CHUNK, :], (CHUNK, GLA_K)) for ci in range(nc)], axis=0)
    e_pos = jnp.exp(b)
    e_neg = jnp.exp(-b)
    q = qkvg_ref[:, 0:GLA_K].astype(F32) * (GLA_DK ** -0.5)
    k = qkvg_ref[:, GLA_K:2 * GLA_K].astype(F32)
    q_pos = (q * e_pos).astype(BF16)
    q_neg = (q * e_neg).astype(BF16)
    k_pos = (k * e_pos).astype(BF16)
    k_neg = (k * e_neg).astype(BF16)
    k_dec = (k * jnp.exp(b_last - b)).astype(BF16)

    lane = lax.broadcasted_iota(jnp.int32, (1, LANES), 1)
    for h in range(GLA_HEADS):
        pair = slice((h // 2) * LANES, (h // 2 + 1) * LANES)
        in_head = (lane // GLA_DK) == (h % 2)
        vsl = slice(2 * GLA_K + h * GLA_DV, 2 * GLA_K + (h + 1) * GLA_DV)
        gsl = slice(2 * GLA_K + GLA_V + h * GLA_DV, 2 * GLA_K + GLA_V + (h + 1) * GLA_DV)
        v = qkvg_ref[:, vsl]
        qp = jnp.where(in_head, q_pos[:, pair], jnp.zeros((), BF16))
        qn = jnp.where(in_head, q_neg[:, pair], jnp.zeros((), BF16))
        a_c = _dot_nt(qp, k_neg[:, pair])
        a_a = _dot_nt(qn, k_pos[:, pair])
        attn = jnp.where(causal, a_c, jnp.where(anti, a_a, 0.0)).astype(BF16)
        o_intra = _dot(attn, v)

        st = state_ref[h]
        inter = []
        for ci in range(nc):
            rows = slice(ci * CHUNK, (ci + 1) * CHUNK)
            inter.append(_dot_nt(qp[rows], st.astype(BF16)))
            kv_t = _dot_tn(v[rows], k_dec[rows, pair])
            dec = e_pos[(ci + 1) * CHUNK - 1:(ci + 1) * CHUNK, pair]
            st = st * dec + jnp.where(in_head, kv_t, 0.0)
        state_ref[h] = st
        o = o_intra + jnp.concatenate(inter, axis=0)

        g = ghead_ref[:, h * GLA_DV:(h + 1) * GLA_DV]
        on = _rms(o, g)
        gg = qkvg_ref[:, gsl].astype(F32)
        o_ref[:, h * GLA_DV:(h + 1) * GLA_DV] = (on * (gg * _sigmoid(gg))).astype(BF16)


def _gla(qkvg, loga, ghead):
    s = qkvg.shape[0]
    t = ROW_TILE
    return pl.pallas_call(
        _gla_kernel,
        grid=(s // t,),
        in_specs=[pl.BlockSpec((t, qkvg.shape[1]), lambda i: (i, 0)),
                  pl.BlockSpec((t, GLA_K), lambda i: (i, 0)),
                  pl.BlockSpec((1, GLA_V), lambda i: (0, 0))],
        out_specs=pl.BlockSpec((t, GLA_V), lambda i: (i, 0)),
        out_shape=jax.ShapeDtypeStruct((s, GLA_V), BF16),
        scratch_shapes=[pltpu.VMEM((GLA_HEADS, GLA_DV, LANES), F32)],
        compiler_params=pltpu.CompilerParams(dimension_semantics=("arbitrary",), vmem_limit_bytes=VMEM_LIMIT),
        name="gla",
    )(qkvg, loga, ghead)


def _fox_kernel(thr_ref, fmin_ref, q_ref, k_ref, vt_ref, o_ref, s_ref):
    p = pl.program_id(0)
    i = pl.program_id(1)
    tq = q_ref.shape[0]
    hrows = FOX_DH + FOX_ONES_ROWS
    q = q_ref[...]
    lane = lax.broadcasted_iota(jnp.int32, (1, 2 * LANES), 1)
    krow = lax.broadcasted_iota(jnp.int32, (tq, tq), 0)
    qcol = lax.broadcasted_iota(jnp.int32, (tq, tq), 1)
    q0 = pl.multiple_of(i * tq, tq)
    outs = []

    for hi in range(2):
        h = 2 * p + hi
        aug0 = LANES + hi * FOX_AUG_PER_HEAD
        mine = ((lane // FOX_DH) == hi) | ((lane >= aug0) & (lane < aug0 + FOX_AUG_PER_HEAD))
        qm = jnp.where(mine, q, jnp.zeros((), BF16))
        vrows = slice(hi * hrows, (hi + 1) * hrows)

        def softmax_pv(s_t, m, acc, k0):
            m_new = jnp.maximum(m, jnp.max(s_t, axis=0, keepdims=True))
            alpha = jnp.exp(m - m_new)
            pt = jnp.exp(s_t - m_new).astype(BF16)
            acc = alpha * acc + _dot(vt_ref[vrows, pl.ds(k0, tq)], pt)
            return m_new, acc

        s0 = _dot_nt(k_ref[pl.ds(q0, tq), :], qm)
        s_ref[i & 1] = jnp.where(krow <= qcol, s0, NEG_BIG)
        thr = thr_ref[h, i]

        def cond(carry):
            j = carry[0]
            return (j >= 0) & (fmin_ref[h, jnp.maximum(j, 0)] <= thr)

        def body(carry):
            j, m, acc = carry
            s_cur = s_ref[(j + 1) & 1]
            k0 = pl.multiple_of(j * tq, tq)
            s_next = _dot_nt(k_ref[pl.ds(k0, tq), :], qm)
            m, acc = softmax_pv(s_cur, m, acc, pl.multiple_of(k0 + tq, tq))
            s_ref[j & 1] = s_next
            return j - 1, m, acc

        m0 = jnp.full((1, tq), NEG_BIG, F32)
        acc0 = jnp.zeros((hrows, tq), F32)
        j_end, m, acc = lax.while_loop(cond, body, (i - 1, m0, acc0))
        _, acc = softmax_pv(s_ref[(j_end + 1) & 1], m, acc, pl.multiple_of((j_end + 1) * tq, tq))
        outs.append(acc[:FOX_DH] / acc[FOX_DH:FOX_DH + 1])

    o_ref[...] = jnp.concatenate(outs, axis=0).T.astype(BF16)


def _fox(thr, fmin_blk, qaug, kaug, vt):
    s = qaug.shape[0]
    tq = ROW_TILE
    nq = s // tq
    npair = FOX_HEADS // 2
    prow = 2 * (FOX_DH + FOX_ONES_ROWS)
    grid_spec = pltpu.PrefetchScalarGridSpec(
        num_scalar_prefetch=2,
        grid=(npair, nq),
        in_specs=[pl.BlockSpec((tq, 2 * LANES), lambda p, i, *_: (i, p)),
                  pl.BlockSpec((s, 2 * LANES), lambda p, i, *_: (0, p)),
                  pl.BlockSpec((prow, s), lambda p, i, *_: (p, 0))],
        out_specs=pl.BlockSpec((tq, LANES), lambda p, i, *_: (i, p)),
        scratch_shapes=[pltpu.VMEM((2, tq, tq), F32)],
    )
    return pl.pallas_call(
        _fox_kernel,
        grid_spec=grid_spec,
        out_shape=jax.ShapeDtypeStruct((s, FOX_W), BF16),
        compiler_params=pltpu.CompilerParams(dimension_semantics=("arbitrary", "arbitrary"),
                                             vmem_limit_bytes=VMEM_LIMIT),
        name="fox",
    )(thr, fmin_blk, qaug, kaug, vt)


def _tail_kernel(x_ref, ogla_ref, ofox_ref, omem_ref, gate_ref, wg_ref, wf_ref, wm_ref, wo_ref,
                 gffn_ref, w1_ref, w2_ref, gfin_ref, out_ref):
    gate = gate_ref[...]
    merged = (gate[:, 0:D_MODEL].astype(F32) * _dot(ogla_ref[...], wg_ref[...])
              + gate[:, D_MODEL:2 * D_MODEL].astype(F32) * _dot(ofox_ref[...], wf_ref[...])
              + gate[:, 2 * D_MODEL:3 * D_MODEL].astype(F32) * _dot(omem_ref[...], wm_ref[...]))
    h = x_ref[...] + _dot(merged.astype(BF16), wo_ref[...])
    u2 = _rms(h, gffn_ref[...]).astype(BF16)
    acc = jnp.zeros_like(h)
    for cidx in range(D_FF // FF_CHUNK):
        cs = slice(cidx * FF_CHUNK, (cidx + 1) * FF_CHUNK)
        a = jnp.maximum(_dot(u2, w1_ref[:, cs]), 0.0)
        acc = acc + _dot((a * a).astype(BF16), w2_ref[cs, :])
    out_ref[...] = _rms(h + acc, gfin_ref[...])


def _tail(x, ogla, ofox, omem, gate, wg, wf, wm, wo, gffn, w1, w2, gfin):
    s = x.shape[0]
    tm = TAIL_TILE
    full = lambda a: pl.BlockSpec(a.shape, lambda i: (0,) * a.ndim, pipeline_mode=pl.Buffered(1))
    row = lambda w: pl.BlockSpec((tm, w), lambda i: (i, 0))
    return pl.pallas_call(
        _tail_kernel,
        grid=(s // tm,),
        in_specs=[row(D_MODEL), row(GLA_V), row(FOX_W), row(MEM_W), row(3 * D_MODEL),
                  full(wg), full(wf), full(wm), full(wo), full(gffn), full(w1), full(w2), full(gfin)],
        out_specs=row(D_MODEL),
        out_shape=jax.ShapeDtypeStruct((s, D_MODEL), F32),
        compiler_params=pltpu.CompilerParams(dimension_semantics=("arbitrary",), vmem_limit_bytes=VMEM_LIMIT),
        name="tail",
    )(x, ogla, ofox, omem, gate, wg, wf, wm, wo, gffn, w1, w2, gfin)


def kernel(x, mem, g_mix, w_in, w_alpha_up, b_alpha, b_forget, g_gla_head, g_mem, w_mem_kv,
           w_gla_o, w_fox_o, w_mem_o, w_out, g_ffn, w_ff1, w_ff2, g_final):
    assert x.shape[0] == 1 and g_mix.shape[0] == 1, "single batch, single layer"
    s = x.shape[1]
    assert s % ROW_TILE == 0 and s % TAIL_TILE == 0
    xs = x[0]
    w = w_in[0]

    o_ga = 2 * GLA_K + 2 * GLA_V
    o_fox = o_ga + GLA_LOWRANK
    o_ff = o_fox + 3 * FOX_W
    o_mq = o_ff + FOX_HEADS
    o_gate = o_mq + MEM_W
    w_gla = w[:, :o_ga].astype(BF16)
    w_fox = w[:, o_fox:o_fox + 2 * FOX_W].astype(BF16)
    w_fvt = w[:, o_fox + 2 * FOX_W:o_ff].T.astype(BF16)
    w_mq = w[:, o_mq:o_gate].astype(BF16)
    w_gate = w[:, o_gate:].astype(BF16)
    w_small = jnp.concatenate(
        [w[:, o_ga:o_fox], w[:, o_ff:o_mq], jnp.zeros((D_MODEL, LANES - GLA_LOWRANK - FOX_HEADS), F32)], axis=1)
    b_f = jnp.zeros((1, LANES), F32).at[0, GLA_LOWRANK:GLA_LOWRANK + FOX_HEADS].set(b_forget[0])

    mk, mv = _memkv(mem[0], g_mem, w_mem_kv[0].astype(BF16))
    gla_qkvg, loga, qaug, kaug, vt, omem, gate, stats = _proj(
        xs, g_mix, w_gla, w_fox, w_fvt, w_mq, w_gate, w_small, w_alpha_up[0], b_alpha, b_f, mk, mv)

    ogla = _gla(gla_qkvg, loga, g_gla_head.reshape(1, GLA_V))

    qn = jnp.sqrt(stats[:, 0, :FOX_HEADS])
    kn = jnp.sqrt(jnp.max(stats[:, 1, :FOX_HEADS], axis=0))
    thr = (2.0 * 1.02 * (FOX_DH ** -0.5)) * qn * kn[None, :] + stats[:, 2, :FOX_HEADS] + PRUNE_LOGIT_GAP
    ofox = _fox(thr.T, stats[:, 3, :FOX_HEADS].T, qaug, kaug, vt)

    out = _tail(xs, ogla, ofox, omem, gate, w_gla_o[0].astype(BF16), w_fox_o[0].astype(BF16),
                w_mem_o[0].astype(BF16), w_out[0].astype(BF16), g_ffn, w_ff1[0].astype(BF16),
                w_ff2[0].astype(BF16), g_final.reshape(1, D_MODEL))
    return out[None]
```

```python
import functools

import jax
import jax.numpy as jnp
import numpy as np
from jax import lax
from jax.experimental import pallas as pl
from jax.experimental.pallas import tpu as pltpu

D_MODEL = 1024
CHUNK = 64
EPS = 1e-6
GLA_HEADS = 4
GLA_DK = 64
GLA_DV = 128
GLA_LOWRANK = 16
GLA_TAU = 16.0
FOX_HEADS = 8
FOX_DH = 64
MEM_HEADS = 4
MEM_DH = 128
D_FF = 4 * D_MODEL
GLA_K = GLA_HEADS * GLA_DK
GLA_V = GLA_HEADS * GLA_DV
FOX_W = FOX_HEADS * FOX_DH
MEM_W = MEM_HEADS * MEM_DH

LANES = 128
FOX_ONES_ROWS = 16
FOX_AUG_PER_HEAD = 6
ROW_TILE = 256
FOX_TQ = 512
FOX_TK = 256
TAIL_TILE = 256
FF_CHUNK = 1024
VMEM_LIMIT = 56 * 1024 * 1024
NEG_BIG = -1e30
PRUNE_LOGIT_GAP = 104.0

F32 = jnp.float32
BF16 = jnp.bfloat16
HIGHEST = lax.Precision.HIGHEST


def _rms(xf, g):
    r = lax.rsqrt(jnp.mean(xf * xf, axis=-1, keepdims=True) + EPS)
    return (xf * r) * g


def _log_sigmoid(x):
    return jnp.minimum(x, 0.0) - jnp.log1p(jnp.exp(-jnp.abs(x)))


def _sigmoid(x):
    return 1.0 / (1.0 + jnp.exp(-x))


def _dot(a, b):
    return jnp.dot(a, b, preferred_element_type=F32)


def _dot_nt(a, b):
    return lax.dot_general(a, b, (((1,), (1,)), ((), ())), preferred_element_type=F32)


def _dot_tn(a, b):
    return lax.dot_general(a, b, (((0,), (0,)), ((), ())), preferred_element_type=F32)


def _memkv_kernel(mem_ref, g_ref, w_ref, mk_ref, mv_ref):
    mn = _rms(mem_ref[...], g_ref[...]).astype(BF16)
    kv = _dot(mn, w_ref[...])
    mk_ref[...] = kv[:, :MEM_W].astype(BF16)
    mv_ref[...] = kv[:, MEM_W:].astype(BF16)


def _memkv(mem, g_mem, w_mem_kv):
    m = mem.shape[0]
    return pl.pallas_call(
        _memkv_kernel,
        out_shape=(jax.ShapeDtypeStruct((m, MEM_W), BF16), jax.ShapeDtypeStruct((m, MEM_W), BF16)),
        name="memkv",
    )(mem, g_mem, w_mem_kv)


def _proj_kernel(x_ref, gmix_ref, wgla_ref, wfk_ref, wqvt_ref, wmq_ref, wgate_ref, wsmall_ref, wup_ref,
                 balpha_ref, bforget_ref, mk_ref, mv_ref, pqt_ref, pk_ref,
                 gla_ref, loga_ref, qt_ref, kaug_ref, vt_ref, omem_ref, gate_ref, stat_ref, statq_ref,
                 carry_ref):
    tm = x_ref.shape[0]

    @pl.when(pl.program_id(0) == 0)
    def _():
        carry_ref[...] = jnp.zeros_like(carry_ref)

    u = _rms(x_ref[...], gmix_ref[...])
    ub = u.astype(BF16)

    gla_ref[...] = _dot(ub, wgla_ref[...]).astype(BF16)
    fkb = _dot(ub, wfk_ref[...]).astype(BF16)
    gate_ref[...] = _sigmoid(_dot(ub, wgate_ref[...])).astype(BF16)

    qvt = _dot_nt(wqvt_ref[...], ub)
    fqt = (qvt[:FOX_W] * (FOX_DH ** -0.5)).astype(BF16)
    vt = qvt[FOX_W:].astype(BF16)
    ones = jnp.ones((FOX_ONES_ROWS, tm), BF16)
    vt_parts = []
    for h in range(FOX_HEADS):
        vt_parts += [vt[h * FOX_DH:(h + 1) * FOX_DH, :], ones]
    vt_ref[...] = jnp.concatenate(vt_parts, axis=0)

    small = jnp.dot(u, wsmall_ref[...], precision=HIGHEST, preferred_element_type=F32)
    ga = small[:, :GLA_LOWRANK]
    alpha_pre = jnp.dot(ga, wup_ref[...], precision=HIGHEST, preferred_element_type=F32) + balpha_ref[...]
    loga_ref[...] = _log_sigmoid(alpha_pre) * (1.0 / GLA_TAU)

    lane = lax.broadcasted_iota(jnp.int32, (tm, LANES), 1)
    ff_valid = (lane >= GLA_LOWRANK) & (lane < GLA_LOWRANK + FOX_HEADS)
    logf = jnp.where(ff_valid, _log_sigmoid(small + bforget_ref[...]), 0.0)
    logf = pltpu.roll(logf, LANES - GLA_LOWRANK, 1)
    r = lax.broadcasted_iota(jnp.int32, (tm, tm), 0)
    c = lax.broadcasted_iota(jnp.int32, (tm, tm), 1)
    tri = (r >= c).astype(F32)
    fcum = jnp.dot(tri, logf, precision=HIGHEST, preferred_element_type=F32) + carry_ref[...]
    carry_ref[...] = fcum[tm - 1:tm, :]

    f_hi = fcum.astype(BF16)
    rem = fcum - f_hi.astype(F32)
    f_mid = rem.astype(BF16)
    f_lo = (rem - f_mid.astype(F32)).astype(BF16)
    f3 = jnp.concatenate([f_hi, f_mid, f_lo], axis=1)
    def aug_slot(idx):
        a = idx % LANES
        return a % FOX_AUG_PER_HEAD, a < 2 * FOX_AUG_PER_HEAD

    slot_q, in_q = aug_slot(lax.broadcasted_iota(jnp.int32, (FOX_W, 1), 0))
    slot_k, in_k = aug_slot(lax.broadcasted_iota(jnp.int32, (1, FOX_W), 1))
    augqt = _dot_nt(pqt_ref[...], f3) + jnp.where(in_q & (slot_q < 3), -1.0, 0.0)
    augk = _dot(f3, pk_ref[...]) + jnp.where(in_k & (slot_k >= 3), 1.0, 0.0)
    for p in range(FOX_HEADS // 2):
        src = slice(p * LANES, (p + 1) * LANES)
        qt_ref[2 * p * LANES:(2 * p + 1) * LANES, :] = fqt[src]
        qt_ref[(2 * p + 1) * LANES:(2 * p + 2) * LANES, :] = augqt[src].astype(BF16)
        kaug_ref[:, 2 * p * LANES:(2 * p + 1) * LANES] = fkb[:, src]
        kaug_ref[:, (2 * p + 1) * LANES:(2 * p + 2) * LANES] = augk[:, src].astype(BF16)

    fq32 = fqt.astype(F32) * (FOX_DH ** 0.5)
    nq2 = jnp.sum((fq32 * fq32).reshape(FOX_HEADS, FOX_DH, tm), axis=1)
    statq_ref[0] = jnp.broadcast_to(jnp.max(nq2, axis=1, keepdims=True), (FOX_HEADS, LANES))
    gi = lax.broadcasted_iota(jnp.int32, (FOX_W, LANES), 0) // FOX_DH
    gj = lax.broadcasted_iota(jnp.int32, (FOX_W, LANES), 1)
    group = (gi == gj).astype(F32)
    fk = fkb.astype(F32)
    nk2 = jnp.max(jnp.dot(fk * fk, group, precision=HIGHEST, preferred_element_type=F32), axis=0, keepdims=True)
    fmax = jnp.max(fcum, axis=0, keepdims=True)
    fmin = jnp.min(fcum, axis=0, keepdims=True)
    stat_ref[0] = jnp.concatenate([nk2, fmax, fmin, jnp.zeros((5, LANES), F32)], axis=0)

    mq = _dot(ub, wmq_ref[...]).astype(BF16)
    scale = MEM_DH ** -0.5
    outs = []
    for h in range(MEM_HEADS):
        sl = slice(h * MEM_DH, (h + 1) * MEM_DH)
        s = _dot_nt(mq[:, sl], mk_ref[:, sl]) * scale
        m = jnp.max(s, axis=-1, keepdims=True)
        p = jnp.exp(s - m)
        l = jnp.sum(p, axis=-1, keepdims=True)
        p = p / l
        outs.append(_dot(p.astype(BF16), mv_ref[:, sl]))
    omem_ref[...] = jnp.concatenate(outs, axis=-1).astype(BF16)


def _aug_placement():
    pq = np.zeros((3 * LANES, FOX_W), np.float32)
    pk = np.zeros((3 * LANES, FOX_W), np.float32)
    for h in range(FOX_HEADS):
        base = (h // 2) * LANES + (h % 2) * FOX_AUG_PER_HEAD
        for c in range(3):
            pq[c * LANES + h, base + 3 + c] = 1.0
            pk[c * LANES + h, base + c] = 1.0
    return jnp.asarray(pq.T, BF16), jnp.asarray(pk, BF16)


def _proj(x, g_mix, w_gla, w_fk, w_qvt, w_mq, w_gate, w_small, w_up, b_alpha, b_forget, mk, mv):
    s = x.shape[0]
    tm = ROW_TILE
    nt = s // tm
    pqt, pk = _aug_placement()
    full = lambda shape: pl.BlockSpec(shape, lambda i: (0,) * len(shape))
    row = lambda w: pl.BlockSpec((tm, w), lambda i: (i, 0))
    col = lambda r: pl.BlockSpec((r, tm), lambda i: (0, i))
    stat = pl.BlockSpec((1, 8, LANES), lambda i: (i, 0, 0))
    vt_rows = FOX_HEADS * (FOX_DH + FOX_ONES_ROWS)
    return pl.pallas_call(
        _proj_kernel,
        grid=(nt,),
        in_specs=[row(D_MODEL), full((1, D_MODEL)), full(w_gla.shape), full(w_fk.shape), full(w_qvt.shape),
                  full(w_mq.shape), full(w_gate.shape), full(w_small.shape), full(w_up.shape),
                  full(b_alpha.shape), full(b_forget.shape), full(mk.shape), full(mv.shape),
                  full(pqt.shape), full(pk.shape)],
        out_specs=[row(w_gla.shape[1]), row(GLA_K), col(2 * FOX_W), row(2 * FOX_W), col(vt_rows), row(MEM_W),
                   row(w_gate.shape[1]), stat, stat],
        out_shape=[jax.ShapeDtypeStruct((s, w_gla.shape[1]), BF16),
                   jax.ShapeDtypeStruct((s, GLA_K), F32),
                   jax.ShapeDtypeStruct((2 * FOX_W, s), BF16),
                   jax.ShapeDtypeStruct((s, 2 * FOX_W), BF16),
                   jax.ShapeDtypeStruct((vt_rows, s), BF16),
                   jax.ShapeDtypeStruct((s, MEM_W), BF16),
                   jax.ShapeDtypeStruct((s, w_gate.shape[1]), BF16),
                   jax.ShapeDtypeStruct((nt, 8, LANES), F32),
                   jax.ShapeDtypeStruct((nt, 8, LANES), F32)],
        scratch_shapes=[pltpu.VMEM((1, LANES), F32)],
        compiler_params=pltpu.CompilerParams(dimension_semantics=("arbitrary",), vmem_limit_bytes=VMEM_LIMIT),
        name="proj",
    )(x, g_mix, w_gla, w_fk, w_qvt, w_mq, w_gate, w_small, w_up, b_alpha, b_forget, mk, mv, pqt, pk)


def _gla_kernel(qkvg_ref, loga_ref, ghead_ref, o_ref, state_ref):
    t = qkvg_ref.shape[0]
    nc = t // CHUNK

    @pl.when(pl.program_id(0) == 0)
    def _():
        state_ref[...] = jnp.zeros_like(state_ref)

    r = lax.broadcasted_iota(jnp.int32, (t, t), 0)
    c = lax.broadcasted_iota(jnp.int32, (t, t), 1)
    same = (r // CHUNK) == (c // CHUNK)
    lower = r >= c
    causal = same & lower
    anti = same & (r < c)

    b = jnp.dot(causal.astype(F32), loga_ref[...], precision=HIGHEST, preferred_element_type=F32)
    b_last = jnp.concatenate(
        [jnp.broadcast_to(b[(ci + 1) * CHUNK - 1:(ci + 1) * CHUNK, :], (CHUNK, GLA_K)) for ci in range(nc)], axis=0)
    e_pos = jnp.exp(b)
    e_neg = jnp.exp(-b)
    q = qkvg_ref[:, 0:GLA_K].astype(F32) * (GLA_DK ** -0.5)
    k = qkvg_ref[:, GLA_K:2 * GLA_K].astype(F32)
    q_pos = (q * e_pos).astype(BF16)
    q_neg = (q * e_neg).astype(BF16)
    k_pos = (k * e_pos).astype(BF16)
    k_neg = (k * e_neg).astype(BF16)
    k_dec = (k * jnp.exp(b_last - b)).astype(BF16)

    lane = lax.broadcasted_iota(jnp.int32, (1, LANES), 1)
    for h in range(GLA_HEADS):
        pair = slice((h // 2) * LANES, (h // 2 + 1) * LANES)
        in_head = (lane // GLA_DK) == (h % 2)
        vsl = slice(2 * GLA_K + h * GLA_DV, 2 * GLA_K + (h + 1) * GLA_DV)
        gsl = slice(2 * GLA_K + GLA_V + h * GLA_DV, 2 * GLA_K + GLA_V + (h + 1) * GLA_DV)
        v = qkvg_ref[:, vsl]
        qp = jnp.where(in_head, q_pos[:, pair], jnp.zeros((), BF16))
        qn = jnp.where(in_head, q_neg[:, pair], jnp.zeros((), BF16))
        a_c = _dot_nt(qp, k_neg[:, pair])
        a_a = _dot_nt(qn, k_pos[:, pair])
        attn = jnp.where(causal, a_c, jnp.where(anti, a_a, 0.0)).astype(BF16)
        o_intra = _dot(attn, v)

        st = state_ref[h]
        inter = []
        for ci in range(nc):
            rows = slice(ci * CHUNK, (ci + 1) * CHUNK)
            inter.append(_dot_nt(qp[rows], st.astype(BF16)))
            kv_t = _dot_tn(v[rows], k_dec[rows, pair])
            dec = e_pos[(ci + 1) * CHUNK - 1:(ci + 1) * CHUNK, pair]
            st = st * dec + jnp.where(in_head, kv_t, 0.0)
        state_ref[h] = st
        o = o_intra + jnp.concatenate(inter, axis=0)

        g = ghead_ref[:, h * GLA_DV:(h + 1) * GLA_DV]
        on = _rms(o, g)
        gg = qkvg_ref[:, gsl].astype(F32)
        o_ref[:, h * GLA_DV:(h + 1) * GLA_DV] = (on * (gg * _sigmoid(gg))).astype(BF16)


def _gla(qkvg, loga, ghead):
    s = qkvg.shape[0]
    t = ROW_TILE
    return pl.pallas_call(
        _gla_kernel,
        grid=(s // t,),
        in_specs=[pl.BlockSpec((t, qkvg.shape[1]), lambda i: (i, 0)),
                  pl.BlockSpec((t, GLA_K), lambda i: (i, 0)),
                  pl.BlockSpec((1, GLA_V), lambda i: (0, 0))],
        out_specs=pl.BlockSpec((t, GLA_V), lambda i: (i, 0)),
        out_shape=jax.ShapeDtypeStruct((s, GLA_V), BF16),
        scratch_shapes=[pltpu.VMEM((GLA_HEADS, GLA_DV, LANES), F32)],
        compiler_params=pltpu.CompilerParams(dimension_semantics=("arbitrary",), vmem_limit_bytes=VMEM_LIMIT),
        name="gla",
    )(qkvg, loga, ghead)


def _fox_kernel(thr_ref, fmin_ref, qt_ref, k_ref, vt_ref, o_ref, sa_ref, sb_ref):
    p = pl.program_id(0)
    i = pl.program_id(1)
    tq = qt_ref.shape[1]
    tk = FOX_TK
    nsub = tq // tk
    hrows = FOX_DH + FOX_ONES_ROWS
    qt = qt_ref[...]
    row = lax.broadcasted_iota(jnp.int32, (2 * LANES, 1), 0)
    krow = lax.broadcasted_iota(jnp.int32, (tk, tq), 0)
    qcol = lax.broadcasted_iota(jnp.int32, (tk, tq), 1)
    q0 = pl.multiple_of(i * tq, tq)
    outs = []

    for hi in range(2):
        h = 2 * p + hi
        aug0 = LANES + hi * FOX_AUG_PER_HEAD
        mine = ((row // FOX_DH) == hi) | ((row >= aug0) & (row < aug0 + FOX_AUG_PER_HEAD))
        qm = jnp.where(mine, qt, jnp.zeros((), BF16))
        vrows = slice(hi * hrows, (hi + 1) * hrows)

        def scores(blk, s_ref, masked=False):
            for u in range(nsub):
                k0 = pl.multiple_of(blk * tq + u * tk, tk)
                s_t = _dot(k_ref[pl.ds(k0, tk), :], qm)
                if masked:
                    s_t = jnp.where(krow + u * tk <= qcol, s_t, NEG_BIG)
                s_ref[u] = s_t

        def softmax_pv(blk, s_ref, m, acc):
            for u in range(nsub):
                s_t = s_ref[u]
                m_new = jnp.maximum(m, jnp.max(s_t, axis=0, keepdims=True))
                alpha = jnp.exp(m - m_new)
                pt = jnp.exp(s_t - m_new).astype(BF16)
                k0 = pl.multiple_of(blk * tq + u * tk, tk)
                acc = alpha * acc + _dot(vt_ref[vrows, pl.ds(k0, tk)], pt)
                m = m_new
            return m, acc

        scores(i, sa_ref, masked=True)
        thr = thr_ref[h, i]
        j_end = lax.while_loop(lambda j: (j >= 0) & (fmin_ref[h, jnp.maximum(j, 0)] <= thr),
                               lambda j: j - 1, i - 1)
        n_more = i - 1 - j_end

        def pair_step(t, carry):
            m, acc = carry
            j = i - 1 - 2 * t
            scores(j, sb_ref)
            m, acc = softmax_pv(j + 1, sa_ref, m, acc)
            scores(j - 1, sa_ref)
            return softmax_pv(j, sb_ref, m, acc)

        m0 = jnp.full((1, tq), NEG_BIG, F32)
        acc0 = jnp.zeros((hrows, tq), F32)
        m, acc = lax.fori_loop(0, n_more // 2, pair_step, (m0, acc0))
        j_last = j_end + 1

        def odd_tail(m, acc):
            scores(j_last, sb_ref)
            m, acc = softmax_pv(j_last + 1, sa_ref, m, acc)
            return softmax_pv(j_last, sb_ref, m, acc)

        def even_tail(m, acc):
            return softmax_pv(j_last, sa_ref, m, acc)

        _, acc = lax.cond((n_more & 1) == 1, odd_tail, even_tail, m, acc)
        outs.append(acc[:FOX_DH] / acc[FOX_DH:FOX_DH + 1])

    o_ref[...] = jnp.concatenate(outs, axis=0).T.astype(BF16)


def _fox(thr, fmin_blk, qt, kaug, vt):
    s = kaug.shape[0]
    tq = FOX_TQ
    nq = s // tq
    npair = FOX_HEADS // 2
    prow = 2 * (FOX_DH + FOX_ONES_ROWS)
    grid_spec = pltpu.PrefetchScalarGridSpec(
        num_scalar_prefetch=2,
        grid=(npair, nq),
        in_specs=[pl.BlockSpec((2 * LANES, tq), lambda p, i, *_: (p, i)),
                  pl.BlockSpec((s, 2 * LANES), lambda p, i, *_: (0, p)),
                  pl.BlockSpec((prow, s), lambda p, i, *_: (p, 0))],
        out_specs=pl.BlockSpec((tq, LANES), lambda p, i, *_: (i, p)),
        scratch_shapes=[pltpu.VMEM((tq // FOX_TK, FOX_TK, tq), F32), pltpu.VMEM((tq // FOX_TK, FOX_TK, tq), F32)],
    )
    return pl.pallas_call(
        _fox_kernel,
        grid_spec=grid_spec,
        out_shape=jax.ShapeDtypeStruct((s, FOX_W), BF16),
        compiler_params=pltpu.CompilerParams(dimension_semantics=("arbitrary", "arbitrary"),
                                             vmem_limit_bytes=VMEM_LIMIT),
        name="fox",
    )(thr, fmin_blk, qt, kaug, vt)


def _tail_kernel(x_ref, ogla_ref, ofox_ref, omem_ref, gate_ref, wg_ref, wf_ref, wm_ref, wo_ref,
                 gffn_ref, w1_ref, w2_ref, gfin_ref, out_ref):
    gate = gate_ref[...]
    merged = (gate[:, 0:D_MODEL].astype(F32) * _dot(ogla_ref[...], wg_ref[...])
              + gate[:, D_MODEL:2 * D_MODEL].astype(F32) * _dot(ofox_ref[...], wf_ref[...])
              + gate[:, 2 * D_MODEL:3 * D_MODEL].astype(F32) * _dot(omem_ref[...], wm_ref[...]))
    h = x_ref[...] + _dot(merged.astype(BF16), wo_ref[...])
    u2 = _rms(h, gffn_ref[...]).astype(BF16)
    acc = jnp.zeros_like(h)
    for cidx in range(D_FF // FF_CHUNK):
        cs = slice(cidx * FF_CHUNK, (cidx + 1) * FF_CHUNK)
        a = jnp.maximum(_dot(u2, w1_ref[:, cs]), 0.0)
        acc = acc + _dot((a * a).astype(BF16), w2_ref[cs, :])
    out_ref[...] = _rms(h + acc, gfin_ref[...])


def _tail(x, ogla, ofox, omem, gate, wg, wf, wm, wo, gffn, w1, w2, gfin):
    s = x.shape[0]
    tm = TAIL_TILE
    full = lambda a: pl.BlockSpec(a.shape, lambda i: (0,) * a.ndim, pipeline_mode=pl.Buffered(1))
    row = lambda w: pl.BlockSpec((tm, w), lambda i: (i, 0))
    return pl.pallas_call(
        _tail_kernel,
        grid=(s // tm,),
        in_specs=[row(D_MODEL), row(GLA_V), row(FOX_W), row(MEM_W), row(3 * D_MODEL),
                  full(wg), full(wf), full(wm), full(wo), full(gffn), full(w1), full(w2), full(gfin)],
        out_specs=row(D_MODEL),
        out_shape=jax.ShapeDtypeStruct((s, D_MODEL), F32),
        compiler_params=pltpu.CompilerParams(dimension_semantics=("arbitrary",), vmem_limit_bytes=VMEM_LIMIT),
        name="tail",
    )(x, ogla, ofox, omem, gate, wg, wf, wm, wo, gffn, w1, w2, gfin)


def kernel(x, mem, g_mix, w_in, w_alpha_up, b_alpha, b_forget, g_gla_head, g_mem, w_mem_kv,
           w_gla_o, w_fox_o, w_mem_o, w_out, g_ffn, w_ff1, w_ff2, g_final):
    assert x.shape[0] == 1 and g_mix.shape[0] == 1, "single batch, single layer"
    s = x.shape[1]
    assert s % ROW_TILE == 0 and s % TAIL_TILE == 0 and s % FOX_TQ == 0
    assert FOX_TQ % FOX_TK == 0 and FOX_TQ % ROW_TILE == 0
    xs = x[0]
    w = w_in[0]

    o_ga = 2 * GLA_K + 2 * GLA_V
    o_fox = o_ga + GLA_LOWRANK
    o_ff = o_fox + 3 * FOX_W
    o_mq = o_ff + FOX_HEADS
    o_gate = o_mq + MEM_W
    w_gla = w[:, :o_ga].astype(BF16)
    w_fk = w[:, o_fox + FOX_W:o_fox + 2 * FOX_W].astype(BF16)
    w_qvt = jnp.concatenate([w[:, o_fox:o_fox + FOX_W], w[:, o_fox + 2 * FOX_W:o_ff]], axis=1).T.astype(BF16)
    w_mq = w[:, o_mq:o_gate].astype(BF16)
    w_gate = w[:, o_gate:].astype(BF16)
    w_small = jnp.concatenate(
        [w[:, o_ga:o_fox], w[:, o_ff:o_mq], jnp.zeros((D_MODEL, LANES - GLA_LOWRANK - FOX_HEADS), F32)], axis=1)
    b_f = jnp.zeros((1, LANES), F32).at[0, GLA_LOWRANK:GLA_LOWRANK + FOX_HEADS].set(b_forget[0])

    mk, mv = _memkv(mem[0], g_mem, w_mem_kv[0].astype(BF16))
    gla_qkvg, loga, qt, kaug, vt, omem, gate, stats, statq = _proj(
        xs, g_mix, w_gla, w_fk, w_qvt, w_mq, w_gate, w_small, w_alpha_up[0], b_alpha, b_f, mk, mv)

    ogla = _gla(gla_qkvg, loga, g_gla_head.reshape(1, GLA_V))

    per_blk = lambda a: a.reshape(s // FOX_TQ, FOX_TQ // ROW_TILE, FOX_HEADS)
    qn = jnp.sqrt(jnp.max(per_blk(statq[:, :, 0]), axis=1))
    kn = jnp.sqrt(jnp.max(stats[:, 0, :FOX_HEADS], axis=0))
    fmax = jnp.max(per_blk(stats[:, 1, :FOX_HEADS]), axis=1)
    fmin = jnp.min(per_blk(stats[:, 2, :FOX_HEADS]), axis=1)
    thr = (2.0 * 1.02 * (FOX_DH ** -0.5)) * qn * kn[None, :] + fmax + PRUNE_LOGIT_GAP
    ofox = _fox(thr.T, fmin.T, qt, kaug, vt)

    out = _tail(xs, ogla, ofox, omem, gate, w_gla_o[0].astype(BF16), w_fox_o[0].astype(BF16),
                w_mem_o[0].astype(BF16), w_out[0].astype(BF16), g_ffn, w_ff1[0].astype(BF16),
                w_ff2[0].astype(BF16), g_final.reshape(1, D_MODEL))
    return out[None]
```

```python
import functools

import jax
import jax.numpy as jnp
import numpy as np
from jax import lax
from jax.experimental import pallas as pl
from jax.experimental.pallas import tpu as pltpu

D_MODEL = 1024
CHUNK = 64
EPS = 1e-6
GLA_HEADS = 4
GLA_DK = 64
GLA_DV = 128
GLA_LOWRANK = 16
GLA_TAU = 16.0
FOX_HEADS = 8
FOX_DH = 64
MEM_HEADS = 4
MEM_DH = 128
D_FF = 4 * D_MODEL
GLA_K = GLA_HEADS * GLA_DK
GLA_V = GLA_HEADS * GLA_DV
FOX_W = FOX_HEADS * FOX_DH
MEM_W = MEM_HEADS * MEM_DH

LANES = 128
FOX_ONES_ROWS = 16
FOX_AUG_PER_HEAD = 6
ROW_TILE = 256
FOX_TQ = 512
FOX_TK = 256
TAIL_TILE = 256
FF_CHUNK = 1024
VMEM_LIMIT = 56 * 1024 * 1024
NEG_BIG = -1e30
PRUNE_LOGIT_GAP = 104.0

F32 = jnp.float32
BF16 = jnp.bfloat16


def _rms(xf, g):
    r = lax.rsqrt(jnp.mean(xf * xf, axis=-1, keepdims=True) + EPS)
    return (xf * r) * g


def _log_sigmoid(x):
    return jnp.minimum(x, 0.0) - jnp.log1p(jnp.exp(-jnp.abs(x)))


def _sigmoid(x):
    return 1.0 / (1.0 + jnp.exp(-x))


def _dot(a, b):
    return jnp.dot(a, b, preferred_element_type=F32)


def _dot_nt(a, b):
    return lax.dot_general(a, b, (((1,), (1,)), ((), ())), preferred_element_type=F32)


def _dot_tn(a, b):
    return lax.dot_general(a, b, (((0,), (0,)), ((), ())), preferred_element_type=F32)


def _memkv_kernel(mem_ref, g_ref, w_ref, mk_ref, mv_ref):
    mn = _rms(mem_ref[...], g_ref[...]).astype(BF16)
    kv = _dot(mn, w_ref[...])
    mk_ref[...] = kv[:, :MEM_W].astype(BF16)
    mv_ref[...] = kv[:, MEM_W:].astype(BF16)


def _memkv(mem, g_mem, w_mem_kv):
    m = mem.shape[0]
    return pl.pallas_call(
        _memkv_kernel,
        out_shape=(jax.ShapeDtypeStruct((m, MEM_W), BF16), jax.ShapeDtypeStruct((m, MEM_W), BF16)),
        name="memkv",
    )(mem, g_mem, w_mem_kv)


def _proj_kernel(x_ref, gmix_ref, wgla_ref, wfk_ref, wqvt_ref, wmq_ref, wgate_ref, wsmall_ref, wup_ref,
                 balpha_ref, bforget_ref, mk_ref, mv_ref, pqt_ref, pk_ref,
                 gla_ref, loga_ref, qt_ref, kaug_ref, vt_ref, omem_ref, gate_ref, stat_ref, statq_ref,
                 carry_ref):
    tm = x_ref.shape[0]

    @pl.when(pl.program_id(0) == 0)
    def _():
        carry_ref[...] = jnp.zeros_like(carry_ref)

    u = _rms(x_ref[...], gmix_ref[...])
    ub = u.astype(BF16)

    gla_ref[...] = _dot(ub, wgla_ref[...]).astype(BF16)
    fkb = _dot(ub, wfk_ref[...]).astype(BF16)
    gate_ref[...] = _sigmoid(_dot(ub, wgate_ref[...])).astype(BF16)

    qvt = _dot_nt(wqvt_ref[...], ub)
    fqt = (qvt[:FOX_W] * (FOX_DH ** -0.5)).astype(BF16)
    vt = qvt[FOX_W:].astype(BF16)
    ones = jnp.ones((FOX_ONES_ROWS, tm), BF16)
    vt_parts = []
    for h in range(FOX_HEADS):
        vt_parts += [vt[h * FOX_DH:(h + 1) * FOX_DH, :], ones]
    vt_ref[...] = jnp.concatenate(vt_parts, axis=0)

    u_lo = (u - ub.astype(F32)).astype(BF16)
    parts = _dot(jnp.concatenate([ub, u_lo], axis=0), wsmall_ref[...])
    small = (parts[:tm, :LANES] + parts[:tm, LANES:]) + (parts[tm:, :LANES] + parts[tm:, LANES:])
    ga = small[:, :GLA_LOWRANK]
    ga_hi = ga.astype(BF16)
    ga_lo = (ga - ga_hi.astype(F32)).astype(BF16)
    up = _dot(jnp.concatenate([ga_hi, ga_lo], axis=0), wup_ref[...])
    alpha_pre = ((up[:tm, :GLA_K] + up[:tm, GLA_K:]) + (up[tm:, :GLA_K] + up[tm:, GLA_K:])) + balpha_ref[...]
    loga_ref[...] = _log_sigmoid(alpha_pre) * (1.0 / GLA_TAU)

    lane = lax.broadcasted_iota(jnp.int32, (tm, LANES), 1)
    ff_valid = (lane >= GLA_LOWRANK) & (lane < GLA_LOWRANK + FOX_HEADS)
    logf = jnp.where(ff_valid, _log_sigmoid(small + bforget_ref[...]), 0.0)
    logf = pltpu.roll(logf, LANES - GLA_LOWRANK, 1)
    r = lax.broadcasted_iota(jnp.int32, (tm, tm), 0)
    c = lax.broadcasted_iota(jnp.int32, (tm, tm), 1)
    tri = jnp.where(r >= c, 1.0, 0.0).astype(BF16)
    lf_hi = logf.astype(BF16)
    lf_r = logf - lf_hi.astype(F32)
    lf_mid = lf_r.astype(BF16)
    lf_lo = (lf_r - lf_mid.astype(F32)).astype(BF16)
    c3 = _dot(tri, jnp.concatenate([lf_hi, lf_mid, lf_lo], axis=1))
    fcum = (c3[:, :LANES] + c3[:, LANES:2 * LANES] + c3[:, 2 * LANES:]) + carry_ref[...]
    carry_ref[...] = fcum[tm - 1:tm, :]

    f_hi = fcum.astype(BF16)
    rem = fcum - f_hi.astype(F32)
    f_mid = rem.astype(BF16)
    f_lo = (rem - f_mid.astype(F32)).astype(BF16)
    f3 = jnp.concatenate([f_hi, f_mid, f_lo], axis=1)
    def aug_slot(idx):
        a = idx % LANES
        return a % FOX_AUG_PER_HEAD, a < 2 * FOX_AUG_PER_HEAD

    slot_q, in_q = aug_slot(lax.broadcasted_iota(jnp.int32, (FOX_W, 1), 0))
    slot_k, in_k = aug_slot(lax.broadcasted_iota(jnp.int32, (1, FOX_W), 1))
    augqt = _dot_nt(pqt_ref[...], f3) + jnp.where(in_q & (slot_q < 3), -1.0, 0.0)
    augk = _dot(f3, pk_ref[...]) + jnp.where(in_k & (slot_k >= 3), 1.0, 0.0)
    for p in range(FOX_HEADS // 2):
        src = slice(p * LANES, (p + 1) * LANES)
        qt_ref[2 * p * LANES:(2 * p + 1) * LANES, :] = fqt[src]
        qt_ref[(2 * p + 1) * LANES:(2 * p + 2) * LANES, :] = augqt[src].astype(BF16)
        kaug_ref[:, 2 * p * LANES:(2 * p + 1) * LANES] = fkb[:, src]
        kaug_ref[:, (2 * p + 1) * LANES:(2 * p + 2) * LANES] = augk[:, src].astype(BF16)

    fq32 = fqt.astype(F32) * (FOX_DH ** 0.5)
    nq2 = jnp.sum((fq32 * fq32).reshape(FOX_HEADS, FOX_DH, tm), axis=1)
    statq_ref[0] = jnp.broadcast_to(jnp.max(nq2, axis=1, keepdims=True), (FOX_HEADS, LANES))
    gi = lax.broadcasted_iota(jnp.int32, (FOX_W, LANES), 0) // FOX_DH
    gj = lax.broadcasted_iota(jnp.int32, (FOX_W, LANES), 1)
    group = jnp.where(gi == gj, 1.0, 0.0).astype(BF16)
    fk = fkb.astype(F32)
    nk2 = jnp.max(_dot((fk * fk).astype(BF16), group), axis=0, keepdims=True)
    fmax = jnp.max(fcum, axis=0, keepdims=True)
    fmin = jnp.min(fcum, axis=0, keepdims=True)
    stat_ref[0] = jnp.concatenate([nk2, fmax, fmin, jnp.zeros((5, LANES), F32)], axis=0)

    mq = _dot(ub, wmq_ref[...]).astype(BF16)
    scale = MEM_DH ** -0.5
    outs = []
    for h in range(MEM_HEADS):
        sl = slice(h * MEM_DH, (h + 1) * MEM_DH)
        s = _dot_nt(mq[:, sl], mk_ref[:, sl]) * scale
        m = jnp.max(s, axis=-1, keepdims=True)
        p = jnp.exp(s - m)
        l = jnp.sum(p, axis=-1, keepdims=True)
        p = p / l
        outs.append(_dot(p.astype(BF16), mv_ref[:, sl]))
    omem_ref[...] = jnp.concatenate(outs, axis=-1).astype(BF16)


def _aug_placement():
    pq = np.zeros((3 * LANES, FOX_W), np.float32)
    pk = np.zeros((3 * LANES, FOX_W), np.float32)
    for h in range(FOX_HEADS):
        base = (h // 2) * LANES + (h % 2) * FOX_AUG_PER_HEAD
        for c in range(3):
            pq[c * LANES + h, base + 3 + c] = 1.0
            pk[c * LANES + h, base + c] = 1.0
    return jnp.asarray(pq.T, BF16), jnp.asarray(pk, BF16)


def _proj(x, g_mix, w_gla, w_fk, w_qvt, w_mq, w_gate, w_small, w_up, b_alpha, b_forget, mk, mv):
    s = x.shape[0]
    tm = ROW_TILE
    nt = s // tm
    pqt, pk = _aug_placement()
    full = lambda shape: pl.BlockSpec(shape, lambda i: (0,) * len(shape))
    row = lambda w: pl.BlockSpec((tm, w), lambda i: (i, 0))
    col = lambda r: pl.BlockSpec((r, tm), lambda i: (0, i))
    stat = pl.BlockSpec((1, 8, LANES), lambda i: (i, 0, 0))
    vt_rows = FOX_HEADS * (FOX_DH + FOX_ONES_ROWS)
    return pl.pallas_call(
        _proj_kernel,
        grid=(nt,),
        in_specs=[row(D_MODEL), full((1, D_MODEL)), full(w_gla.shape), full(w_fk.shape), full(w_qvt.shape),
                  full(w_mq.shape), full(w_gate.shape), full(w_small.shape), full(w_up.shape),
                  full(b_alpha.shape), full(b_forget.shape), full(mk.shape), full(mv.shape),
                  full(pqt.shape), full(pk.shape)],
        out_specs=[row(w_gla.shape[1]), row(GLA_K), col(2 * FOX_W), row(2 * FOX_W), col(vt_rows), row(MEM_W),
                   row(w_gate.shape[1]), stat, stat],
        out_shape=[jax.ShapeDtypeStruct((s, w_gla.shape[1]), BF16),
                   jax.ShapeDtypeStruct((s, GLA_K), F32),
                   jax.ShapeDtypeStruct((2 * FOX_W, s), BF16),
                   jax.ShapeDtypeStruct((s, 2 * FOX_W), BF16),
                   jax.ShapeDtypeStruct((vt_rows, s), BF16),
                   jax.ShapeDtypeStruct((s, MEM_W), BF16),
                   jax.ShapeDtypeStruct((s, w_gate.shape[1]), BF16),
                   jax.ShapeDtypeStruct((nt, 8, LANES), F32),
                   jax.ShapeDtypeStruct((nt, 8, LANES), F32)],
        scratch_shapes=[pltpu.VMEM((1, LANES), F32)],
        compiler_params=pltpu.CompilerParams(dimension_semantics=("arbitrary",), vmem_limit_bytes=VMEM_LIMIT),
        name="proj",
    )(x, g_mix, w_gla, w_fk, w_qvt, w_mq, w_gate, w_small, w_up, b_alpha, b_forget, mk, mv, pqt, pk)


def _gla_kernel(qkvg_ref, loga_ref, ghead_ref, o_ref, state_ref):
    t = qkvg_ref.shape[0]
    nc = t // CHUNK

    @pl.when(pl.program_id(0) == 0)
    def _():
        state_ref[...] = jnp.zeros_like(state_ref)

    r = lax.broadcasted_iota(jnp.int32, (t, t), 0)
    c = lax.broadcasted_iota(jnp.int32, (t, t), 1)
    same = (r // CHUNK) == (c // CHUNK)
    lower = r >= c
    causal = same & lower
    anti = same & (r < c)

    la = loga_ref[...]
    la_hi = la.astype(BF16)
    la_r = la - la_hi.astype(F32)
    la_mid = la_r.astype(BF16)
    la_lo = (la_r - la_mid.astype(F32)).astype(BF16)
    b3 = _dot(jnp.where(causal, 1.0, 0.0).astype(BF16), jnp.concatenate([la_hi, la_mid, la_lo], axis=1))
    b = b3[:, :GLA_K] + b3[:, GLA_K:2 * GLA_K] + b3[:, 2 * GLA_K:]
    b_last = jnp.concatenate(
        [jnp.broadcast_to(b[(ci + 1) * CHUNK - 1:(ci + 1) * CHUNK, :], (CHUNK, GLA_K)) for ci in range(nc)], axis=0)
    e_pos = jnp.exp(b)
    e_neg = jnp.exp(-b)
    q = qkvg_ref[:, 0:GLA_K].astype(F32) * (GLA_DK ** -0.5)
    k = qkvg_ref[:, GLA_K:2 * GLA_K].astype(F32)
    q_pos = (q * e_pos).astype(BF16)
    q_neg = (q * e_neg).astype(BF16)
    k_pos = (k * e_pos).astype(BF16)
    k_neg = (k * e_neg).astype(BF16)
    k_dec = (k * jnp.exp(b_last - b)).astype(BF16)

    lane = lax.broadcasted_iota(jnp.int32, (1, LANES), 1)
    for h in range(GLA_HEADS):
        pair = slice((h // 2) * LANES, (h // 2 + 1) * LANES)
        in_head = (lane // GLA_DK) == (h % 2)
        vsl = slice(2 * GLA_K + h * GLA_DV, 2 * GLA_K + (h + 1) * GLA_DV)
        gsl = slice(2 * GLA_K + GLA_V + h * GLA_DV, 2 * GLA_K + GLA_V + (h + 1) * GLA_DV)
        v = qkvg_ref[:, vsl]
        qp = jnp.where(in_head, q_pos[:, pair], jnp.zeros((), BF16))
        qn = jnp.where(in_head, q_neg[:, pair], jnp.zeros((), BF16))
        a_c = _dot_nt(qp, k_neg[:, pair])
        a_a = _dot_nt(qn, k_pos[:, pair])
        attn = jnp.where(causal, a_c, jnp.where(anti, a_a, 0.0)).astype(BF16)
        o_intra = _dot(attn, v)

        st = state_ref[h]
        inter = []
        for ci in range(nc):
            rows = slice(ci * CHUNK, (ci + 1) * CHUNK)
            inter.append(_dot_nt(qp[rows], st.astype(BF16)))
            kv_t = _dot_tn(v[rows], k_dec[rows, pair])
            dec = e_pos[(ci + 1) * CHUNK - 1:(ci + 1) * CHUNK, pair]
            st = st * dec + jnp.where(in_head, kv_t, 0.0)
        state_ref[h] = st
        o = o_intra + jnp.concatenate(inter, axis=0)

        g = ghead_ref[:, h * GLA_DV:(h + 1) * GLA_DV]
        on = _rms(o, g)
        gg = qkvg_ref[:, gsl].astype(F32)
        o_ref[:, h * GLA_DV:(h + 1) * GLA_DV] = (on * (gg * _sigmoid(gg))).astype(BF16)


def _gla(qkvg, loga, ghead):
    s = qkvg.shape[0]
    t = ROW_TILE
    return pl.pallas_call(
        _gla_kernel,
        grid=(s // t,),
        in_specs=[pl.BlockSpec((t, qkvg.shape[1]), lambda i: (i, 0)),
                  pl.BlockSpec((t, GLA_K), lambda i: (i, 0)),
                  pl.BlockSpec((1, GLA_V), lambda i: (0, 0))],
        out_specs=pl.BlockSpec((t, GLA_V), lambda i: (i, 0)),
        out_shape=jax.ShapeDtypeStruct((s, GLA_V), BF16),
        scratch_shapes=[pltpu.VMEM((GLA_HEADS, GLA_DV, LANES), F32)],
        compiler_params=pltpu.CompilerParams(dimension_semantics=("arbitrary",), vmem_limit_bytes=VMEM_LIMIT),
        name="gla",
    )(qkvg, loga, ghead)


def _fox_kernel(thr_ref, fmin_ref, qt_ref, k_ref, vt_ref, o_ref, sa_ref, sb_ref):
    p = pl.program_id(0)
    i = pl.program_id(1)
    tq = qt_ref.shape[1]
    tk = FOX_TK
    nsub = tq // tk
    hrows = FOX_DH + FOX_ONES_ROWS
    qt = qt_ref[...]
    row = lax.broadcasted_iota(jnp.int32, (2 * LANES, 1), 0)
    krow = lax.broadcasted_iota(jnp.int32, (tk, tq), 0)
    qcol = lax.broadcasted_iota(jnp.int32, (tk, tq), 1)
    q0 = pl.multiple_of(i * tq, tq)
    outs = []

    for hi in range(2):
        h = 2 * p + hi
        aug0 = LANES + hi * FOX_AUG_PER_HEAD
        mine = ((row // FOX_DH) == hi) | ((row >= aug0) & (row < aug0 + FOX_AUG_PER_HEAD))
        qm = jnp.where(mine, qt, jnp.zeros((), BF16))
        vrows = slice(hi * hrows, (hi + 1) * hrows)

        def scores(blk, s_ref, masked=False):
            for u in range(nsub):
                k0 = pl.multiple_of(blk * tq + u * tk, tk)
                s_t = _dot(k_ref[pl.ds(k0, tk), :], qm)
                if masked:
                    s_t = jnp.where(krow + u * tk <= qcol, s_t, NEG_BIG)
                s_ref[u] = s_t

        def softmax_pv(blk, s_ref, m, acc):
            for u in range(nsub):
                s_t = s_ref[u]
                m_new = jnp.maximum(m, jnp.max(s_t, axis=0, keepdims=True))
                alpha = jnp.exp(m - m_new)
                pt = jnp.exp(s_t - m_new).astype(BF16)
                k0 = pl.multiple_of(blk * tq + u * tk, tk)
                acc = alpha * acc + _dot(vt_ref[vrows, pl.ds(k0, tk)], pt)
                m = m_new
            return m, acc

        scores(i, sa_ref, masked=True)
        thr = thr_ref[h, i]
        j_end = lax.while_loop(lambda j: (j >= 0) & (fmin_ref[h, jnp.maximum(j, 0)] <= thr),
                               lambda j: j - 1, i - 1)
        n_more = i - 1 - j_end

        def pair_step(t, carry):
            m, acc = carry
            j = i - 1 - 2 * t
            scores(j, sb_ref)
            m, acc = softmax_pv(j + 1, sa_ref, m, acc)
            scores(j - 1, sa_ref)
            return softmax_pv(j, sb_ref, m, acc)

        m0 = jnp.full((1, tq), NEG_BIG, F32)
        acc0 = jnp.zeros((hrows, tq), F32)
        m, acc = lax.fori_loop(0, n_more // 2, pair_step, (m0, acc0))
        j_last = j_end + 1

        def odd_tail(m, acc):
            scores(j_last, sb_ref)
            m, acc = softmax_pv(j_last + 1, sa_ref, m, acc)
            return softmax_pv(j_last, sb_ref, m, acc)

        def even_tail(m, acc):
            return softmax_pv(j_last, sa_ref, m, acc)

        _, acc = lax.cond((n_more & 1) == 1, odd_tail, even_tail, m, acc)
        outs.append(acc[:FOX_DH] / acc[FOX_DH:FOX_DH + 1])

    o_ref[...] = jnp.concatenate(outs, axis=0).T.astype(BF16)


def _fox(thr, fmin_blk, qt, kaug, vt):
    s = kaug.shape[0]
    tq = FOX_TQ
    nq = s // tq
    npair = FOX_HEADS // 2
    prow = 2 * (FOX_DH + FOX_ONES_ROWS)
    grid_spec = pltpu.PrefetchScalarGridSpec(
        num_scalar_prefetch=2,
        grid=(npair, nq),
        in_specs=[pl.BlockSpec((2 * LANES, tq), lambda p, i, *_: (p, i)),
                  pl.BlockSpec((s, 2 * LANES), lambda p, i, *_: (0, p)),
                  pl.BlockSpec((prow, s), lambda p, i, *_: (p, 0))],
        out_specs=pl.BlockSpec((tq, LANES), lambda p, i, *_: (i, p)),
        scratch_shapes=[pltpu.VMEM((tq // FOX_TK, FOX_TK, tq), F32), pltpu.VMEM((tq // FOX_TK, FOX_TK, tq), F32)],
    )
    return pl.pallas_call(
        _fox_kernel,
        grid_spec=grid_spec,
        out_shape=jax.ShapeDtypeStruct((s, FOX_W), BF16),
        compiler_params=pltpu.CompilerParams(dimension_semantics=("arbitrary", "arbitrary"),
                                             vmem_limit_bytes=VMEM_LIMIT),
        name="fox",
    )(thr, fmin_blk, qt, kaug, vt)


def _tail_kernel(x_ref, ogla_ref, ofox_ref, omem_ref, gate_ref, wg_ref, wf_ref, wm_ref, wo_ref,
                 gffn_ref, w1_ref, w2_ref, gfin_ref, out_ref):
    gate = gate_ref[...]
    merged = (gate[:, 0:D_MODEL].astype(F32) * _dot(ogla_ref[...], wg_ref[...])
              + gate[:, D_MODEL:2 * D_MODEL].astype(F32) * _dot(ofox_ref[...], wf_ref[...])
              + gate[:, 2 * D_MODEL:3 * D_MODEL].astype(F32) * _dot(omem_ref[...], wm_ref[...]))
    h = x_ref[...] + _dot(merged.astype(BF16), wo_ref[...])
    u2 = _rms(h, gffn_ref[...]).astype(BF16)
    acc = jnp.zeros_like(h)
    for cidx in range(D_FF // FF_CHUNK):
        cs = slice(cidx * FF_CHUNK, (cidx + 1) * FF_CHUNK)
        a = jnp.maximum(_dot(u2, w1_ref[:, cs]), 0.0)
        acc = acc + _dot((a * a).astype(BF16), w2_ref[cs, :])
    out_ref[...] = _rms(h + acc, gfin_ref[...])


def _tail(x, ogla, ofox, omem, gate, wg, wf, wm, wo, gffn, w1, w2, gfin):
    s = x.shape[0]
    tm = TAIL_TILE
    full = lambda a: pl.BlockSpec(a.shape, lambda i: (0,) * a.ndim, pipeline_mode=pl.Buffered(1))
    row = lambda w: pl.BlockSpec((tm, w), lambda i: (i, 0))
    return pl.pallas_call(
        _tail_kernel,
        grid=(s // tm,),
        in_specs=[row(D_MODEL), row(GLA_V), row(FOX_W), row(MEM_W), row(3 * D_MODEL),
                  full(wg), full(wf), full(wm), full(wo), full(gffn), full(w1), full(w2), full(gfin)],
        out_specs=row(D_MODEL),
        out_shape=jax.ShapeDtypeStruct((s, D_MODEL), F32),
        compiler_params=pltpu.CompilerParams(dimension_semantics=("arbitrary",), vmem_limit_bytes=VMEM_LIMIT),
        name="tail",
    )(x, ogla, ofox, omem, gate, wg, wf, wm, wo, gffn, w1, w2, gfin)


def kernel(x, mem, g_mix, w_in, w_alpha_up, b_alpha, b_forget, g_gla_head, g_mem, w_mem_kv,
           w_gla_o, w_fox_o, w_mem_o, w_out, g_ffn, w_ff1, w_ff2, g_final):
    assert x.shape[0] == 1 and g_mix.shape[0] == 1, "single batch, single layer"
    s = x.shape[1]
    assert s % ROW_TILE == 0 and s % TAIL_TILE == 0 and s % FOX_TQ == 0
    assert FOX_TQ % FOX_TK == 0 and FOX_TQ % ROW_TILE == 0
    xs = x[0]
    w = w_in[0]

    o_ga = 2 * GLA_K + 2 * GLA_V
    o_fox = o_ga + GLA_LOWRANK
    o_ff = o_fox + 3 * FOX_W
    o_mq = o_ff + FOX_HEADS
    o_gate = o_mq + MEM_W
    w_gla = w[:, :o_ga].astype(BF16)
    w_fk = w[:, o_fox + FOX_W:o_fox + 2 * FOX_W].astype(BF16)
    w_qvt = jnp.concatenate([w[:, o_fox:o_fox + FOX_W], w[:, o_fox + 2 * FOX_W:o_ff]], axis=1).T.astype(BF16)
    w_mq = w[:, o_mq:o_gate].astype(BF16)
    w_gate = w[:, o_gate:].astype(BF16)
    w_small = jnp.concatenate(
        [w[:, o_ga:o_fox], w[:, o_ff:o_mq], jnp.zeros((D_MODEL, LANES - GLA_LOWRANK - FOX_HEADS), F32)], axis=1)
    w_small_hi = w_small.astype(BF16)
    w_small = jnp.concatenate([w_small_hi, (w_small - w_small_hi.astype(F32)).astype(BF16)], axis=1)
    w_up_hi = w_alpha_up[0].astype(BF16)
    w_up = jnp.concatenate([w_up_hi, (w_alpha_up[0] - w_up_hi.astype(F32)).astype(BF16)], axis=1)
    b_f = jnp.zeros((1, LANES), F32).at[0, GLA_LOWRANK:GLA_LOWRANK + FOX_HEADS].set(b_forget[0])

    mk, mv = _memkv(mem[0], g_mem, w_mem_kv[0].astype(BF16))
    gla_qkvg, loga, qt, kaug, vt, omem, gate, stats, statq = _proj(
        xs, g_mix, w_gla, w_fk, w_qvt, w_mq, w_gate, w_small, w_up, b_alpha, b_f, mk, mv)

    ogla = _gla(gla_qkvg, loga, g_gla_head.reshape(1, GLA_V))

    per_blk = lambda a: a.reshape(s // FOX_TQ, FOX_TQ // ROW_TILE, FOX_HEADS)
    qn = jnp.sqrt(jnp.max(per_blk(statq[:, :, 0]), axis=1))
    kn = jnp.sqrt(jnp.max(stats[:, 0, :FOX_HEADS], axis=0))
    fmax = jnp.max(per_blk(stats[:, 1, :FOX_HEADS]), axis=1)
    fmin = jnp.min(per_blk(stats[:, 2, :FOX_HEADS]), axis=1)
    thr = (2.0 * 1.02 * (FOX_DH ** -0.5)) * qn * kn[None, :] + fmax + PRUNE_LOGIT_GAP
    ofox = _fox(thr.T, fmin.T, qt, kaug, vt)

    out = _tail(xs, ogla, ofox, omem, gate, w_gla_o[0].astype(BF16), w_fox_o[0].astype(BF16),
                w_mem_o[0].astype(BF16), w_out[0].astype(BF16), g_ffn, w_ff1[0].astype(BF16),
                w_ff2[0].astype(BF16), g_final.reshape(1, D_MODEL))
    return out[None]
```

```python
import functools

import jax
import jax.numpy as jnp
import numpy as np
from jax import lax
from jax.experimental import pallas as pl
from jax.experimental.pallas import tpu as pltpu

D_MODEL = 1024
CHUNK = 64
EPS = 1e-6
GLA_HEADS = 4
GLA_DK = 64
GLA_DV = 128
GLA_LOWRANK = 16
GLA_TAU = 16.0
FOX_HEADS = 8
FOX_DH = 64
MEM_HEADS = 4
MEM_DH = 128
D_FF = 4 * D_MODEL
GLA_K = GLA_HEADS * GLA_DK
GLA_V = GLA_HEADS * GLA_DV
FOX_W = FOX_HEADS * FOX_DH
MEM_W = MEM_HEADS * MEM_DH

LANES = 128
FOX_ONES_ROWS = 16
FOX_AUG_PER_HEAD = 6
ROW_TILE = 256
FOX_TQ = 512
FOX_TK = 256
TAIL_TILE = 256
FF_CHUNK = 1024
VMEM_LIMIT = 56 * 1024 * 1024
NEG_BIG = -1e30
PRUNE_LOGIT_GAP = 104.0

F32 = jnp.float32
BF16 = jnp.bfloat16


def _rms(xf, g):
    r = lax.rsqrt(jnp.mean(xf * xf, axis=-1, keepdims=True) + EPS)
    return (xf * r) * g


def _log_sigmoid(x):
    return jnp.minimum(x, 0.0) - jnp.log1p(jnp.exp(-jnp.abs(x)))


def _sigmoid(x):
    return 1.0 / (1.0 + jnp.exp(-x))


def _dot(a, b):
    return jnp.dot(a, b, preferred_element_type=F32)


def _dot_nt(a, b):
    return lax.dot_general(a, b, (((1,), (1,)), ((), ())), preferred_element_type=F32)


def _dot_tn(a, b):
    return lax.dot_general(a, b, (((0,), (0,)), ((), ())), preferred_element_type=F32)


def _memkv_kernel(mem_ref, g_ref, w_ref, mk_ref, mv_ref):
    mn = _rms(mem_ref[...], g_ref[...]).astype(BF16)
    kv = _dot(mn, w_ref[...])
    mk_ref[...] = kv[:, :MEM_W].astype(BF16)
    mv_ref[...] = kv[:, MEM_W:].astype(BF16)


def _memkv(mem, g_mem, w_mem_kv):
    m = mem.shape[0]
    return pl.pallas_call(
        _memkv_kernel,
        out_shape=(jax.ShapeDtypeStruct((m, MEM_W), BF16), jax.ShapeDtypeStruct((m, MEM_W), BF16)),
        name="memkv",
    )(mem, g_mem, w_mem_kv)


def _proj_kernel(x_ref, gmix_ref, wgla_ref, wfk_ref, wqvt_ref, wmq_ref, wgate_ref, wsmall_ref, wup_ref,
                 balpha_ref, bforget_ref, mk_ref, mv_ref, pqt_ref, pk_ref,
                 gla_ref, loga_ref, qt_ref, kaug_ref, vt_ref, omem_ref, gate_ref, stat_ref, statq_ref,
                 carry_ref):
    tm = x_ref.shape[0]

    @pl.when(pl.program_id(0) == 0)
    def _():
        carry_ref[...] = jnp.zeros_like(carry_ref)

    u = _rms(x_ref[...], gmix_ref[...])
    ub = u.astype(BF16)

    gla_ref[...] = _dot(ub, wgla_ref[...]).astype(BF16)
    fkb = _dot(ub, wfk_ref[...]).astype(BF16)
    gate_ref[...] = _sigmoid(_dot(ub, wgate_ref[...])).astype(BF16)

    qvt = _dot_nt(wqvt_ref[...], ub)
    fqt = (qvt[:FOX_W] * (FOX_DH ** -0.5)).astype(BF16)
    vt = qvt[FOX_W:].astype(BF16)
    ones = jnp.ones((FOX_ONES_ROWS, tm), BF16)
    vt_parts = []
    for h in range(FOX_HEADS):
        vt_parts += [vt[h * FOX_DH:(h + 1) * FOX_DH, :], ones]
    vt_ref[...] = jnp.concatenate(vt_parts, axis=0)

    u_lo = (u - ub.astype(F32)).astype(BF16)
    parts = _dot(jnp.concatenate([ub, u_lo], axis=0), wsmall_ref[...])
    small = (parts[:tm, :LANES] + parts[:tm, LANES:]) + (parts[tm:, :LANES] + parts[tm:, LANES:])
    ga = small[:, :GLA_LOWRANK]
    ga_hi = ga.astype(BF16)
    ga_lo = (ga - ga_hi.astype(F32)).astype(BF16)
    up = _dot(jnp.concatenate([ga_hi, ga_lo], axis=0), wup_ref[...])
    alpha_pre = ((up[:tm, :GLA_K] + up[:tm, GLA_K:]) + (up[tm:, :GLA_K] + up[tm:, GLA_K:])) + balpha_ref[...]
    loga_ref[...] = _log_sigmoid(alpha_pre) * (1.0 / GLA_TAU)

    lane = lax.broadcasted_iota(jnp.int32, (tm, LANES), 1)
    ff_valid = (lane >= GLA_LOWRANK) & (lane < GLA_LOWRANK + FOX_HEADS)
    logf = jnp.where(ff_valid, _log_sigmoid(small + bforget_ref[...]), 0.0)
    logf = pltpu.roll(logf, LANES - GLA_LOWRANK, 1)
    r = lax.broadcasted_iota(jnp.int32, (tm, tm), 0)
    c = lax.broadcasted_iota(jnp.int32, (tm, tm), 1)
    tri = jnp.where(r >= c, 1.0, 0.0).astype(BF16)
    lf_hi = logf.astype(BF16)
    lf_r = logf - lf_hi.astype(F32)
    lf_mid = lf_r.astype(BF16)
    lf_lo = (lf_r - lf_mid.astype(F32)).astype(BF16)
    c3 = _dot(tri, jnp.concatenate([lf_hi, lf_mid, lf_lo], axis=1))
    fcum = (c3[:, :LANES] + c3[:, LANES:2 * LANES] + c3[:, 2 * LANES:]) + carry_ref[...]
    carry_ref[...] = fcum[tm - 1:tm, :]

    f_hi = fcum.astype(BF16)
    rem = fcum - f_hi.astype(F32)
    f_mid = rem.astype(BF16)
    f_lo = (rem - f_mid.astype(F32)).astype(BF16)
    f3 = jnp.concatenate([f_hi, f_mid, f_lo], axis=1)
    def aug_slot(idx):
        a = idx % LANES
        return a % FOX_AUG_PER_HEAD, a < 2 * FOX_AUG_PER_HEAD

    slot_q, in_q = aug_slot(lax.broadcasted_iota(jnp.int32, (FOX_W, 1), 0))
    slot_k, in_k = aug_slot(lax.broadcasted_iota(jnp.int32, (1, FOX_W), 1))
    augqt = _dot_nt(pqt_ref[...], f3) + jnp.where(in_q & (slot_q < 3), -1.0, 0.0)
    augk = _dot(f3, pk_ref[...]) + jnp.where(in_k & (slot_k >= 3), 1.0, 0.0)
    for p in range(FOX_HEADS // 2):
        src = slice(p * LANES, (p + 1) * LANES)
        qt_ref[2 * p * LANES:(2 * p + 1) * LANES, :] = fqt[src]
        qt_ref[(2 * p + 1) * LANES:(2 * p + 2) * LANES, :] = augqt[src].astype(BF16)
        kaug_ref[:, 2 * p * LANES:(2 * p + 1) * LANES] = fkb[:, src]
        kaug_ref[:, (2 * p + 1) * LANES:(2 * p + 2) * LANES] = augk[:, src].astype(BF16)

    fq32 = fqt.astype(F32) * (FOX_DH ** 0.5)
    nq2 = jnp.sum((fq32 * fq32).reshape(FOX_HEADS, FOX_DH, tm), axis=1)
    statq_ref[0] = jnp.broadcast_to(jnp.max(nq2, axis=1, keepdims=True), (FOX_HEADS, LANES))
    gi = lax.broadcasted_iota(jnp.int32, (FOX_W, LANES), 0) // FOX_DH
    gj = lax.broadcasted_iota(jnp.int32, (FOX_W, LANES), 1)
    group = jnp.where(gi == gj, 1.0, 0.0).astype(BF16)
    fk = fkb.astype(F32)
    nk2 = jnp.max(_dot((fk * fk).astype(BF16), group), axis=0, keepdims=True)
    fmax = jnp.max(fcum, axis=0, keepdims=True)
    fmin = jnp.min(fcum, axis=0, keepdims=True)
    stat_ref[0] = jnp.concatenate([nk2, fmax, fmin, jnp.zeros((5, LANES), F32)], axis=0)

    mq = _dot(ub, wmq_ref[...]).astype(BF16)
    scale = MEM_DH ** -0.5
    outs = []
    for h in range(MEM_HEADS):
        sl = slice(h * MEM_DH, (h + 1) * MEM_DH)
        s = _dot_nt(mq[:, sl], mk_ref[:, sl]) * scale
        m = jnp.max(s, axis=-1, keepdims=True)
        p = jnp.exp(s - m)
        l = jnp.sum(p, axis=-1, keepdims=True)
        p = p / l
        outs.append(_dot(p.astype(BF16), mv_ref[:, sl]))
    omem_ref[...] = jnp.concatenate(outs, axis=-1).astype(BF16)


def _aug_placement():
    pq = np.zeros((3 * LANES, FOX_W), np.float32)
    pk = np.zeros((3 * LANES, FOX_W), np.float32)
    for h in range(FOX_HEADS):
        base = (h // 2) * LANES + (h % 2) * FOX_AUG_PER_HEAD
        for c in range(3):
            pq[c * LANES + h, base + 3 + c] = 1.0
            pk[c * LANES + h, base + c] = 1.0
    return jnp.asarray(pq.T, BF16), jnp.asarray(pk, BF16)


def _proj(x, g_mix, w_gla, w_fk, w_qvt, w_mq, w_gate, w_small, w_up, b_alpha, b_forget, mk, mv):
    s = x.shape[0]
    tm = ROW_TILE
    nt = s // tm
    pqt, pk = _aug_placement()
    full = lambda shape: pl.BlockSpec(shape, lambda i: (0,) * len(shape))
    row = lambda w: pl.BlockSpec((tm, w), lambda i: (i, 0))
    col = lambda r: pl.BlockSpec((r, tm), lambda i: (0, i))
    stat = pl.BlockSpec((1, 8, LANES), lambda i: (i, 0, 0))
    vt_rows = FOX_HEADS * (FOX_DH + FOX_ONES_ROWS)
    return pl.pallas_call(
        _proj_kernel,
        grid=(nt,),
        in_specs=[row(D_MODEL), full((1, D_MODEL)), full(w_gla.shape), full(w_fk.shape), full(w_qvt.shape),
                  full(w_mq.shape), full(w_gate.shape), full(w_small.shape), full(w_up.shape),
                  full(b_alpha.shape), full(b_forget.shape), full(mk.shape), full(mv.shape),
                  full(pqt.shape), full(pk.shape)],
        out_specs=[row(w_gla.shape[1]), row(GLA_K), col(2 * FOX_W), row(2 * FOX_W), col(vt_rows), row(MEM_W),
                   row(w_gate.shape[1]), stat, stat],
        out_shape=[jax.ShapeDtypeStruct((s, w_gla.shape[1]), BF16),
                   jax.ShapeDtypeStruct((s, GLA_K), F32),
                   jax.ShapeDtypeStruct((2 * FOX_W, s), BF16),
                   jax.ShapeDtypeStruct((s, 2 * FOX_W), BF16),
                   jax.ShapeDtypeStruct((vt_rows, s), BF16),
                   jax.ShapeDtypeStruct((s, MEM_W), BF16),
                   jax.ShapeDtypeStruct((s, w_gate.shape[1]), BF16),
                   jax.ShapeDtypeStruct((nt, 8, LANES), F32),
                   jax.ShapeDtypeStruct((nt, 8, LANES), F32)],
        scratch_shapes=[pltpu.VMEM((1, LANES), F32)],
        compiler_params=pltpu.CompilerParams(dimension_semantics=("arbitrary",), vmem_limit_bytes=VMEM_LIMIT),
        name="proj",
    )(x, g_mix, w_gla, w_fk, w_qvt, w_mq, w_gate, w_small, w_up, b_alpha, b_forget, mk, mv, pqt, pk)


def _gla_kernel(qkvg_ref, loga_ref, ghead_ref, o_ref, state_ref):
    t = qkvg_ref.shape[0]
    nc = t // CHUNK

    @pl.when(pl.program_id(0) == 0)
    def _():
        state_ref[...] = jnp.zeros_like(state_ref)

    r = lax.broadcasted_iota(jnp.int32, (t, t), 0)
    c = lax.broadcasted_iota(jnp.int32, (t, t), 1)
    same = (r // CHUNK) == (c // CHUNK)
    lower = r >= c
    causal = same & lower
    anti = same & (r < c)

    la = loga_ref[...]
    la_hi = la.astype(BF16)
    la_r = la - la_hi.astype(F32)
    la_mid = la_r.astype(BF16)
    la_lo = (la_r - la_mid.astype(F32)).astype(BF16)
    b3 = _dot(jnp.where(causal, 1.0, 0.0).astype(BF16), jnp.concatenate([la_hi, la_mid, la_lo], axis=1))
    b = b3[:, :GLA_K] + b3[:, GLA_K:2 * GLA_K] + b3[:, 2 * GLA_K:]
    b_last = jnp.concatenate(
        [jnp.broadcast_to(b[(ci + 1) * CHUNK - 1:(ci + 1) * CHUNK, :], (CHUNK, GLA_K)) for ci in range(nc)], axis=0)
    e_pos = jnp.exp(b)
    e_neg = jnp.exp(-b)
    q = qkvg_ref[:, 0:GLA_K].astype(F32) * (GLA_DK ** -0.5)
    k = qkvg_ref[:, GLA_K:2 * GLA_K].astype(F32)
    q_pos = (q * e_pos).astype(BF16)
    q_neg = (q * e_neg).astype(BF16)
    k_pos = (k * e_pos).astype(BF16)
    k_neg = (k * e_neg).astype(BF16)
    k_dec = (k * jnp.exp(b_last - b)).astype(BF16)

    lane = lax.broadcasted_iota(jnp.int32, (1, LANES), 1)
    for h in range(GLA_HEADS):
        pair = slice((h // 2) * LANES, (h // 2 + 1) * LANES)
        in_head = (lane // GLA_DK) == (h % 2)
        vsl = slice(2 * GLA_K + h * GLA_DV, 2 * GLA_K + (h + 1) * GLA_DV)
        gsl = slice(2 * GLA_K + GLA_V + h * GLA_DV, 2 * GLA_K + GLA_V + (h + 1) * GLA_DV)
        v = qkvg_ref[:, vsl]
        qp = jnp.where(in_head, q_pos[:, pair], jnp.zeros((), BF16))
        qn = jnp.where(in_head, q_neg[:, pair], jnp.zeros((), BF16))
        a_c = _dot_nt(qp, k_neg[:, pair])
        a_a = _dot_nt(qn, k_pos[:, pair])
        attn = jnp.where(causal, a_c, jnp.where(anti, a_a, 0.0)).astype(BF16)
        o_intra = _dot(attn, v)

        st = state_ref[h]
        inter = []
        for ci in range(nc):
            rows = slice(ci * CHUNK, (ci + 1) * CHUNK)
            inter.append(_dot_nt(qp[rows], st.astype(BF16)))
            kv_t = _dot_tn(v[rows], k_dec[rows, pair])
            dec = e_pos[(ci + 1) * CHUNK - 1:(ci + 1) * CHUNK, pair]
            st = st * dec + jnp.where(in_head, kv_t, 0.0)
        state_ref[h] = st
        o = o_intra + jnp.concatenate(inter, axis=0)

        g = ghead_ref[:, h * GLA_DV:(h + 1) * GLA_DV]
        on = _rms(o, g)
        gg = qkvg_ref[:, gsl].astype(F32)
        o_ref[:, h * GLA_DV:(h + 1) * GLA_DV] = (on * (gg * _sigmoid(gg))).astype(BF16)


def _gla(qkvg, loga, ghead):
    s = qkvg.shape[0]
    t = ROW_TILE
    return pl.pallas_call(
        _gla_kernel,
        grid=(s // t,),
        in_specs=[pl.BlockSpec((t, qkvg.shape[1]), lambda i: (i, 0)),
                  pl.BlockSpec((t, GLA_K), lambda i: (i, 0)),
                  pl.BlockSpec((1, GLA_V), lambda i: (0, 0))],
        out_specs=pl.BlockSpec((t, GLA_V), lambda i: (i, 0)),
        out_shape=jax.ShapeDtypeStruct((s, GLA_V), BF16),
        scratch_shapes=[pltpu.VMEM((GLA_HEADS, GLA_DV, LANES), F32)],
        compiler_params=pltpu.CompilerParams(dimension_semantics=("arbitrary",), vmem_limit_bytes=VMEM_LIMIT),
        name="gla",
    )(qkvg, loga, ghead)


def _fox_kernel(thr_ref, fmin_ref, qt_ref, k_ref, vt_ref, o_ref,
                sa_ref, sb_ref, m_ref, acc_ref, si_ref, sj_ref):
    p = pl.program_id(0)
    tq = FOX_TQ
    tk = FOX_TK
    nsub = tq // tk
    nq = k_ref.shape[0] // tq
    hrows = FOX_DH + FOX_ONES_ROWS
    row = lax.broadcasted_iota(jnp.int32, (2 * LANES, 1), 0)
    krow = lax.broadcasted_iota(jnp.int32, (tk, tq), 0)
    qcol = lax.broadcasted_iota(jnp.int32, (tk, tq), 1)

    for hi in range(2):
        h = 2 * p + hi
        aug0 = LANES + hi * FOX_AUG_PER_HEAD
        mine = ((row // FOX_DH) == hi) | ((row >= aug0) & (row < aug0 + FOX_AUG_PER_HEAD))
        vrows = slice(hi * hrows, (hi + 1) * hrows)

        def scores(qi, kj, s_ref, masked=False):
            q0 = pl.multiple_of(qi * tq, tq)
            qm = jnp.where(mine, qt_ref[:, pl.ds(q0, tq)], jnp.zeros((), BF16))
            for u in range(nsub):
                k0 = pl.multiple_of(kj * tq + u * tk, tk)
                s_t = _dot(k_ref[pl.ds(k0, tk), :], qm)
                if masked:
                    s_t = jnp.where(krow + u * tk <= qcol, s_t, NEG_BIG)
                s_ref[u] = s_t

        def softmax_pv(kj, s_ref, m, acc):
            for u in range(nsub):
                s_t = s_ref[u]
                m_new = jnp.maximum(m, jnp.max(s_t, axis=0, keepdims=True))
                alpha = jnp.exp(m - m_new)
                pt = jnp.exp(s_t - m_new).astype(BF16)
                k0 = pl.multiple_of(kj * tq + u * tk, tk)
                acc = alpha * acc + _dot(vt_ref[vrows, pl.ds(k0, tk)], pt)
                m = m_new
            return m, acc

        def list_block(i, t):
            thr = thr_ref[h, i]
            j_end = lax.while_loop(lambda j: (j >= 0) & (fmin_ref[h, jnp.maximum(j, 0)] <= thr),
                                   lambda j: j - 1, i - 1)

            def emit(j, t):
                si_ref[t] = i
                sj_ref[t] = j
                return t + 1

            return lax.fori_loop(j_end + 1, i, emit, t)

        n_tiles = lax.fori_loop(0, nq, list_block, 0)
        for pad in range(2):
            si_ref[n_tiles + pad] = nq
            sj_ref[n_tiles + pad] = 0

        m0 = jnp.full((1, tq), NEG_BIG, F32)
        acc0 = jnp.zeros((hrows, tq), F32)

        def diag_step(t, carry):
            i0 = 2 * t
            scores(i0, i0, sa_ref, masked=True)
            scores(i0 + 1, i0 + 1, sb_ref, masked=True)
            m_ref[hi, i0], acc_ref[hi, i0] = softmax_pv(i0, sa_ref, m0, acc0)
            m_ref[hi, i0 + 1], acc_ref[hi, i0 + 1] = softmax_pv(i0 + 1, sb_ref, m0, acc0)
            return carry

        lax.fori_loop(0, nq // 2, diag_step, 0)
        m_ref[hi, nq] = m0
        acc_ref[hi, nq] = acc0

        def update(t, s_ref):
            i = si_ref[t]
            m, acc = softmax_pv(sj_ref[t], s_ref, m_ref[hi, i], acc_ref[hi, i])
            m_ref[hi, i] = m
            acc_ref[hi, i] = acc

        def prefetch(t, s_ref):
            scores(jnp.minimum(si_ref[t], nq - 1), sj_ref[t], s_ref)

        prefetch(0, sa_ref)

        def pair_step(t2, carry):
            t = 2 * t2
            prefetch(t + 1, sb_ref)
            update(t, sa_ref)
            prefetch(t + 2, sa_ref)
            update(t + 1, sb_ref)
            return carry

        lax.fori_loop(0, (n_tiles + 1) // 2, pair_step, 0)

    def finish(i, carry):
        a0 = acc_ref[0, i]
        a1 = acc_ref[1, i]
        ot = jnp.concatenate([a0[:FOX_DH] / a0[FOX_DH:FOX_DH + 1], a1[:FOX_DH] / a1[FOX_DH:FOX_DH + 1]], axis=0)
        o_ref[pl.ds(pl.multiple_of(i * tq, tq), tq), :] = ot.T.astype(BF16)
        return carry

    lax.fori_loop(0, nq, finish, 0)


def _fox(thr, fmin_blk, qt, kaug, vt):
    s = kaug.shape[0]
    tq = FOX_TQ
    nq = s // tq
    npair = FOX_HEADS // 2
    prow = 2 * (FOX_DH + FOX_ONES_ROWS)
    assert nq % 2 == 0
    hrows = FOX_DH + FOX_ONES_ROWS
    max_tiles = nq * (nq - 1) // 2 + 2
    once = pl.Buffered(1)
    grid_spec = pltpu.PrefetchScalarGridSpec(
        num_scalar_prefetch=2,
        grid=(npair,),
        in_specs=[pl.BlockSpec((2 * LANES, s), lambda p, *_: (p, 0), pipeline_mode=once),
                  pl.BlockSpec((s, 2 * LANES), lambda p, *_: (0, p), pipeline_mode=once),
                  pl.BlockSpec((prow, s), lambda p, *_: (p, 0), pipeline_mode=once)],
        out_specs=pl.BlockSpec((s, LANES), lambda p, *_: (0, p)),
        scratch_shapes=[pltpu.VMEM((tq // FOX_TK, FOX_TK, tq), F32), pltpu.VMEM((tq // FOX_TK, FOX_TK, tq), F32),
                        pltpu.VMEM((2, nq + 1, 1, tq), F32), pltpu.VMEM((2, nq + 1, hrows, tq), F32),
                        pltpu.SMEM((max_tiles,), jnp.int32), pltpu.SMEM((max_tiles,), jnp.int32)],
    )
    return pl.pallas_call(
        _fox_kernel,
        grid_spec=grid_spec,
        out_shape=jax.ShapeDtypeStruct((s, FOX_W), BF16),
        compiler_params=pltpu.CompilerParams(dimension_semantics=("arbitrary",), vmem_limit_bytes=VMEM_LIMIT),
        name="fox",
    )(thr, fmin_blk, qt, kaug, vt)


def _tail_kernel(x_ref, ogla_ref, ofox_ref, omem_ref, gate_ref, wg_ref, wf_ref, wm_ref, wo_ref,
                 gffn_ref, w1_ref, w2_ref, gfin_ref, out_ref):
    gate = gate_ref[...]
    merged = (gate[:, 0:D_MODEL].astype(F32) * _dot(ogla_ref[...], wg_ref[...])
              + gate[:, D_MODEL:2 * D_MODEL].astype(F32) * _dot(ofox_ref[...], wf_ref[...])
              + gate[:, 2 * D_MODEL:3 * D_MODEL].astype(F32) * _dot(omem_ref[...], wm_ref[...]))
    h = x_ref[...] + _dot(merged.astype(BF16), wo_ref[...])
    u2 = _rms(h, gffn_ref[...]).astype(BF16)
    acc = jnp.zeros_like(h)
    for cidx in range(D_FF // FF_CHUNK):
        cs = slice(cidx * FF_CHUNK, (cidx + 1) * FF_CHUNK)
        a = jnp.maximum(_dot(u2, w1_ref[:, cs]), 0.0)
        acc = acc + _dot((a * a).astype(BF16), w2_ref[cs, :])
    out_ref[...] = _rms(h + acc, gfin_ref[...])


def _tail(x, ogla, ofox, omem, gate, wg, wf, wm, wo, gffn, w1, w2, gfin):
    s = x.shape[0]
    tm = TAIL_TILE
    full = lambda a: pl.BlockSpec(a.shape, lambda i: (0,) * a.ndim, pipeline_mode=pl.Buffered(1))
    row = lambda w: pl.BlockSpec((tm, w), lambda i: (i, 0))
    return pl.pallas_call(
        _tail_kernel,
        grid=(s // tm,),
        in_specs=[row(D_MODEL), row(GLA_V), row(FOX_W), row(MEM_W), row(3 * D_MODEL),
                  full(wg), full(wf), full(wm), full(wo), full(gffn), full(w1), full(w2), full(gfin)],
        out_specs=row(D_MODEL),
        out_shape=jax.ShapeDtypeStruct((s, D_MODEL), F32),
        compiler_params=pltpu.CompilerParams(dimension_semantics=("arbitrary",), vmem_limit_bytes=VMEM_LIMIT),
        name="tail",
    )(x, ogla, ofox, omem, gate, wg, wf, wm, wo, gffn, w1, w2, gfin)


def kernel(x, mem, g_mix, w_in, w_alpha_up, b_alpha, b_forget, g_gla_head, g_mem, w_mem_kv,
           w_gla_o, w_fox_o, w_mem_o, w_out, g_ffn, w_ff1, w_ff2, g_final):
    assert x.shape[0] == 1 and g_mix.shape[0] == 1, "single batch, single layer"
    s = x.shape[1]
    assert s % ROW_TILE == 0 and s % TAIL_TILE == 0 and s % FOX_TQ == 0
    assert FOX_TQ % FOX_TK == 0 and FOX_TQ % ROW_TILE == 0
    xs = x[0]
    w = w_in[0]

    o_ga = 2 * GLA_K + 2 * GLA_V
    o_fox = o_ga + GLA_LOWRANK
    o_ff = o_fox + 3 * FOX_W
    o_mq = o_ff + FOX_HEADS
    o_gate = o_mq + MEM_W
    w_gla = w[:, :o_ga].astype(BF16)
    w_fk = w[:, o_fox + FOX_W:o_fox + 2 * FOX_W].astype(BF16)
    w_qvt = jnp.concatenate([w[:, o_fox:o_fox + FOX_W], w[:, o_fox + 2 * FOX_W:o_ff]], axis=1).T.astype(BF16)
    w_mq = w[:, o_mq:o_gate].astype(BF16)
    w_gate = w[:, o_gate:].astype(BF16)
    w_small = jnp.concatenate(
        [w[:, o_ga:o_fox], w[:, o_ff:o_mq], jnp.zeros((D_MODEL, LANES - GLA_LOWRANK - FOX_HEADS), F32)], axis=1)
    w_small_hi = w_small.astype(BF16)
    w_small = jnp.concatenate([w_small_hi, (w_small - w_small_hi.astype(F32)).astype(BF16)], axis=1)
    w_up_hi = w_alpha_up[0].astype(BF16)
    w_up = jnp.concatenate([w_up_hi, (w_alpha_up[0] - w_up_hi.astype(F32)).astype(BF16)], axis=1)
    b_f = jnp.zeros((1, LANES), F32).at[0, GLA_LOWRANK:GLA_LOWRANK + FOX_HEADS].set(b_forget[0])

    mk, mv = _memkv(mem[0], g_mem, w_mem_kv[0].astype(BF16))
    gla_qkvg, loga, qt, kaug, vt, omem, gate, stats, statq = _proj(
        xs, g_mix, w_gla, w_fk, w_qvt, w_mq, w_gate, w_small, w_up, b_alpha, b_f, mk, mv)

    ogla = _gla(gla_qkvg, loga, g_gla_head.reshape(1, GLA_V))

    per_blk = lambda a: a.reshape(s // FOX_TQ, FOX_TQ // ROW_TILE, FOX_HEADS)
    qn = jnp.sqrt(jnp.max(per_blk(statq[:, :, 0]), axis=1))
    kn = jnp.sqrt(jnp.max(stats[:, 0, :FOX_HEADS], axis=0))
    fmax = jnp.max(per_blk(stats[:, 1, :FOX_HEADS]), axis=1)
    fmin = jnp.min(per_blk(stats[:, 2, :FOX_HEADS]), axis=1)
    thr = (2.0 * 1.02 * (FOX_DH ** -0.5)) * qn * kn[None, :] + fmax + PRUNE_LOGIT_GAP
    ofox = _fox(thr.T, fmin.T, qt, kaug, vt)

    out = _tail(xs, ogla, ofox, omem, gate, w_gla_o[0].astype(BF16), w_fox_o[0].astype(BF16),
                w_mem_o[0].astype(BF16), w_out[0].astype(BF16), g_ffn, w_ff1[0].astype(BF16),
                w_ff2[0].astype(BF16), g_final.reshape(1, D_MODEL))
    return out[None]
```

```python
import functools

import jax
import jax.numpy as jnp
import numpy as np
from jax import lax
from jax.experimental import pallas as pl
from jax.experimental.pallas import tpu as pltpu

D_MODEL = 1024
CHUNK = 64
EPS = 1e-6
GLA_HEADS = 4
GLA_DK = 64
GLA_DV = 128
GLA_LOWRANK = 16
GLA_TAU = 16.0
FOX_HEADS = 8
FOX_DH = 64
MEM_HEADS = 4
MEM_DH = 128
D_FF = 4 * D_MODEL
GLA_K = GLA_HEADS * GLA_DK
GLA_V = GLA_HEADS * GLA_DV
FOX_W = FOX_HEADS * FOX_DH
MEM_W = MEM_HEADS * MEM_DH

LANES = 128
FOX_ONES_ROWS = 16
FOX_AUG_PER_HEAD = 6
ROW_TILE = 256
FOX_TQ = 512
FOX_TK = 256
FOX_STEPS = 4
TAIL_TILE = 256
FF_CHUNK = 1024
VMEM_LIMIT = 56 * 1024 * 1024
NEG_BIG = -1e30
PRUNE_LOGIT_GAP = 104.0
FOX_MAX_SHIFT_GAP = 50.0

F32 = jnp.float32
BF16 = jnp.bfloat16


def _rms(xf, g):
    r = lax.rsqrt(jnp.mean(xf * xf, axis=-1, keepdims=True) + EPS)
    return (xf * r) * g


def _log_sigmoid(x):
    return jnp.minimum(x, 0.0) - jnp.log1p(jnp.exp(-jnp.abs(x)))


def _sigmoid(x):
    return 1.0 / (1.0 + jnp.exp(-x))


def _dot(a, b):
    return jnp.dot(a, b, preferred_element_type=F32)


def _dot_nt(a, b):
    return lax.dot_general(a, b, (((1,), (1,)), ((), ())), preferred_element_type=F32)


def _dot_tn(a, b):
    return lax.dot_general(a, b, (((0,), (0,)), ((), ())), preferred_element_type=F32)


def _memkv_kernel(mem_ref, g_ref, w_ref, mk_ref, mv_ref):
    mn = _rms(mem_ref[...], g_ref[...]).astype(BF16)
    kv = _dot(mn, w_ref[...])
    mk_ref[...] = kv[:, :MEM_W].astype(BF16)
    mv_ref[...] = kv[:, MEM_W:].astype(BF16)


def _memkv(mem, g_mem, w_mem_kv):
    m = mem.shape[0]
    return pl.pallas_call(
        _memkv_kernel,
        out_shape=(jax.ShapeDtypeStruct((m, MEM_W), BF16), jax.ShapeDtypeStruct((m, MEM_W), BF16)),
        name="memkv",
    )(mem, g_mem, w_mem_kv)


def _proj_kernel(x_ref, gmix_ref, wgla_ref, wfk_ref, wqvt_ref, wmq_ref, wgate_ref, wsmall_ref, wup_ref,
                 balpha_ref, bforget_ref, mk_ref, mv_ref, pqt_ref, pk_ref,
                 gla_ref, loga_ref, qt_ref, kaug_ref, vt_ref, omem_ref, gate_ref, stat_ref, statq_ref,
                 carry_ref):
    tm = x_ref.shape[0]

    @pl.when(pl.program_id(0) == 0)
    def _():
        carry_ref[...] = jnp.zeros_like(carry_ref)

    u = _rms(x_ref[...], gmix_ref[...])
    ub = u.astype(BF16)

    gla_ref[...] = _dot(ub, wgla_ref[...]).astype(BF16)
    fkb = _dot(ub, wfk_ref[...]).astype(BF16)
    gate_ref[...] = _sigmoid(_dot(ub, wgate_ref[...])).astype(BF16)

    qvt = _dot_nt(wqvt_ref[...], ub)
    fqt = (qvt[:FOX_W] * (FOX_DH ** -0.5)).astype(BF16)
    vt = qvt[FOX_W:].astype(BF16)
    ones = jnp.ones((FOX_ONES_ROWS, tm), BF16)
    vt_parts = []
    for h in range(FOX_HEADS):
        vt_parts += [vt[h * FOX_DH:(h + 1) * FOX_DH, :], ones]
    vt_ref[...] = jnp.concatenate(vt_parts, axis=0)

    u_lo = (u - ub.astype(F32)).astype(BF16)
    parts = _dot(jnp.concatenate([ub, u_lo], axis=0), wsmall_ref[...])
    small = (parts[:tm, :LANES] + parts[:tm, LANES:]) + (parts[tm:, :LANES] + parts[tm:, LANES:])
    ga = small[:, :GLA_LOWRANK]
    ga_hi = ga.astype(BF16)
    ga_lo = (ga - ga_hi.astype(F32)).astype(BF16)
    up = _dot(jnp.concatenate([ga_hi, ga_lo], axis=0), wup_ref[...])
    alpha_pre = ((up[:tm, :GLA_K] + up[:tm, GLA_K:]) + (up[tm:, :GLA_K] + up[tm:, GLA_K:])) + balpha_ref[...]
    loga_ref[...] = _log_sigmoid(alpha_pre) * (1.0 / GLA_TAU)

    lane = lax.broadcasted_iota(jnp.int32, (tm, LANES), 1)
    ff_valid = (lane >= GLA_LOWRANK) & (lane < GLA_LOWRANK + FOX_HEADS)
    logf = jnp.where(ff_valid, _log_sigmoid(small + bforget_ref[...]), 0.0)
    logf = pltpu.roll(logf, LANES - GLA_LOWRANK, 1)
    r = lax.broadcasted_iota(jnp.int32, (tm, tm), 0)
    c = lax.broadcasted_iota(jnp.int32, (tm, tm), 1)
    tri = jnp.where(r >= c, 1.0, 0.0).astype(BF16)
    lf_hi = logf.astype(BF16)
    lf_r = logf - lf_hi.astype(F32)
    lf_mid = lf_r.astype(BF16)
    lf_lo = (lf_r - lf_mid.astype(F32)).astype(BF16)
    c3 = _dot(tri, jnp.concatenate([lf_hi, lf_mid, lf_lo], axis=1))
    fcum = (c3[:, :LANES] + c3[:, LANES:2 * LANES] + c3[:, 2 * LANES:]) + carry_ref[...]
    carry_ref[...] = fcum[tm - 1:tm, :]

    f_hi = fcum.astype(BF16)
    rem = fcum - f_hi.astype(F32)
    f_mid = rem.astype(BF16)
    f_lo = (rem - f_mid.astype(F32)).astype(BF16)
    f3 = jnp.concatenate([f_hi, f_mid, f_lo], axis=1)
    def aug_slot(idx):
        a = idx % LANES
        return a % FOX_AUG_PER_HEAD, a < 2 * FOX_AUG_PER_HEAD

    slot_q, in_q = aug_slot(lax.broadcasted_iota(jnp.int32, (FOX_W, 1), 0))
    slot_k, in_k = aug_slot(lax.broadcasted_iota(jnp.int32, (1, FOX_W), 1))
    augqt = _dot_nt(pqt_ref[...], f3) + jnp.where(in_q & (slot_q < 3), -1.0, 0.0)
    augk = _dot(f3, pk_ref[...]) + jnp.where(in_k & (slot_k >= 3), 1.0, 0.0)
    for p in range(FOX_HEADS // 2):
        src = slice(p * LANES, (p + 1) * LANES)
        qt_ref[2 * p * LANES:(2 * p + 1) * LANES, :] = fqt[src]
        qt_ref[(2 * p + 1) * LANES:(2 * p + 2) * LANES, :] = augqt[src].astype(BF16)
        kaug_ref[:, 2 * p * LANES:(2 * p + 1) * LANES] = fkb[:, src]
        kaug_ref[:, (2 * p + 1) * LANES:(2 * p + 2) * LANES] = augk[:, src].astype(BF16)

    fq32 = fqt.astype(F32) * (FOX_DH ** 0.5)
    nq2 = jnp.sum((fq32 * fq32).reshape(FOX_HEADS, FOX_DH, tm), axis=1)
    statq_ref[0] = jnp.broadcast_to(jnp.max(nq2, axis=1, keepdims=True), (FOX_HEADS, LANES))
    gi = lax.broadcasted_iota(jnp.int32, (FOX_W, LANES), 0) // FOX_DH
    gj = lax.broadcasted_iota(jnp.int32, (FOX_W, LANES), 1)
    group = jnp.where(gi == gj, 1.0, 0.0).astype(BF16)
    fk = fkb.astype(F32)
    nk2 = jnp.max(_dot((fk * fk).astype(BF16), group), axis=0, keepdims=True)
    fmax = jnp.max(fcum, axis=0, keepdims=True)
    fmin = jnp.min(fcum, axis=0, keepdims=True)
    stat_ref[0] = jnp.concatenate([nk2, fmax, fmin, jnp.zeros((5, LANES), F32)], axis=0)

    mq = _dot(ub, wmq_ref[...]).astype(BF16)
    scale = MEM_DH ** -0.5
    outs = []
    for h in range(MEM_HEADS):
        sl = slice(h * MEM_DH, (h + 1) * MEM_DH)
        s = _dot_nt(mq[:, sl], mk_ref[:, sl]) * scale
        m = jnp.max(s, axis=-1, keepdims=True)
        p = jnp.exp(s - m)
        l = jnp.sum(p, axis=-1, keepdims=True)
        p = p / l
        outs.append(_dot(p.astype(BF16), mv_ref[:, sl]))
    omem_ref[...] = jnp.concatenate(outs, axis=-1).astype(BF16)


def _aug_placement():
    pq = np.zeros((3 * LANES, FOX_W), np.float32)
    pk = np.zeros((3 * LANES, FOX_W), np.float32)
    for h in range(FOX_HEADS):
        base = (h // 2) * LANES + (h % 2) * FOX_AUG_PER_HEAD
        for c in range(3):
            pq[c * LANES + h, base + 3 + c] = 1.0
            pk[c * LANES + h, base + c] = 1.0
    return jnp.asarray(pq.T, BF16), jnp.asarray(pk, BF16)


def _proj(x, g_mix, w_gla, w_fk, w_qvt, w_mq, w_gate, w_small, w_up, b_alpha, b_forget, mk, mv):
    s = x.shape[0]
    tm = ROW_TILE
    nt = s // tm
    pqt, pk = _aug_placement()
    full = lambda shape: pl.BlockSpec(shape, lambda i: (0,) * len(shape))
    row = lambda w: pl.BlockSpec((tm, w), lambda i: (i, 0))
    col = lambda r: pl.BlockSpec((r, tm), lambda i: (0, i))
    stat = pl.BlockSpec((1, 8, LANES), lambda i: (i, 0, 0))
    vt_rows = FOX_HEADS * (FOX_DH + FOX_ONES_ROWS)
    return pl.pallas_call(
        _proj_kernel,
        grid=(nt,),
        in_specs=[row(D_MODEL), full((1, D_MODEL)), full(w_gla.shape), full(w_fk.shape), full(w_qvt.shape),
                  full(w_mq.shape), full(w_gate.shape), full(w_small.shape), full(w_up.shape),
                  full(b_alpha.shape), full(b_forget.shape), full(mk.shape), full(mv.shape),
                  full(pqt.shape), full(pk.shape)],
        out_specs=[row(w_gla.shape[1]), row(GLA_K), col(2 * FOX_W), row(2 * FOX_W), col(vt_rows), row(MEM_W),
                   row(w_gate.shape[1]), stat, stat],
        out_shape=[jax.ShapeDtypeStruct((s, w_gla.shape[1]), BF16),
                   jax.ShapeDtypeStruct((s, GLA_K), F32),
                   jax.ShapeDtypeStruct((2 * FOX_W, s), BF16),
                   jax.ShapeDtypeStruct((s, 2 * FOX_W), BF16),
                   jax.ShapeDtypeStruct((vt_rows, s), BF16),
                   jax.ShapeDtypeStruct((s, MEM_W), BF16),
                   jax.ShapeDtypeStruct((s, w_gate.shape[1]), BF16),
                   jax.ShapeDtypeStruct((nt, 8, LANES), F32),
                   jax.ShapeDtypeStruct((nt, 8, LANES), F32)],
        scratch_shapes=[pltpu.VMEM((1, LANES), F32)],
        compiler_params=pltpu.CompilerParams(dimension_semantics=("arbitrary",), vmem_limit_bytes=VMEM_LIMIT),
        name="proj",
    )(x, g_mix, w_gla, w_fk, w_qvt, w_mq, w_gate, w_small, w_up, b_alpha, b_forget, mk, mv, pqt, pk)


def _gla_kernel(qkvg_ref, loga_ref, ghead_ref, o_ref, state_ref):
    t = qkvg_ref.shape[0]
    nc = t // CHUNK

    @pl.when(pl.program_id(0) == 0)
    def _():
        state_ref[...] = jnp.zeros_like(state_ref)

    r = lax.broadcasted_iota(jnp.int32, (t, t), 0)
    c = lax.broadcasted_iota(jnp.int32, (t, t), 1)
    same = (r // CHUNK) == (c // CHUNK)
    lower = r >= c
    causal = same & lower
    anti = same & (r < c)

    la = loga_ref[...]
    la_hi = la.astype(BF16)
    la_r = la - la_hi.astype(F32)
    la_mid = la_r.astype(BF16)
    la_lo = (la_r - la_mid.astype(F32)).astype(BF16)
    b3 = _dot(jnp.where(causal, 1.0, 0.0).astype(BF16), jnp.concatenate([la_hi, la_mid, la_lo], axis=1))
    b = b3[:, :GLA_K] + b3[:, GLA_K:2 * GLA_K] + b3[:, 2 * GLA_K:]
    b_last = jnp.concatenate(
        [jnp.broadcast_to(b[(ci + 1) * CHUNK - 1:(ci + 1) * CHUNK, :], (CHUNK, GLA_K)) for ci in range(nc)], axis=0)
    e_pos = jnp.exp(b)
    e_neg = jnp.exp(-b)
    q = qkvg_ref[:, 0:GLA_K].astype(F32) * (GLA_DK ** -0.5)
    k = qkvg_ref[:, GLA_K:2 * GLA_K].astype(F32)
    q_pos = (q * e_pos).astype(BF16)
    q_neg = (q * e_neg).astype(BF16)
    k_pos = (k * e_pos).astype(BF16)
    k_neg = (k * e_neg).astype(BF16)
    k_dec = (k * jnp.exp(b_last - b)).astype(BF16)

    lane = lax.broadcasted_iota(jnp.int32, (1, LANES), 1)
    for h in range(GLA_HEADS):
        pair = slice((h // 2) * LANES, (h // 2 + 1) * LANES)
        in_head = (lane // GLA_DK) == (h % 2)
        vsl = slice(2 * GLA_K + h * GLA_DV, 2 * GLA_K + (h + 1) * GLA_DV)
        gsl = slice(2 * GLA_K + GLA_V + h * GLA_DV, 2 * GLA_K + GLA_V + (h + 1) * GLA_DV)
        v = qkvg_ref[:, vsl]
        qp = jnp.where(in_head, q_pos[:, pair], jnp.zeros((), BF16))
        qn = jnp.where(in_head, q_neg[:, pair], jnp.zeros((), BF16))
        a_c = _dot_nt(qp, k_neg[:, pair])
        a_a = _dot_nt(qn, k_pos[:, pair])
        attn = jnp.where(causal, a_c, jnp.where(anti, a_a, 0.0)).astype(BF16)
        o_intra = _dot(attn, v)

        st = state_ref[h]
        inter = []
        for ci in range(nc):
            rows = slice(ci * CHUNK, (ci + 1) * CHUNK)
            inter.append(_dot_nt(qp[rows], st.astype(BF16)))
            kv_t = _dot_tn(v[rows], k_dec[rows, pair])
            dec = e_pos[(ci + 1) * CHUNK - 1:(ci + 1) * CHUNK, pair]
            st = st * dec + jnp.where(in_head, kv_t, 0.0)
        state_ref[h] = st
        o = o_intra + jnp.concatenate(inter, axis=0)

        g = ghead_ref[:, h * GLA_DV:(h + 1) * GLA_DV]
        on = _rms(o, g)
        gg = qkvg_ref[:, gsl].astype(F32)
        o_ref[:, h * GLA_DV:(h + 1) * GLA_DV] = (on * (gg * _sigmoid(gg))).astype(BF16)


def _gla(qkvg, loga, ghead):
    s = qkvg.shape[0]
    t = ROW_TILE
    return pl.pallas_call(
        _gla_kernel,
        grid=(s // t,),
        in_specs=[pl.BlockSpec((t, qkvg.shape[1]), lambda i: (i, 0)),
                  pl.BlockSpec((t, GLA_K), lambda i: (i, 0)),
                  pl.BlockSpec((1, GLA_V), lambda i: (0, 0))],
        out_specs=pl.BlockSpec((t, GLA_V), lambda i: (i, 0)),
        out_shape=jax.ShapeDtypeStruct((s, GLA_V), BF16),
        scratch_shapes=[pltpu.VMEM((GLA_HEADS, GLA_DV, LANES), F32)],
        compiler_params=pltpu.CompilerParams(dimension_semantics=("arbitrary",), vmem_limit_bytes=VMEM_LIMIT),
        name="gla",
    )(qkvg, loga, ghead)


def _fox_kernel(thr_ref, fmin_ref, shift_ref, qt_ref, k_ref, vt_ref, o_ref,
                sa_ref, sb_ref, pa_ref, pb_ref, m_ref, acc_ref, si_ref, sj_ref):
    p = pl.program_id(0)
    tq = FOX_TQ
    tk = FOX_TK
    nsub = tq // tk
    nq = k_ref.shape[0] // tq
    hrows = FOX_DH + FOX_ONES_ROWS
    row = lax.broadcasted_iota(jnp.int32, (2 * LANES, 1), 0)
    krow = lax.broadcasted_iota(jnp.int32, (tk, tq), 0)
    qcol = lax.broadcasted_iota(jnp.int32, (tk, tq), 1)
    m0 = jnp.full((1, tq), NEG_BIG, F32)
    acc0 = jnp.zeros((hrows, tq), F32)

    for hi in range(2):
        h = 2 * p + hi
        aug0 = LANES + hi * FOX_AUG_PER_HEAD
        mine = ((row // FOX_DH) == hi) | ((row >= aug0) & (row < aug0 + FOX_AUG_PER_HEAD))
        vrows = slice(hi * hrows, (hi + 1) * hrows)

        def score_tiles(qi, kj, masked):
            q0 = pl.multiple_of(qi * tq, tq)
            qm = jnp.where(mine, qt_ref[:, pl.ds(q0, tq)], jnp.zeros((), BF16))
            for u in range(nsub):
                k0 = pl.multiple_of(kj * tq + u * tk, tk)
                s_t = _dot(k_ref[pl.ds(k0, tk), :], qm)
                if masked:
                    s_t = jnp.where(krow + u * tk <= qcol, s_t, NEG_BIG)
                yield u, s_t

        def values(kj, u):
            return vt_ref[vrows, pl.ds(pl.multiple_of(kj * tq + u * tk, tk), tk)]

        def fast_produce(qi, kj, buf, masked=False):
            shift = shift_ref[h, qi]
            for u, s_t in score_tiles(qi, kj, masked):
                buf[u] = jnp.exp(s_t - shift).astype(BF16)

        def fast_consume(i, kj, buf, fresh=False):
            acc = acc0 if fresh else acc_ref[hi, i]
            for u in range(nsub):
                acc = acc + _dot(values(kj, u), buf[u])
            acc_ref[hi, i] = acc

        def slow_produce(qi, kj, buf, masked=False):
            for u, s_t in score_tiles(qi, kj, masked):
                buf[u] = s_t

        def slow_consume(i, kj, buf, fresh=False):
            m = m0 if fresh else m_ref[hi, i]
            acc = acc0 if fresh else acc_ref[hi, i]
            for u in range(nsub):
                s_t = buf[u]
                m_new = jnp.maximum(m, jnp.max(s_t, axis=0, keepdims=True))
                alpha = jnp.exp(m - m_new)
                pt = jnp.exp(s_t - m_new).astype(BF16)
                acc = alpha * acc + _dot(values(kj, u), pt)
                m = m_new
            m_ref[hi, i] = m
            acc_ref[hi, i] = acc

        def list_block(i, carry):
            t, worst = carry
            thr = thr_ref[h, i]
            j_end = lax.while_loop(lambda j: (j >= 0) & (fmin_ref[h, jnp.maximum(j, 0)] <= thr),
                                   lambda j: j - 1, i - 1)

            def emit(j, t):
                si_ref[t] = i
                sj_ref[t] = j
                return t + 1

            return lax.fori_loop(j_end + 1, i, emit, t), jnp.maximum(worst, shift_ref[h, i])

        n_tiles, worst_shift = lax.fori_loop(0, nq, list_block, (0, jnp.float32(0.0)))
        for pad in range(FOX_STEPS):
            si_ref[n_tiles + pad] = nq
            sj_ref[n_tiles + pad] = 0

        def run_head(produce, consume, buf_a, buf_b):
            def diag_step(t, carry):
                i0 = 2 * t
                produce(i0, i0, buf_a, masked=True)
                produce(i0 + 1, i0 + 1, buf_b, masked=True)
                consume(i0, i0, buf_a, fresh=True)
                consume(i0 + 1, i0 + 1, buf_b, fresh=True)
                return carry

            lax.fori_loop(0, nq // 2, diag_step, 0)
            m_ref[hi, nq] = m0
            acc_ref[hi, nq] = acc0

            def prefetch(t, buf):
                produce(jnp.minimum(si_ref[t], nq - 1), sj_ref[t], buf)

            prefetch(0, buf_a)

            def multi_step(tn, carry):
                t = FOX_STEPS * tn
                for d in range(0, FOX_STEPS, 2):
                    prefetch(t + d + 1, buf_b)
                    consume(si_ref[t + d], sj_ref[t + d], buf_a)
                    prefetch(t + d + 2, buf_a)
                    consume(si_ref[t + d + 1], sj_ref[t + d + 1], buf_b)
                return carry

            lax.fori_loop(0, (n_tiles + FOX_STEPS - 1) // FOX_STEPS, multi_step, 0)

        no_running_max = 2.0 * worst_shift <= FOX_MAX_SHIFT_GAP

        @pl.when(no_running_max)
        def _():
            run_head(fast_produce, fast_consume, pa_ref, pb_ref)

        @pl.when(jnp.logical_not(no_running_max))
        def _():
            run_head(slow_produce, slow_consume, sa_ref, sb_ref)

    def finish(i, carry):
        a0 = acc_ref[0, i]
        a1 = acc_ref[1, i]
        ot = jnp.concatenate([a0[:FOX_DH] / a0[FOX_DH:FOX_DH + 1], a1[:FOX_DH] / a1[FOX_DH:FOX_DH + 1]], axis=0)
        o_ref[pl.ds(pl.multiple_of(i * tq, tq), tq), :] = ot.T.astype(BF16)
        return carry

    lax.fori_loop(0, nq, finish, 0)


def _fox(thr, fmin_blk, shift, qt, kaug, vt):
    s = kaug.shape[0]
    tq = FOX_TQ
    nq = s // tq
    npair = FOX_HEADS // 2
    prow = 2 * (FOX_DH + FOX_ONES_ROWS)
    assert nq % 2 == 0
    hrows = FOX_DH + FOX_ONES_ROWS
    max_tiles = nq * (nq - 1) // 2 + FOX_STEPS
    once = pl.Buffered(1)
    grid_spec = pltpu.PrefetchScalarGridSpec(
        num_scalar_prefetch=3,
        grid=(npair,),
        in_specs=[pl.BlockSpec((2 * LANES, s), lambda p, *_: (p, 0), pipeline_mode=once),
                  pl.BlockSpec((s, 2 * LANES), lambda p, *_: (0, p), pipeline_mode=once),
                  pl.BlockSpec((prow, s), lambda p, *_: (p, 0), pipeline_mode=once)],
        out_specs=pl.BlockSpec((s, LANES), lambda p, *_: (0, p)),
        scratch_shapes=[pltpu.VMEM((tq // FOX_TK, FOX_TK, tq), F32), pltpu.VMEM((tq // FOX_TK, FOX_TK, tq), F32),
                        pltpu.VMEM((tq // FOX_TK, FOX_TK, tq), BF16), pltpu.VMEM((tq // FOX_TK, FOX_TK, tq), BF16),
                        pltpu.VMEM((2, nq + 1, 1, tq), F32), pltpu.VMEM((2, nq + 1, hrows, tq), F32),
                        pltpu.SMEM((max_tiles,), jnp.int32), pltpu.SMEM((max_tiles,), jnp.int32)],
    )
    return pl.pallas_call(
        _fox_kernel,
        grid_spec=grid_spec,
        out_shape=jax.ShapeDtypeStruct((s, FOX_W), BF16),
        compiler_params=pltpu.CompilerParams(dimension_semantics=("arbitrary",), vmem_limit_bytes=VMEM_LIMIT),
        name="fox",
    )(thr, fmin_blk, shift, qt, kaug, vt)


def _tail_kernel(x_ref, ogla_ref, ofox_ref, omem_ref, gate_ref, wg_ref, wf_ref, wm_ref, wo_ref,
                 gffn_ref, w1_ref, w2_ref, gfin_ref, out_ref):
    gate = gate_ref[...]
    merged = (gate[:, 0:D_MODEL].astype(F32) * _dot(ogla_ref[...], wg_ref[...])
              + gate[:, D_MODEL:2 * D_MODEL].astype(F32) * _dot(ofox_ref[...], wf_ref[...])
              + gate[:, 2 * D_MODEL:3 * D_MODEL].astype(F32) * _dot(omem_ref[...], wm_ref[...]))
    h = x_ref[...] + _dot(merged.astype(BF16), wo_ref[...])
    u2 = _rms(h, gffn_ref[...]).astype(BF16)
    acc = jnp.zeros_like(h)
    for cidx in range(D_FF // FF_CHUNK):
        cs = slice(cidx * FF_CHUNK, (cidx + 1) * FF_CHUNK)
        a = jnp.maximum(_dot(u2, w1_ref[:, cs]), 0.0)
        acc = acc + _dot((a * a).astype(BF16), w2_ref[cs, :])
    out_ref[...] = _rms(h + acc, gfin_ref[...])


def _tail(x, ogla, ofox, omem, gate, wg, wf, wm, wo, gffn, w1, w2, gfin):
    s = x.shape[0]
    tm = TAIL_TILE
    full = lambda a: pl.BlockSpec(a.shape, lambda i: (0,) * a.ndim, pipeline_mode=pl.Buffered(1))
    row = lambda w: pl.BlockSpec((tm, w), lambda i: (i, 0))
    return pl.pallas_call(
        _tail_kernel,
        grid=(s // tm,),
        in_specs=[row(D_MODEL), row(GLA_V), row(FOX_W), row(MEM_W), row(3 * D_MODEL),
                  full(wg), full(wf), full(wm), full(wo), full(gffn), full(w1), full(w2), full(gfin)],
        out_specs=row(D_MODEL),
        out_shape=jax.ShapeDtypeStruct((s, D_MODEL), F32),
        compiler_params=pltpu.CompilerParams(dimension_semantics=("arbitrary",), vmem_limit_bytes=VMEM_LIMIT),
        name="tail",
    )(x, ogla, ofox, omem, gate, wg, wf, wm, wo, gffn, w1, w2, gfin)


def kernel(x, mem, g_mix, w_in, w_alpha_up, b_alpha, b_forget, g_gla_head, g_mem, w_mem_kv,
           w_gla_o, w_fox_o, w_mem_o, w_out, g_ffn, w_ff1, w_ff2, g_final):
    assert x.shape[0] == 1 and g_mix.shape[0] == 1, "single batch, single layer"
    s = x.shape[1]
    assert s % ROW_TILE == 0 and s % TAIL_TILE == 0 and s % FOX_TQ == 0
    assert FOX_TQ % FOX_TK == 0 and FOX_TQ % ROW_TILE == 0
    xs = x[0]
    w = w_in[0]

    o_ga = 2 * GLA_K + 2 * GLA_V
    o_fox = o_ga + GLA_LOWRANK
    o_ff = o_fox + 3 * FOX_W
    o_mq = o_ff + FOX_HEADS
    o_gate = o_mq + MEM_W
    w_gla = w[:, :o_ga].astype(BF16)
    w_fk = w[:, o_fox + FOX_W:o_fox + 2 * FOX_W].astype(BF16)
    w_qvt = jnp.concatenate([w[:, o_fox:o_fox + FOX_W], w[:, o_fox + 2 * FOX_W:o_ff]], axis=1).T.astype(BF16)
    w_mq = w[:, o_mq:o_gate].astype(BF16)
    w_gate = w[:, o_gate:].astype(BF16)
    w_small = jnp.concatenate(
        [w[:, o_ga:o_fox], w[:, o_ff:o_mq], jnp.zeros((D_MODEL, LANES - GLA_LOWRANK - FOX_HEADS), F32)], axis=1)
    w_small_hi = w_small.astype(BF16)
    w_small = jnp.concatenate([w_small_hi, (w_small - w_small_hi.astype(F32)).astype(BF16)], axis=1)
    w_up_hi = w_alpha_up[0].astype(BF16)
    w_up = jnp.concatenate([w_up_hi, (w_alpha_up[0] - w_up_hi.astype(F32)).astype(BF16)], axis=1)
    b_f = jnp.zeros((1, LANES), F32).at[0, GLA_LOWRANK:GLA_LOWRANK + FOX_HEADS].set(b_forget[0])

    mk, mv = _memkv(mem[0], g_mem, w_mem_kv[0].astype(BF16))
    gla_qkvg, loga, qt, kaug, vt, omem, gate, stats, statq = _proj(
        xs, g_mix, w_gla, w_fk, w_qvt, w_mq, w_gate, w_small, w_up, b_alpha, b_f, mk, mv)

    ogla = _gla(gla_qkvg, loga, g_gla_head.reshape(1, GLA_V))

    per_blk = lambda a: a.reshape(s // FOX_TQ, FOX_TQ // ROW_TILE, FOX_HEADS)
    qn = jnp.sqrt(jnp.max(per_blk(statq[:, :, 0]), axis=1))
    kn = jnp.sqrt(jnp.max(stats[:, 0, :FOX_HEADS], axis=0))
    fmax = jnp.max(per_blk(stats[:, 1, :FOX_HEADS]), axis=1)
    fmin = jnp.min(per_blk(stats[:, 2, :FOX_HEADS]), axis=1)
    shift = (1.02 * (FOX_DH ** -0.5)) * qn * kn[None, :]
    thr = 2.0 * shift + fmax + PRUNE_LOGIT_GAP
    ofox = _fox(thr.T, fmin.T, shift.T, qt, kaug, vt)

    out = _tail(xs, ogla, ofox, omem, gate, w_gla_o[0].astype(BF16), w_fox_o[0].astype(BF16),
                w_mem_o[0].astype(BF16), w_out[0].astype(BF16), g_ffn, w_ff1[0].astype(BF16),
                w_ff2[0].astype(BF16), g_final.reshape(1, D_MODEL))
    return out[None]
```

```python
import functools

import jax
import jax.numpy as jnp
import numpy as np
from jax import lax
from jax.experimental import pallas as pl
from jax.experimental.pallas import tpu as pltpu

D_MODEL = 1024
CHUNK = 64
EPS = 1e-6
GLA_HEADS = 4
GLA_DK = 64
GLA_DV = 128
GLA_LOWRANK = 16
GLA_TAU = 16.0
FOX_HEADS = 8
FOX_DH = 64
MEM_HEADS = 4
MEM_DH = 128
D_FF = 4 * D_MODEL
GLA_K = GLA_HEADS * GLA_DK
GLA_V = GLA_HEADS * GLA_DV
FOX_W = FOX_HEADS * FOX_DH
MEM_W = MEM_HEADS * MEM_DH

LANES = 128
FOX_ONES_ROWS = 16
FOX_AUG_PER_HEAD = 6
ROW_TILE = 256
FOX_TQ = 512
FOX_TK = 256
FOX_STEPS = 4
TAIL_TILE = 256
FF_CHUNK = 1024
VMEM_LIMIT = 56 * 1024 * 1024
NEG_BIG = -1e30
PRUNE_LOGIT_GAP = 104.0
FOX_MAX_SHIFT_GAP = 50.0

F32 = jnp.float32
BF16 = jnp.bfloat16


def _rms(xf, g):
    r = lax.rsqrt(jnp.mean(xf * xf, axis=-1, keepdims=True) + EPS)
    return (xf * r) * g


def _log_sigmoid(x):
    return jnp.minimum(x, 0.0) - jnp.log1p(jnp.exp(-jnp.abs(x)))


def _sigmoid(x):
    return 1.0 / (1.0 + jnp.exp(-x))


def _dot(a, b):
    return jnp.dot(a, b, preferred_element_type=F32)


def _dot_nt(a, b):
    return lax.dot_general(a, b, (((1,), (1,)), ((), ())), preferred_element_type=F32)


def _dot_tn(a, b):
    return lax.dot_general(a, b, (((0,), (0,)), ((), ())), preferred_element_type=F32)


def _memkv_kernel(mem_ref, g_ref, w_ref, mk_ref, mv_ref):
    mn = _rms(mem_ref[...], g_ref[...]).astype(BF16)
    kv = _dot(mn, w_ref[...])
    mk_ref[...] = kv[:, :MEM_W].astype(BF16)
    mv_ref[...] = kv[:, MEM_W:].astype(BF16)


def _memkv(mem, g_mem, w_mem_kv):
    m = mem.shape[0]
    return pl.pallas_call(
        _memkv_kernel,
        out_shape=(jax.ShapeDtypeStruct((m, MEM_W), BF16), jax.ShapeDtypeStruct((m, MEM_W), BF16)),
        name="memkv",
    )(mem, g_mem, w_mem_kv)


def _proj_kernel(x_ref, gmix_ref, wgla_ref, wfk_ref, wqvt_ref, wmq_ref, wgate_ref, wsmall_ref, wup_ref,
                 balpha_ref, bforget_ref, mk_ref, mv_ref, pqt_ref, pk_ref,
                 gla_ref, loga_ref, qt_ref, kaug_ref, vt_ref, omem_ref, gate_ref, stat_ref, statq_ref,
                 carry_ref):
    tm = x_ref.shape[0]

    @pl.when(pl.program_id(0) == 0)
    def _():
        carry_ref[...] = jnp.zeros_like(carry_ref)

    u = _rms(x_ref[...], gmix_ref[...])
    ub = u.astype(BF16)

    gla_ref[...] = _dot(ub, wgla_ref[...]).astype(BF16)
    fkb = _dot(ub, wfk_ref[...]).astype(BF16)
    gate_ref[...] = _sigmoid(_dot(ub, wgate_ref[...])).astype(BF16)

    qvt = _dot_nt(wqvt_ref[...], ub)
    fqt = (qvt[:FOX_W] * (FOX_DH ** -0.5)).astype(BF16)
    vt = qvt[FOX_W:].astype(BF16)
    ones = jnp.ones((FOX_ONES_ROWS, tm), BF16)
    vt_parts = []
    for h in range(FOX_HEADS):
        vt_parts += [vt[h * FOX_DH:(h + 1) * FOX_DH, :], ones]
    vt_ref[...] = jnp.concatenate(vt_parts, axis=0)

    u_lo = (u - ub.astype(F32)).astype(BF16)
    parts = _dot(jnp.concatenate([ub, u_lo], axis=0), wsmall_ref[...])
    small = (parts[:tm, :LANES] + parts[:tm, LANES:]) + (parts[tm:, :LANES] + parts[tm:, LANES:])
    ga = small[:, :GLA_LOWRANK]
    ga_hi = ga.astype(BF16)
    ga_lo = (ga - ga_hi.astype(F32)).astype(BF16)
    up = _dot(jnp.concatenate([ga_hi, ga_lo], axis=0), wup_ref[...])
    alpha_pre = ((up[:tm, :GLA_K] + up[:tm, GLA_K:]) + (up[tm:, :GLA_K] + up[tm:, GLA_K:])) + balpha_ref[...]
    loga_ref[...] = _log_sigmoid(alpha_pre) * (1.0 / GLA_TAU)

    lane = lax.broadcasted_iota(jnp.int32, (tm, LANES), 1)
    ff_valid = (lane >= GLA_LOWRANK) & (lane < GLA_LOWRANK + FOX_HEADS)
    logf = jnp.where(ff_valid, _log_sigmoid(small + bforget_ref[...]), 0.0)
    logf = pltpu.roll(logf, LANES - GLA_LOWRANK, 1)
    r = lax.broadcasted_iota(jnp.int32, (tm, tm), 0)
    c = lax.broadcasted_iota(jnp.int32, (tm, tm), 1)
    tri = jnp.where(r >= c, 1.0, 0.0).astype(BF16)
    lf_hi = logf.astype(BF16)
    lf_r = logf - lf_hi.astype(F32)
    lf_mid = lf_r.astype(BF16)
    lf_lo = (lf_r - lf_mid.astype(F32)).astype(BF16)
    c3 = _dot(tri, jnp.concatenate([lf_hi, lf_mid, lf_lo], axis=1))
    fcum = (c3[:, :LANES] + c3[:, LANES:2 * LANES] + c3[:, 2 * LANES:]) + carry_ref[...]
    carry_ref[...] = fcum[tm - 1:tm, :]

    f_hi = fcum.astype(BF16)
    rem = fcum - f_hi.astype(F32)
    f_mid = rem.astype(BF16)
    f_lo = (rem - f_mid.astype(F32)).astype(BF16)
    f3 = jnp.concatenate([f_hi, f_mid, f_lo], axis=1)
    def aug_slot(idx):
        a = idx % LANES
        return a % FOX_AUG_PER_HEAD, a < 2 * FOX_AUG_PER_HEAD

    slot_q, in_q = aug_slot(lax.broadcasted_iota(jnp.int32, (FOX_W, 1), 0))
    slot_k, in_k = aug_slot(lax.broadcasted_iota(jnp.int32, (1, FOX_W), 1))
    augqt = _dot_nt(pqt_ref[...], f3) + jnp.where(in_q & (slot_q < 3), -1.0, 0.0)
    augk = _dot(f3, pk_ref[...]) + jnp.where(in_k & (slot_k >= 3), 1.0, 0.0)
    for p in range(FOX_HEADS // 2):
        src = slice(p * LANES, (p + 1) * LANES)
        qt_ref[2 * p * LANES:(2 * p + 1) * LANES, :] = fqt[src]
        qt_ref[(2 * p + 1) * LANES:(2 * p + 2) * LANES, :] = augqt[src].astype(BF16)
        kaug_ref[p, :, :LANES] = fkb[:, src]
        kaug_ref[p, :, LANES:] = augk[:, src].astype(BF16)

    fq32 = fqt.astype(F32) * (FOX_DH ** 0.5)
    nq2 = jnp.sum((fq32 * fq32).reshape(FOX_HEADS, FOX_DH, tm), axis=1)
    statq_ref[0] = jnp.broadcast_to(jnp.max(nq2, axis=1, keepdims=True), (FOX_HEADS, LANES))
    gi = lax.broadcasted_iota(jnp.int32, (FOX_W, LANES), 0) // FOX_DH
    gj = lax.broadcasted_iota(jnp.int32, (FOX_W, LANES), 1)
    group = jnp.where(gi == gj, 1.0, 0.0).astype(BF16)
    fk = fkb.astype(F32)
    nk2 = jnp.max(_dot((fk * fk).astype(BF16), group), axis=0, keepdims=True)
    fmax = jnp.max(fcum, axis=0, keepdims=True)
    fmin = jnp.min(fcum, axis=0, keepdims=True)
    stat_ref[0] = jnp.concatenate([nk2, fmax, fmin, jnp.zeros((5, LANES), F32)], axis=0)

    mq = _dot(ub, wmq_ref[...]).astype(BF16)
    scale = MEM_DH ** -0.5
    outs = []
    for h in range(MEM_HEADS):
        sl = slice(h * MEM_DH, (h + 1) * MEM_DH)
        s = _dot_nt(mq[:, sl], mk_ref[:, sl]) * scale
        m = jnp.max(s, axis=-1, keepdims=True)
        p = jnp.exp(s - m)
        l = jnp.sum(p, axis=-1, keepdims=True)
        p = p / l
        outs.append(_dot(p.astype(BF16), mv_ref[:, sl]))
    omem_ref[...] = jnp.concatenate(outs, axis=-1).astype(BF16)


def _aug_placement():
    pq = np.zeros((3 * LANES, FOX_W), np.float32)
    pk = np.zeros((3 * LANES, FOX_W), np.float32)
    for h in range(FOX_HEADS):
        base = (h // 2) * LANES + (h % 2) * FOX_AUG_PER_HEAD
        for c in range(3):
            pq[c * LANES + h, base + 3 + c] = 1.0
            pk[c * LANES + h, base + c] = 1.0
    return jnp.asarray(pq.T, BF16), jnp.asarray(pk, BF16)


def _proj(x, g_mix, w_gla, w_fk, w_qvt, w_mq, w_gate, w_small, w_up, b_alpha, b_forget, mk, mv):
    s = x.shape[0]
    tm = ROW_TILE
    nt = s // tm
    pqt, pk = _aug_placement()
    full = lambda shape: pl.BlockSpec(shape, lambda i: (0,) * len(shape))
    row = lambda w: pl.BlockSpec((tm, w), lambda i: (i, 0))
    col = lambda r: pl.BlockSpec((r, tm), lambda i: (0, i))
    stat = pl.BlockSpec((1, 8, LANES), lambda i: (i, 0, 0))
    vt_rows = FOX_HEADS * (FOX_DH + FOX_ONES_ROWS)
    return pl.pallas_call(
        _proj_kernel,
        grid=(nt,),
        in_specs=[row(D_MODEL), full((1, D_MODEL)), full(w_gla.shape), full(w_fk.shape), full(w_qvt.shape),
                  full(w_mq.shape), full(w_gate.shape), full(w_small.shape), full(w_up.shape),
                  full(b_alpha.shape), full(b_forget.shape), full(mk.shape), full(mv.shape),
                  full(pqt.shape), full(pk.shape)],
        out_specs=[row(w_gla.shape[1]), row(GLA_K), col(2 * FOX_W),
                   pl.BlockSpec((FOX_HEADS // 2, tm, 2 * LANES), lambda i: (0, i, 0)), col(vt_rows), row(MEM_W),
                   row(w_gate.shape[1]), stat, stat],
        out_shape=[jax.ShapeDtypeStruct((s, w_gla.shape[1]), BF16),
                   jax.ShapeDtypeStruct((s, GLA_K), F32),
                   jax.ShapeDtypeStruct((2 * FOX_W, s), BF16),
                   jax.ShapeDtypeStruct((FOX_HEADS // 2, s, 2 * LANES), BF16),
                   jax.ShapeDtypeStruct((vt_rows, s), BF16),
                   jax.ShapeDtypeStruct((s, MEM_W), BF16),
                   jax.ShapeDtypeStruct((s, w_gate.shape[1]), BF16),
                   jax.ShapeDtypeStruct((nt, 8, LANES), F32),
                   jax.ShapeDtypeStruct((nt, 8, LANES), F32)],
        scratch_shapes=[pltpu.VMEM((1, LANES), F32)],
        compiler_params=pltpu.CompilerParams(dimension_semantics=("arbitrary",), vmem_limit_bytes=VMEM_LIMIT),
        name="proj",
    )(x, g_mix, w_gla, w_fk, w_qvt, w_mq, w_gate, w_small, w_up, b_alpha, b_forget, mk, mv, pqt, pk)


def _gla_kernel(qkvg_ref, loga_ref, ghead_ref, o_ref, state_ref):
    t = qkvg_ref.shape[0]
    nc = t // CHUNK

    @pl.when(pl.program_id(0) == 0)
    def _():
        state_ref[...] = jnp.zeros_like(state_ref)

    r = lax.broadcasted_iota(jnp.int32, (t, t), 0)
    c = lax.broadcasted_iota(jnp.int32, (t, t), 1)
    same = (r // CHUNK) == (c // CHUNK)
    lower = r >= c
    causal = same & lower
    anti = same & (r < c)

    la = loga_ref[...]
    la_hi = la.astype(BF16)
    la_r = la - la_hi.astype(F32)
    la_mid = la_r.astype(BF16)
    la_lo = (la_r - la_mid.astype(F32)).astype(BF16)
    b3 = _dot(jnp.where(causal, 1.0, 0.0).astype(BF16), jnp.concatenate([la_hi, la_mid, la_lo], axis=1))
    b = b3[:, :GLA_K] + b3[:, GLA_K:2 * GLA_K] + b3[:, 2 * GLA_K:]
    b_last = jnp.concatenate(
        [jnp.broadcast_to(b[(ci + 1) * CHUNK - 1:(ci + 1) * CHUNK, :], (CHUNK, GLA_K)) for ci in range(nc)], axis=0)
    e_pos = jnp.exp(b)
    e_neg = jnp.exp(-b)
    q = qkvg_ref[:, 0:GLA_K].astype(F32) * (GLA_DK ** -0.5)
    k = qkvg_ref[:, GLA_K:2 * GLA_K].astype(F32)
    q_pos = (q * e_pos).astype(BF16)
    q_neg = (q * e_neg).astype(BF16)
    k_pos = (k * e_pos).astype(BF16)
    k_neg = (k * e_neg).astype(BF16)
    k_dec = (k * jnp.exp(b_last - b)).astype(BF16)

    lane = lax.broadcasted_iota(jnp.int32, (1, LANES), 1)
    for h in range(GLA_HEADS):
        pair = slice((h // 2) * LANES, (h // 2 + 1) * LANES)
        in_head = (lane // GLA_DK) == (h % 2)
        vsl = slice(2 * GLA_K + h * GLA_DV, 2 * GLA_K + (h + 1) * GLA_DV)
        gsl = slice(2 * GLA_K + GLA_V + h * GLA_DV, 2 * GLA_K + GLA_V + (h + 1) * GLA_DV)
        v = qkvg_ref[:, vsl]
        qp = jnp.where(in_head, q_pos[:, pair], jnp.zeros((), BF16))
        qn = jnp.where(in_head, q_neg[:, pair], jnp.zeros((), BF16))
        a_c = _dot_nt(qp, k_neg[:, pair])
        a_a = _dot_nt(qn, k_pos[:, pair])
        attn = jnp.where(causal, a_c, jnp.where(anti, a_a, 0.0)).astype(BF16)
        o_intra = _dot(attn, v)

        st = state_ref[h]
        inter = []
        for ci in range(nc):
            rows = slice(ci * CHUNK, (ci + 1) * CHUNK)
            inter.append(_dot_nt(qp[rows], st.astype(BF16)))
            kv_t = _dot_tn(v[rows], k_dec[rows, pair])
            dec = e_pos[(ci + 1) * CHUNK - 1:(ci + 1) * CHUNK, pair]
            st = st * dec + jnp.where(in_head, kv_t, 0.0)
        state_ref[h] = st
        o = o_intra + jnp.concatenate(inter, axis=0)

        g = ghead_ref[:, h * GLA_DV:(h + 1) * GLA_DV]
        on = _rms(o, g)
        gg = qkvg_ref[:, gsl].astype(F32)
        o_ref[:, h * GLA_DV:(h + 1) * GLA_DV] = (on * (gg * _sigmoid(gg))).astype(BF16)


def _gla(qkvg, loga, ghead):
    s = qkvg.shape[0]
    t = ROW_TILE
    return pl.pallas_call(
        _gla_kernel,
        grid=(s // t,),
        in_specs=[pl.BlockSpec((t, qkvg.shape[1]), lambda i: (i, 0)),
                  pl.BlockSpec((t, GLA_K), lambda i: (i, 0)),
                  pl.BlockSpec((1, GLA_V), lambda i: (0, 0))],
        out_specs=pl.BlockSpec((t, GLA_V), lambda i: (i, 0)),
        out_shape=jax.ShapeDtypeStruct((s, GLA_V), BF16),
        scratch_shapes=[pltpu.VMEM((GLA_HEADS, GLA_DV, LANES), F32)],
        compiler_params=pltpu.CompilerParams(dimension_semantics=("arbitrary",), vmem_limit_bytes=VMEM_LIMIT),
        name="gla",
    )(qkvg, loga, ghead)


def _fox_kernel(thr_ref, fmin_ref, shift_ref, qt_ref, k_ref, vt_ref, o_ref,
                sa_ref, sb_ref, pa_ref, pb_ref, m_ref, acc_ref, si_ref, sj_ref):
    p = pl.program_id(0)
    tq = FOX_TQ
    tk = FOX_TK
    nsub = tq // tk
    nq = k_ref.shape[0] // tq
    hrows = FOX_DH + FOX_ONES_ROWS
    row = lax.broadcasted_iota(jnp.int32, (2 * LANES, 1), 0)
    krow = lax.broadcasted_iota(jnp.int32, (tk, tq), 0)
    qcol = lax.broadcasted_iota(jnp.int32, (tk, tq), 1)
    m0 = jnp.full((1, tq), NEG_BIG, F32)
    acc0 = jnp.zeros((hrows, tq), F32)

    for hi in range(2):
        h = 2 * p + hi
        aug0 = LANES + hi * FOX_AUG_PER_HEAD
        mine = ((row // FOX_DH) == hi) | ((row >= aug0) & (row < aug0 + FOX_AUG_PER_HEAD))
        vrows = slice(hi * hrows, (hi + 1) * hrows)

        def score_tiles(qi, kj, masked):
            q0 = pl.multiple_of(qi * tq, tq)
            qm = jnp.where(mine, qt_ref[:, pl.ds(q0, tq)], jnp.zeros((), BF16))
            for u in range(nsub):
                k0 = pl.multiple_of(kj * tq + u * tk, tk)
                s_t = _dot(k_ref[pl.ds(k0, tk), :], qm)
                if masked:
                    s_t = jnp.where(krow + u * tk <= qcol, s_t, NEG_BIG)
                yield u, s_t

        def values(kj, u):
            return vt_ref[vrows, pl.ds(pl.multiple_of(kj * tq + u * tk, tk), tk)]

        def fast_produce(qi, kj, buf, masked=False):
            shift = shift_ref[h, qi]
            for u, s_t in score_tiles(qi, kj, masked):
                buf[u] = jnp.exp(s_t - shift).astype(BF16)

        def fast_consume(i, kj, buf, fresh=False):
            acc = acc0 if fresh else acc_ref[hi, i]
            for u in range(nsub):
                acc = acc + _dot(values(kj, u), buf[u])
            acc_ref[hi, i] = acc

        def slow_produce(qi, kj, buf, masked=False):
            for u, s_t in score_tiles(qi, kj, masked):
                buf[u] = s_t

        def slow_consume(i, kj, buf, fresh=False):
            m = m0 if fresh else m_ref[hi, i]
            acc = acc0 if fresh else acc_ref[hi, i]
            for u in range(nsub):
                s_t = buf[u]
                m_new = jnp.maximum(m, jnp.max(s_t, axis=0, keepdims=True))
                alpha = jnp.exp(m - m_new)
                pt = jnp.exp(s_t - m_new).astype(BF16)
                acc = alpha * acc + _dot(values(kj, u), pt)
                m = m_new
            m_ref[hi, i] = m
            acc_ref[hi, i] = acc

        def list_block(i, carry):
            t, worst = carry
            thr = thr_ref[h, i]
            j_end = lax.while_loop(lambda j: (j >= 0) & (fmin_ref[h, jnp.maximum(j, 0)] <= thr),
                                   lambda j: j - 1, i - 1)

            def emit(j, t):
                si_ref[t] = i
                sj_ref[t] = j
                return t + 1

            return lax.fori_loop(j_end + 1, i, emit, t), jnp.maximum(worst, shift_ref[h, i])

        n_tiles, worst_shift = lax.fori_loop(0, nq, list_block, (0, jnp.float32(0.0)))
        for pad in range(FOX_STEPS):
            si_ref[n_tiles + pad] = nq
            sj_ref[n_tiles + pad] = 0

        def run_head(produce, consume, buf_a, buf_b):
            def diag_step(t, carry):
                i0 = 2 * t
                produce(i0, i0, buf_a, masked=True)
                produce(i0 + 1, i0 + 1, buf_b, masked=True)
                consume(i0, i0, buf_a, fresh=True)
                consume(i0 + 1, i0 + 1, buf_b, fresh=True)
                return carry

            lax.fori_loop(0, nq // 2, diag_step, 0)
            m_ref[hi, nq] = m0
            acc_ref[hi, nq] = acc0

            def prefetch(t, buf):
                produce(jnp.minimum(si_ref[t], nq - 1), sj_ref[t], buf)

            prefetch(0, buf_a)

            def multi_step(tn, carry):
                t = FOX_STEPS * tn
                for d in range(0, FOX_STEPS, 2):
                    prefetch(t + d + 1, buf_b)
                    consume(si_ref[t + d], sj_ref[t + d], buf_a)
                    prefetch(t + d + 2, buf_a)
                    consume(si_ref[t + d + 1], sj_ref[t + d + 1], buf_b)
                return carry

            lax.fori_loop(0, (n_tiles + FOX_STEPS - 1) // FOX_STEPS, multi_step, 0)

        no_running_max = 2.0 * worst_shift <= FOX_MAX_SHIFT_GAP

        @pl.when(no_running_max)
        def _():
            run_head(fast_produce, fast_consume, pa_ref, pb_ref)

        @pl.when(jnp.logical_not(no_running_max))
        def _():
            run_head(slow_produce, slow_consume, sa_ref, sb_ref)

    def finish(i, carry):
        a0 = acc_ref[0, i]
        a1 = acc_ref[1, i]
        ot = jnp.concatenate([a0[:FOX_DH] / a0[FOX_DH:FOX_DH + 1], a1[:FOX_DH] / a1[FOX_DH:FOX_DH + 1]], axis=0)
        o_ref[pl.ds(pl.multiple_of(i * tq, tq), tq), :] = ot.T.astype(BF16)
        return carry

    lax.fori_loop(0, nq, finish, 0)


def _fox(thr, fmin_blk, shift, qt, kaug, vt):
    s = kaug.shape[1]
    tq = FOX_TQ
    nq = s // tq
    npair = FOX_HEADS // 2
    prow = 2 * (FOX_DH + FOX_ONES_ROWS)
    assert nq % 2 == 0
    hrows = FOX_DH + FOX_ONES_ROWS
    max_tiles = nq * (nq - 1) // 2 + FOX_STEPS
    once = pl.Buffered(1)
    grid_spec = pltpu.PrefetchScalarGridSpec(
        num_scalar_prefetch=3,
        grid=(npair,),
        in_specs=[pl.BlockSpec((2 * LANES, s), lambda p, *_: (p, 0), pipeline_mode=once),
                  pl.BlockSpec((None, s, 2 * LANES), lambda p, *_: (p, 0, 0), pipeline_mode=once),
                  pl.BlockSpec((prow, s), lambda p, *_: (p, 0), pipeline_mode=once)],
        out_specs=pl.BlockSpec((None, s, LANES), lambda p, *_: (p, 0, 0)),
        scratch_shapes=[pltpu.VMEM((tq // FOX_TK, FOX_TK, tq), F32), pltpu.VMEM((tq // FOX_TK, FOX_TK, tq), F32),
                        pltpu.VMEM((tq // FOX_TK, FOX_TK, tq), BF16), pltpu.VMEM((tq // FOX_TK, FOX_TK, tq), BF16),
                        pltpu.VMEM((2, nq + 1, 1, tq), F32), pltpu.VMEM((2, nq + 1, hrows, tq), F32),
                        pltpu.SMEM((max_tiles,), jnp.int32), pltpu.SMEM((max_tiles,), jnp.int32)],
    )
    return pl.pallas_call(
        _fox_kernel,
        grid_spec=grid_spec,
        out_shape=jax.ShapeDtypeStruct((npair, s, LANES), BF16),
        compiler_params=pltpu.CompilerParams(dimension_semantics=("arbitrary",), vmem_limit_bytes=VMEM_LIMIT),
        name="fox",
    )(thr, fmin_blk, shift, qt, kaug, vt)


def _tail_kernel(x_ref, ogla_ref, ofox_ref, omem_ref, gate_ref, wg_ref, wf_ref, wm_ref, wo_ref,
                 gffn_ref, w1_ref, w2_ref, gfin_ref, out_ref):
    gate = gate_ref[...]
    ofox = jnp.concatenate([ofox_ref[p] for p in range(FOX_HEADS // 2)], axis=1)
    merged = (gate[:, 0:D_MODEL].astype(F32) * _dot(ogla_ref[...], wg_ref[...])
              + gate[:, D_MODEL:2 * D_MODEL].astype(F32) * _dot(ofox, wf_ref[...])
              + gate[:, 2 * D_MODEL:3 * D_MODEL].astype(F32) * _dot(omem_ref[...], wm_ref[...]))
    h = x_ref[...] + _dot(merged.astype(BF16), wo_ref[...])
    u2 = _rms(h, gffn_ref[...]).astype(BF16)
    acc = jnp.zeros_like(h)
    for cidx in range(D_FF // FF_CHUNK):
        cs = slice(cidx * FF_CHUNK, (cidx + 1) * FF_CHUNK)
        a = jnp.maximum(_dot(u2, w1_ref[:, cs]), 0.0)
        acc = acc + _dot((a * a).astype(BF16), w2_ref[cs, :])
    out_ref[...] = _rms(h + acc, gfin_ref[...])


def _tail(x, ogla, ofox, omem, gate, wg, wf, wm, wo, gffn, w1, w2, gfin):
    s = x.shape[0]
    tm = TAIL_TILE
    full = lambda a: pl.BlockSpec(a.shape, lambda i: (0,) * a.ndim, pipeline_mode=pl.Buffered(1))
    row = lambda w: pl.BlockSpec((tm, w), lambda i: (i, 0))
    return pl.pallas_call(
        _tail_kernel,
        grid=(s // tm,),
        in_specs=[row(D_MODEL), row(GLA_V), pl.BlockSpec((FOX_HEADS // 2, tm, LANES), lambda i: (0, i, 0)),
                  row(MEM_W), row(3 * D_MODEL),
                  full(wg), full(wf), full(wm), full(wo), full(gffn), full(w1), full(w2), full(gfin)],
        out_specs=row(D_MODEL),
        out_shape=jax.ShapeDtypeStruct((s, D_MODEL), F32),
        compiler_params=pltpu.CompilerParams(dimension_semantics=("arbitrary",), vmem_limit_bytes=VMEM_LIMIT),
        name="tail",
    )(x, ogla, ofox, omem, gate, wg, wf, wm, wo, gffn, w1, w2, gfin)


def kernel(x, mem, g_mix, w_in, w_alpha_up, b_alpha, b_forget, g_gla_head, g_mem, w_mem_kv,
           w_gla_o, w_fox_o, w_mem_o, w_out, g_ffn, w_ff1, w_ff2, g_final):
    assert x.shape[0] == 1 and g_mix.shape[0] == 1, "single batch, single layer"
    s = x.shape[1]
    assert s % ROW_TILE == 0 and s % TAIL_TILE == 0 and s % FOX_TQ == 0
    assert FOX_TQ % FOX_TK == 0 and FOX_TQ % ROW_TILE == 0
    xs = x[0]
    w = w_in[0]

    o_ga = 2 * GLA_K + 2 * GLA_V
    o_fox = o_ga + GLA_LOWRANK
    o_ff = o_fox + 3 * FOX_W
    o_mq = o_ff + FOX_HEADS
    o_gate = o_mq + MEM_W
    w_gla = w[:, :o_ga].astype(BF16)
    w_fk = w[:, o_fox + FOX_W:o_fox + 2 * FOX_W].astype(BF16)
    w_qvt = jnp.concatenate([w[:, o_fox:o_fox + FOX_W], w[:, o_fox + 2 * FOX_W:o_ff]], axis=1).T.astype(BF16)
    w_mq = w[:, o_mq:o_gate].astype(BF16)
    w_gate = w[:, o_gate:].astype(BF16)
    w_small = jnp.concatenate(
        [w[:, o_ga:o_fox], w[:, o_ff:o_mq], jnp.zeros((D_MODEL, LANES - GLA_LOWRANK - FOX_HEADS), F32)], axis=1)
    w_small_hi = w_small.astype(BF16)
    w_small = jnp.concatenate([w_small_hi, (w_small - w_small_hi.astype(F32)).astype(BF16)], axis=1)
    w_up_hi = w_alpha_up[0].astype(BF16)
    w_up = jnp.concatenate([w_up_hi, (w_alpha_up[0] - w_up_hi.astype(F32)).astype(BF16)], axis=1)
    b_f = jnp.zeros((1, LANES), F32).at[0, GLA_LOWRANK:GLA_LOWRANK + FOX_HEADS].set(b_forget[0])

    mk, mv = _memkv(mem[0], g_mem, w_mem_kv[0].astype(BF16))
    gla_qkvg, loga, qt, kaug, vt, omem, gate, stats, statq = _proj(
        xs, g_mix, w_gla, w_fk, w_qvt, w_mq, w_gate, w_small, w_up, b_alpha, b_f, mk, mv)

    ogla = _gla(gla_qkvg, loga, g_gla_head.reshape(1, GLA_V))

    per_blk = lambda a: a.reshape(s // FOX_TQ, FOX_TQ // ROW_TILE, FOX_HEADS)
    qn = jnp.sqrt(jnp.max(per_blk(statq[:, :, 0]), axis=1))
    kn = jnp.sqrt(jnp.max(stats[:, 0, :FOX_HEADS], axis=0))
    fmax = jnp.max(per_blk(stats[:, 1, :FOX_HEADS]), axis=1)
    fmin = jnp.min(per_blk(stats[:, 2, :FOX_HEADS]), axis=1)
    shift = (1.02 * (FOX_DH ** -0.5)) * qn * kn[None, :]
    thr = 2.0 * shift + fmax + PRUNE_LOGIT_GAP
    ofox = _fox(thr.T, fmin.T, shift.T, qt, kaug, vt)

    out = _tail(xs, ogla, ofox, omem, gate, w_gla_o[0].astype(BF16), w_fox_o[0].astype(BF16),
                w_mem_o[0].astype(BF16), w_out[0].astype(BF16), g_ffn, w_ff1[0].astype(BF16),
                w_ff2[0].astype(BF16), g_final.reshape(1, D_MODEL))
    return out[None]
```

```python
import functools

import jax
import jax.numpy as jnp
import numpy as np
from jax import lax
from jax.experimental import pallas as pl
from jax.experimental.pallas import tpu as pltpu

D_MODEL = 1024
CHUNK = 64
EPS = 1e-6
GLA_HEADS = 4
GLA_DK = 64
GLA_DV = 128
GLA_LOWRANK = 16
GLA_TAU = 16.0
FOX_HEADS = 8
FOX_DH = 64
MEM_HEADS = 4
MEM_DH = 128
D_FF = 4 * D_MODEL
GLA_K = GLA_HEADS * GLA_DK
GLA_V = GLA_HEADS * GLA_DV
FOX_W = FOX_HEADS * FOX_DH
MEM_W = MEM_HEADS * MEM_DH

LANES = 128
FOX_ONES_ROWS = 16
FOX_AUG_PER_HEAD = 6
ROW_TILE = 256
FOX_TQ = 512
FOX_TK = 256
FOX_STEPS = 8
TAIL_TILE = 256
FF_CHUNK = 1024
VMEM_LIMIT = 56 * 1024 * 1024
NEG_BIG = -1e30
PRUNE_LOGIT_GAP = 104.0
FOX_MAX_SHIFT_GAP = 50.0

F32 = jnp.float32
BF16 = jnp.bfloat16


def _rms(xf, g):
    r = lax.rsqrt(jnp.mean(xf * xf, axis=-1, keepdims=True) + EPS)
    return (xf * r) * g


def _log_sigmoid(x):
    return jnp.minimum(x, 0.0) - jnp.log1p(jnp.exp(-jnp.abs(x)))


def _sigmoid(x):
    return 1.0 / (1.0 + jnp.exp(-x))


def _dot(a, b):
    return jnp.dot(a, b, preferred_element_type=F32)


def _dot_nt(a, b):
    return lax.dot_general(a, b, (((1,), (1,)), ((), ())), preferred_element_type=F32)


def _dot_tn(a, b):
    return lax.dot_general(a, b, (((0,), (0,)), ((), ())), preferred_element_type=F32)


def _memkv_kernel(mem_ref, g_ref, w_ref, mk_ref, mv_ref):
    mn = _rms(mem_ref[...], g_ref[...]).astype(BF16)
    kv = _dot(mn, w_ref[...])
    mk_ref[...] = kv[:, :MEM_W].astype(BF16)
    mv_ref[...] = kv[:, MEM_W:].astype(BF16)


def _memkv(mem, g_mem, w_mem_kv):
    m = mem.shape[0]
    return pl.pallas_call(
        _memkv_kernel,
        out_shape=(jax.ShapeDtypeStruct((m, MEM_W), BF16), jax.ShapeDtypeStruct((m, MEM_W), BF16)),
        name="memkv",
    )(mem, g_mem, w_mem_kv)


def _proj_kernel(x_ref, gmix_ref, wgla_ref, wfk_ref, wqvt_ref, wmq_ref, wgate_ref, wsmall_ref, wup_ref,
                 balpha_ref, bforget_ref, mk_ref, mv_ref, pqt_ref, pk_ref,
                 gla_ref, loga_ref, qt_ref, kaug_ref, vt_ref, omem_ref, gate_ref, stat_ref, statq_ref,
                 carry_ref):
    tm = x_ref.shape[0]

    @pl.when(pl.program_id(0) == 0)
    def _():
        carry_ref[...] = jnp.zeros_like(carry_ref)

    u = _rms(x_ref[...], gmix_ref[...])
    ub = u.astype(BF16)

    gla_ref[...] = _dot(ub, wgla_ref[...]).astype(BF16)
    fkb = _dot(ub, wfk_ref[...]).astype(BF16)
    gate_ref[...] = _sigmoid(_dot(ub, wgate_ref[...])).astype(BF16)

    qvt = _dot_nt(wqvt_ref[...], ub)
    fqt = (qvt[:FOX_W] * (FOX_DH ** -0.5)).astype(BF16)
    vt = qvt[FOX_W:].astype(BF16)
    ones = jnp.ones((FOX_ONES_ROWS, tm), BF16)
    vt_parts = []
    for h in range(FOX_HEADS):
        vt_parts += [vt[h * FOX_DH:(h + 1) * FOX_DH, :], ones]
    vt_ref[...] = jnp.concatenate(vt_parts, axis=0)

    u_lo = (u - ub.astype(F32)).astype(BF16)
    parts = _dot(jnp.concatenate([ub, u_lo], axis=0), wsmall_ref[...])
    small = (parts[:tm, :LANES] + parts[:tm, LANES:]) + (parts[tm:, :LANES] + parts[tm:, LANES:])
    ga = small[:, :GLA_LOWRANK]
    ga_hi = ga.astype(BF16)
    ga_lo = (ga - ga_hi.astype(F32)).astype(BF16)
    up = _dot(jnp.concatenate([ga_hi, ga_lo], axis=0), wup_ref[...])
    alpha_pre = ((up[:tm, :GLA_K] + up[:tm, GLA_K:]) + (up[tm:, :GLA_K] + up[tm:, GLA_K:])) + balpha_ref[...]
    loga_ref[...] = _log_sigmoid(alpha_pre) * (1.0 / GLA_TAU)

    lane = lax.broadcasted_iota(jnp.int32, (tm, LANES), 1)
    ff_valid = (lane >= GLA_LOWRANK) & (lane < GLA_LOWRANK + FOX_HEADS)
    logf = jnp.where(ff_valid, _log_sigmoid(small + bforget_ref[...]), 0.0)
    logf = pltpu.roll(logf, LANES - GLA_LOWRANK, 1)
    r = lax.broadcasted_iota(jnp.int32, (tm, tm), 0)
    c = lax.broadcasted_iota(jnp.int32, (tm, tm), 1)
    tri = jnp.where(r >= c, 1.0, 0.0).astype(BF16)
    lf_hi = logf.astype(BF16)
    lf_r = logf - lf_hi.astype(F32)
    lf_mid = lf_r.astype(BF16)
    lf_lo = (lf_r - lf_mid.astype(F32)).astype(BF16)
    c3 = _dot(tri, jnp.concatenate([lf_hi, lf_mid, lf_lo], axis=1))
    fcum = (c3[:, :LANES] + c3[:, LANES:2 * LANES] + c3[:, 2 * LANES:]) + carry_ref[...]
    carry_ref[...] = fcum[tm - 1:tm, :]

    f_hi = fcum.astype(BF16)
    rem = fcum - f_hi.astype(F32)
    f_mid = rem.astype(BF16)
    f_lo = (rem - f_mid.astype(F32)).astype(BF16)
    f3 = jnp.concatenate([f_hi, f_mid, f_lo], axis=1)
    def aug_slot(idx):
        a = idx % LANES
        return a % FOX_AUG_PER_HEAD, a < 2 * FOX_AUG_PER_HEAD

    slot_q, in_q = aug_slot(lax.broadcasted_iota(jnp.int32, (FOX_W, 1), 0))
    slot_k, in_k = aug_slot(lax.broadcasted_iota(jnp.int32, (1, FOX_W), 1))
    augqt = _dot_nt(pqt_ref[...], f3) + jnp.where(in_q & (slot_q < 3), -1.0, 0.0)
    augk = _dot(f3, pk_ref[...]) + jnp.where(in_k & (slot_k >= 3), 1.0, 0.0)
    for p in range(FOX_HEADS // 2):
        src = slice(p * LANES, (p + 1) * LANES)
        qt_ref[2 * p * LANES:(2 * p + 1) * LANES, :] = fqt[src]
        qt_ref[(2 * p + 1) * LANES:(2 * p + 2) * LANES, :] = augqt[src].astype(BF16)
        kaug_ref[p, :, :LANES] = fkb[:, src]
        kaug_ref[p, :, LANES:] = augk[:, src].astype(BF16)

    fq32 = fqt.astype(F32) * (FOX_DH ** 0.5)
    nq2 = jnp.sum((fq32 * fq32).reshape(FOX_HEADS, FOX_DH, tm), axis=1)
    statq_ref[0] = jnp.broadcast_to(jnp.max(nq2, axis=1, keepdims=True), (FOX_HEADS, LANES))
    gi = lax.broadcasted_iota(jnp.int32, (FOX_W, LANES), 0) // FOX_DH
    gj = lax.broadcasted_iota(jnp.int32, (FOX_W, LANES), 1)
    group = jnp.where(gi == gj, 1.0, 0.0).astype(BF16)
    fk = fkb.astype(F32)
    nk2 = jnp.max(_dot((fk * fk).astype(BF16), group), axis=0, keepdims=True)
    fmax = jnp.max(fcum, axis=0, keepdims=True)
    fmin = jnp.min(fcum, axis=0, keepdims=True)
    stat_ref[0] = jnp.concatenate([nk2, fmax, fmin, jnp.zeros((5, LANES), F32)], axis=0)

    mq = _dot(ub, wmq_ref[...]).astype(BF16)
    scale = MEM_DH ** -0.5
    outs = []
    for h in range(MEM_HEADS):
        sl = slice(h * MEM_DH, (h + 1) * MEM_DH)
        s = _dot_nt(mq[:, sl], mk_ref[:, sl]) * scale
        m = jnp.max(s, axis=-1, keepdims=True)
        p = jnp.exp(s - m)
        l = jnp.sum(p, axis=-1, keepdims=True)
        p = p / l
        outs.append(_dot(p.astype(BF16), mv_ref[:, sl]))
    omem_ref[...] = jnp.concatenate(outs, axis=-1).astype(BF16)


def _aug_placement():
    pq = np.zeros((3 * LANES, FOX_W), np.float32)
    pk = np.zeros((3 * LANES, FOX_W), np.float32)
    for h in range(FOX_HEADS):
        base = (h // 2) * LANES + (h % 2) * FOX_AUG_PER_HEAD
        for c in range(3):
            pq[c * LANES + h, base + 3 + c] = 1.0
            pk[c * LANES + h, base + c] = 1.0
    return jnp.asarray(pq.T, BF16), jnp.asarray(pk, BF16)


def _proj(x, g_mix, w_gla, w_fk, w_qvt, w_mq, w_gate, w_small, w_up, b_alpha, b_forget, mk, mv):
    s = x.shape[0]
    tm = ROW_TILE
    nt = s // tm
    pqt, pk = _aug_placement()
    full = lambda shape: pl.BlockSpec(shape, lambda i: (0,) * len(shape))
    row = lambda w: pl.BlockSpec((tm, w), lambda i: (i, 0))
    col = lambda r: pl.BlockSpec((r, tm), lambda i: (0, i))
    stat = pl.BlockSpec((1, 8, LANES), lambda i: (i, 0, 0))
    vt_rows = FOX_HEADS * (FOX_DH + FOX_ONES_ROWS)
    return pl.pallas_call(
        _proj_kernel,
        grid=(nt,),
        in_specs=[row(D_MODEL), full((1, D_MODEL)), full(w_gla.shape), full(w_fk.shape), full(w_qvt.shape),
                  full(w_mq.shape), full(w_gate.shape), full(w_small.shape), full(w_up.shape),
                  full(b_alpha.shape), full(b_forget.shape), full(mk.shape), full(mv.shape),
                  full(pqt.shape), full(pk.shape)],
        out_specs=[row(w_gla.shape[1]), row(GLA_K), col(2 * FOX_W),
                   pl.BlockSpec((FOX_HEADS // 2, tm, 2 * LANES), lambda i: (0, i, 0)), col(vt_rows), row(MEM_W),
                   row(w_gate.shape[1]), stat, stat],
        out_shape=[jax.ShapeDtypeStruct((s, w_gla.shape[1]), BF16),
                   jax.ShapeDtypeStruct((s, GLA_K), F32),
                   jax.ShapeDtypeStruct((2 * FOX_W, s), BF16),
                   jax.ShapeDtypeStruct((FOX_HEADS // 2, s, 2 * LANES), BF16),
                   jax.ShapeDtypeStruct((vt_rows, s), BF16),
                   jax.ShapeDtypeStruct((s, MEM_W), BF16),
                   jax.ShapeDtypeStruct((s, w_gate.shape[1]), BF16),
                   jax.ShapeDtypeStruct((nt, 8, LANES), F32),
                   jax.ShapeDtypeStruct((nt, 8, LANES), F32)],
        scratch_shapes=[pltpu.VMEM((1, LANES), F32)],
        compiler_params=pltpu.CompilerParams(dimension_semantics=("arbitrary",), vmem_limit_bytes=VMEM_LIMIT),
        name="proj",
    )(x, g_mix, w_gla, w_fk, w_qvt, w_mq, w_gate, w_small, w_up, b_alpha, b_forget, mk, mv, pqt, pk)


def _gla_kernel(qkvg_ref, loga_ref, ghead_ref, o_ref, state_ref):
    t = qkvg_ref.shape[0]
    nc = t // CHUNK

    @pl.when(pl.program_id(0) == 0)
    def _():
        state_ref[...] = jnp.zeros_like(state_ref)

    r = lax.broadcasted_iota(jnp.int32, (t, t), 0)
    c = lax.broadcasted_iota(jnp.int32, (t, t), 1)
    same = (r // CHUNK) == (c // CHUNK)
    lower = r >= c
    causal = same & lower
    anti = same & (r < c)

    la = loga_ref[...]
    la_hi = la.astype(BF16)
    la_r = la - la_hi.astype(F32)
    la_mid = la_r.astype(BF16)
    la_lo = (la_r - la_mid.astype(F32)).astype(BF16)
    b3 = _dot(jnp.where(causal, 1.0, 0.0).astype(BF16), jnp.concatenate([la_hi, la_mid, la_lo], axis=1))
    b = b3[:, :GLA_K] + b3[:, GLA_K:2 * GLA_K] + b3[:, 2 * GLA_K:]
    b_last = jnp.concatenate(
        [jnp.broadcast_to(b[(ci + 1) * CHUNK - 1:(ci + 1) * CHUNK, :], (CHUNK, GLA_K)) for ci in range(nc)], axis=0)
    e_pos = jnp.exp(b)
    e_neg = jnp.exp(-b)
    q = qkvg_ref[:, 0:GLA_K].astype(F32) * (GLA_DK ** -0.5)
    k = qkvg_ref[:, GLA_K:2 * GLA_K].astype(F32)
    q_pos = (q * e_pos).astype(BF16)
    q_neg = (q * e_neg).astype(BF16)
    k_pos = (k * e_pos).astype(BF16)
    k_neg = (k * e_neg).astype(BF16)
    k_dec = (k * jnp.exp(b_last - b)).astype(BF16)

    lane = lax.broadcasted_iota(jnp.int32, (1, LANES), 1)
    for h in range(GLA_HEADS):
        pair = slice((h // 2) * LANES, (h // 2 + 1) * LANES)
        in_head = (lane // GLA_DK) == (h % 2)
        vsl = slice(2 * GLA_K + h * GLA_DV, 2 * GLA_K + (h + 1) * GLA_DV)
        gsl = slice(2 * GLA_K + GLA_V + h * GLA_DV, 2 * GLA_K + GLA_V + (h + 1) * GLA_DV)
        v = qkvg_ref[:, vsl]
        qp = jnp.where(in_head, q_pos[:, pair], jnp.zeros((), BF16))
        qn = jnp.where(in_head, q_neg[:, pair], jnp.zeros((), BF16))
        a_c = _dot_nt(qp, k_neg[:, pair])
        a_a = _dot_nt(qn, k_pos[:, pair])
        attn = jnp.where(causal, a_c, jnp.where(anti, a_a, 0.0)).astype(BF16)
        o_intra = _dot(attn, v)

        st = state_ref[h]
        inter = []
        for ci in range(nc):
            rows = slice(ci * CHUNK, (ci + 1) * CHUNK)
            inter.append(_dot_nt(qp[rows], st.astype(BF16)))
            kv_t = _dot_tn(v[rows], k_dec[rows, pair])
            dec = e_pos[(ci + 1) * CHUNK - 1:(ci + 1) * CHUNK, pair]
            st = st * dec + jnp.where(in_head, kv_t, 0.0)
        state_ref[h] = st
        o = o_intra + jnp.concatenate(inter, axis=0)

        g = ghead_ref[:, h * GLA_DV:(h + 1) * GLA_DV]
        on = _rms(o, g)
        gg = qkvg_ref[:, gsl].astype(F32)
        o_ref[:, h * GLA_DV:(h + 1) * GLA_DV] = (on * (gg * _sigmoid(gg))).astype(BF16)


def _gla(qkvg, loga, ghead):
    s = qkvg.shape[0]
    t = ROW_TILE
    return pl.pallas_call(
        _gla_kernel,
        grid=(s // t,),
        in_specs=[pl.BlockSpec((t, qkvg.shape[1]), lambda i: (i, 0)),
                  pl.BlockSpec((t, GLA_K), lambda i: (i, 0)),
                  pl.BlockSpec((1, GLA_V), lambda i: (0, 0))],
        out_specs=pl.BlockSpec((t, GLA_V), lambda i: (i, 0)),
        out_shape=jax.ShapeDtypeStruct((s, GLA_V), BF16),
        scratch_shapes=[pltpu.VMEM((GLA_HEADS, GLA_DV, LANES), F32)],
        compiler_params=pltpu.CompilerParams(dimension_semantics=("arbitrary",), vmem_limit_bytes=VMEM_LIMIT),
        name="gla",
    )(qkvg, loga, ghead)


def _fox_kernel(thr_ref, fmin_ref, shift_ref, qt_ref, k_ref, vt_ref, o_ref,
                sa_ref, sb_ref, pa_ref, pb_ref, m_ref, acc_ref, si_ref, sj_ref):
    p = pl.program_id(0)
    tq = FOX_TQ
    tk = FOX_TK
    nsub = tq // tk
    nq = k_ref.shape[0] // tq
    hrows = FOX_DH + FOX_ONES_ROWS
    row = lax.broadcasted_iota(jnp.int32, (2 * LANES, 1), 0)
    krow = lax.broadcasted_iota(jnp.int32, (tk, tq), 0)
    qcol = lax.broadcasted_iota(jnp.int32, (tk, tq), 1)
    m0 = jnp.full((1, tq), NEG_BIG, F32)
    acc0 = jnp.zeros((hrows, tq), F32)

    for hi in range(2):
        h = 2 * p + hi
        aug0 = LANES + hi * FOX_AUG_PER_HEAD
        mine = ((row // FOX_DH) == hi) | ((row >= aug0) & (row < aug0 + FOX_AUG_PER_HEAD))
        vrows = slice(hi * hrows, (hi + 1) * hrows)

        def score_tiles(qi, kj, masked):
            q0 = pl.multiple_of(qi * tq, tq)
            qm = jnp.where(mine, qt_ref[:, pl.ds(q0, tq)], jnp.zeros((), BF16))
            for u in range(nsub):
                k0 = pl.multiple_of(kj * tq + u * tk, tk)
                s_t = _dot(k_ref[pl.ds(k0, tk), :], qm)
                if masked:
                    s_t = jnp.where(krow + u * tk <= qcol, s_t, NEG_BIG)
                yield u, s_t

        def values(kj, u):
            return vt_ref[vrows, pl.ds(pl.multiple_of(kj * tq + u * tk, tk), tk)]

        def fast_produce(qi, kj, buf, masked=False):
            shift = shift_ref[h, qi]
            for u, s_t in score_tiles(qi, kj, masked):
                buf[u] = jnp.exp(s_t - shift).astype(BF16)

        def fast_consume(i, kj, buf, fresh=False):
            acc = acc0 if fresh else acc_ref[hi, i]
            for u in range(nsub):
                acc = acc + _dot(values(kj, u), buf[u])
            acc_ref[hi, i] = acc

        def slow_produce(qi, kj, buf, masked=False):
            for u, s_t in score_tiles(qi, kj, masked):
                buf[u] = s_t

        def slow_consume(i, kj, buf, fresh=False):
            m = m0 if fresh else m_ref[hi, i]
            acc = acc0 if fresh else acc_ref[hi, i]
            for u in range(nsub):
                s_t = buf[u]
                m_new = jnp.maximum(m, jnp.max(s_t, axis=0, keepdims=True))
                alpha = jnp.exp(m - m_new)
                pt = jnp.exp(s_t - m_new).astype(BF16)
                acc = alpha * acc + _dot(values(kj, u), pt)
                m = m_new
            m_ref[hi, i] = m
            acc_ref[hi, i] = acc

        def list_block(i, t):
            thr = thr_ref[h, i]
            for j in range(nq):
                si_ref[t] = i
                sj_ref[t] = j
                t = t + jnp.where((j < i) & (fmin_ref[h, j] <= thr), 1, 0)
            return t

        worst_shift = lax.fori_loop(0, nq, lambda i, w: jnp.maximum(w, shift_ref[h, i]), jnp.float32(0.0))

        def run_head(produce, consume, buf_a, buf_b):
            def diag_step(t, n):
                i0 = 2 * t
                produce(i0, i0, buf_a, masked=True)
                produce(i0 + 1, i0 + 1, buf_b, masked=True)
                consume(i0, i0, buf_a, fresh=True)
                consume(i0 + 1, i0 + 1, buf_b, fresh=True)
                return list_block(i0 + 1, list_block(i0, n))

            n_tiles = lax.fori_loop(0, nq // 2, diag_step, 0)
            for pad in range(FOX_STEPS):
                si_ref[n_tiles + pad] = nq
                sj_ref[n_tiles + pad] = 0
            m_ref[hi, nq] = m0
            acc_ref[hi, nq] = acc0

            def prefetch(t, buf):
                produce(jnp.minimum(si_ref[t], nq - 1), sj_ref[t], buf)

            prefetch(0, buf_a)

            def multi_step(tn, carry):
                t = FOX_STEPS * tn
                for d in range(0, FOX_STEPS, 2):
                    prefetch(t + d + 1, buf_b)
                    consume(si_ref[t + d], sj_ref[t + d], buf_a)
                    prefetch(t + d + 2, buf_a)
                    consume(si_ref[t + d + 1], sj_ref[t + d + 1], buf_b)
                return carry

            lax.fori_loop(0, (n_tiles + FOX_STEPS - 1) // FOX_STEPS, multi_step, 0)

        no_running_max = 2.0 * worst_shift <= FOX_MAX_SHIFT_GAP

        @pl.when(no_running_max)
        def _():
            run_head(fast_produce, fast_consume, pa_ref, pb_ref)

        @pl.when(jnp.logical_not(no_running_max))
        def _():
            run_head(slow_produce, slow_consume, sa_ref, sb_ref)

    def finish(i, carry):
        a0 = acc_ref[0, i]
        a1 = acc_ref[1, i]
        ot = jnp.concatenate([a0[:FOX_DH] / a0[FOX_DH:FOX_DH + 1], a1[:FOX_DH] / a1[FOX_DH:FOX_DH + 1]], axis=0)
        o_ref[pl.ds(pl.multiple_of(i * tq, tq), tq), :] = ot.T.astype(BF16)
        return carry

    lax.fori_loop(0, nq, finish, 0)


def _fox(thr, fmin_blk, shift, qt, kaug, vt):
    s = kaug.shape[1]
    tq = FOX_TQ
    nq = s // tq
    npair = FOX_HEADS // 2
    prow = 2 * (FOX_DH + FOX_ONES_ROWS)
    assert nq % 2 == 0
    hrows = FOX_DH + FOX_ONES_ROWS
    max_tiles = nq * (nq - 1) // 2 + FOX_STEPS
    once = pl.Buffered(1)
    grid_spec = pltpu.PrefetchScalarGridSpec(
        num_scalar_prefetch=3,
        grid=(npair,),
        in_specs=[pl.BlockSpec((2 * LANES, s), lambda p, *_: (p, 0), pipeline_mode=once),
                  pl.BlockSpec((None, s, 2 * LANES), lambda p, *_: (p, 0, 0), pipeline_mode=once),
                  pl.BlockSpec((prow, s), lambda p, *_: (p, 0), pipeline_mode=once)],
        out_specs=pl.BlockSpec((None, s, LANES), lambda p, *_: (p, 0, 0)),
        scratch_shapes=[pltpu.VMEM((tq // FOX_TK, FOX_TK, tq), F32), pltpu.VMEM((tq // FOX_TK, FOX_TK, tq), F32),
                        pltpu.VMEM((tq // FOX_TK, FOX_TK, tq), BF16), pltpu.VMEM((tq // FOX_TK, FOX_TK, tq), BF16),
                        pltpu.VMEM((2, nq + 1, 1, tq), F32), pltpu.VMEM((2, nq + 1, hrows, tq), F32),
                        pltpu.SMEM((max_tiles,), jnp.int32), pltpu.SMEM((max_tiles,), jnp.int32)],
    )
    return pl.pallas_call(
        _fox_kernel,
        grid_spec=grid_spec,
        out_shape=jax.ShapeDtypeStruct((npair, s, LANES), BF16),
        compiler_params=pltpu.CompilerParams(dimension_semantics=("arbitrary",), vmem_limit_bytes=VMEM_LIMIT),
        name="fox",
    )(thr, fmin_blk, shift, qt, kaug, vt)


def _tail_kernel(x_ref, ogla_ref, ofox_ref, omem_ref, gate_ref, wg_ref, wf_ref, wm_ref, wo_ref,
                 gffn_ref, w1_ref, w2_ref, gfin_ref, out_ref):
    gate = gate_ref[...]
    ofox = jnp.concatenate([ofox_ref[p] for p in range(FOX_HEADS // 2)], axis=1)
    merged = (gate[:, 0:D_MODEL].astype(F32) * _dot(ogla_ref[...], wg_ref[...])
              + gate[:, D_MODEL:2 * D_MODEL].astype(F32) * _dot(ofox, wf_ref[...])
              + gate[:, 2 * D_MODEL:3 * D_MODEL].astype(F32) * _dot(omem_ref[...], wm_ref[...]))
    h = x_ref[...] + _dot(merged.astype(BF16), wo_ref[...])
    u2 = _rms(h, gffn_ref[...]).astype(BF16)
    acc = jnp.zeros_like(h)
    for cidx in range(D_FF // FF_CHUNK):
        cs = slice(cidx * FF_CHUNK, (cidx + 1) * FF_CHUNK)
        a = jnp.maximum(_dot(u2, w1_ref[:, cs]), 0.0)
        acc = acc + _dot((a * a).astype(BF16), w2_ref[cs, :])
    out_ref[...] = _rms(h + acc, gfin_ref[...])


def _tail(x, ogla, ofox, omem, gate, wg, wf, wm, wo, gffn, w1, w2, gfin):
    s = x.shape[0]
    tm = TAIL_TILE
    full = lambda a: pl.BlockSpec(a.shape, lambda i: (0,) * a.ndim, pipeline_mode=pl.Buffered(1))
    row = lambda w: pl.BlockSpec((tm, w), lambda i: (i, 0))
    return pl.pallas_call(
        _tail_kernel,
        grid=(s // tm,),
        in_specs=[row(D_MODEL), row(GLA_V), pl.BlockSpec((FOX_HEADS // 2, tm, LANES), lambda i: (0, i, 0)),
                  row(MEM_W), row(3 * D_MODEL),
                  full(wg), full(wf), full(wm), full(wo), full(gffn), full(w1), full(w2), full(gfin)],
        out_specs=row(D_MODEL),
        out_shape=jax.ShapeDtypeStruct((s, D_MODEL), F32),
        compiler_params=pltpu.CompilerParams(dimension_semantics=("arbitrary",), vmem_limit_bytes=VMEM_LIMIT),
        name="tail",
    )(x, ogla, ofox, omem, gate, wg, wf, wm, wo, gffn, w1, w2, gfin)


def kernel(x, mem, g_mix, w_in, w_alpha_up, b_alpha, b_forget, g_gla_head, g_mem, w_mem_kv,
           w_gla_o, w_fox_o, w_mem_o, w_out, g_ffn, w_ff1, w_ff2, g_final):
    assert x.shape[0] == 1 and g_mix.shape[0] == 1, "single batch, single layer"
    s = x.shape[1]
    assert s % ROW_TILE == 0 and s % TAIL_TILE == 0 and s % FOX_TQ == 0
    assert FOX_TQ % FOX_TK == 0 and FOX_TQ % ROW_TILE == 0
    xs = x[0]
    w = w_in[0]

    o_ga = 2 * GLA_K + 2 * GLA_V
    o_fox = o_ga + GLA_LOWRANK
    o_ff = o_fox + 3 * FOX_W
    o_mq = o_ff + FOX_HEADS
    o_gate = o_mq + MEM_W
    w_gla = w[:, :o_ga].astype(BF16)
    w_fk = w[:, o_fox + FOX_W:o_fox + 2 * FOX_W].astype(BF16)
    w_qvt = jnp.concatenate([w[:, o_fox:o_fox + FOX_W], w[:, o_fox + 2 * FOX_W:o_ff]], axis=1).T.astype(BF16)
    w_mq = w[:, o_mq:o_gate].astype(BF16)
    w_gate = w[:, o_gate:].astype(BF16)
    w_small = jnp.concatenate(
        [w[:, o_ga:o_fox], w[:, o_ff:o_mq], jnp.zeros((D_MODEL, LANES - GLA_LOWRANK - FOX_HEADS), F32)], axis=1)
    w_small_hi = w_small.astype(BF16)
    w_small = jnp.concatenate([w_small_hi, (w_small - w_small_hi.astype(F32)).astype(BF16)], axis=1)
    w_up_hi = w_alpha_up[0].astype(BF16)
    w_up = jnp.concatenate([w_up_hi, (w_alpha_up[0] - w_up_hi.astype(F32)).astype(BF16)], axis=1)
    b_f = jnp.zeros((1, LANES), F32).at[0, GLA_LOWRANK:GLA_LOWRANK + FOX_HEADS].set(b_forget[0])

    mk, mv = _memkv(mem[0], g_mem, w_mem_kv[0].astype(BF16))
    gla_qkvg, loga, qt, kaug, vt, omem, gate, stats, statq = _proj(
        xs, g_mix, w_gla, w_fk, w_qvt, w_mq, w_gate, w_small, w_up, b_alpha, b_f, mk, mv)

    ogla = _gla(gla_qkvg, loga, g_gla_head.reshape(1, GLA_V))

    per_blk = lambda a: a.reshape(s // FOX_TQ, FOX_TQ // ROW_TILE, FOX_HEADS)
    qn = jnp.sqrt(jnp.max(per_blk(statq[:, :, 0]), axis=1))
    kn = jnp.sqrt(jnp.max(stats[:, 0, :FOX_HEADS], axis=0))
    fmax = jnp.max(per_blk(stats[:, 1, :FOX_HEADS]), axis=1)
    fmin = jnp.min(per_blk(stats[:, 2, :FOX_HEADS]), axis=1)
    shift = (1.02 * (FOX_DH ** -0.5)) * qn * kn[None, :]
    thr = 2.0 * shift + fmax + PRUNE_LOGIT_GAP
    ofox = _fox(thr.T, fmin.T, shift.T, qt, kaug, vt)

    out = _tail(xs, ogla, ofox, omem, gate, w_gla_o[0].astype(BF16), w_fox_o[0].astype(BF16),
                w_mem_o[0].astype(BF16), w_out[0].astype(BF16), g_ffn, w_ff1[0].astype(BF16),
                w_ff2[0].astype(BF16), g_final.reshape(1, D_MODEL))
    return out[None]
```

```python
import functools

import jax
import jax.numpy as jnp
import numpy as np
from jax import lax
from jax.experimental import pallas as pl
from jax.experimental.pallas import tpu as pltpu

D_MODEL = 1024
CHUNK = 64
EPS = 1e-6
GLA_HEADS = 4
GLA_DK = 64
GLA_DV = 128
GLA_LOWRANK = 16
GLA_TAU = 16.0
FOX_HEADS = 8
FOX_DH = 64
MEM_HEADS = 4
MEM_DH = 128
D_FF = 4 * D_MODEL
GLA_K = GLA_HEADS * GLA_DK
GLA_V = GLA_HEADS * GLA_DV
FOX_W = FOX_HEADS * FOX_DH
MEM_W = MEM_HEADS * MEM_DH

LANES = 128
FOX_ONES_ROWS = 16
FOX_AUG_PER_HEAD = 6
ROW_TILE = 256
GLA_TILE = 512
GLA_GROUP = 256
FOX_TQ = 512
FOX_TK = 256
FOX_STEPS = 8
TAIL_TILE = 256
FF_CHUNK = 1024
VMEM_LIMIT = 56 * 1024 * 1024
NEG_BIG = -1e30
PRUNE_LOGIT_GAP = 104.0
FOX_MAX_SHIFT_GAP = 50.0

F32 = jnp.float32
BF16 = jnp.bfloat16


def _rms(xf, g):
    r = lax.rsqrt(jnp.mean(xf * xf, axis=-1, keepdims=True) + EPS)
    return (xf * r) * g


def _log_sigmoid(x):
    return jnp.minimum(x, 0.0) - jnp.log1p(jnp.exp(-jnp.abs(x)))


def _sigmoid(x):
    return 1.0 / (1.0 + jnp.exp(-x))


def _dot(a, b):
    return jnp.dot(a, b, preferred_element_type=F32)


def _dot_nt(a, b):
    return lax.dot_general(a, b, (((1,), (1,)), ((), ())), preferred_element_type=F32)


def _dot_tn(a, b):
    return lax.dot_general(a, b, (((0,), (0,)), ((), ())), preferred_element_type=F32)


def _memkv_kernel(mem_ref, g_ref, w_ref, mk_ref, mv_ref):
    mn = _rms(mem_ref[...], g_ref[...]).astype(BF16)
    kv = _dot(mn, w_ref[...])
    mk_ref[...] = kv[:, :MEM_W].astype(BF16)
    mv_ref[...] = kv[:, MEM_W:].astype(BF16)


def _memkv(mem, g_mem, w_mem_kv):
    m = mem.shape[0]
    return pl.pallas_call(
        _memkv_kernel,
        out_shape=(jax.ShapeDtypeStruct((m, MEM_W), BF16), jax.ShapeDtypeStruct((m, MEM_W), BF16)),
        name="memkv",
    )(mem, g_mem, w_mem_kv)


def _proj_kernel(x_ref, gmix_ref, wgla_ref, wfk_ref, wqvt_ref, wmq_ref, wgate_ref, wsmall_ref, wup_ref,
                 balpha_ref, bforget_ref, mk_ref, mv_ref, pqt_ref, pk_ref,
                 gla_ref, loga_ref, qt_ref, kaug_ref, vt_ref, omem_ref, gate_ref, stat_ref, statq_ref,
                 carry_ref):
    tm = x_ref.shape[0]

    @pl.when(pl.program_id(0) == 0)
    def _():
        carry_ref[...] = jnp.zeros_like(carry_ref)

    u = _rms(x_ref[...], gmix_ref[...])
    ub = u.astype(BF16)

    gla_ref[...] = _dot(ub, wgla_ref[...]).astype(BF16)
    fkb = _dot(ub, wfk_ref[...]).astype(BF16)
    gate_ref[...] = _sigmoid(_dot(ub, wgate_ref[...])).astype(BF16)

    qvt = _dot_nt(wqvt_ref[...], ub)
    fqt = (qvt[:FOX_W] * (FOX_DH ** -0.5)).astype(BF16)
    vt = qvt[FOX_W:].astype(BF16)
    ones = jnp.ones((FOX_ONES_ROWS, tm), BF16)
    vt_parts = []
    for h in range(FOX_HEADS):
        vt_parts += [vt[h * FOX_DH:(h + 1) * FOX_DH, :], ones]
    vt_ref[...] = jnp.concatenate(vt_parts, axis=0)

    u_lo = (u - ub.astype(F32)).astype(BF16)
    parts = _dot(jnp.concatenate([ub, u_lo], axis=0), wsmall_ref[...])
    small = (parts[:tm, :LANES] + parts[:tm, LANES:]) + (parts[tm:, :LANES] + parts[tm:, LANES:])
    ga = small[:, :GLA_LOWRANK]
    ga_hi = ga.astype(BF16)
    ga_lo = (ga - ga_hi.astype(F32)).astype(BF16)
    up = _dot(jnp.concatenate([ga_hi, ga_lo], axis=0), wup_ref[...])
    alpha_pre = ((up[:tm, :GLA_K] + up[:tm, GLA_K:]) + (up[tm:, :GLA_K] + up[tm:, GLA_K:])) + balpha_ref[...]
    loga_ref[...] = _log_sigmoid(alpha_pre) * (1.0 / GLA_TAU)

    lane = lax.broadcasted_iota(jnp.int32, (tm, LANES), 1)
    ff_valid = (lane >= GLA_LOWRANK) & (lane < GLA_LOWRANK + FOX_HEADS)
    logf = jnp.where(ff_valid, _log_sigmoid(small + bforget_ref[...]), 0.0)
    logf = pltpu.roll(logf, LANES - GLA_LOWRANK, 1)
    r = lax.broadcasted_iota(jnp.int32, (tm, tm), 0)
    c = lax.broadcasted_iota(jnp.int32, (tm, tm), 1)
    tri = jnp.where(r >= c, 1.0, 0.0).astype(BF16)
    lf_hi = logf.astype(BF16)
    lf_r = logf - lf_hi.astype(F32)
    lf_mid = lf_r.astype(BF16)
    lf_lo = (lf_r - lf_mid.astype(F32)).astype(BF16)
    c3 = _dot(tri, jnp.concatenate([lf_hi, lf_mid, lf_lo], axis=1))
    fcum = (c3[:, :LANES] + c3[:, LANES:2 * LANES] + c3[:, 2 * LANES:]) + carry_ref[...]
    carry_ref[...] = fcum[tm - 1:tm, :]

    f_hi = fcum.astype(BF16)
    rem = fcum - f_hi.astype(F32)
    f_mid = rem.astype(BF16)
    f_lo = (rem - f_mid.astype(F32)).astype(BF16)
    f3 = jnp.concatenate([f_hi, f_mid, f_lo], axis=1)
    def aug_slot(idx):
        a = idx % LANES
        return a % FOX_AUG_PER_HEAD, a < 2 * FOX_AUG_PER_HEAD

    slot_q, in_q = aug_slot(lax.broadcasted_iota(jnp.int32, (FOX_W, 1), 0))
    slot_k, in_k = aug_slot(lax.broadcasted_iota(jnp.int32, (1, FOX_W), 1))
    augqt = _dot_nt(pqt_ref[...], f3) + jnp.where(in_q & (slot_q < 3), -1.0, 0.0)
    augk = _dot(f3, pk_ref[...]) + jnp.where(in_k & (slot_k >= 3), 1.0, 0.0)
    for p in range(FOX_HEADS // 2):
        src = slice(p * LANES, (p + 1) * LANES)
        qt_ref[2 * p * LANES:(2 * p + 1) * LANES, :] = fqt[src]
        qt_ref[(2 * p + 1) * LANES:(2 * p + 2) * LANES, :] = augqt[src].astype(BF16)
        kaug_ref[p, :, :LANES] = fkb[:, src]
        kaug_ref[p, :, LANES:] = augk[:, src].astype(BF16)

    fq32 = fqt.astype(F32) * (FOX_DH ** 0.5)
    nq2 = jnp.sum((fq32 * fq32).reshape(FOX_HEADS, FOX_DH, tm), axis=1)
    statq_ref[0] = jnp.broadcast_to(jnp.max(nq2, axis=1, keepdims=True), (FOX_HEADS, LANES))
    gi = lax.broadcasted_iota(jnp.int32, (FOX_W, LANES), 0) // FOX_DH
    gj = lax.broadcasted_iota(jnp.int32, (FOX_W, LANES), 1)
    group = jnp.where(gi == gj, 1.0, 0.0).astype(BF16)
    fk = fkb.astype(F32)
    nk2 = jnp.max(_dot((fk * fk).astype(BF16), group), axis=0, keepdims=True)
    fmax = jnp.max(fcum, axis=0, keepdims=True)
    fmin = jnp.min(fcum, axis=0, keepdims=True)
    stat_ref[0] = jnp.concatenate([nk2, fmax, fmin, jnp.zeros((5, LANES), F32)], axis=0)

    mq = _dot(ub, wmq_ref[...]).astype(BF16)
    scale = MEM_DH ** -0.5
    outs = []
    for h in range(MEM_HEADS):
        sl = slice(h * MEM_DH, (h + 1) * MEM_DH)
        s = _dot_nt(mq[:, sl], mk_ref[:, sl]) * scale
        m = jnp.max(s, axis=-1, keepdims=True)
        p = jnp.exp(s - m)
        l = jnp.sum(p, axis=-1, keepdims=True)
        p = p / l
        outs.append(_dot(p.astype(BF16), mv_ref[:, sl]))
    omem_ref[...] = jnp.concatenate(outs, axis=-1).astype(BF16)


def _aug_placement():
    pq = np.zeros((3 * LANES, FOX_W), np.float32)
    pk = np.zeros((3 * LANES, FOX_W), np.float32)
    for h in range(FOX_HEADS):
        base = (h // 2) * LANES + (h % 2) * FOX_AUG_PER_HEAD
        for c in range(3):
            pq[c * LANES + h, base + 3 + c] = 1.0
            pk[c * LANES + h, base + c] = 1.0
    return jnp.asarray(pq.T, BF16), jnp.asarray(pk, BF16)


def _proj(x, g_mix, w_gla, w_fk, w_qvt, w_mq, w_gate, w_small, w_up, b_alpha, b_forget, mk, mv):
    s = x.shape[0]
    tm = ROW_TILE
    nt = s // tm
    pqt, pk = _aug_placement()
    full = lambda shape: pl.BlockSpec(shape, lambda i: (0,) * len(shape))
    row = lambda w: pl.BlockSpec((tm, w), lambda i: (i, 0))
    col = lambda r: pl.BlockSpec((r, tm), lambda i: (0, i))
    stat = pl.BlockSpec((1, 8, LANES), lambda i: (i, 0, 0))
    vt_rows = FOX_HEADS * (FOX_DH + FOX_ONES_ROWS)
    return pl.pallas_call(
        _proj_kernel,
        grid=(nt,),
        in_specs=[row(D_MODEL), full((1, D_MODEL)), full(w_gla.shape), full(w_fk.shape), full(w_qvt.shape),
                  full(w_mq.shape), full(w_gate.shape), full(w_small.shape), full(w_up.shape),
                  full(b_alpha.shape), full(b_forget.shape), full(mk.shape), full(mv.shape),
                  full(pqt.shape), full(pk.shape)],
        out_specs=[row(w_gla.shape[1]), row(GLA_K), col(2 * FOX_W),
                   pl.BlockSpec((FOX_HEADS // 2, tm, 2 * LANES), lambda i: (0, i, 0)), col(vt_rows), row(MEM_W),
                   row(w_gate.shape[1]), stat, stat],
        out_shape=[jax.ShapeDtypeStruct((s, w_gla.shape[1]), BF16),
                   jax.ShapeDtypeStruct((s, GLA_K), F32),
                   jax.ShapeDtypeStruct((2 * FOX_W, s), BF16),
                   jax.ShapeDtypeStruct((FOX_HEADS // 2, s, 2 * LANES), BF16),
                   jax.ShapeDtypeStruct((vt_rows, s), BF16),
                   jax.ShapeDtypeStruct((s, MEM_W), BF16),
                   jax.ShapeDtypeStruct((s, w_gate.shape[1]), BF16),
                   jax.ShapeDtypeStruct((nt, 8, LANES), F32),
                   jax.ShapeDtypeStruct((nt, 8, LANES), F32)],
        scratch_shapes=[pltpu.VMEM((1, LANES), F32)],
        compiler_params=pltpu.CompilerParams(dimension_semantics=("arbitrary",), vmem_limit_bytes=VMEM_LIMIT),
        name="proj",
    )(x, g_mix, w_gla, w_fk, w_qvt, w_mq, w_gate, w_small, w_up, b_alpha, b_forget, mk, mv, pqt, pk)


def _gla_kernel(qkvg_ref, loga_ref, ghead_ref, o_ref, state_ref):
    t = GLA_GROUP
    nc = t // CHUNK

    @pl.when(pl.program_id(0) == 0)
    def _():
        state_ref[...] = jnp.zeros_like(state_ref)

    r = lax.broadcasted_iota(jnp.int32, (t, t), 0)
    c = lax.broadcasted_iota(jnp.int32, (t, t), 1)
    same = (r // CHUNK) == (c // CHUNK)
    lower = r >= c
    causal = same & lower
    anti = same & (r < c)
    cum = jnp.where(causal, 1.0, 0.0).astype(BF16)
    lane = lax.broadcasted_iota(jnp.int32, (1, LANES), 1)

    for g0 in range(0, qkvg_ref.shape[0], t):
        rg = slice(g0, g0 + t)
        la = loga_ref[rg, :]
        la_hi = la.astype(BF16)
        la_r = la - la_hi.astype(F32)
        la_mid = la_r.astype(BF16)
        la_lo = (la_r - la_mid.astype(F32)).astype(BF16)
        b3 = _dot(cum, jnp.concatenate([la_hi, la_mid, la_lo], axis=1))
        b = b3[:, :GLA_K] + b3[:, GLA_K:2 * GLA_K] + b3[:, 2 * GLA_K:]
        b_last = jnp.concatenate(
            [jnp.broadcast_to(b[(ci + 1) * CHUNK - 1:(ci + 1) * CHUNK, :], (CHUNK, GLA_K)) for ci in range(nc)],
            axis=0)
        e_pos = jnp.exp(b)
        e_neg = jnp.exp(-b)
        q = qkvg_ref[rg, 0:GLA_K].astype(F32) * (GLA_DK ** -0.5)
        k = qkvg_ref[rg, GLA_K:2 * GLA_K].astype(F32)
        q_pos = (q * e_pos).astype(BF16)
        q_neg = (q * e_neg).astype(BF16)
        k_pos = (k * e_pos).astype(BF16)
        k_neg = (k * e_neg).astype(BF16)
        k_dec = (k * jnp.exp(b_last - b)).astype(BF16)

        chunks = [slice(ci * CHUNK, (ci + 1) * CHUNK) for ci in range(nc)]
        pairs = [slice((h // 2) * LANES, (h // 2 + 1) * LANES) for h in range(GLA_HEADS)]
        in_head = [(lane // GLA_DK) == (h % 2) for h in range(GLA_HEADS)]
        vs = [qkvg_ref[rg, 2 * GLA_K + h * GLA_DV:2 * GLA_K + (h + 1) * GLA_DV] for h in range(GLA_HEADS)]
        qps = [jnp.where(in_head[h], q_pos[:, pairs[h]], jnp.zeros((), BF16)) for h in range(GLA_HEADS)]

        kvs = [[jnp.where(in_head[h], _dot_tn(vs[h][rows], k_dec[rows, pairs[h]]), 0.0) for rows in chunks]
               for h in range(GLA_HEADS)]
        sts = [state_ref[h] for h in range(GLA_HEADS)]
        inter = [[] for _ in range(GLA_HEADS)]
        for ci, rows in enumerate(chunks):
            for h in range(GLA_HEADS):
                inter[h].append(_dot_nt(qps[h][rows], sts[h].astype(BF16)))
                dec = e_pos[(ci + 1) * CHUNK - 1:(ci + 1) * CHUNK, pairs[h]]
                sts[h] = sts[h] * dec + kvs[h][ci]
        for h in range(GLA_HEADS):
            state_ref[h] = sts[h]

        attns = []
        for h in range(GLA_HEADS):
            qn = jnp.where(in_head[h], q_neg[:, pairs[h]], jnp.zeros((), BF16))
            a_c = _dot_nt(qps[h], k_neg[:, pairs[h]])
            a_a = _dot_nt(qn, k_pos[:, pairs[h]])
            attns.append(jnp.where(causal, a_c, jnp.where(anti, a_a, 0.0)).astype(BF16))
        outs = [_dot(attns[h], vs[h]) + jnp.concatenate(inter[h], axis=0) for h in range(GLA_HEADS)]
        for h in range(GLA_HEADS):
            o = outs[h]

            gh = ghead_ref[:, h * GLA_DV:(h + 1) * GLA_DV]
            on = _rms(o, gh)
            gsl = slice(2 * GLA_K + GLA_V + h * GLA_DV, 2 * GLA_K + GLA_V + (h + 1) * GLA_DV)
            gg = qkvg_ref[rg, gsl].astype(F32)
            o_ref[rg, h * GLA_DV:(h + 1) * GLA_DV] = (on * (gg * _sigmoid(gg))).astype(BF16)


def _gla(qkvg, loga, ghead):
    s = qkvg.shape[0]
    t = GLA_TILE
    assert s % t == 0 and t % GLA_GROUP == 0
    return pl.pallas_call(
        _gla_kernel,
        grid=(s // t,),
        in_specs=[pl.BlockSpec((t, qkvg.shape[1]), lambda i: (i, 0)),
                  pl.BlockSpec((t, GLA_K), lambda i: (i, 0)),
                  pl.BlockSpec((1, GLA_V), lambda i: (0, 0))],
        out_specs=pl.BlockSpec((t, GLA_V), lambda i: (i, 0)),
        out_shape=jax.ShapeDtypeStruct((s, GLA_V), BF16),
        scratch_shapes=[pltpu.VMEM((GLA_HEADS, GLA_DV, LANES), F32)],
        compiler_params=pltpu.CompilerParams(dimension_semantics=("arbitrary",), vmem_limit_bytes=VMEM_LIMIT),
        name="gla",
    )(qkvg, loga, ghead)


def _fox_kernel(thr_ref, fmin_ref, shift_ref, qt_ref, k_ref, vt_ref, o_ref,
                sa_ref, sb_ref, pa_ref, pb_ref, m_ref, acc_ref, si_ref, sj_ref):
    p = pl.program_id(0)
    tq = FOX_TQ
    tk = FOX_TK
    nsub = tq // tk
    nq = k_ref.shape[0] // tq
    hrows = FOX_DH + FOX_ONES_ROWS
    row = lax.broadcasted_iota(jnp.int32, (2 * LANES, 1), 0)
    krow = lax.broadcasted_iota(jnp.int32, (tk, tq), 0)
    qcol = lax.broadcasted_iota(jnp.int32, (tk, tq), 1)
    m0 = jnp.full((1, tq), NEG_BIG, F32)
    acc0 = jnp.zeros((hrows, tq), F32)

    for hi in range(2):
        h = 2 * p + hi
        aug0 = LANES + hi * FOX_AUG_PER_HEAD
        mine = ((row // FOX_DH) == hi) | ((row >= aug0) & (row < aug0 + FOX_AUG_PER_HEAD))
        vrows = slice(hi * hrows, (hi + 1) * hrows)

        def score_tiles(qi, kj, masked):
            q0 = pl.multiple_of(qi * tq, tq)
            qm = jnp.where(mine, qt_ref[:, pl.ds(q0, tq)], jnp.zeros((), BF16))
            for u in range(nsub):
                k0 = pl.multiple_of(kj * tq + u * tk, tk)
                s_t = _dot(k_ref[pl.ds(k0, tk), :], qm)
                if masked:
                    s_t = jnp.where(krow + u * tk <= qcol, s_t, NEG_BIG)
                yield u, s_t

        def values(kj, u):
            return vt_ref[vrows, pl.ds(pl.multiple_of(kj * tq + u * tk, tk), tk)]

        def fast_produce(qi, kj, buf, masked=False):
            shift = shift_ref[h, qi]
            for u, s_t in score_tiles(qi, kj, masked):
                buf[u] = jnp.exp(s_t - shift).astype(BF16)

        def fast_consume(i, kj, buf, fresh=False):
            acc = acc0 if fresh else acc_ref[hi, i]
            for u in range(nsub):
                acc = acc + _dot(values(kj, u), buf[u])
            acc_ref[hi, i] = acc

        def slow_produce(qi, kj, buf, masked=False):
            for u, s_t in score_tiles(qi, kj, masked):
                buf[u] = s_t

        def slow_consume(i, kj, buf, fresh=False):
            m = m0 if fresh else m_ref[hi, i]
            acc = acc0 if fresh else acc_ref[hi, i]
            for u in range(nsub):
                s_t = buf[u]
                m_new = jnp.maximum(m, jnp.max(s_t, axis=0, keepdims=True))
                alpha = jnp.exp(m - m_new)
                pt = jnp.exp(s_t - m_new).astype(BF16)
                acc = alpha * acc + _dot(values(kj, u), pt)
                m = m_new
            m_ref[hi, i] = m
            acc_ref[hi, i] = acc

        def list_block(i, t):
            thr = thr_ref[h, i]
            for j in range(nq):
                si_ref[t] = i
                sj_ref[t] = j
                t = t + jnp.where((j < i) & (fmin_ref[h, j] <= thr), 1, 0)
            return t

        worst_shift = lax.fori_loop(0, nq, lambda i, w: jnp.maximum(w, shift_ref[h, i]), jnp.float32(0.0))

        def run_head(produce, consume, buf_a, buf_b):
            def diag_step(t, n):
                i0 = 2 * t
                produce(i0, i0, buf_a, masked=True)
                produce(i0 + 1, i0 + 1, buf_b, masked=True)
                consume(i0, i0, buf_a, fresh=True)
                consume(i0 + 1, i0 + 1, buf_b, fresh=True)
                return list_block(i0 + 1, list_block(i0, n))

            n_tiles = lax.fori_loop(0, nq // 2, diag_step, 0)
            for pad in range(FOX_STEPS):
                si_ref[n_tiles + pad] = nq
                sj_ref[n_tiles + pad] = 0
            m_ref[hi, nq] = m0
            acc_ref[hi, nq] = acc0

            def prefetch(t, buf):
                produce(jnp.minimum(si_ref[t], nq - 1), sj_ref[t], buf)

            prefetch(0, buf_a)

            def multi_step(tn, carry):
                t = FOX_STEPS * tn
                for d in range(0, FOX_STEPS, 2):
                    prefetch(t + d + 1, buf_b)
                    consume(si_ref[t + d], sj_ref[t + d], buf_a)
                    prefetch(t + d + 2, buf_a)
                    consume(si_ref[t + d + 1], sj_ref[t + d + 1], buf_b)
                return carry

            lax.fori_loop(0, (n_tiles + FOX_STEPS - 1) // FOX_STEPS, multi_step, 0)

        no_running_max = 2.0 * worst_shift <= FOX_MAX_SHIFT_GAP

        @pl.when(no_running_max)
        def _():
            run_head(fast_produce, fast_consume, pa_ref, pb_ref)

        @pl.when(jnp.logical_not(no_running_max))
        def _():
            run_head(slow_produce, slow_consume, sa_ref, sb_ref)

    def finish(i, carry):
        a0 = acc_ref[0, i]
        a1 = acc_ref[1, i]
        ot = jnp.concatenate([a0[:FOX_DH] / a0[FOX_DH:FOX_DH + 1], a1[:FOX_DH] / a1[FOX_DH:FOX_DH + 1]], axis=0)
        o_ref[pl.ds(pl.multiple_of(i * tq, tq), tq), :] = ot.T.astype(BF16)
        return carry

    lax.fori_loop(0, nq, finish, 0)


def _fox(thr, fmin_blk, shift, qt, kaug, vt):
    s = kaug.shape[1]
    tq = FOX_TQ
    nq = s // tq
    npair = FOX_HEADS // 2
    prow = 2 * (FOX_DH + FOX_ONES_ROWS)
    assert nq % 2 == 0
    hrows = FOX_DH + FOX_ONES_ROWS
    max_tiles = nq * (nq - 1) // 2 + FOX_STEPS
    once = pl.Buffered(1)
    grid_spec = pltpu.PrefetchScalarGridSpec(
        num_scalar_prefetch=3,
        grid=(npair,),
        in_specs=[pl.BlockSpec((2 * LANES, s), lambda p, *_: (p, 0), pipeline_mode=once),
                  pl.BlockSpec((None, s, 2 * LANES), lambda p, *_: (p, 0, 0), pipeline_mode=once),
                  pl.BlockSpec((prow, s), lambda p, *_: (p, 0), pipeline_mode=once)],
        out_specs=pl.BlockSpec((None, s, LANES), lambda p, *_: (p, 0, 0)),
        scratch_shapes=[pltpu.VMEM((tq // FOX_TK, FOX_TK, tq), F32), pltpu.VMEM((tq // FOX_TK, FOX_TK, tq), F32),
                        pltpu.VMEM((tq // FOX_TK, FOX_TK, tq), BF16), pltpu.VMEM((tq // FOX_TK, FOX_TK, tq), BF16),
                        pltpu.VMEM((2, nq + 1, 1, tq), F32), pltpu.VMEM((2, nq + 1, hrows, tq), F32),
                        pltpu.SMEM((max_tiles,), jnp.int32), pltpu.SMEM((max_tiles,), jnp.int32)],
    )
    return pl.pallas_call(
        _fox_kernel,
        grid_spec=grid_spec,
        out_shape=jax.ShapeDtypeStruct((npair, s, LANES), BF16),
        compiler_params=pltpu.CompilerParams(dimension_semantics=("arbitrary",), vmem_limit_bytes=VMEM_LIMIT),
        name="fox",
    )(thr, fmin_blk, shift, qt, kaug, vt)


def _tail_kernel(x_ref, ogla_ref, ofox_ref, omem_ref, gate_ref, wg_ref, wf_ref, wm_ref, wo_ref,
                 gffn_ref, w1_ref, w2_ref, gfin_ref, out_ref):
    gate = gate_ref[...]
    ofox = jnp.concatenate([ofox_ref[p] for p in range(FOX_HEADS // 2)], axis=1)
    merged = (gate[:, 0:D_MODEL].astype(F32) * _dot(ogla_ref[...], wg_ref[...])
              + gate[:, D_MODEL:2 * D_MODEL].astype(F32) * _dot(ofox, wf_ref[...])
              + gate[:, 2 * D_MODEL:3 * D_MODEL].astype(F32) * _dot(omem_ref[...], wm_ref[...]))
    h = x_ref[...] + _dot(merged.astype(BF16), wo_ref[...])
    u2 = _rms(h, gffn_ref[...]).astype(BF16)
    acc = jnp.zeros_like(h)
    for cidx in range(D_FF // FF_CHUNK):
        cs = slice(cidx * FF_CHUNK, (cidx + 1) * FF_CHUNK)
        a = jnp.maximum(_dot(u2, w1_ref[:, cs]), 0.0)
        acc = acc + _dot((a * a).astype(BF16), w2_ref[cs, :])
    out_ref[...] = _rms(h + acc, gfin_ref[...])


def _tail(x, ogla, ofox, omem, gate, wg, wf, wm, wo, gffn, w1, w2, gfin):
    s = x.shape[0]
    tm = TAIL_TILE
    full = lambda a: pl.BlockSpec(a.shape, lambda i: (0,) * a.ndim, pipeline_mode=pl.Buffered(1))
    row = lambda w: pl.BlockSpec((tm, w), lambda i: (i, 0))
    return pl.pallas_call(
        _tail_kernel,
        grid=(s // tm,),
        in_specs=[row(D_MODEL), row(GLA_V), pl.BlockSpec((FOX_HEADS // 2, tm, LANES), lambda i: (0, i, 0)),
                  row(MEM_W), row(3 * D_MODEL),
                  full(wg), full(wf), full(wm), full(wo), full(gffn), full(w1), full(w2), full(gfin)],
        out_specs=row(D_MODEL),
        out_shape=jax.ShapeDtypeStruct((s, D_MODEL), F32),
        compiler_params=pltpu.CompilerParams(dimension_semantics=("arbitrary",), vmem_limit_bytes=VMEM_LIMIT),
        name="tail",
    )(x, ogla, ofox, omem, gate, wg, wf, wm, wo, gffn, w1, w2, gfin)


def kernel(x, mem, g_mix, w_in, w_alpha_up, b_alpha, b_forget, g_gla_head, g_mem, w_mem_kv,
           w_gla_o, w_fox_o, w_mem_o, w_out, g_ffn, w_ff1, w_ff2, g_final):
    assert x.shape[0] == 1 and g_mix.shape[0] == 1, "single batch, single layer"
    s = x.shape[1]
    assert s % ROW_TILE == 0 and s % TAIL_TILE == 0 and s % FOX_TQ == 0
    assert FOX_TQ % FOX_TK == 0 and FOX_TQ % ROW_TILE == 0
    xs = x[0]
    w = w_in[0]

    o_ga = 2 * GLA_K + 2 * GLA_V
    o_fox = o_ga + GLA_LOWRANK
    o_ff = o_fox + 3 * FOX_W
    o_mq = o_ff + FOX_HEADS
    o_gate = o_mq + MEM_W
    w_gla = w[:, :o_ga].astype(BF16)
    w_fk = w[:, o_fox + FOX_W:o_fox + 2 * FOX_W].astype(BF16)
    w_qvt = jnp.concatenate([w[:, o_fox:o_fox + FOX_W], w[:, o_fox + 2 * FOX_W:o_ff]], axis=1).T.astype(BF16)
    w_mq = w[:, o_mq:o_gate].astype(BF16)
    w_gate = w[:, o_gate:].astype(BF16)
    w_small = jnp.concatenate(
        [w[:, o_ga:o_fox], w[:, o_ff:o_mq], jnp.zeros((D_MODEL, LANES - GLA_LOWRANK - FOX_HEADS), F32)], axis=1)
    w_small_hi = w_small.astype(BF16)
    w_small = jnp.concatenate([w_small_hi, (w_small - w_small_hi.astype(F32)).astype(BF16)], axis=1)
    w_up_hi = w_alpha_up[0].astype(BF16)
    w_up = jnp.concatenate([w_up_hi, (w_alpha_up[0] - w_up_hi.astype(F32)).astype(BF16)], axis=1)
    b_f = jnp.zeros((1, LANES), F32).at[0, GLA_LOWRANK:GLA_LOWRANK + FOX_HEADS].set(b_forget[0])

    mk, mv = _memkv(mem[0], g_mem, w_mem_kv[0].astype(BF16))
    gla_qkvg, loga, qt, kaug, vt, omem, gate, stats, statq = _proj(
        xs, g_mix, w_gla, w_fk, w_qvt, w_mq, w_gate, w_small, w_up, b_alpha, b_f, mk, mv)

    ogla = _gla(gla_qkvg, loga, g_gla_head.reshape(1, GLA_V))

    per_blk = lambda a: a.reshape(s // FOX_TQ, FOX_TQ // ROW_TILE, FOX_HEADS)
    qn = jnp.sqrt(jnp.max(per_blk(statq[:, :, 0]), axis=1))
    kn = jnp.sqrt(jnp.max(stats[:, 0, :FOX_HEADS], axis=0))
    fmax = jnp.max(per_blk(stats[:, 1, :FOX_HEADS]), axis=1)
    fmin = jnp.min(per_blk(stats[:, 2, :FOX_HEADS]), axis=1)
    shift = (1.02 * (FOX_DH ** -0.5)) * qn * kn[None, :]
    thr = 2.0 * shift + fmax + PRUNE_LOGIT_GAP
    ofox = _fox(thr.T, fmin.T, shift.T, qt, kaug, vt)

    out = _tail(xs, ogla, ofox, omem, gate, w_gla_o[0].astype(BF16), w_fox_o[0].astype(BF16),
                w_mem_o[0].astype(BF16), w_out[0].astype(BF16), g_ffn, w_ff1[0].astype(BF16),
                w_ff2[0].astype(BF16), g_final.reshape(1, D_MODEL))
    return out[None]
```

```python
import functools

import jax
import jax.numpy as jnp
import numpy as np
from jax import lax
from jax.experimental import pallas as pl
from jax.experimental.pallas import tpu as pltpu

D_MODEL = 1024
CHUNK = 64
EPS = 1e-6
GLA_HEADS = 4
GLA_DK = 64
GLA_DV = 128
GLA_LOWRANK = 16
GLA_TAU = 16.0
FOX_HEADS = 8
FOX_DH = 64
MEM_HEADS = 4
MEM_DH = 128
D_FF = 4 * D_MODEL
GLA_K = GLA_HEADS * GLA_DK
GLA_V = GLA_HEADS * GLA_DV
FOX_W = FOX_HEADS * FOX_DH
MEM_W = MEM_HEADS * MEM_DH

LANES = 128
FOX_ONES_ROWS = 16
FOX_AUG_PER_HEAD = 6
ROW_TILE = 256
GLA_TILE = 512
GLA_GROUP = 256
FOX_TQ = 512
FOX_TK = 256
FOX_STEPS = 8
TAIL_TILE = 256
FF_CHUNK = 1024
VMEM_LIMIT = 56 * 1024 * 1024
NEG_BIG = -1e30
PRUNE_LOGIT_GAP = 104.0
FOX_MAX_SHIFT_GAP = 50.0

F32 = jnp.float32
BF16 = jnp.bfloat16


def _rms(xf, g):
    r = lax.rsqrt(jnp.mean(xf * xf, axis=-1, keepdims=True) + EPS)
    return (xf * r) * g


def _log_sigmoid(x):
    return jnp.minimum(x, 0.0) - jnp.log1p(jnp.exp(-jnp.abs(x)))


def _sigmoid(x):
    return 1.0 / (1.0 + jnp.exp(-x))


def _dot(a, b):
    return jnp.dot(a, b, preferred_element_type=F32)


def _dot_nt(a, b):
    return lax.dot_general(a, b, (((1,), (1,)), ((), ())), preferred_element_type=F32)


def _dot_tn(a, b):
    return lax.dot_general(a, b, (((0,), (0,)), ((), ())), preferred_element_type=F32)


def _memkv_kernel(mem_ref, g_ref, w_ref, mk_ref, mv_ref):
    mn = _rms(mem_ref[...], g_ref[...]).astype(BF16)
    kv = _dot(mn, w_ref[...])
    mk_ref[...] = kv[:, :MEM_W].astype(BF16)
    mv_ref[...] = kv[:, MEM_W:].astype(BF16)


def _memkv(mem, g_mem, w_mem_kv):
    m = mem.shape[0]
    return pl.pallas_call(
        _memkv_kernel,
        out_shape=(jax.ShapeDtypeStruct((m, MEM_W), BF16), jax.ShapeDtypeStruct((m, MEM_W), BF16)),
        name="memkv",
    )(mem, g_mem, w_mem_kv)


def _proj_kernel(x_ref, gmix_ref, wgla_ref, wfk_ref, wqv_ref, wmq_ref, wgate_ref, wsmall_ref, wup_ref,
                 balpha_ref, bforget_ref, mk_ref, mv_ref, pqt_ref, pk_ref,
                 gla_ref, loga_ref, qt_ref, kaug_ref, vt_ref, omem_ref, gate_ref, stat_ref, statq_ref,
                 carry_ref, wqvt_ref):
    tm = x_ref.shape[0]

    @pl.when(pl.program_id(0) == 0)
    def _():
        carry_ref[...] = jnp.zeros_like(carry_ref)
        wqvt_ref[...] = wqv_ref[...].T

    u = _rms(x_ref[...], gmix_ref[...])
    ub = u.astype(BF16)

    u_lo = (u - ub.astype(F32)).astype(BF16)
    parts = _dot(jnp.concatenate([ub, u_lo], axis=0), wsmall_ref[...])
    small = (parts[:tm, :LANES] + parts[:tm, LANES:]) + (parts[tm:, :LANES] + parts[tm:, LANES:])

    mq = _dot(ub, wmq_ref[...]).astype(BF16)
    scale = MEM_DH ** -0.5
    heads = [slice(h * MEM_DH, (h + 1) * MEM_DH) for h in range(MEM_HEADS)]
    scores = [_dot_nt(mq[:, sl], mk_ref[:, sl]) * scale for sl in heads]

    gla_ref[...] = _dot(ub, wgla_ref[...]).astype(BF16)
    fkb = _dot(ub, wfk_ref[...]).astype(BF16)

    ga = small[:, :GLA_LOWRANK]
    ga_hi = ga.astype(BF16)
    ga_lo = (ga - ga_hi.astype(F32)).astype(BF16)
    up = _dot(jnp.concatenate([ga_hi, ga_lo], axis=0), wup_ref[...])
    alpha_pre = ((up[:tm, :GLA_K] + up[:tm, GLA_K:]) + (up[tm:, :GLA_K] + up[tm:, GLA_K:])) + balpha_ref[...]
    loga_ref[...] = _log_sigmoid(alpha_pre) * (1.0 / GLA_TAU)

    lane = lax.broadcasted_iota(jnp.int32, (tm, LANES), 1)
    ff_valid = (lane >= GLA_LOWRANK) & (lane < GLA_LOWRANK + FOX_HEADS)
    logf = jnp.where(ff_valid, _log_sigmoid(small + bforget_ref[...]), 0.0)
    logf = pltpu.roll(logf, LANES - GLA_LOWRANK, 1)
    r = lax.broadcasted_iota(jnp.int32, (tm, tm), 0)
    c = lax.broadcasted_iota(jnp.int32, (tm, tm), 1)
    tri = jnp.where(r >= c, 1.0, 0.0).astype(BF16)
    lf_hi = logf.astype(BF16)
    lf_r = logf - lf_hi.astype(F32)
    lf_mid = lf_r.astype(BF16)
    lf_lo = (lf_r - lf_mid.astype(F32)).astype(BF16)
    c3 = _dot(tri, jnp.concatenate([lf_hi, lf_mid, lf_lo], axis=1))
    fcum = (c3[:, :LANES] + c3[:, LANES:2 * LANES] + c3[:, 2 * LANES:]) + carry_ref[...]
    carry_ref[...] = fcum[tm - 1:tm, :]

    f_hi = fcum.astype(BF16)
    rem = fcum - f_hi.astype(F32)
    f_mid = rem.astype(BF16)
    f_lo = (rem - f_mid.astype(F32)).astype(BF16)
    f3 = jnp.concatenate([f_hi, f_mid, f_lo], axis=1)
    def aug_slot(idx):
        a = idx % LANES
        return a % FOX_AUG_PER_HEAD, a < 2 * FOX_AUG_PER_HEAD

    slot_q, in_q = aug_slot(lax.broadcasted_iota(jnp.int32, (FOX_W, 1), 0))
    slot_k, in_k = aug_slot(lax.broadcasted_iota(jnp.int32, (1, FOX_W), 1))
    augqt = _dot_nt(pqt_ref[...], f3) + jnp.where(in_q & (slot_q < 3), -1.0, 0.0)
    augk = _dot(f3, pk_ref[...]) + jnp.where(in_k & (slot_k >= 3), 1.0, 0.0)
    gate_ref[...] = _sigmoid(_dot(ub, wgate_ref[...])).astype(BF16)

    probs = []
    for s in scores:
        m = jnp.max(s, axis=-1, keepdims=True)
        e = jnp.exp(s - m)
        probs.append((e / jnp.sum(e, axis=-1, keepdims=True)).astype(BF16))
    omem_ref[...] = jnp.concatenate([_dot(pr, mv_ref[:, sl]) for pr, sl in zip(probs, heads)], axis=-1).astype(BF16)

    qvt = _dot_nt(wqvt_ref[...], ub)
    fqt = (qvt[:FOX_W] * (FOX_DH ** -0.5)).astype(BF16)
    vt = qvt[FOX_W:].astype(BF16)
    ones = jnp.ones((FOX_ONES_ROWS, tm), BF16)
    vt_parts = []
    for h in range(FOX_HEADS):
        vt_parts += [vt[h * FOX_DH:(h + 1) * FOX_DH, :], ones]
    vt_ref[...] = jnp.concatenate(vt_parts, axis=0)

    for p in range(FOX_HEADS // 2):
        src = slice(p * LANES, (p + 1) * LANES)
        qt_ref[2 * p * LANES:(2 * p + 1) * LANES, :] = fqt[src]
        qt_ref[(2 * p + 1) * LANES:(2 * p + 2) * LANES, :] = augqt[src].astype(BF16)
        kaug_ref[p, :, :LANES] = fkb[:, src]
        kaug_ref[p, :, LANES:] = augk[:, src].astype(BF16)

    fq32 = fqt.astype(F32) * (FOX_DH ** 0.5)
    nq2 = jnp.sum((fq32 * fq32).reshape(FOX_HEADS, FOX_DH, tm), axis=1)
    statq_ref[0] = jnp.broadcast_to(jnp.max(nq2, axis=1, keepdims=True), (FOX_HEADS, LANES))
    gi = lax.broadcasted_iota(jnp.int32, (FOX_W, LANES), 0) // FOX_DH
    gj = lax.broadcasted_iota(jnp.int32, (FOX_W, LANES), 1)
    group = jnp.where(gi == gj, 1.0, 0.0).astype(BF16)
    fk = fkb.astype(F32)
    nk2 = jnp.max(_dot((fk * fk).astype(BF16), group), axis=0, keepdims=True)
    fmax = jnp.max(fcum, axis=0, keepdims=True)
    fmin = jnp.min(fcum, axis=0, keepdims=True)
    stat_ref[0] = jnp.concatenate([nk2, fmax, fmin, jnp.zeros((5, LANES), F32)], axis=0)


def _aug_placement():
    pq = np.zeros((3 * LANES, FOX_W), np.float32)
    pk = np.zeros((3 * LANES, FOX_W), np.float32)
    for h in range(FOX_HEADS):
        base = (h // 2) * LANES + (h % 2) * FOX_AUG_PER_HEAD
        for c in range(3):
            pq[c * LANES + h, base + 3 + c] = 1.0
            pk[c * LANES + h, base + c] = 1.0
    return jnp.asarray(pq.T, BF16), jnp.asarray(pk, BF16)


def _proj(x, g_mix, w_gla, w_fk, w_qvt, w_mq, w_gate, w_small, w_up, b_alpha, b_forget, mk, mv):
    s = x.shape[0]
    tm = ROW_TILE
    nt = s // tm
    pqt, pk = _aug_placement()
    full = lambda shape: pl.BlockSpec(shape, lambda i: (0,) * len(shape))
    row = lambda w: pl.BlockSpec((tm, w), lambda i: (i, 0))
    col = lambda r: pl.BlockSpec((r, tm), lambda i: (0, i))
    stat = pl.BlockSpec((1, 8, LANES), lambda i: (i, 0, 0))
    vt_rows = FOX_HEADS * (FOX_DH + FOX_ONES_ROWS)
    return pl.pallas_call(
        _proj_kernel,
        grid=(nt,),
        in_specs=[row(D_MODEL), full((1, D_MODEL)), full(w_gla.shape), full(w_fk.shape), full(w_qvt.shape),
                  full(w_mq.shape), full(w_gate.shape), full(w_small.shape), full(w_up.shape),
                  full(b_alpha.shape), full(b_forget.shape), full(mk.shape), full(mv.shape),
                  full(pqt.shape), full(pk.shape)],
        out_specs=[row(w_gla.shape[1]), row(GLA_K), col(2 * FOX_W),
                   pl.BlockSpec((FOX_HEADS // 2, tm, 2 * LANES), lambda i: (0, i, 0)), col(vt_rows), row(MEM_W),
                   row(w_gate.shape[1]), stat, stat],
        out_shape=[jax.ShapeDtypeStruct((s, w_gla.shape[1]), BF16),
                   jax.ShapeDtypeStruct((s, GLA_K), F32),
                   jax.ShapeDtypeStruct((2 * FOX_W, s), BF16),
                   jax.ShapeDtypeStruct((FOX_HEADS // 2, s, 2 * LANES), BF16),
                   jax.ShapeDtypeStruct((vt_rows, s), BF16),
                   jax.ShapeDtypeStruct((s, MEM_W), BF16),
                   jax.ShapeDtypeStruct((s, w_gate.shape[1]), BF16),
                   jax.ShapeDtypeStruct((nt, 8, LANES), F32),
                   jax.ShapeDtypeStruct((nt, 8, LANES), F32)],
        scratch_shapes=[pltpu.VMEM((1, LANES), F32), pltpu.VMEM((w_qvt.shape[1], w_qvt.shape[0]), BF16)],
        compiler_params=pltpu.CompilerParams(dimension_semantics=("arbitrary",), vmem_limit_bytes=VMEM_LIMIT),
        name="proj",
    )(x, g_mix, w_gla, w_fk, w_qvt, w_mq, w_gate, w_small, w_up, b_alpha, b_forget, mk, mv, pqt, pk)


def _gla_kernel(qkvg_ref, loga_ref, ghead_ref, o_ref, state_ref):
    t = GLA_GROUP
    nc = t // CHUNK

    @pl.when(pl.program_id(0) == 0)
    def _():
        state_ref[...] = jnp.zeros_like(state_ref)

    r = lax.broadcasted_iota(jnp.int32, (t, t), 0)
    c = lax.broadcasted_iota(jnp.int32, (t, t), 1)
    same = (r // CHUNK) == (c // CHUNK)
    lower = r >= c
    causal = same & lower
    anti = same & (r < c)
    cum = jnp.where(causal, 1.0, 0.0).astype(BF16)
    lane = lax.broadcasted_iota(jnp.int32, (1, LANES), 1)

    for g0 in range(0, qkvg_ref.shape[0], t):
        rg = slice(g0, g0 + t)
        la = loga_ref[rg, :]
        la_hi = la.astype(BF16)
        la_r = la - la_hi.astype(F32)
        la_mid = la_r.astype(BF16)
        la_lo = (la_r - la_mid.astype(F32)).astype(BF16)
        b3 = _dot(cum, jnp.concatenate([la_hi, la_mid, la_lo], axis=1))
        b = b3[:, :GLA_K] + b3[:, GLA_K:2 * GLA_K] + b3[:, 2 * GLA_K:]
        b_last = jnp.concatenate(
            [jnp.broadcast_to(b[(ci + 1) * CHUNK - 1:(ci + 1) * CHUNK, :], (CHUNK, GLA_K)) for ci in range(nc)],
            axis=0)
        e_pos = jnp.exp(b)
        e_neg = jnp.exp(-b)
        q = qkvg_ref[rg, 0:GLA_K].astype(F32) * (GLA_DK ** -0.5)
        k = qkvg_ref[rg, GLA_K:2 * GLA_K].astype(F32)
        q_pos = (q * e_pos).astype(BF16)
        q_neg = (q * e_neg).astype(BF16)
        k_pos = (k * e_pos).astype(BF16)
        k_neg = (k * e_neg).astype(BF16)
        k_dec = (k * jnp.exp(b_last - b)).astype(BF16)

        chunks = [slice(ci * CHUNK, (ci + 1) * CHUNK) for ci in range(nc)]
        pairs = [slice((h // 2) * LANES, (h // 2 + 1) * LANES) for h in range(GLA_HEADS)]
        in_head = [(lane // GLA_DK) == (h % 2) for h in range(GLA_HEADS)]
        vs = [qkvg_ref[rg, 2 * GLA_K + h * GLA_DV:2 * GLA_K + (h + 1) * GLA_DV] for h in range(GLA_HEADS)]
        qps = [jnp.where(in_head[h], q_pos[:, pairs[h]], jnp.zeros((), BF16)) for h in range(GLA_HEADS)]

        kvs = [[jnp.where(in_head[h], _dot_tn(vs[h][rows], k_dec[rows, pairs[h]]), 0.0) for rows in chunks]
               for h in range(GLA_HEADS)]
        sts = [state_ref[h] for h in range(GLA_HEADS)]
        inter = [[] for _ in range(GLA_HEADS)]
        for ci, rows in enumerate(chunks):
            for h in range(GLA_HEADS):
                inter[h].append(_dot_nt(qps[h][rows], sts[h].astype(BF16)))
                dec = e_pos[(ci + 1) * CHUNK - 1:(ci + 1) * CHUNK, pairs[h]]
                sts[h] = sts[h] * dec + kvs[h][ci]
        for h in range(GLA_HEADS):
            state_ref[h] = sts[h]

        attns = []
        for h in range(GLA_HEADS):
            qn = jnp.where(in_head[h], q_neg[:, pairs[h]], jnp.zeros((), BF16))
            a_c = _dot_nt(qps[h], k_neg[:, pairs[h]])
            a_a = _dot_nt(qn, k_pos[:, pairs[h]])
            attns.append(jnp.where(causal, a_c, jnp.where(anti, a_a, 0.0)).astype(BF16))
        outs = [_dot(attns[h], vs[h]) + jnp.concatenate(inter[h], axis=0) for h in range(GLA_HEADS)]
        for h in range(GLA_HEADS):
            o = outs[h]

            gh = ghead_ref[:, h * GLA_DV:(h + 1) * GLA_DV]
            on = _rms(o, gh)
            gsl = slice(2 * GLA_K + GLA_V + h * GLA_DV, 2 * GLA_K + GLA_V + (h + 1) * GLA_DV)
            gg = qkvg_ref[rg, gsl].astype(F32)
            o_ref[rg, h * GLA_DV:(h + 1) * GLA_DV] = (on * (gg * _sigmoid(gg))).astype(BF16)


def _gla(qkvg, loga, ghead):
    s = qkvg.shape[0]
    t = GLA_TILE
    assert s % t == 0 and t % GLA_GROUP == 0
    return pl.pallas_call(
        _gla_kernel,
        grid=(s // t,),
        in_specs=[pl.BlockSpec((t, qkvg.shape[1]), lambda i: (i, 0)),
                  pl.BlockSpec((t, GLA_K), lambda i: (i, 0)),
                  pl.BlockSpec((1, GLA_V), lambda i: (0, 0))],
        out_specs=pl.BlockSpec((t, GLA_V), lambda i: (i, 0)),
        out_shape=jax.ShapeDtypeStruct((s, GLA_V), BF16),
        scratch_shapes=[pltpu.VMEM((GLA_HEADS, GLA_DV, LANES), F32)],
        compiler_params=pltpu.CompilerParams(dimension_semantics=("arbitrary",), vmem_limit_bytes=VMEM_LIMIT),
        name="gla",
    )(qkvg, loga, ghead)


def _fox_kernel(thr_ref, fmin_ref, shift_ref, qt_ref, k_ref, vt_ref, o_ref,
                sa_ref, sb_ref, pa_ref, pb_ref, m_ref, acc_ref, si_ref, sj_ref):
    p = pl.program_id(0)
    tq = FOX_TQ
    tk = FOX_TK
    nsub = tq // tk
    nq = k_ref.shape[0] // tq
    hrows = FOX_DH + FOX_ONES_ROWS
    row = lax.broadcasted_iota(jnp.int32, (2 * LANES, 1), 0)
    krow = lax.broadcasted_iota(jnp.int32, (tk, tq), 0)
    qcol = lax.broadcasted_iota(jnp.int32, (tk, tq), 1)
    m0 = jnp.full((1, tq), NEG_BIG, F32)
    acc0 = jnp.zeros((hrows, tq), F32)

    for hi in range(2):
        h = 2 * p + hi
        aug0 = LANES + hi * FOX_AUG_PER_HEAD
        mine = ((row // FOX_DH) == hi) | ((row >= aug0) & (row < aug0 + FOX_AUG_PER_HEAD))
        vrows = slice(hi * hrows, (hi + 1) * hrows)

        def score_tiles(qi, kj, masked):
            q0 = pl.multiple_of(qi * tq, tq)
            qm = jnp.where(mine, qt_ref[:, pl.ds(q0, tq)], jnp.zeros((), BF16))
            for u in range(nsub):
                k0 = pl.multiple_of(kj * tq + u * tk, tk)
                s_t = _dot(k_ref[pl.ds(k0, tk), :], qm)
                if masked:
                    s_t = jnp.where(krow + u * tk <= qcol, s_t, NEG_BIG)
                yield u, s_t

        def values(kj, u):
            return vt_ref[vrows, pl.ds(pl.multiple_of(kj * tq + u * tk, tk), tk)]

        def fast_produce(qi, kj, buf, masked=False):
            shift = shift_ref[h, qi]
            for u, s_t in score_tiles(qi, kj, masked):
                buf[u] = jnp.exp(s_t - shift).astype(BF16)

        def fast_consume(i, kj, buf, fresh=False):
            acc = acc0 if fresh else acc_ref[hi, i]
            for u in range(nsub):
                acc = acc + _dot(values(kj, u), buf[u])
            acc_ref[hi, i] = acc

        def slow_produce(qi, kj, buf, masked=False):
            for u, s_t in score_tiles(qi, kj, masked):
                buf[u] = s_t

        def slow_consume(i, kj, buf, fresh=False):
            m = m0 if fresh else m_ref[hi, i]
            acc = acc0 if fresh else acc_ref[hi, i]
            for u in range(nsub):
                s_t = buf[u]
                m_new = jnp.maximum(m, jnp.max(s_t, axis=0, keepdims=True))
                alpha = jnp.exp(m - m_new)
                pt = jnp.exp(s_t - m_new).astype(BF16)
                acc = alpha * acc + _dot(values(kj, u), pt)
                m = m_new
            m_ref[hi, i] = m
            acc_ref[hi, i] = acc

        def list_block(i, t):
            thr = thr_ref[h, i]
            for j in range(nq):
                si_ref[t] = i
                sj_ref[t] = j
                t = t + jnp.where((j < i) & (fmin_ref[h, j] <= thr), 1, 0)
            return t

        worst_shift = lax.fori_loop(0, nq, lambda i, w: jnp.maximum(w, shift_ref[h, i]), jnp.float32(0.0))

        def run_head(produce, consume, buf_a, buf_b):
            def diag_step(t, n):
                i0 = 2 * t
                produce(i0, i0, buf_a, masked=True)
                produce(i0 + 1, i0 + 1, buf_b, masked=True)
                consume(i0, i0, buf_a, fresh=True)
                consume(i0 + 1, i0 + 1, buf_b, fresh=True)
                return list_block(i0 + 1, list_block(i0, n))

            n_tiles = lax.fori_loop(0, nq // 2, diag_step, 0)
            for pad in range(FOX_STEPS):
                si_ref[n_tiles + pad] = nq
                sj_ref[n_tiles + pad] = 0
            m_ref[hi, nq] = m0
            acc_ref[hi, nq] = acc0

            def prefetch(t, buf):
                produce(jnp.minimum(si_ref[t], nq - 1), sj_ref[t], buf)

            prefetch(0, buf_a)

            def multi_step(tn, carry):
                t = FOX_STEPS * tn
                for d in range(0, FOX_STEPS, 2):
                    prefetch(t + d + 1, buf_b)
                    consume(si_ref[t + d], sj_ref[t + d], buf_a)
                    prefetch(t + d + 2, buf_a)
                    consume(si_ref[t + d + 1], sj_ref[t + d + 1], buf_b)
                return carry

            lax.fori_loop(0, (n_tiles + FOX_STEPS - 1) // FOX_STEPS, multi_step, 0)

        no_running_max = 2.0 * worst_shift <= FOX_MAX_SHIFT_GAP

        @pl.when(no_running_max)
        def _():
            run_head(fast_produce, fast_consume, pa_ref, pb_ref)

        @pl.when(jnp.logical_not(no_running_max))
        def _():
            run_head(slow_produce, slow_consume, sa_ref, sb_ref)

    def finish(i, carry):
        a0 = acc_ref[0, i]
        a1 = acc_ref[1, i]
        ot = jnp.concatenate([a0[:FOX_DH] / a0[FOX_DH:FOX_DH + 1], a1[:FOX_DH] / a1[FOX_DH:FOX_DH + 1]], axis=0)
        o_ref[pl.ds(pl.multiple_of(i * tq, tq), tq), :] = ot.T.astype(BF16)
        return carry

    lax.fori_loop(0, nq, finish, 0)


def _fox(thr, fmin_blk, shift, qt, kaug, vt):
    s = kaug.shape[1]
    tq = FOX_TQ
    nq = s // tq
    npair = FOX_HEADS // 2
    prow = 2 * (FOX_DH + FOX_ONES_ROWS)
    assert nq % 2 == 0
    hrows = FOX_DH + FOX_ONES_ROWS
    max_tiles = nq * (nq - 1) // 2 + FOX_STEPS
    once = pl.Buffered(1)
    grid_spec = pltpu.PrefetchScalarGridSpec(
        num_scalar_prefetch=3,
        grid=(npair,),
        in_specs=[pl.BlockSpec((2 * LANES, s), lambda p, *_: (p, 0), pipeline_mode=once),
                  pl.BlockSpec((None, s, 2 * LANES), lambda p, *_: (p, 0, 0), pipeline_mode=once),
                  pl.BlockSpec((prow, s), lambda p, *_: (p, 0), pipeline_mode=once)],
        out_specs=pl.BlockSpec((None, s, LANES), lambda p, *_: (p, 0, 0)),
        scratch_shapes=[pltpu.VMEM((tq // FOX_TK, FOX_TK, tq), F32), pltpu.VMEM((tq // FOX_TK, FOX_TK, tq), F32),
                        pltpu.VMEM((tq // FOX_TK, FOX_TK, tq), BF16), pltpu.VMEM((tq // FOX_TK, FOX_TK, tq), BF16),
                        pltpu.VMEM((2, nq + 1, 1, tq), F32), pltpu.VMEM((2, nq + 1, hrows, tq), F32),
                        pltpu.SMEM((max_tiles,), jnp.int32), pltpu.SMEM((max_tiles,), jnp.int32)],
    )
    return pl.pallas_call(
        _fox_kernel,
        grid_spec=grid_spec,
        out_shape=jax.ShapeDtypeStruct((npair, s, LANES), BF16),
        compiler_params=pltpu.CompilerParams(dimension_semantics=("arbitrary",), vmem_limit_bytes=VMEM_LIMIT),
        name="fox",
    )(thr, fmin_blk, shift, qt, kaug, vt)


def _tail_kernel(x_ref, ogla_ref, ofox_ref, omem_ref, gate_ref, wg_ref, wf_ref, wm_ref, wo_ref,
                 gffn_ref, w1_ref, w2_ref, gfin_ref, out_ref):
    gate = gate_ref[...]
    ofox = jnp.concatenate([ofox_ref[p] for p in range(FOX_HEADS // 2)], axis=1)
    merged = (gate[:, 0:D_MODEL].astype(F32) * _dot(ogla_ref[...], wg_ref[...])
              + gate[:, D_MODEL:2 * D_MODEL].astype(F32) * _dot(ofox, wf_ref[...])
              + gate[:, 2 * D_MODEL:3 * D_MODEL].astype(F32) * _dot(omem_ref[...], wm_ref[...]))
    h = x_ref[...] + _dot(merged.astype(BF16), wo_ref[...])
    u2 = _rms(h, gffn_ref[...]).astype(BF16)
    acc = jnp.zeros_like(h)
    for cidx in range(D_FF // FF_CHUNK):
        cs = slice(cidx * FF_CHUNK, (cidx + 1) * FF_CHUNK)
        a = jnp.maximum(_dot(u2, w1_ref[:, cs]), 0.0)
        acc = acc + _dot((a * a).astype(BF16), w2_ref[cs, :])
    out_ref[...] = _rms(h + acc, gfin_ref[...])


def _tail(x, ogla, ofox, omem, gate, wg, wf, wm, wo, gffn, w1, w2, gfin):
    s = x.shape[0]
    tm = TAIL_TILE
    full = lambda a: pl.BlockSpec(a.shape, lambda i: (0,) * a.ndim, pipeline_mode=pl.Buffered(1))
    row = lambda w: pl.BlockSpec((tm, w), lambda i: (i, 0))
    return pl.pallas_call(
        _tail_kernel,
        grid=(s // tm,),
        in_specs=[row(D_MODEL), row(GLA_V), pl.BlockSpec((FOX_HEADS // 2, tm, LANES), lambda i: (0, i, 0)),
                  row(MEM_W), row(3 * D_MODEL),
                  full(wg), full(wf), full(wm), full(wo), full(gffn), full(w1), full(w2), full(gfin)],
        out_specs=row(D_MODEL),
        out_shape=jax.ShapeDtypeStruct((s, D_MODEL), F32),
        compiler_params=pltpu.CompilerParams(dimension_semantics=("arbitrary",), vmem_limit_bytes=VMEM_LIMIT),
        name="tail",
    )(x, ogla, ofox, omem, gate, wg, wf, wm, wo, gffn, w1, w2, gfin)


def kernel(x, mem, g_mix, w_in, w_alpha_up, b_alpha, b_forget, g_gla_head, g_mem, w_mem_kv,
           w_gla_o, w_fox_o, w_mem_o, w_out, g_ffn, w_ff1, w_ff2, g_final):
    assert x.shape[0] == 1 and g_mix.shape[0] == 1, "single batch, single layer"
    s = x.shape[1]
    assert s % ROW_TILE == 0 and s % TAIL_TILE == 0 and s % FOX_TQ == 0
    assert FOX_TQ % FOX_TK == 0 and FOX_TQ % ROW_TILE == 0
    xs = x[0]
    w = w_in[0]

    o_ga = 2 * GLA_K + 2 * GLA_V
    o_fox = o_ga + GLA_LOWRANK
    o_ff = o_fox + 3 * FOX_W
    o_mq = o_ff + FOX_HEADS
    o_gate = o_mq + MEM_W
    w_gla = w[:, :o_ga].astype(BF16)
    w_fk = w[:, o_fox + FOX_W:o_fox + 2 * FOX_W].astype(BF16)
    w_qvt = jnp.concatenate([w[:, o_fox:o_fox + FOX_W], w[:, o_fox + 2 * FOX_W:o_ff]], axis=1).astype(BF16)
    w_mq = w[:, o_mq:o_gate].astype(BF16)
    w_gate = w[:, o_gate:].astype(BF16)
    w_small = jnp.concatenate(
        [w[:, o_ga:o_fox], w[:, o_ff:o_mq], jnp.zeros((D_MODEL, LANES - GLA_LOWRANK - FOX_HEADS), F32)], axis=1)
    w_small_hi = w_small.astype(BF16)
    w_small = jnp.concatenate([w_small_hi, (w_small - w_small_hi.astype(F32)).astype(BF16)], axis=1)
    w_up_hi = w_alpha_up[0].astype(BF16)
    w_up = jnp.concatenate([w_up_hi, (w_alpha_up[0] - w_up_hi.astype(F32)).astype(BF16)], axis=1)
    b_f = jnp.zeros((1, LANES), F32).at[0, GLA_LOWRANK:GLA_LOWRANK + FOX_HEADS].set(b_forget[0])

    mk, mv = _memkv(mem[0], g_mem, w_mem_kv[0].astype(BF16))
    gla_qkvg, loga, qt, kaug, vt, omem, gate, stats, statq = _proj(
        xs, g_mix, w_gla, w_fk, w_qvt, w_mq, w_gate, w_small, w_up, b_alpha, b_f, mk, mv)

    ogla = _gla(gla_qkvg, loga, g_gla_head.reshape(1, GLA_V))

    per_blk = lambda a: a.reshape(s // FOX_TQ, FOX_TQ // ROW_TILE, FOX_HEADS)
    qn = jnp.sqrt(jnp.max(per_blk(statq[:, :, 0]), axis=1))
    kn = jnp.sqrt(jnp.max(stats[:, 0, :FOX_HEADS], axis=0))
    fmax = jnp.max(per_blk(stats[:, 1, :FOX_HEADS]), axis=1)
    fmin = jnp.min(per_blk(stats[:, 2, :FOX_HEADS]), axis=1)
    shift = (1.02 * (FOX_DH ** -0.5)) * qn * kn[None, :]
    thr = 2.0 * shift + fmax + PRUNE_LOGIT_GAP
    ofox = _fox(thr.T, fmin.T, shift.T, qt, kaug, vt)

    out = _tail(xs, ogla, ofox, omem, gate, w_gla_o[0].astype(BF16), w_fox_o[0].astype(BF16),
                w_mem_o[0].astype(BF16), w_out[0].astype(BF16), g_ffn, w_ff1[0].astype(BF16),
                w_ff2[0].astype(BF16), g_final.reshape(1, D_MODEL))
    return out[None]
```

```python
import functools

import jax
import jax.numpy as jnp
import numpy as np
from jax import lax
from jax.experimental import pallas as pl
from jax.experimental.pallas import tpu as pltpu

D_MODEL = 1024
CHUNK = 64
EPS = 1e-6
GLA_HEADS = 4
GLA_DK = 64
GLA_DV = 128
GLA_LOWRANK = 16
GLA_TAU = 16.0
FOX_HEADS = 8
FOX_DH = 64
MEM_HEADS = 4
MEM_DH = 128
D_FF = 4 * D_MODEL
GLA_K = GLA_HEADS * GLA_DK
GLA_V = GLA_HEADS * GLA_DV
FOX_W = FOX_HEADS * FOX_DH
MEM_W = MEM_HEADS * MEM_DH

LANES = 128
FOX_ONES_ROWS = 16
FOX_AUG_PER_HEAD = 6
ROW_TILE = 256
GLA_TILE = 512
GLA_GROUP = 256
FOX_TQ = 512
FOX_TK = 256
FOX_STEPS = 8
TAIL_TILE = 512
TAIL_GROUP = 256
FF_CHUNK = 1024
VMEM_LIMIT = 56 * 1024 * 1024
NEG_BIG = -1e30
PRUNE_LOGIT_GAP = 104.0
FOX_MAX_SHIFT_GAP = 50.0

F32 = jnp.float32
BF16 = jnp.bfloat16


def _rms(xf, g):
    r = lax.rsqrt(jnp.mean(xf * xf, axis=-1, keepdims=True) + EPS)
    return (xf * r) * g


def _log_sigmoid(x):
    return jnp.minimum(x, 0.0) - jnp.log1p(jnp.exp(-jnp.abs(x)))


def _sigmoid(x):
    return 1.0 / (1.0 + jnp.exp(-x))


def _dot(a, b):
    return jnp.dot(a, b, preferred_element_type=F32)


def _dot_nt(a, b):
    return lax.dot_general(a, b, (((1,), (1,)), ((), ())), preferred_element_type=F32)


def _dot_tn(a, b):
    return lax.dot_general(a, b, (((0,), (0,)), ((), ())), preferred_element_type=F32)


def _memkv_kernel(mem_ref, g_ref, w_ref, mk_ref, mv_ref):
    mn = _rms(mem_ref[...], g_ref[...]).astype(BF16)
    kv = _dot(mn, w_ref[...])
    mk_ref[...] = kv[:, :MEM_W].astype(BF16)
    mv_ref[...] = kv[:, MEM_W:].astype(BF16)


def _memkv(mem, g_mem, w_mem_kv):
    m = mem.shape[0]
    return pl.pallas_call(
        _memkv_kernel,
        out_shape=(jax.ShapeDtypeStruct((m, MEM_W), BF16), jax.ShapeDtypeStruct((m, MEM_W), BF16)),
        name="memkv",
    )(mem, g_mem, w_mem_kv)


def _proj_kernel(x_ref, gmix_ref, wgla_ref, wfk_ref, wqv_ref, wmq_ref, wgate_ref, wsmall_ref, wup_ref,
                 balpha_ref, bforget_ref, mk_ref, mv_ref, pqt_ref, pk_ref,
                 gla_ref, loga_ref, qt_ref, kaug_ref, vt_ref, omem_ref, gate_ref, stat_ref, statq_ref,
                 carry_ref, wqvt_ref):
    tm = x_ref.shape[0]

    @pl.when(pl.program_id(0) == 0)
    def _():
        carry_ref[...] = jnp.zeros_like(carry_ref)
        wqvt_ref[...] = wqv_ref[...].T

    u = _rms(x_ref[...], gmix_ref[...])
    ub = u.astype(BF16)

    u_lo = (u - ub.astype(F32)).astype(BF16)
    parts = _dot(jnp.concatenate([ub, u_lo], axis=0), wsmall_ref[...])
    small = (parts[:tm, :LANES] + parts[:tm, LANES:]) + (parts[tm:, :LANES] + parts[tm:, LANES:])

    mq = _dot(ub, wmq_ref[...]).astype(BF16)
    scale = MEM_DH ** -0.5
    heads = [slice(h * MEM_DH, (h + 1) * MEM_DH) for h in range(MEM_HEADS)]
    scores = [_dot_nt(mq[:, sl], mk_ref[:, sl]) * scale for sl in heads]

    gla_ref[...] = _dot(ub, wgla_ref[...]).astype(BF16)
    fkb = _dot(ub, wfk_ref[...]).astype(BF16)

    ga = small[:, :GLA_LOWRANK]
    ga_hi = ga.astype(BF16)
    ga_lo = (ga - ga_hi.astype(F32)).astype(BF16)
    up = _dot(jnp.concatenate([ga_hi, ga_lo], axis=0), wup_ref[...])
    alpha_pre = ((up[:tm, :GLA_K] + up[:tm, GLA_K:]) + (up[tm:, :GLA_K] + up[tm:, GLA_K:])) + balpha_ref[...]
    loga_ref[...] = _log_sigmoid(alpha_pre) * (1.0 / GLA_TAU)

    lane = lax.broadcasted_iota(jnp.int32, (tm, LANES), 1)
    ff_valid = (lane >= GLA_LOWRANK) & (lane < GLA_LOWRANK + FOX_HEADS)
    logf = jnp.where(ff_valid, _log_sigmoid(small + bforget_ref[...]), 0.0)
    logf = pltpu.roll(logf, LANES - GLA_LOWRANK, 1)
    r = lax.broadcasted_iota(jnp.int32, (tm, tm), 0)
    c = lax.broadcasted_iota(jnp.int32, (tm, tm), 1)
    tri = jnp.where(r >= c, 1.0, 0.0).astype(BF16)
    lf_hi = logf.astype(BF16)
    lf_r = logf - lf_hi.astype(F32)
    lf_mid = lf_r.astype(BF16)
    lf_lo = (lf_r - lf_mid.astype(F32)).astype(BF16)
    c3 = _dot(tri, jnp.concatenate([lf_hi, lf_mid, lf_lo], axis=1))
    fcum = (c3[:, :LANES] + c3[:, LANES:2 * LANES] + c3[:, 2 * LANES:]) + carry_ref[...]
    carry_ref[...] = fcum[tm - 1:tm, :]

    f_hi = fcum.astype(BF16)
    rem = fcum - f_hi.astype(F32)
    f_mid = rem.astype(BF16)
    f_lo = (rem - f_mid.astype(F32)).astype(BF16)
    f3 = jnp.concatenate([f_hi, f_mid, f_lo], axis=1)
    def aug_slot(idx):
        a = idx % LANES
        return a % FOX_AUG_PER_HEAD, a < 2 * FOX_AUG_PER_HEAD

    slot_q, in_q = aug_slot(lax.broadcasted_iota(jnp.int32, (FOX_W, 1), 0))
    slot_k, in_k = aug_slot(lax.broadcasted_iota(jnp.int32, (1, FOX_W), 1))
    augqt = _dot_nt(pqt_ref[...], f3) + jnp.where(in_q & (slot_q < 3), -1.0, 0.0)
    augk = _dot(f3, pk_ref[...]) + jnp.where(in_k & (slot_k >= 3), 1.0, 0.0)
    gate_ref[...] = _sigmoid(_dot(ub, wgate_ref[...])).astype(BF16)

    probs = []
    for s in scores:
        m = jnp.max(s, axis=-1, keepdims=True)
        e = jnp.exp(s - m)
        probs.append((e / jnp.sum(e, axis=-1, keepdims=True)).astype(BF16))
    omem_ref[...] = jnp.concatenate([_dot(pr, mv_ref[:, sl]) for pr, sl in zip(probs, heads)], axis=-1).astype(BF16)

    qvt = _dot_nt(wqvt_ref[...], ub)
    fqt = (qvt[:FOX_W] * (FOX_DH ** -0.5)).astype(BF16)
    vt = qvt[FOX_W:].astype(BF16)
    ones = jnp.ones((FOX_ONES_ROWS, tm), BF16)
    vt_parts = []
    for h in range(FOX_HEADS):
        vt_parts += [vt[h * FOX_DH:(h + 1) * FOX_DH, :], ones]
    vt_ref[...] = jnp.concatenate(vt_parts, axis=0)

    for p in range(FOX_HEADS // 2):
        src = slice(p * LANES, (p + 1) * LANES)
        qt_ref[2 * p * LANES:(2 * p + 1) * LANES, :] = fqt[src]
        qt_ref[(2 * p + 1) * LANES:(2 * p + 2) * LANES, :] = augqt[src].astype(BF16)
        kaug_ref[p, :, :LANES] = fkb[:, src]
        kaug_ref[p, :, LANES:] = augk[:, src].astype(BF16)

    fq32 = fqt.astype(F32) * (FOX_DH ** 0.5)
    nq2 = jnp.sum((fq32 * fq32).reshape(FOX_HEADS, FOX_DH, tm), axis=1)
    statq_ref[0] = jnp.broadcast_to(jnp.max(nq2, axis=1, keepdims=True), (FOX_HEADS, LANES))
    gi = lax.broadcasted_iota(jnp.int32, (FOX_W, LANES), 0) // FOX_DH
    gj = lax.broadcasted_iota(jnp.int32, (FOX_W, LANES), 1)
    group = jnp.where(gi == gj, 1.0, 0.0).astype(BF16)
    fk = fkb.astype(F32)
    nk2 = jnp.max(_dot((fk * fk).astype(BF16), group), axis=0, keepdims=True)
    fmax = jnp.max(fcum, axis=0, keepdims=True)
    fmin = jnp.min(fcum, axis=0, keepdims=True)
    stat_ref[0] = jnp.concatenate([nk2, fmax, fmin, jnp.zeros((5, LANES), F32)], axis=0)


def _aug_placement():
    pq = np.zeros((3 * LANES, FOX_W), np.float32)
    pk = np.zeros((3 * LANES, FOX_W), np.float32)
    for h in range(FOX_HEADS):
        base = (h // 2) * LANES + (h % 2) * FOX_AUG_PER_HEAD
        for c in range(3):
            pq[c * LANES + h, base + 3 + c] = 1.0
            pk[c * LANES + h, base + c] = 1.0
    return jnp.asarray(pq.T, BF16), jnp.asarray(pk, BF16)


def _proj(x, g_mix, w_gla, w_fk, w_qvt, w_mq, w_gate, w_small, w_up, b_alpha, b_forget, mk, mv):
    s = x.shape[0]
    tm = ROW_TILE
    nt = s // tm
    pqt, pk = _aug_placement()
    full = lambda shape: pl.BlockSpec(shape, lambda i: (0,) * len(shape))
    row = lambda w: pl.BlockSpec((tm, w), lambda i: (i, 0))
    col = lambda r: pl.BlockSpec((r, tm), lambda i: (0, i))
    stat = pl.BlockSpec((1, 8, LANES), lambda i: (i, 0, 0))
    vt_rows = FOX_HEADS * (FOX_DH + FOX_ONES_ROWS)
    return pl.pallas_call(
        _proj_kernel,
        grid=(nt,),
        in_specs=[row(D_MODEL), full((1, D_MODEL)), full(w_gla.shape), full(w_fk.shape), full(w_qvt.shape),
                  full(w_mq.shape), full(w_gate.shape), full(w_small.shape), full(w_up.shape),
                  full(b_alpha.shape), full(b_forget.shape), full(mk.shape), full(mv.shape),
                  full(pqt.shape), full(pk.shape)],
        out_specs=[row(w_gla.shape[1]), row(GLA_K), col(2 * FOX_W),
                   pl.BlockSpec((FOX_HEADS // 2, tm, 2 * LANES), lambda i: (0, i, 0)), col(vt_rows), row(MEM_W),
                   row(w_gate.shape[1]), stat, stat],
        out_shape=[jax.ShapeDtypeStruct((s, w_gla.shape[1]), BF16),
                   jax.ShapeDtypeStruct((s, GLA_K), F32),
                   jax.ShapeDtypeStruct((2 * FOX_W, s), BF16),
                   jax.ShapeDtypeStruct((FOX_HEADS // 2, s, 2 * LANES), BF16),
                   jax.ShapeDtypeStruct((vt_rows, s), BF16),
                   jax.ShapeDtypeStruct((s, MEM_W), BF16),
                   jax.ShapeDtypeStruct((s, w_gate.shape[1]), BF16),
                   jax.ShapeDtypeStruct((nt, 8, LANES), F32),
                   jax.ShapeDtypeStruct((nt, 8, LANES), F32)],
        scratch_shapes=[pltpu.VMEM((1, LANES), F32), pltpu.VMEM((w_qvt.shape[1], w_qvt.shape[0]), BF16)],
        compiler_params=pltpu.CompilerParams(dimension_semantics=("arbitrary",), vmem_limit_bytes=VMEM_LIMIT),
        name="proj",
    )(x, g_mix, w_gla, w_fk, w_qvt, w_mq, w_gate, w_small, w_up, b_alpha, b_forget, mk, mv, pqt, pk)


def _gla_kernel(qkvg_ref, loga_ref, ghead_ref, o_ref, state_ref):
    t = GLA_GROUP
    nc = t // CHUNK

    @pl.when(pl.program_id(0) == 0)
    def _():
        state_ref[...] = jnp.zeros_like(state_ref)

    r = lax.broadcasted_iota(jnp.int32, (t, t), 0)
    c = lax.broadcasted_iota(jnp.int32, (t, t), 1)
    same = (r // CHUNK) == (c // CHUNK)
    lower = r >= c
    causal = same & lower
    anti = same & (r < c)
    cum = jnp.where(causal, 1.0, 0.0).astype(BF16)
    lane = lax.broadcasted_iota(jnp.int32, (1, LANES), 1)

    for g0 in range(0, qkvg_ref.shape[0], t):
        rg = slice(g0, g0 + t)
        la = loga_ref[rg, :]
        la_hi = la.astype(BF16)
        la_r = la - la_hi.astype(F32)
        la_mid = la_r.astype(BF16)
        la_lo = (la_r - la_mid.astype(F32)).astype(BF16)
        b3 = _dot(cum, jnp.concatenate([la_hi, la_mid, la_lo], axis=1))
        b = b3[:, :GLA_K] + b3[:, GLA_K:2 * GLA_K] + b3[:, 2 * GLA_K:]
        b_last = jnp.concatenate(
            [jnp.broadcast_to(b[(ci + 1) * CHUNK - 1:(ci + 1) * CHUNK, :], (CHUNK, GLA_K)) for ci in range(nc)],
            axis=0)
        e_pos = jnp.exp(b)
        e_neg = jnp.exp(-b)
        q = qkvg_ref[rg, 0:GLA_K].astype(F32) * (GLA_DK ** -0.5)
        k = qkvg_ref[rg, GLA_K:2 * GLA_K].astype(F32)
        q_pos = (q * e_pos).astype(BF16)
        q_neg = (q * e_neg).astype(BF16)
        k_pos = (k * e_pos).astype(BF16)
        k_neg = (k * e_neg).astype(BF16)
        k_dec = (k * jnp.exp(b_last - b)).astype(BF16)

        chunks = [slice(ci * CHUNK, (ci + 1) * CHUNK) for ci in range(nc)]
        pairs = [slice((h // 2) * LANES, (h // 2 + 1) * LANES) for h in range(GLA_HEADS)]
        in_head = [(lane // GLA_DK) == (h % 2) for h in range(GLA_HEADS)]
        vs = [qkvg_ref[rg, 2 * GLA_K + h * GLA_DV:2 * GLA_K + (h + 1) * GLA_DV] for h in range(GLA_HEADS)]
        qps = [jnp.where(in_head[h], q_pos[:, pairs[h]], jnp.zeros((), BF16)) for h in range(GLA_HEADS)]

        kvs = [[jnp.where(in_head[h], _dot_tn(vs[h][rows], k_dec[rows, pairs[h]]), 0.0) for rows in chunks]
               for h in range(GLA_HEADS)]
        sts = [state_ref[h] for h in range(GLA_HEADS)]
        inter = [[] for _ in range(GLA_HEADS)]
        for ci, rows in enumerate(chunks):
            for h in range(GLA_HEADS):
                inter[h].append(_dot_nt(qps[h][rows], sts[h].astype(BF16)))
                dec = e_pos[(ci + 1) * CHUNK - 1:(ci + 1) * CHUNK, pairs[h]]
                sts[h] = sts[h] * dec + kvs[h][ci]
        for h in range(GLA_HEADS):
            state_ref[h] = sts[h]

        attns = []
        for h in range(GLA_HEADS):
            qn = jnp.where(in_head[h], q_neg[:, pairs[h]], jnp.zeros((), BF16))
            a_c = _dot_nt(qps[h], k_neg[:, pairs[h]])
            a_a = _dot_nt(qn, k_pos[:, pairs[h]])
            attns.append(jnp.where(causal, a_c, jnp.where(anti, a_a, 0.0)).astype(BF16))
        outs = [_dot(attns[h], vs[h]) + jnp.concatenate(inter[h], axis=0) for h in range(GLA_HEADS)]
        for h in range(GLA_HEADS):
            o = outs[h]

            gh = ghead_ref[:, h * GLA_DV:(h + 1) * GLA_DV]
            on = _rms(o, gh)
            gsl = slice(2 * GLA_K + GLA_V + h * GLA_DV, 2 * GLA_K + GLA_V + (h + 1) * GLA_DV)
            gg = qkvg_ref[rg, gsl].astype(F32)
            o_ref[rg, h * GLA_DV:(h + 1) * GLA_DV] = (on * (gg * _sigmoid(gg))).astype(BF16)


def _gla(qkvg, loga, ghead):
    s = qkvg.shape[0]
    t = GLA_TILE
    assert s % t == 0 and t % GLA_GROUP == 0
    return pl.pallas_call(
        _gla_kernel,
        grid=(s // t,),
        in_specs=[pl.BlockSpec((t, qkvg.shape[1]), lambda i: (i, 0)),
                  pl.BlockSpec((t, GLA_K), lambda i: (i, 0)),
                  pl.BlockSpec((1, GLA_V), lambda i: (0, 0))],
        out_specs=pl.BlockSpec((t, GLA_V), lambda i: (i, 0)),
        out_shape=jax.ShapeDtypeStruct((s, GLA_V), BF16),
        scratch_shapes=[pltpu.VMEM((GLA_HEADS, GLA_DV, LANES), F32)],
        compiler_params=pltpu.CompilerParams(dimension_semantics=("arbitrary",), vmem_limit_bytes=VMEM_LIMIT),
        name="gla",
    )(qkvg, loga, ghead)


def _fox_kernel(thr_ref, fmin_ref, shift_ref, qt_ref, k_ref, vt_ref, o_ref,
                sa_ref, sb_ref, pa_ref, pb_ref, m_ref, acc_ref, si_ref, sj_ref):
    p = pl.program_id(0)
    tq = FOX_TQ
    tk = FOX_TK
    nsub = tq // tk
    nq = k_ref.shape[0] // tq
    hrows = FOX_DH + FOX_ONES_ROWS
    row = lax.broadcasted_iota(jnp.int32, (2 * LANES, 1), 0)
    krow = lax.broadcasted_iota(jnp.int32, (tk, tq), 0)
    qcol = lax.broadcasted_iota(jnp.int32, (tk, tq), 1)
    m0 = jnp.full((1, tq), NEG_BIG, F32)
    acc0 = jnp.zeros((hrows, tq), F32)

    for hi in range(2):
        h = 2 * p + hi
        aug0 = LANES + hi * FOX_AUG_PER_HEAD
        mine = ((row // FOX_DH) == hi) | ((row >= aug0) & (row < aug0 + FOX_AUG_PER_HEAD))
        vrows = slice(hi * hrows, (hi + 1) * hrows)

        def score_tiles(qi, kj, masked):
            q0 = pl.multiple_of(qi * tq, tq)
            qm = jnp.where(mine, qt_ref[:, pl.ds(q0, tq)], jnp.zeros((), BF16))
            for u in range(nsub):
                k0 = pl.multiple_of(kj * tq + u * tk, tk)
                s_t = _dot(k_ref[pl.ds(k0, tk), :], qm)
                if masked:
                    s_t = jnp.where(krow + u * tk <= qcol, s_t, NEG_BIG)
                yield u, s_t

        def values(kj, u):
            return vt_ref[vrows, pl.ds(pl.multiple_of(kj * tq + u * tk, tk), tk)]

        def fast_produce(qi, kj, buf, masked=False):
            shift = shift_ref[h, qi]
            for u, s_t in score_tiles(qi, kj, masked):
                buf[u] = jnp.exp(s_t - shift).astype(BF16)

        def fast_consume(i, kj, buf, fresh=False):
            acc = acc0 if fresh else acc_ref[hi, i]
            for u in range(nsub):
                acc = acc + _dot(values(kj, u), buf[u])
            acc_ref[hi, i] = acc

        def slow_produce(qi, kj, buf, masked=False):
            for u, s_t in score_tiles(qi, kj, masked):
                buf[u] = s_t

        def slow_consume(i, kj, buf, fresh=False):
            m = m0 if fresh else m_ref[hi, i]
            acc = acc0 if fresh else acc_ref[hi, i]
            for u in range(nsub):
                s_t = buf[u]
                m_new = jnp.maximum(m, jnp.max(s_t, axis=0, keepdims=True))
                alpha = jnp.exp(m - m_new)
                pt = jnp.exp(s_t - m_new).astype(BF16)
                acc = alpha * acc + _dot(values(kj, u), pt)
                m = m_new
            m_ref[hi, i] = m
            acc_ref[hi, i] = acc

        def list_block(i, t):
            thr = thr_ref[h, i]
            for j in range(nq):
                si_ref[t] = i
                sj_ref[t] = j
                t = t + jnp.where((j < i) & (fmin_ref[h, j] <= thr), 1, 0)
            return t

        worst_shift = lax.fori_loop(0, nq, lambda i, w: jnp.maximum(w, shift_ref[h, i]), jnp.float32(0.0))

        def run_head(produce, consume, buf_a, buf_b):
            def diag_prefetch(i, buf):
                ic = jnp.minimum(i, nq - 1)
                produce(ic, ic, buf, masked=True)

            diag_prefetch(0, buf_a)

            def diag_step(tn, n):
                i = FOX_STEPS * tn
                for d in range(0, FOX_STEPS, 2):
                    diag_prefetch(i + d + 1, buf_b)
                    consume(i + d, i + d, buf_a, fresh=True)
                    diag_prefetch(i + d + 2, buf_a)
                    consume(i + d + 1, i + d + 1, buf_b, fresh=True)
                for d in range(FOX_STEPS):
                    n = list_block(i + d, n)
                return n

            n_tiles = lax.fori_loop(0, nq // FOX_STEPS, diag_step, 0)
            for pad in range(FOX_STEPS):
                si_ref[n_tiles + pad] = nq
                sj_ref[n_tiles + pad] = 0
            m_ref[hi, nq] = m0
            acc_ref[hi, nq] = acc0

            def prefetch(t, buf):
                produce(jnp.minimum(si_ref[t], nq - 1), sj_ref[t], buf)

            prefetch(0, buf_a)

            def multi_step(tn, carry):
                t = FOX_STEPS * tn
                for d in range(0, FOX_STEPS, 2):
                    prefetch(t + d + 1, buf_b)
                    consume(si_ref[t + d], sj_ref[t + d], buf_a)
                    prefetch(t + d + 2, buf_a)
                    consume(si_ref[t + d + 1], sj_ref[t + d + 1], buf_b)
                return carry

            lax.fori_loop(0, (n_tiles + FOX_STEPS - 1) // FOX_STEPS, multi_step, 0)

        no_running_max = 2.0 * worst_shift <= FOX_MAX_SHIFT_GAP

        @pl.when(no_running_max)
        def _():
            run_head(fast_produce, fast_consume, pa_ref, pb_ref)

        @pl.when(jnp.logical_not(no_running_max))
        def _():
            run_head(slow_produce, slow_consume, sa_ref, sb_ref)

    def finish(i, carry):
        a0 = acc_ref[0, i]
        a1 = acc_ref[1, i]
        ot = jnp.concatenate([a0[:FOX_DH] / a0[FOX_DH:FOX_DH + 1], a1[:FOX_DH] / a1[FOX_DH:FOX_DH + 1]], axis=0)
        o_ref[pl.ds(pl.multiple_of(i * tq, tq), tq), :] = ot.T.astype(BF16)
        return carry

    lax.fori_loop(0, nq, finish, 0)


def _fox(thr, fmin_blk, shift, qt, kaug, vt):
    s = kaug.shape[1]
    tq = FOX_TQ
    nq = s // tq
    npair = FOX_HEADS // 2
    prow = 2 * (FOX_DH + FOX_ONES_ROWS)
    assert nq % FOX_STEPS == 0 and FOX_STEPS % 2 == 0
    hrows = FOX_DH + FOX_ONES_ROWS
    max_tiles = nq * (nq - 1) // 2 + FOX_STEPS
    once = pl.Buffered(1)
    grid_spec = pltpu.PrefetchScalarGridSpec(
        num_scalar_prefetch=3,
        grid=(npair,),
        in_specs=[pl.BlockSpec((2 * LANES, s), lambda p, *_: (p, 0), pipeline_mode=once),
                  pl.BlockSpec((None, s, 2 * LANES), lambda p, *_: (p, 0, 0), pipeline_mode=once),
                  pl.BlockSpec((prow, s), lambda p, *_: (p, 0), pipeline_mode=once)],
        out_specs=pl.BlockSpec((None, s, LANES), lambda p, *_: (p, 0, 0)),
        scratch_shapes=[pltpu.VMEM((tq // FOX_TK, FOX_TK, tq), F32), pltpu.VMEM((tq // FOX_TK, FOX_TK, tq), F32),
                        pltpu.VMEM((tq // FOX_TK, FOX_TK, tq), BF16), pltpu.VMEM((tq // FOX_TK, FOX_TK, tq), BF16),
                        pltpu.VMEM((2, nq + 1, 1, tq), F32), pltpu.VMEM((2, nq + 1, hrows, tq), F32),
                        pltpu.SMEM((max_tiles,), jnp.int32), pltpu.SMEM((max_tiles,), jnp.int32)],
    )
    return pl.pallas_call(
        _fox_kernel,
        grid_spec=grid_spec,
        out_shape=jax.ShapeDtypeStruct((npair, s, LANES), BF16),
        compiler_params=pltpu.CompilerParams(dimension_semantics=("arbitrary",), vmem_limit_bytes=VMEM_LIMIT),
        name="fox",
    )(thr, fmin_blk, shift, qt, kaug, vt)


def _tail_kernel(x_ref, ogla_ref, ofox_ref, omem_ref, gate_ref, wg_ref, wf_ref, wm_ref, wo_ref,
                 gffn_ref, w1_ref, w2_ref, gfin_ref, out_ref):
    groups = [slice(r0, r0 + TAIL_GROUP) for r0 in range(0, x_ref.shape[0], TAIL_GROUP)]
    merged = []
    for rg in groups:
        ofox = jnp.concatenate([ofox_ref[p, rg, :] for p in range(FOX_HEADS // 2)], axis=1)
        merged.append((gate_ref[rg, 0:D_MODEL].astype(F32) * _dot(ogla_ref[rg, :], wg_ref[...])
                       + gate_ref[rg, D_MODEL:2 * D_MODEL].astype(F32) * _dot(ofox, wf_ref[...])
                       + gate_ref[rg, 2 * D_MODEL:3 * D_MODEL].astype(F32) * _dot(omem_ref[rg, :], wm_ref[...])
                       ).astype(BF16))
    hs = [x_ref[rg, :] + _dot(mg, wo_ref[...]) for rg, mg in zip(groups, merged)]
    u2s = [_rms(h, gffn_ref[...]).astype(BF16) for h in hs]
    accs = [jnp.zeros_like(h) for h in hs]
    for cidx in range(D_FF // FF_CHUNK):
        cs = slice(cidx * FF_CHUNK, (cidx + 1) * FF_CHUNK)
        for g, u2 in enumerate(u2s):
            a = jnp.maximum(_dot(u2, w1_ref[:, cs]), 0.0)
            accs[g] = accs[g] + _dot((a * a).astype(BF16), w2_ref[cs, :])
    for rg, h, acc in zip(groups, hs, accs):
        out_ref[rg, :] = _rms(h + acc, gfin_ref[...])


def _tail(x, ogla, ofox, omem, gate, wg, wf, wm, wo, gffn, w1, w2, gfin):
    s = x.shape[0]
    tm = TAIL_TILE
    full = lambda a: pl.BlockSpec(a.shape, lambda i: (0,) * a.ndim, pipeline_mode=pl.Buffered(1))
    row = lambda w: pl.BlockSpec((tm, w), lambda i: (i, 0))
    return pl.pallas_call(
        _tail_kernel,
        grid=(s // tm,),
        in_specs=[row(D_MODEL), row(GLA_V), pl.BlockSpec((FOX_HEADS // 2, tm, LANES), lambda i: (0, i, 0)),
                  row(MEM_W), row(3 * D_MODEL),
                  full(wg), full(wf), full(wm), full(wo), full(gffn), full(w1), full(w2), full(gfin)],
        out_specs=row(D_MODEL),
        out_shape=jax.ShapeDtypeStruct((s, D_MODEL), F32),
        compiler_params=pltpu.CompilerParams(dimension_semantics=("arbitrary",), vmem_limit_bytes=VMEM_LIMIT),
        name="tail",
    )(x, ogla, ofox, omem, gate, wg, wf, wm, wo, gffn, w1, w2, gfin)


def kernel(x, mem, g_mix, w_in, w_alpha_up, b_alpha, b_forget, g_gla_head, g_mem, w_mem_kv,
           w_gla_o, w_fox_o, w_mem_o, w_out, g_ffn, w_ff1, w_ff2, g_final):
    assert x.shape[0] == 1 and g_mix.shape[0] == 1, "single batch, single layer"
    s = x.shape[1]
    assert s % ROW_TILE == 0 and s % TAIL_TILE == 0 and s % FOX_TQ == 0
    assert FOX_TQ % FOX_TK == 0 and FOX_TQ % ROW_TILE == 0
    xs = x[0]
    w = w_in[0]

    o_ga = 2 * GLA_K + 2 * GLA_V
    o_fox = o_ga + GLA_LOWRANK
    o_ff = o_fox + 3 * FOX_W
    o_mq = o_ff + FOX_HEADS
    o_gate = o_mq + MEM_W
    w_gla = w[:, :o_ga].astype(BF16)
    w_fk = w[:, o_fox + FOX_W:o_fox + 2 * FOX_W].astype(BF16)
    w_qvt = jnp.concatenate([w[:, o_fox:o_fox + FOX_W], w[:, o_fox + 2 * FOX_W:o_ff]], axis=1).astype(BF16)
    w_mq = w[:, o_mq:o_gate].astype(BF16)
    w_gate = w[:, o_gate:].astype(BF16)
    w_small = jnp.concatenate(
        [w[:, o_ga:o_fox], w[:, o_ff:o_mq], jnp.zeros((D_MODEL, LANES - GLA_LOWRANK - FOX_HEADS), F32)], axis=1)
    w_small_hi = w_small.astype(BF16)
    w_small = jnp.concatenate([w_small_hi, (w_small - w_small_hi.astype(F32)).astype(BF16)], axis=1)
    w_up_hi = w_alpha_up[0].astype(BF16)
    w_up = jnp.concatenate([w_up_hi, (w_alpha_up[0] - w_up_hi.astype(F32)).astype(BF16)], axis=1)
    b_f = jnp.zeros((1, LANES), F32).at[0, GLA_LOWRANK:GLA_LOWRANK + FOX_HEADS].set(b_forget[0])

    mk, mv = _memkv(mem[0], g_mem, w_mem_kv[0].astype(BF16))
    gla_qkvg, loga, qt, kaug, vt, omem, gate, stats, statq = _proj(
        xs, g_mix, w_gla, w_fk, w_qvt, w_mq, w_gate, w_small, w_up, b_alpha, b_f, mk, mv)

    ogla = _gla(gla_qkvg, loga, g_gla_head.reshape(1, GLA_V))

    per_blk = lambda a: a.reshape(s // FOX_TQ, FOX_TQ // ROW_TILE, FOX_HEADS)
    qn = jnp.sqrt(jnp.max(per_blk(statq[:, :, 0]), axis=1))
    kn = jnp.sqrt(jnp.max(stats[:, 0, :FOX_HEADS], axis=0))
    fmax = jnp.max(per_blk(stats[:, 1, :FOX_HEADS]), axis=1)
    fmin = jnp.min(per_blk(stats[:, 2, :FOX_HEADS]), axis=1)
    shift = (1.02 * (FOX_DH ** -0.5)) * qn * kn[None, :]
    thr = 2.0 * shift + fmax + PRUNE_LOGIT_GAP
    ofox = _fox(thr.T, fmin.T, shift.T, qt, kaug, vt)

    out = _tail(xs, ogla, ofox, omem, gate, w_gla_o[0].astype(BF16), w_fox_o[0].astype(BF16),
                w_mem_o[0].astype(BF16), w_out[0].astype(BF16), g_ffn, w_ff1[0].astype(BF16),
                w_ff2[0].astype(BF16), g_final.reshape(1, D_MODEL))
    return out[None]
```

```python
import functools

import jax
import jax.numpy as jnp
import numpy as np
from jax import lax
from jax.experimental import pallas as pl
from jax.experimental.pallas import tpu as pltpu

D_MODEL = 1024
CHUNK = 64
EPS = 1e-6
GLA_HEADS = 4
GLA_DK = 64
GLA_DV = 128
GLA_LOWRANK = 16
GLA_TAU = 16.0
FOX_HEADS = 8
FOX_DH = 64
MEM_HEADS = 4
MEM_DH = 128
D_FF = 4 * D_MODEL
GLA_K = GLA_HEADS * GLA_DK
GLA_V = GLA_HEADS * GLA_DV
FOX_W = FOX_HEADS * FOX_DH
MEM_W = MEM_HEADS * MEM_DH
OFF_GA = 2 * GLA_K + 2 * GLA_V
OFF_FQ = OFF_GA + GLA_LOWRANK
OFF_FK = OFF_FQ + FOX_W
OFF_FV = OFF_FK + FOX_W
OFF_FF = OFF_FV + FOX_W
OFF_MQ = OFF_FF + FOX_HEADS
OFF_GATE = OFF_MQ + MEM_W
D_IN = OFF_GATE + 3 * D_MODEL

LANES = 128
FOX_ONES_ROWS = 16
FOX_AUG_PER_HEAD = 6
ROW_TILE = 256
GLA_TILE = 512
GLA_GROUP = 256
FOX_TQ = 512
FOX_TK = 256
FOX_STEPS = 8
TAIL_TILE = 512
TAIL_GROUP = 256
FF_CHUNK = 1024
VMEM_LIMIT = 56 * 1024 * 1024
NEG_BIG = -1e30
PRUNE_LOGIT_GAP = 104.0
FOX_MAX_SHIFT_GAP = 50.0

F32 = jnp.float32
BF16 = jnp.bfloat16


def _rms(xf, g):
    r = lax.rsqrt(jnp.mean(xf * xf, axis=-1, keepdims=True) + EPS)
    return (xf * r) * g


def _log_sigmoid(x):
    return jnp.minimum(x, 0.0) - jnp.log1p(jnp.exp(-jnp.abs(x)))


def _sigmoid(x):
    return 1.0 / (1.0 + jnp.exp(-x))


def _dot(a, b):
    return jnp.dot(a, b, preferred_element_type=F32)


def _dot_nt(a, b):
    return lax.dot_general(a, b, (((1,), (1,)), ((), ())), preferred_element_type=F32)


def _dot_tn(a, b):
    return lax.dot_general(a, b, (((0,), (0,)), ((), ())), preferred_element_type=F32)


def _memkv_kernel(mem_ref, g_ref, w_ref, mk_ref, mv_ref):
    mn = _rms(mem_ref[...], g_ref[...]).astype(BF16)
    kv = _dot(mn, w_ref[...])
    mk_ref[...] = kv[:, :MEM_W].astype(BF16)
    mv_ref[...] = kv[:, MEM_W:].astype(BF16)


def _memkv(mem, g_mem, w_mem_kv):
    m = mem.shape[0]
    return pl.pallas_call(
        _memkv_kernel,
        out_shape=(jax.ShapeDtypeStruct((m, MEM_W), BF16), jax.ShapeDtypeStruct((m, MEM_W), BF16)),
        name="memkv",
    )(mem, g_mem, w_mem_kv)


def _proj_kernel(x_ref, gmix_ref, wall_ref, wsmall_ref, wup_ref,
                 balpha_ref, bforget_ref, mk_ref, mv_ref, pqt_ref, pk_ref,
                 gla_ref, loga_ref, qt_ref, kaug_ref, vt_ref, omem_ref, gate_ref, stat_ref, statq_ref,
                 carry_ref, wgla_ref, wfk_ref, wmq_ref, wgate_ref, wqvt_ref):
    tm = x_ref.shape[0]

    @pl.when(pl.program_id(0) == 0)
    def _():
        carry_ref[...] = jnp.zeros_like(carry_ref)
        wgla_ref[...] = wall_ref[:, :OFF_GA]
        wfk_ref[...] = wall_ref[:, OFF_FK:OFF_FV]
        wmq_ref[...] = wall_ref[:, OFF_MQ:OFF_GATE]
        wgate_ref[...] = wall_ref[:, OFF_GATE:]
        wqvt_ref[:FOX_W, :] = wall_ref[:, OFF_FQ:OFF_FK].T
        wqvt_ref[FOX_W:, :] = wall_ref[:, OFF_FV:OFF_FF].T

    u = _rms(x_ref[...], gmix_ref[...])
    ub = u.astype(BF16)

    u_lo = (u - ub.astype(F32)).astype(BF16)
    parts = _dot(jnp.concatenate([ub, u_lo], axis=0), wsmall_ref[...])
    small = (parts[:tm, :LANES] + parts[:tm, LANES:]) + (parts[tm:, :LANES] + parts[tm:, LANES:])

    mq = _dot(ub, wmq_ref[...]).astype(BF16)
    scale = MEM_DH ** -0.5
    heads = [slice(h * MEM_DH, (h + 1) * MEM_DH) for h in range(MEM_HEADS)]
    scores = [_dot_nt(mq[:, sl], mk_ref[:, sl]) * scale for sl in heads]

    gla_ref[...] = _dot(ub, wgla_ref[...]).astype(BF16)
    fkb = _dot(ub, wfk_ref[...]).astype(BF16)

    ga = small[:, :GLA_LOWRANK]
    ga_hi = ga.astype(BF16)
    ga_lo = (ga - ga_hi.astype(F32)).astype(BF16)
    up = _dot(jnp.concatenate([ga_hi, ga_lo], axis=0), wup_ref[...])
    alpha_pre = ((up[:tm, :GLA_K] + up[:tm, GLA_K:]) + (up[tm:, :GLA_K] + up[tm:, GLA_K:])) + balpha_ref[...]
    loga_ref[...] = _log_sigmoid(alpha_pre) * (1.0 / GLA_TAU)

    lane = lax.broadcasted_iota(jnp.int32, (tm, LANES), 1)
    ff_valid = (lane >= GLA_LOWRANK) & (lane < GLA_LOWRANK + FOX_HEADS)
    logf = jnp.where(ff_valid, _log_sigmoid(small + bforget_ref[...]), 0.0)
    logf = pltpu.roll(logf, LANES - GLA_LOWRANK, 1)
    r = lax.broadcasted_iota(jnp.int32, (tm, tm), 0)
    c = lax.broadcasted_iota(jnp.int32, (tm, tm), 1)
    tri = jnp.where(r >= c, 1.0, 0.0).astype(BF16)
    lf_hi = logf.astype(BF16)
    lf_r = logf - lf_hi.astype(F32)
    lf_mid = lf_r.astype(BF16)
    lf_lo = (lf_r - lf_mid.astype(F32)).astype(BF16)
    c3 = _dot(tri, jnp.concatenate([lf_hi, lf_mid, lf_lo], axis=1))
    fcum = (c3[:, :LANES] + c3[:, LANES:2 * LANES] + c3[:, 2 * LANES:]) + carry_ref[...]
    carry_ref[...] = fcum[tm - 1:tm, :]

    f_hi = fcum.astype(BF16)
    rem = fcum - f_hi.astype(F32)
    f_mid = rem.astype(BF16)
    f_lo = (rem - f_mid.astype(F32)).astype(BF16)
    f3 = jnp.concatenate([f_hi, f_mid, f_lo], axis=1)
    def aug_slot(idx):
        a = idx % LANES
        return a % FOX_AUG_PER_HEAD, a < 2 * FOX_AUG_PER_HEAD

    slot_q, in_q = aug_slot(lax.broadcasted_iota(jnp.int32, (FOX_W, 1), 0))
    slot_k, in_k = aug_slot(lax.broadcasted_iota(jnp.int32, (1, FOX_W), 1))
    augqt = _dot_nt(pqt_ref[...], f3) + jnp.where(in_q & (slot_q < 3), -1.0, 0.0)
    augk = _dot(f3, pk_ref[...]) + jnp.where(in_k & (slot_k >= 3), 1.0, 0.0)
    gate_ref[...] = _sigmoid(_dot(ub, wgate_ref[...])).astype(BF16)

    probs = []
    for s in scores:
        m = jnp.max(s, axis=-1, keepdims=True)
        e = jnp.exp(s - m)
        probs.append((e / jnp.sum(e, axis=-1, keepdims=True)).astype(BF16))
    omem_ref[...] = jnp.concatenate([_dot(pr, mv_ref[:, sl]) for pr, sl in zip(probs, heads)], axis=-1).astype(BF16)

    qvt = _dot_nt(wqvt_ref[...], ub)
    fqt = (qvt[:FOX_W] * (FOX_DH ** -0.5)).astype(BF16)
    vt = qvt[FOX_W:].astype(BF16)
    ones = jnp.ones((FOX_ONES_ROWS, tm), BF16)
    vt_parts = []
    for h in range(FOX_HEADS):
        vt_parts += [vt[h * FOX_DH:(h + 1) * FOX_DH, :], ones]
    vt_ref[...] = jnp.concatenate(vt_parts, axis=0)

    for p in range(FOX_HEADS // 2):
        src = slice(p * LANES, (p + 1) * LANES)
        qt_ref[2 * p * LANES:(2 * p + 1) * LANES, :] = fqt[src]
        qt_ref[(2 * p + 1) * LANES:(2 * p + 2) * LANES, :] = augqt[src].astype(BF16)
        kaug_ref[p, :, :LANES] = fkb[:, src]
        kaug_ref[p, :, LANES:] = augk[:, src].astype(BF16)

    fq32 = fqt.astype(F32) * (FOX_DH ** 0.5)
    nq2 = jnp.sum((fq32 * fq32).reshape(FOX_HEADS, FOX_DH, tm), axis=1)
    statq_ref[0] = jnp.broadcast_to(jnp.max(nq2, axis=1, keepdims=True), (FOX_HEADS, LANES))
    gi = lax.broadcasted_iota(jnp.int32, (FOX_W, LANES), 0) // FOX_DH
    gj = lax.broadcasted_iota(jnp.int32, (FOX_W, LANES), 1)
    group = jnp.where(gi == gj, 1.0, 0.0).astype(BF16)
    fk = fkb.astype(F32)
    nk2 = jnp.max(_dot((fk * fk).astype(BF16), group), axis=0, keepdims=True)
    fmax = jnp.max(fcum, axis=0, keepdims=True)
    fmin = jnp.min(fcum, axis=0, keepdims=True)
    stat_ref[0] = jnp.concatenate([nk2, fmax, fmin, jnp.zeros((5, LANES), F32)], axis=0)


def _aug_placement():
    pq = np.zeros((3 * LANES, FOX_W), np.float32)
    pk = np.zeros((3 * LANES, FOX_W), np.float32)
    for h in range(FOX_HEADS):
        base = (h // 2) * LANES + (h % 2) * FOX_AUG_PER_HEAD
        for c in range(3):
            pq[c * LANES + h, base + 3 + c] = 1.0
            pk[c * LANES + h, base + c] = 1.0
    return jnp.asarray(pq.T, BF16), jnp.asarray(pk, BF16)


def _proj(x, g_mix, w_all, w_small, w_up, b_alpha, b_forget, mk, mv):
    s = x.shape[0]
    tm = ROW_TILE
    nt = s // tm
    assert w_all.shape == (D_MODEL, D_IN)
    pqt, pk = _aug_placement()
    full = lambda shape: pl.BlockSpec(shape, lambda i: (0,) * len(shape), pipeline_mode=pl.Buffered(1))
    row = lambda w: pl.BlockSpec((tm, w), lambda i: (i, 0))
    col = lambda r: pl.BlockSpec((r, tm), lambda i: (0, i))
    stat = pl.BlockSpec((1, 8, LANES), lambda i: (i, 0, 0))
    vt_rows = FOX_HEADS * (FOX_DH + FOX_ONES_ROWS)
    return pl.pallas_call(
        _proj_kernel,
        grid=(nt,),
        in_specs=[row(D_MODEL), full((1, D_MODEL)), full(w_all.shape), full(w_small.shape), full(w_up.shape),
                  full(b_alpha.shape), full(b_forget.shape), full(mk.shape), full(mv.shape),
                  full(pqt.shape), full(pk.shape)],
        out_specs=[row(OFF_GA), row(GLA_K), col(2 * FOX_W),
                   pl.BlockSpec((FOX_HEADS // 2, tm, 2 * LANES), lambda i: (0, i, 0)), col(vt_rows), row(MEM_W),
                   row(3 * D_MODEL), stat, stat],
        out_shape=[jax.ShapeDtypeStruct((s, OFF_GA), BF16),
                   jax.ShapeDtypeStruct((s, GLA_K), F32),
                   jax.ShapeDtypeStruct((2 * FOX_W, s), BF16),
                   jax.ShapeDtypeStruct((FOX_HEADS // 2, s, 2 * LANES), BF16),
                   jax.ShapeDtypeStruct((vt_rows, s), BF16),
                   jax.ShapeDtypeStruct((s, MEM_W), BF16),
                   jax.ShapeDtypeStruct((s, 3 * D_MODEL), BF16),
                   jax.ShapeDtypeStruct((nt, 8, LANES), F32),
                   jax.ShapeDtypeStruct((nt, 8, LANES), F32)],
        scratch_shapes=[pltpu.VMEM((1, LANES), F32),
                        pltpu.VMEM((D_MODEL, OFF_GA), BF16), pltpu.VMEM((D_MODEL, FOX_W), BF16),
                        pltpu.VMEM((D_MODEL, MEM_W), BF16), pltpu.VMEM((D_MODEL, 3 * D_MODEL), BF16),
                        pltpu.VMEM((2 * FOX_W, D_MODEL), BF16)],
        compiler_params=pltpu.CompilerParams(dimension_semantics=("arbitrary",), vmem_limit_bytes=VMEM_LIMIT),
        name="proj",
    )(x, g_mix, w_all, w_small, w_up, b_alpha, b_forget, mk, mv, pqt, pk)


def _gla_kernel(qkvg_ref, loga_ref, ghead_ref, o_ref, state_ref):
    t = GLA_GROUP
    nc = t // CHUNK

    @pl.when(pl.program_id(0) == 0)
    def _():
        state_ref[...] = jnp.zeros_like(state_ref)

    r = lax.broadcasted_iota(jnp.int32, (t, t), 0)
    c = lax.broadcasted_iota(jnp.int32, (t, t), 1)
    same = (r // CHUNK) == (c // CHUNK)
    lower = r >= c
    causal = same & lower
    anti = same & (r < c)
    cum = jnp.where(causal, 1.0, 0.0).astype(BF16)
    lane = lax.broadcasted_iota(jnp.int32, (1, LANES), 1)

    for g0 in range(0, qkvg_ref.shape[0], t):
        rg = slice(g0, g0 + t)
        la = loga_ref[rg, :]
        la_hi = la.astype(BF16)
        la_r = la - la_hi.astype(F32)
        la_mid = la_r.astype(BF16)
        la_lo = (la_r - la_mid.astype(F32)).astype(BF16)
        b3 = _dot(cum, jnp.concatenate([la_hi, la_mid, la_lo], axis=1))
        b = b3[:, :GLA_K] + b3[:, GLA_K:2 * GLA_K] + b3[:, 2 * GLA_K:]
        b_last = jnp.concatenate(
            [jnp.broadcast_to(b[(ci + 1) * CHUNK - 1:(ci + 1) * CHUNK, :], (CHUNK, GLA_K)) for ci in range(nc)],
            axis=0)
        e_pos = jnp.exp(b)
        e_neg = jnp.exp(-b)
        q = qkvg_ref[rg, 0:GLA_K].astype(F32) * (GLA_DK ** -0.5)
        k = qkvg_ref[rg, GLA_K:2 * GLA_K].astype(F32)
        q_pos = (q * e_pos).astype(BF16)
        q_neg = (q * e_neg).astype(BF16)
        k_pos = (k * e_pos).astype(BF16)
        k_neg = (k * e_neg).astype(BF16)
        k_dec = (k * jnp.exp(b_last - b)).astype(BF16)

        chunks = [slice(ci * CHUNK, (ci + 1) * CHUNK) for ci in range(nc)]
        pairs = [slice((h // 2) * LANES, (h // 2 + 1) * LANES) for h in range(GLA_HEADS)]
        in_head = [(lane // GLA_DK) == (h % 2) for h in range(GLA_HEADS)]
        vs = [qkvg_ref[rg, 2 * GLA_K + h * GLA_DV:2 * GLA_K + (h + 1) * GLA_DV] for h in range(GLA_HEADS)]
        qps = [jnp.where(in_head[h], q_pos[:, pairs[h]], jnp.zeros((), BF16)) for h in range(GLA_HEADS)]

        kvs = [[jnp.where(in_head[h], _dot_tn(vs[h][rows], k_dec[rows, pairs[h]]), 0.0) for rows in chunks]
               for h in range(GLA_HEADS)]
        sts = [state_ref[h] for h in range(GLA_HEADS)]
        inter = [[] for _ in range(GLA_HEADS)]
        for ci, rows in enumerate(chunks):
            for h in range(GLA_HEADS):
                inter[h].append(_dot_nt(qps[h][rows], sts[h].astype(BF16)))
                dec = e_pos[(ci + 1) * CHUNK - 1:(ci + 1) * CHUNK, pairs[h]]
                sts[h] = sts[h] * dec + kvs[h][ci]
        for h in range(GLA_HEADS):
            state_ref[h] = sts[h]

        attns = []
        for h in range(GLA_HEADS):
            qn = jnp.where(in_head[h], q_neg[:, pairs[h]], jnp.zeros((), BF16))
            a_c = _dot_nt(qps[h], k_neg[:, pairs[h]])
            a_a = _dot_nt(qn, k_pos[:, pairs[h]])
            attns.append(jnp.where(causal, a_c, jnp.where(anti, a_a, 0.0)).astype(BF16))
        outs = [_dot(attns[h], vs[h]) + jnp.concatenate(inter[h], axis=0) for h in range(GLA_HEADS)]
        for h in range(GLA_HEADS):
            o = outs[h]

            gh = ghead_ref[:, h * GLA_DV:(h + 1) * GLA_DV]
            on = _rms(o, gh)
            gsl = slice(2 * GLA_K + GLA_V + h * GLA_DV, 2 * GLA_K + GLA_V + (h + 1) * GLA_DV)
            gg = qkvg_ref[rg, gsl].astype(F32)
            o_ref[rg, h * GLA_DV:(h + 1) * GLA_DV] = (on * (gg * _sigmoid(gg))).astype(BF16)


def _gla(qkvg, loga, ghead):
    s = qkvg.shape[0]
    t = GLA_TILE
    assert s % t == 0 and t % GLA_GROUP == 0
    return pl.pallas_call(
        _gla_kernel,
        grid=(s // t,),
        in_specs=[pl.BlockSpec((t, qkvg.shape[1]), lambda i: (i, 0)),
                  pl.BlockSpec((t, GLA_K), lambda i: (i, 0)),
                  pl.BlockSpec((1, GLA_V), lambda i: (0, 0))],
        out_specs=pl.BlockSpec((t, GLA_V), lambda i: (i, 0)),
        out_shape=jax.ShapeDtypeStruct((s, GLA_V), BF16),
        scratch_shapes=[pltpu.VMEM((GLA_HEADS, GLA_DV, LANES), F32)],
        compiler_params=pltpu.CompilerParams(dimension_semantics=("arbitrary",), vmem_limit_bytes=VMEM_LIMIT),
        name="gla",
    )(qkvg, loga, ghead)


def _fox_kernel(thr_ref, fmin_ref, shift_ref, qt_ref, k_ref, vt_ref, o_ref,
                sa_ref, sb_ref, pa_ref, pb_ref, m_ref, acc_ref, si_ref, sj_ref):
    p = pl.program_id(0)
    tq = FOX_TQ
    tk = FOX_TK
    nsub = tq // tk
    nq = k_ref.shape[0] // tq
    hrows = FOX_DH + FOX_ONES_ROWS
    row = lax.broadcasted_iota(jnp.int32, (2 * LANES, 1), 0)
    krow = lax.broadcasted_iota(jnp.int32, (tk, tq), 0)
    qcol = lax.broadcasted_iota(jnp.int32, (tk, tq), 1)
    m0 = jnp.full((1, tq), NEG_BIG, F32)
    acc0 = jnp.zeros((hrows, tq), F32)

    for hi in range(2):
        h = 2 * p + hi
        aug0 = LANES + hi * FOX_AUG_PER_HEAD
        mine = ((row // FOX_DH) == hi) | ((row >= aug0) & (row < aug0 + FOX_AUG_PER_HEAD))
        vrows = slice(hi * hrows, (hi + 1) * hrows)

        def score_tiles(qi, kj, masked):
            q0 = pl.multiple_of(qi * tq, tq)
            qm = jnp.where(mine, qt_ref[:, pl.ds(q0, tq)], jnp.zeros((), BF16))
            for u in range(nsub):
                k0 = pl.multiple_of(kj * tq + u * tk, tk)
                s_t = _dot(k_ref[pl.ds(k0, tk), :], qm)
                if masked:
                    s_t = jnp.where(krow + u * tk <= qcol, s_t, NEG_BIG)
                yield u, s_t

        def values(kj, u):
            return vt_ref[vrows, pl.ds(pl.multiple_of(kj * tq + u * tk, tk), tk)]

        def fast_produce(qi, kj, buf, masked=False):
            shift = shift_ref[h, qi]
            for u, s_t in score_tiles(qi, kj, masked):
                buf[u] = jnp.exp(s_t - shift).astype(BF16)

        def fast_consume(i, kj, buf, fresh=False):
            acc = acc0 if fresh else acc_ref[hi, i]
            for u in range(nsub):
                acc = acc + _dot(values(kj, u), buf[u])
            acc_ref[hi, i] = acc

        def slow_produce(qi, kj, buf, masked=False):
            for u, s_t in score_tiles(qi, kj, masked):
                buf[u] = s_t

        def slow_consume(i, kj, buf, fresh=False):
            m = m0 if fresh else m_ref[hi, i]
            acc = acc0 if fresh else acc_ref[hi, i]
            for u in range(nsub):
                s_t = buf[u]
                m_new = jnp.maximum(m, jnp.max(s_t, axis=0, keepdims=True))
                alpha = jnp.exp(m - m_new)
                pt = jnp.exp(s_t - m_new).astype(BF16)
                acc = alpha * acc + _dot(values(kj, u), pt)
                m = m_new
            m_ref[hi, i] = m
            acc_ref[hi, i] = acc

        def list_block(i, t):
            thr = thr_ref[h, i]
            for j in range(nq):
                si_ref[t] = i
                sj_ref[t] = j
                t = t + jnp.where((j < i) & (fmin_ref[h, j] <= thr), 1, 0)
            return t

        worst_shift = lax.fori_loop(0, nq, lambda i, w: jnp.maximum(w, shift_ref[h, i]), jnp.float32(0.0))

        def run_head(produce, consume, buf_a, buf_b):
            def diag_prefetch(i, buf):
                ic = jnp.minimum(i, nq - 1)
                produce(ic, ic, buf, masked=True)

            diag_prefetch(0, buf_a)

            def diag_step(tn, n):
                i = FOX_STEPS * tn
                for d in range(0, FOX_STEPS, 2):
                    diag_prefetch(i + d + 1, buf_b)
                    consume(i + d, i + d, buf_a, fresh=True)
                    diag_prefetch(i + d + 2, buf_a)
                    consume(i + d + 1, i + d + 1, buf_b, fresh=True)
                for d in range(FOX_STEPS):
                    n = list_block(i + d, n)
                return n

            n_tiles = lax.fori_loop(0, nq // FOX_STEPS, diag_step, 0)
            for pad in range(FOX_STEPS):
                si_ref[n_tiles + pad] = nq
                sj_ref[n_tiles + pad] = 0
            m_ref[hi, nq] = m0
            acc_ref[hi, nq] = acc0

            def prefetch(t, buf):
                produce(jnp.minimum(si_ref[t], nq - 1), sj_ref[t], buf)

            prefetch(0, buf_a)

            def multi_step(tn, carry):
                t = FOX_STEPS * tn
                for d in range(0, FOX_STEPS, 2):
                    prefetch(t + d + 1, buf_b)
                    consume(si_ref[t + d], sj_ref[t + d], buf_a)
                    prefetch(t + d + 2, buf_a)
                    consume(si_ref[t + d + 1], sj_ref[t + d + 1], buf_b)
                return carry

            lax.fori_loop(0, (n_tiles + FOX_STEPS - 1) // FOX_STEPS, multi_step, 0)

        no_running_max = 2.0 * worst_shift <= FOX_MAX_SHIFT_GAP

        @pl.when(no_running_max)
        def _():
            run_head(fast_produce, fast_consume, pa_ref, pb_ref)

        @pl.when(jnp.logical_not(no_running_max))
        def _():
            run_head(slow_produce, slow_consume, sa_ref, sb_ref)

    def finish(i, carry):
        a0 = acc_ref[0, i]
        a1 = acc_ref[1, i]
        ot = jnp.concatenate([a0[:FOX_DH] / a0[FOX_DH:FOX_DH + 1], a1[:FOX_DH] / a1[FOX_DH:FOX_DH + 1]], axis=0)
        o_ref[pl.ds(pl.multiple_of(i * tq, tq), tq), :] = ot.T.astype(BF16)
        return carry

    lax.fori_loop(0, nq, finish, 0)


def _fox(thr, fmin_blk, shift, qt, kaug, vt):
    s = kaug.shape[1]
    tq = FOX_TQ
    nq = s // tq
    npair = FOX_HEADS // 2
    prow = 2 * (FOX_DH + FOX_ONES_ROWS)
    assert nq % FOX_STEPS == 0 and FOX_STEPS % 2 == 0
    hrows = FOX_DH + FOX_ONES_ROWS
    max_tiles = nq * (nq - 1) // 2 + FOX_STEPS
    once = pl.Buffered(1)
    grid_spec = pltpu.PrefetchScalarGridSpec(
        num_scalar_prefetch=3,
        grid=(npair,),
        in_specs=[pl.BlockSpec((2 * LANES, s), lambda p, *_: (p, 0), pipeline_mode=once),
                  pl.BlockSpec((None, s, 2 * LANES), lambda p, *_: (p, 0, 0), pipeline_mode=once),
                  pl.BlockSpec((prow, s), lambda p, *_: (p, 0), pipeline_mode=once)],
        out_specs=pl.BlockSpec((None, s, LANES), lambda p, *_: (p, 0, 0)),
        scratch_shapes=[pltpu.VMEM((tq // FOX_TK, FOX_TK, tq), F32), pltpu.VMEM((tq // FOX_TK, FOX_TK, tq), F32),
                        pltpu.VMEM((tq // FOX_TK, FOX_TK, tq), BF16), pltpu.VMEM((tq // FOX_TK, FOX_TK, tq), BF16),
                        pltpu.VMEM((2, nq + 1, 1, tq), F32), pltpu.VMEM((2, nq + 1, hrows, tq), F32),
                        pltpu.SMEM((max_tiles,), jnp.int32), pltpu.SMEM((max_tiles,), jnp.int32)],
    )
    return pl.pallas_call(
        _fox_kernel,
        grid_spec=grid_spec,
        out_shape=jax.ShapeDtypeStruct((npair, s, LANES), BF16),
        compiler_params=pltpu.CompilerParams(dimension_semantics=("arbitrary",), vmem_limit_bytes=VMEM_LIMIT),
        name="fox",
    )(thr, fmin_blk, shift, qt, kaug, vt)


def _tail_kernel(x_ref, ogla_ref, ofox_ref, omem_ref, gate_ref, wg_ref, wf_ref, wm_ref, wo_ref,
                 gffn_ref, w1_ref, w2_ref, gfin_ref, out_ref):
    groups = [slice(r0, r0 + TAIL_GROUP) for r0 in range(0, x_ref.shape[0], TAIL_GROUP)]
    merged = []
    for rg in groups:
        ofox = jnp.concatenate([ofox_ref[p, rg, :] for p in range(FOX_HEADS // 2)], axis=1)
        merged.append((gate_ref[rg, 0:D_MODEL].astype(F32) * _dot(ogla_ref[rg, :], wg_ref[...])
                       + gate_ref[rg, D_MODEL:2 * D_MODEL].astype(F32) * _dot(ofox, wf_ref[...])
                       + gate_ref[rg, 2 * D_MODEL:3 * D_MODEL].astype(F32) * _dot(omem_ref[rg, :], wm_ref[...])
                       ).astype(BF16))
    hs = [x_ref[rg, :] + _dot(mg, wo_ref[...]) for rg, mg in zip(groups, merged)]
    u2s = [_rms(h, gffn_ref[...]).astype(BF16) for h in hs]
    accs = [jnp.zeros_like(h) for h in hs]
    for cidx in range(D_FF // FF_CHUNK):
        cs = slice(cidx * FF_CHUNK, (cidx + 1) * FF_CHUNK)
        for g, u2 in enumerate(u2s):
            a = jnp.maximum(_dot(u2, w1_ref[:, cs]), 0.0)
            accs[g] = accs[g] + _dot((a * a).astype(BF16), w2_ref[cs, :])
    for rg, h, acc in zip(groups, hs, accs):
        out_ref[rg, :] = _rms(h + acc, gfin_ref[...])


def _tail(x, ogla, ofox, omem, gate, wg, wf, wm, wo, gffn, w1, w2, gfin):
    s = x.shape[0]
    tm = TAIL_TILE
    full = lambda a: pl.BlockSpec(a.shape, lambda i: (0,) * a.ndim, pipeline_mode=pl.Buffered(1))
    row = lambda w: pl.BlockSpec((tm, w), lambda i: (i, 0))
    return pl.pallas_call(
        _tail_kernel,
        grid=(s // tm,),
        in_specs=[row(D_MODEL), row(GLA_V), pl.BlockSpec((FOX_HEADS // 2, tm, LANES), lambda i: (0, i, 0)),
                  row(MEM_W), row(3 * D_MODEL),
                  full(wg), full(wf), full(wm), full(wo), full(gffn), full(w1), full(w2), full(gfin)],
        out_specs=row(D_MODEL),
        out_shape=jax.ShapeDtypeStruct((s, D_MODEL), F32),
        compiler_params=pltpu.CompilerParams(dimension_semantics=("arbitrary",), vmem_limit_bytes=VMEM_LIMIT),
        name="tail",
    )(x, ogla, ofox, omem, gate, wg, wf, wm, wo, gffn, w1, w2, gfin)


def kernel(x, mem, g_mix, w_in, w_alpha_up, b_alpha, b_forget, g_gla_head, g_mem, w_mem_kv,
           w_gla_o, w_fox_o, w_mem_o, w_out, g_ffn, w_ff1, w_ff2, g_final):
    assert x.shape[0] == 1 and g_mix.shape[0] == 1, "single batch, single layer"
    s = x.shape[1]
    assert s % ROW_TILE == 0 and s % TAIL_TILE == 0 and s % FOX_TQ == 0
    assert FOX_TQ % FOX_TK == 0 and FOX_TQ % ROW_TILE == 0
    xs = x[0]
    w = w_in[0]

    w_bf = w.astype(BF16)
    assert OFF_GA % LANES == 0 and OFF_FF % LANES == GLA_LOWRANK
    win_ga = w[:, OFF_GA:OFF_GA + LANES]
    win_ff = w[:, OFF_FF - GLA_LOWRANK:OFF_FF - GLA_LOWRANK + LANES]
    col = lax.broadcasted_iota(jnp.int32, (1, LANES), 1)
    w_small = jnp.where(col < GLA_LOWRANK, win_ga, jnp.where(col < GLA_LOWRANK + FOX_HEADS, win_ff, 0.0))
    w_small_hi = w_small.astype(BF16)
    w_small = jnp.concatenate([w_small_hi, (w_small - w_small_hi.astype(F32)).astype(BF16)], axis=1)
    w_up_hi = w_alpha_up[0].astype(BF16)
    w_up = jnp.concatenate([w_up_hi, (w_alpha_up[0] - w_up_hi.astype(F32)).astype(BF16)], axis=1)
    b_f = jnp.zeros((1, LANES), F32).at[0, GLA_LOWRANK:GLA_LOWRANK + FOX_HEADS].set(b_forget[0])

    mk, mv = _memkv(mem[0], g_mem, w_mem_kv[0].astype(BF16))
    gla_qkvg, loga, qt, kaug, vt, omem, gate, stats, statq = _proj(
        xs, g_mix, w_bf, w_small, w_up, b_alpha, b_f, mk, mv)

    ogla = _gla(gla_qkvg, loga, g_gla_head.reshape(1, GLA_V))

    per_blk = lambda a: a.reshape(s // FOX_TQ, FOX_TQ // ROW_TILE, FOX_HEADS)
    qn = jnp.sqrt(jnp.max(per_blk(statq[:, :, 0]), axis=1))
    kn = jnp.sqrt(jnp.max(stats[:, 0, :FOX_HEADS], axis=0))
    fmax = jnp.max(per_blk(stats[:, 1, :FOX_HEADS]), axis=1)
    fmin = jnp.min(per_blk(stats[:, 2, :FOX_HEADS]), axis=1)
    shift = (1.02 * (FOX_DH ** -0.5)) * qn * kn[None, :]
    thr = 2.0 * shift + fmax + PRUNE_LOGIT_GAP
    ofox = _fox(thr.T, fmin.T, shift.T, qt, kaug, vt)

    out = _tail(xs, ogla, ofox, omem, gate, w_gla_o[0].astype(BF16), w_fox_o[0].astype(BF16),
                w_mem_o[0].astype(BF16), w_out[0].astype(BF16), g_ffn, w_ff1[0].astype(BF16),
                w_ff2[0].astype(BF16), g_final.reshape(1, D_MODEL))
    return out[None]
```

```python
import jax
import jax.numpy as jnp
import numpy as np
from jax import lax
from jax.experimental import pallas as pl
from jax.experimental.pallas import tpu as pltpu

D_MODEL = 1024
CHUNK = 64
EPS = 1e-6
GLA_HEADS = 4
GLA_DK = 64
GLA_DV = 128
GLA_LOWRANK = 16
GLA_TAU = 16.0
FOX_HEADS = 8
FOX_DH = 64
MEM_HEADS = 4
MEM_DH = 128
D_FF = 4 * D_MODEL
GLA_K = GLA_HEADS * GLA_DK
GLA_V = GLA_HEADS * GLA_DV
FOX_W = FOX_HEADS * FOX_DH
MEM_W = MEM_HEADS * MEM_DH
OFF_GA = 2 * GLA_K + 2 * GLA_V
OFF_FQ = OFF_GA + GLA_LOWRANK
OFF_FK = OFF_FQ + FOX_W
OFF_FV = OFF_FK + FOX_W
OFF_FF = OFF_FV + FOX_W
OFF_MQ = OFF_FF + FOX_HEADS
OFF_GATE = OFF_MQ + MEM_W

LANES = 128
FOX_ONES_ROWS = 16
FOX_AUG_PER_HEAD = 6
ROW_TILE = 256
GLA_TILE = 512
GLA_GROUP = 256
FOX_TQ = 512
FOX_TK = 256
FOX_STEPS = 8
TAIL_TILE = 512
TAIL_GROUP = 256
FF_CHUNK = 1024
VMEM_LIMIT = 56 * 1024 * 1024
NEG_BIG = -1e30
PRUNE_LOGIT_GAP = 104.0
FOX_MAX_SHIFT_GAP = 50.0

F32 = jnp.float32
BF16 = jnp.bfloat16


def _rms(xf, g):
    r = lax.rsqrt(jnp.mean(xf * xf, axis=-1, keepdims=True) + EPS)
    return (xf * r) * g


def _log_sigmoid(x):
    return jnp.minimum(x, 0.0) - jnp.log1p(jnp.exp(-jnp.abs(x)))


def _sigmoid(x):
    return 1.0 / (1.0 + jnp.exp(-x))


def _dot(a, b):
    return jnp.dot(a, b, preferred_element_type=F32)


def _dot_nt(a, b):
    return lax.dot_general(a, b, (((1,), (1,)), ((), ())), preferred_element_type=F32)


def _dot_tn(a, b):
    return lax.dot_general(a, b, (((0,), (0,)), ((), ())), preferred_element_type=F32)


def _memkv_kernel(mem_ref, g_ref, w_ref, mk_ref, mv_ref):
    mn = _rms(mem_ref[...], g_ref[...]).astype(BF16)
    kv = _dot(mn, w_ref[...])
    mk_ref[...] = kv[:, :MEM_W].astype(BF16)
    mv_ref[...] = kv[:, MEM_W:].astype(BF16)


def _memkv(mem, g_mem, w_mem_kv):
    m = mem.shape[0]
    return pl.pallas_call(
        _memkv_kernel,
        out_shape=(jax.ShapeDtypeStruct((m, MEM_W), BF16), jax.ShapeDtypeStruct((m, MEM_W), BF16)),
        name="memkv",
    )(mem, g_mem, w_mem_kv)


def _proj_kernel(x_ref, gmix_ref, wgla_ref, wfk_ref, wqv_ref, wmq_ref, wgate_ref, wsmall_ref, wup_ref,
                 balpha_ref, bforget_ref, mk_ref, mv_ref, pqt_ref, pk_ref,
                 gla_ref, loga_ref, qt_ref, kaug_ref, vt_ref, omem_ref, gate_ref, stat_ref, statq_ref,
                 carry_ref, wqvt_ref):
    tm = x_ref.shape[0]

    @pl.when(pl.program_id(0) == 0)
    def _():
        carry_ref[...] = jnp.zeros_like(carry_ref)
        wqvt_ref[...] = wqv_ref[...].T

    u = _rms(x_ref[...], gmix_ref[...])
    ub = u.astype(BF16)

    u_lo = (u - ub.astype(F32)).astype(BF16)
    parts = _dot(jnp.concatenate([ub, u_lo], axis=0), wsmall_ref[...])
    small = (parts[:tm, :LANES] + parts[:tm, LANES:]) + (parts[tm:, :LANES] + parts[tm:, LANES:])

    mq = _dot(ub, wmq_ref[...]).astype(BF16)
    scale = MEM_DH ** -0.5
    heads = [slice(h * MEM_DH, (h + 1) * MEM_DH) for h in range(MEM_HEADS)]
    scores = [_dot_nt(mq[:, sl], mk_ref[:, sl]) * scale for sl in heads]

    gla_ref[...] = _dot(ub, wgla_ref[...]).astype(BF16)
    fkb = _dot(ub, wfk_ref[...]).astype(BF16)

    ga = small[:, :GLA_LOWRANK]
    ga_hi = ga.astype(BF16)
    ga_lo = (ga - ga_hi.astype(F32)).astype(BF16)
    up = _dot(jnp.concatenate([ga_hi, ga_lo], axis=0), wup_ref[...])
    alpha_pre = ((up[:tm, :GLA_K] + up[:tm, GLA_K:]) + (up[tm:, :GLA_K] + up[tm:, GLA_K:])) + balpha_ref[...]
    loga_ref[...] = _log_sigmoid(alpha_pre) * (1.0 / GLA_TAU)

    lane = lax.broadcasted_iota(jnp.int32, (tm, LANES), 1)
    ff_valid = (lane >= GLA_LOWRANK) & (lane < GLA_LOWRANK + FOX_HEADS)
    logf = jnp.where(ff_valid, _log_sigmoid(small + bforget_ref[...]), 0.0)
    logf = pltpu.roll(logf, LANES - GLA_LOWRANK, 1)
    r = lax.broadcasted_iota(jnp.int32, (tm, tm), 0)
    c = lax.broadcasted_iota(jnp.int32, (tm, tm), 1)
    tri = jnp.where(r >= c, 1.0, 0.0).astype(BF16)
    lf_hi = logf.astype(BF16)
    lf_r = logf - lf_hi.astype(F32)
    lf_mid = lf_r.astype(BF16)
    lf_lo = (lf_r - lf_mid.astype(F32)).astype(BF16)
    c3 = _dot(tri, jnp.concatenate([lf_hi, lf_mid, lf_lo], axis=1))
    fcum = (c3[:, :LANES] + c3[:, LANES:2 * LANES] + c3[:, 2 * LANES:]) + carry_ref[...]
    carry_ref[...] = fcum[tm - 1:tm, :]

    f_hi = fcum.astype(BF16)
    rem = fcum - f_hi.astype(F32)
    f_mid = rem.astype(BF16)
    f_lo = (rem - f_mid.astype(F32)).astype(BF16)
    f3 = jnp.concatenate([f_hi, f_mid, f_lo], axis=1)
    def aug_slot(idx):
        a = idx % LANES
        return a % FOX_AUG_PER_HEAD, a < 2 * FOX_AUG_PER_HEAD

    slot_q, in_q = aug_slot(lax.broadcasted_iota(jnp.int32, (FOX_W, 1), 0))
    slot_k, in_k = aug_slot(lax.broadcasted_iota(jnp.int32, (1, FOX_W), 1))
    augqt = _dot_nt(pqt_ref[...], f3) + jnp.where(in_q & (slot_q < 3), -1.0, 0.0)
    augk = _dot(f3, pk_ref[...]) + jnp.where(in_k & (slot_k >= 3), 1.0, 0.0)
    gate_ref[...] = _sigmoid(_dot(ub, wgate_ref[...])).astype(BF16)

    probs = []
    for s in scores:
        m = jnp.max(s, axis=-1, keepdims=True)
        e = jnp.exp(s - m)
        probs.append((e / jnp.sum(e, axis=-1, keepdims=True)).astype(BF16))
    omem_ref[...] = jnp.concatenate([_dot(pr, mv_ref[:, sl]) for pr, sl in zip(probs, heads)], axis=-1).astype(BF16)

    qvt = _dot_nt(wqvt_ref[...], ub)
    fqt = (qvt[:FOX_W] * (FOX_DH ** -0.5)).astype(BF16)
    vt = qvt[FOX_W:].astype(BF16)
    ones = jnp.ones((FOX_ONES_ROWS, tm), BF16)
    vt_parts = []
    for h in range(FOX_HEADS):
        vt_parts += [vt[h * FOX_DH:(h + 1) * FOX_DH, :], ones]
    vt_ref[...] = jnp.concatenate(vt_parts, axis=0)

    for p in range(FOX_HEADS // 2):
        src = slice(p * LANES, (p + 1) * LANES)
        qt_ref[2 * p * LANES:(2 * p + 1) * LANES, :] = fqt[src]
        qt_ref[(2 * p + 1) * LANES:(2 * p + 2) * LANES, :] = augqt[src].astype(BF16)
        kaug_ref[p, :, :LANES] = fkb[:, src]
        kaug_ref[p, :, LANES:] = augk[:, src].astype(BF16)

    fq32 = fqt.astype(F32) * (FOX_DH ** 0.5)
    nq2 = jnp.sum((fq32 * fq32).reshape(FOX_HEADS, FOX_DH, tm), axis=1)
    statq_ref[0] = jnp.broadcast_to(jnp.max(nq2, axis=1, keepdims=True), (FOX_HEADS, LANES))
    gi = lax.broadcasted_iota(jnp.int32, (FOX_W, LANES), 0) // FOX_DH
    gj = lax.broadcasted_iota(jnp.int32, (FOX_W, LANES), 1)
    group = jnp.where(gi == gj, 1.0, 0.0).astype(BF16)
    fk = fkb.astype(F32)
    nk2 = jnp.max(_dot((fk * fk).astype(BF16), group), axis=0, keepdims=True)
    fmax = jnp.max(fcum, axis=0, keepdims=True)
    fmin = jnp.min(fcum, axis=0, keepdims=True)
    stat_ref[0] = jnp.concatenate([nk2, fmax, fmin, jnp.zeros((5, LANES), F32)], axis=0)


def _aug_placement():
    pq = np.zeros((3 * LANES, FOX_W), np.float32)
    pk = np.zeros((3 * LANES, FOX_W), np.float32)
    for h in range(FOX_HEADS):
        base = (h // 2) * LANES + (h % 2) * FOX_AUG_PER_HEAD
        for c in range(3):
            pq[c * LANES + h, base + 3 + c] = 1.0
            pk[c * LANES + h, base + c] = 1.0
    return jnp.asarray(pq.T, BF16), jnp.asarray(pk, BF16)


def _proj(x, g_mix, w_gla, w_fk, w_qvt, w_mq, w_gate, w_small, w_up, b_alpha, b_forget, mk, mv):
    s = x.shape[0]
    tm = ROW_TILE
    nt = s // tm
    pqt, pk = _aug_placement()
    full = lambda shape: pl.BlockSpec(shape, lambda i: (0,) * len(shape))
    row = lambda w: pl.BlockSpec((tm, w), lambda i: (i, 0))
    col = lambda r: pl.BlockSpec((r, tm), lambda i: (0, i))
    stat = pl.BlockSpec((1, 8, LANES), lambda i: (i, 0, 0))
    vt_rows = FOX_HEADS * (FOX_DH + FOX_ONES_ROWS)
    return pl.pallas_call(
        _proj_kernel,
        grid=(nt,),
        in_specs=[row(D_MODEL), full((1, D_MODEL)), full(w_gla.shape), full(w_fk.shape), full(w_qvt.shape),
                  full(w_mq.shape), full(w_gate.shape), full(w_small.shape), full(w_up.shape),
                  full(b_alpha.shape), full(b_forget.shape), full(mk.shape), full(mv.shape),
                  full(pqt.shape), full(pk.shape)],
        out_specs=[row(w_gla.shape[1]), row(GLA_K), col(2 * FOX_W),
                   pl.BlockSpec((FOX_HEADS // 2, tm, 2 * LANES), lambda i: (0, i, 0)), col(vt_rows), row(MEM_W),
                   row(w_gate.shape[1]), stat, stat],
        out_shape=[jax.ShapeDtypeStruct((s, w_gla.shape[1]), BF16),
                   jax.ShapeDtypeStruct((s, GLA_K), F32),
                   jax.ShapeDtypeStruct((2 * FOX_W, s), BF16),
                   jax.ShapeDtypeStruct((FOX_HEADS // 2, s, 2 * LANES), BF16),
                   jax.ShapeDtypeStruct((vt_rows, s), BF16),
                   jax.ShapeDtypeStruct((s, MEM_W), BF16),
                   jax.ShapeDtypeStruct((s, w_gate.shape[1]), BF16),
                   jax.ShapeDtypeStruct((nt, 8, LANES), F32),
                   jax.ShapeDtypeStruct((nt, 8, LANES), F32)],
        scratch_shapes=[pltpu.VMEM((1, LANES), F32), pltpu.VMEM((w_qvt.shape[1], w_qvt.shape[0]), BF16)],
        compiler_params=pltpu.CompilerParams(dimension_semantics=("arbitrary",), vmem_limit_bytes=VMEM_LIMIT),
        name="proj",
    )(x, g_mix, w_gla, w_fk, w_qvt, w_mq, w_gate, w_small, w_up, b_alpha, b_forget, mk, mv, pqt, pk)


def _gla_kernel(qkvg_ref, loga_ref, ghead_ref, o_ref, state_ref):
    t = GLA_GROUP
    nc = t // CHUNK

    @pl.when(pl.program_id(0) == 0)
    def _():
        state_ref[...] = jnp.zeros_like(state_ref)

    r = lax.broadcasted_iota(jnp.int32, (t, t), 0)
    c = lax.broadcasted_iota(jnp.int32, (t, t), 1)
    same = (r // CHUNK) == (c // CHUNK)
    lower = r >= c
    causal = same & lower
    anti = same & (r < c)
    cum = jnp.where(causal, 1.0, 0.0).astype(BF16)
    lane = lax.broadcasted_iota(jnp.int32, (1, LANES), 1)

    chunks = [slice(ci * CHUNK, (ci + 1) * CHUNK) for ci in range(nc)]
    pairs = [slice((h // 2) * LANES, (h // 2 + 1) * LANES) for h in range(GLA_HEADS)]
    in_head = [(lane // GLA_DK) == (h % 2) for h in range(GLA_HEADS)]
    heads = range(GLA_HEADS)
    groups = [slice(g0, g0 + t) for g0 in range(0, qkvg_ref.shape[0], t)]

    pre = []
    for rg in groups:
        la = loga_ref[rg, :]
        la_hi = la.astype(BF16)
        la_r = la - la_hi.astype(F32)
        la_mid = la_r.astype(BF16)
        la_lo = (la_r - la_mid.astype(F32)).astype(BF16)
        b3 = _dot(cum, jnp.concatenate([la_hi, la_mid, la_lo], axis=1))
        b = b3[:, :GLA_K] + b3[:, GLA_K:2 * GLA_K] + b3[:, 2 * GLA_K:]
        b_last = jnp.concatenate(
            [jnp.broadcast_to(b[(ci + 1) * CHUNK - 1:(ci + 1) * CHUNK, :], (CHUNK, GLA_K)) for ci in range(nc)],
            axis=0)
        e_pos = jnp.exp(b)
        e_neg = jnp.exp(-b)
        q = qkvg_ref[rg, 0:GLA_K].astype(F32) * (GLA_DK ** -0.5)
        k = qkvg_ref[rg, GLA_K:2 * GLA_K].astype(F32)
        q_pos = (q * e_pos).astype(BF16)
        q_neg = (q * e_neg).astype(BF16)
        k_pos = (k * e_pos).astype(BF16)
        k_neg = (k * e_neg).astype(BF16)
        k_dec = (k * jnp.exp(b_last - b)).astype(BF16)

        vs = [qkvg_ref[rg, 2 * GLA_K + h * GLA_DV:2 * GLA_K + (h + 1) * GLA_DV] for h in heads]
        qps = [jnp.where(in_head[h], q_pos[:, pairs[h]], jnp.zeros((), BF16)) for h in heads]
        kvs = [[jnp.where(in_head[h], _dot_tn(vs[h][rows], k_dec[rows, pairs[h]]), 0.0) for rows in chunks]
               for h in heads]
        pre.append((e_pos, q_neg, k_pos, k_neg, vs, qps, kvs))

    sts = [state_ref[h] for h in heads]
    inters = []
    for e_pos, _, _, _, _, qps, kvs in pre:
        inter = [[] for _ in heads]
        for ci, rows in enumerate(chunks):
            for h in heads:
                inter[h].append(_dot_nt(qps[h][rows], sts[h].astype(BF16)))
                dec = e_pos[(ci + 1) * CHUNK - 1:(ci + 1) * CHUNK, pairs[h]]
                sts[h] = sts[h] * dec + kvs[h][ci]
        inters.append(inter)
    for h in heads:
        state_ref[h] = sts[h]

    attns = []
    for _, q_neg, k_pos, k_neg, _, qps, _ in pre:
        grp = []
        for h in heads:
            qn = jnp.where(in_head[h], q_neg[:, pairs[h]], jnp.zeros((), BF16))
            a_c = _dot_nt(qps[h], k_neg[:, pairs[h]])
            a_a = _dot_nt(qn, k_pos[:, pairs[h]])
            grp.append(jnp.where(causal, a_c, jnp.where(anti, a_a, 0.0)).astype(BF16))
        attns.append(grp)

    for rg, grp, inter, (_, _, _, _, vs, _, _) in zip(groups, attns, inters, pre):
        outs = [_dot(grp[h], vs[h]) + jnp.concatenate(inter[h], axis=0) for h in heads]
        for h in heads:
            gh = ghead_ref[:, h * GLA_DV:(h + 1) * GLA_DV]
            on = _rms(outs[h], gh)
            gsl = slice(2 * GLA_K + GLA_V + h * GLA_DV, 2 * GLA_K + GLA_V + (h + 1) * GLA_DV)
            gg = qkvg_ref[rg, gsl].astype(F32)
            o_ref[rg, h * GLA_DV:(h + 1) * GLA_DV] = (on * (gg * _sigmoid(gg))).astype(BF16)


def _gla(qkvg, loga, ghead):
    s = qkvg.shape[0]
    t = GLA_TILE
    assert s % t == 0 and t % GLA_GROUP == 0
    return pl.pallas_call(
        _gla_kernel,
        grid=(s // t,),
        in_specs=[pl.BlockSpec((t, qkvg.shape[1]), lambda i: (i, 0)),
                  pl.BlockSpec((t, GLA_K), lambda i: (i, 0)),
                  pl.BlockSpec((1, GLA_V), lambda i: (0, 0))],
        out_specs=pl.BlockSpec((t, GLA_V), lambda i: (i, 0)),
        out_shape=jax.ShapeDtypeStruct((s, GLA_V), BF16),
        scratch_shapes=[pltpu.VMEM((GLA_HEADS, GLA_DV, LANES), F32)],
        compiler_params=pltpu.CompilerParams(dimension_semantics=("arbitrary",), vmem_limit_bytes=VMEM_LIMIT),
        name="gla",
    )(qkvg, loga, ghead)


def _fox_kernel(thr_ref, fmin_ref, shift_ref, qt_ref, k_ref, vt_ref, o_ref,
                sa_ref, sb_ref, pa_ref, pb_ref, m_ref, acc_ref, si_ref, sj_ref):
    p = pl.program_id(0)
    tq = FOX_TQ
    tk = FOX_TK
    nsub = tq // tk
    nq = k_ref.shape[0] // tq
    hrows = FOX_DH + FOX_ONES_ROWS
    row = lax.broadcasted_iota(jnp.int32, (2 * LANES, 1), 0)
    krow = lax.broadcasted_iota(jnp.int32, (tk, tq), 0)
    qcol = lax.broadcasted_iota(jnp.int32, (tk, tq), 1)
    m0 = jnp.full((1, tq), NEG_BIG, F32)
    acc0 = jnp.zeros((hrows, tq), F32)

    for hi in range(2):
        h = 2 * p + hi
        aug0 = LANES + hi * FOX_AUG_PER_HEAD
        mine = ((row // FOX_DH) == hi) | ((row >= aug0) & (row < aug0 + FOX_AUG_PER_HEAD))
        vrows = slice(hi * hrows, (hi + 1) * hrows)

        def score_tiles(qi, kj, masked):
            q0 = pl.multiple_of(qi * tq, tq)
            qm = jnp.where(mine, qt_ref[:, pl.ds(q0, tq)], jnp.zeros((), BF16))
            for u in range(nsub):
                k0 = pl.multiple_of(kj * tq + u * tk, tk)
                s_t = _dot(k_ref[pl.ds(k0, tk), :], qm)
                if masked:
                    s_t = jnp.where(krow + u * tk <= qcol, s_t, NEG_BIG)
                yield u, s_t

        def values(kj, u):
            return vt_ref[vrows, pl.ds(pl.multiple_of(kj * tq + u * tk, tk), tk)]

        def fast_produce(qi, kj, buf, masked=False):
            shift = shift_ref[h, qi]
            for u, s_t in score_tiles(qi, kj, masked):
                buf[u] = jnp.exp(s_t - shift).astype(BF16)

        def fast_consume(i, kj, buf, fresh=False):
            acc = acc0 if fresh else acc_ref[hi, i]
            for u in range(nsub):
                acc = acc + _dot(values(kj, u), buf[u])
            acc_ref[hi, i] = acc

        def slow_produce(qi, kj, buf, masked=False):
            for u, s_t in score_tiles(qi, kj, masked):
                buf[u] = s_t

        def slow_consume(i, kj, buf, fresh=False):
            m = m0 if fresh else m_ref[hi, i]
            acc = acc0 if fresh else acc_ref[hi, i]
            for u in range(nsub):
                s_t = buf[u]
                m_new = jnp.maximum(m, jnp.max(s_t, axis=0, keepdims=True))
                alpha = jnp.exp(m - m_new)
                pt = jnp.exp(s_t - m_new).astype(BF16)
                acc = alpha * acc + _dot(values(kj, u), pt)
                m = m_new
            m_ref[hi, i] = m
            acc_ref[hi, i] = acc

        def list_block(i, t):
            thr = thr_ref[h, i]
            for j in range(nq):
                si_ref[t] = i
                sj_ref[t] = j
                t = t + jnp.where((j < i) & (fmin_ref[h, j] <= thr), 1, 0)
            return t

        worst_shift = lax.fori_loop(0, nq, lambda i, w: jnp.maximum(w, shift_ref[h, i]), jnp.float32(0.0))

        def run_head(produce, consume, buf_a, buf_b):
            def diag_prefetch(i, buf):
                ic = jnp.minimum(i, nq - 1)
                produce(ic, ic, buf, masked=True)

            diag_prefetch(0, buf_a)

            def diag_step(tn, n):
                i = FOX_STEPS * tn
                for d in range(0, FOX_STEPS, 2):
                    diag_prefetch(i + d + 1, buf_b)
                    consume(i + d, i + d, buf_a, fresh=True)
                    diag_prefetch(i + d + 2, buf_a)
                    consume(i + d + 1, i + d + 1, buf_b, fresh=True)
                for d in range(FOX_STEPS):
                    n = list_block(i + d, n)
                return n

            n_tiles = lax.fori_loop(0, nq // FOX_STEPS, diag_step, 0)
            for pad in range(FOX_STEPS):
                si_ref[n_tiles + pad] = nq
                sj_ref[n_tiles + pad] = 0
            m_ref[hi, nq] = m0
            acc_ref[hi, nq] = acc0

            def prefetch(t, buf):
                produce(jnp.minimum(si_ref[t], nq - 1), sj_ref[t], buf)

            prefetch(0, buf_a)

            def multi_step(tn, carry):
                t = FOX_STEPS * tn
                for d in range(0, FOX_STEPS, 2):
                    prefetch(t + d + 1, buf_b)
                    consume(si_ref[t + d], sj_ref[t + d], buf_a)
                    prefetch(t + d + 2, buf_a)
                    consume(si_ref[t + d + 1], sj_ref[t + d + 1], buf_b)
                return carry

            lax.fori_loop(0, (n_tiles + FOX_STEPS - 1) // FOX_STEPS, multi_step, 0)

        no_running_max = 2.0 * worst_shift <= FOX_MAX_SHIFT_GAP

        @pl.when(no_running_max)
        def _():
            run_head(fast_produce, fast_consume, pa_ref, pb_ref)

        @pl.when(jnp.logical_not(no_running_max))
        def _():
            run_head(slow_produce, slow_consume, sa_ref, sb_ref)

    def finish(i, carry):
        a0 = acc_ref[0, i]
        a1 = acc_ref[1, i]
        ot = jnp.concatenate([a0[:FOX_DH] / a0[FOX_DH:FOX_DH + 1], a1[:FOX_DH] / a1[FOX_DH:FOX_DH + 1]], axis=0)
        o_ref[pl.ds(pl.multiple_of(i * tq, tq), tq), :] = ot.T.astype(BF16)
        return carry

    lax.fori_loop(0, nq, finish, 0)


def _fox(thr, fmin_blk, shift, qt, kaug, vt):
    s = kaug.shape[1]
    tq = FOX_TQ
    nq = s // tq
    npair = FOX_HEADS // 2
    prow = 2 * (FOX_DH + FOX_ONES_ROWS)
    assert nq % FOX_STEPS == 0 and FOX_STEPS % 2 == 0
    hrows = FOX_DH + FOX_ONES_ROWS
    max_tiles = nq * (nq - 1) // 2 + FOX_STEPS
    once = pl.Buffered(1)
    grid_spec = pltpu.PrefetchScalarGridSpec(
        num_scalar_prefetch=3,
        grid=(npair,),
        in_specs=[pl.BlockSpec((2 * LANES, s), lambda p, *_: (p, 0), pipeline_mode=once),
                  pl.BlockSpec((None, s, 2 * LANES), lambda p, *_: (p, 0, 0), pipeline_mode=once),
                  pl.BlockSpec((prow, s), lambda p, *_: (p, 0), pipeline_mode=once)],
        out_specs=pl.BlockSpec((None, s, LANES), lambda p, *_: (p, 0, 0)),
        scratch_shapes=[pltpu.VMEM((tq // FOX_TK, FOX_TK, tq), F32), pltpu.VMEM((tq // FOX_TK, FOX_TK, tq), F32),
                        pltpu.VMEM((tq // FOX_TK, FOX_TK, tq), BF16), pltpu.VMEM((tq // FOX_TK, FOX_TK, tq), BF16),
                        pltpu.VMEM((2, nq + 1, 1, tq), F32), pltpu.VMEM((2, nq + 1, hrows, tq), F32),
                        pltpu.SMEM((max_tiles,), jnp.int32), pltpu.SMEM((max_tiles,), jnp.int32)],
    )
    return pl.pallas_call(
        _fox_kernel,
        grid_spec=grid_spec,
        out_shape=jax.ShapeDtypeStruct((npair, s, LANES), BF16),
        compiler_params=pltpu.CompilerParams(dimension_semantics=("arbitrary",), vmem_limit_bytes=VMEM_LIMIT),
        name="fox",
    )(thr, fmin_blk, shift, qt, kaug, vt)


def _tail_kernel(x_ref, ogla_ref, ofox_ref, omem_ref, gate_ref, wg_ref, wf_ref, wm_ref, wo_ref,
                 gffn_ref, w1_ref, w2_ref, gfin_ref, out_ref):
    groups = [slice(r0, r0 + TAIL_GROUP) for r0 in range(0, x_ref.shape[0], TAIL_GROUP)]
    merged = []
    for rg in groups:
        ofox = jnp.concatenate([ofox_ref[p, rg, :] for p in range(FOX_HEADS // 2)], axis=1)
        merged.append((gate_ref[rg, 0:D_MODEL].astype(F32) * _dot(ogla_ref[rg, :], wg_ref[...])
                       + gate_ref[rg, D_MODEL:2 * D_MODEL].astype(F32) * _dot(ofox, wf_ref[...])
                       + gate_ref[rg, 2 * D_MODEL:3 * D_MODEL].astype(F32) * _dot(omem_ref[rg, :], wm_ref[...])
                       ).astype(BF16))
    hs = [x_ref[rg, :] + _dot(mg, wo_ref[...]) for rg, mg in zip(groups, merged)]
    u2s = [_rms(h, gffn_ref[...]).astype(BF16) for h in hs]
    accs = [jnp.zeros_like(h) for h in hs]
    for cidx in range(D_FF // FF_CHUNK):
        cs = slice(cidx * FF_CHUNK, (cidx + 1) * FF_CHUNK)
        for g, u2 in enumerate(u2s):
            a = jnp.maximum(_dot(u2, w1_ref[:, cs]), 0.0)
            accs[g] = accs[g] + _dot((a * a).astype(BF16), w2_ref[cs, :])
    for rg, h, acc in zip(groups, hs, accs):
        out_ref[rg, :] = _rms(h + acc, gfin_ref[...])


def _tail(x, ogla, ofox, omem, gate, wg, wf, wm, wo, gffn, w1, w2, gfin):
    s = x.shape[0]
    tm = TAIL_TILE
    full = lambda a: pl.BlockSpec(a.shape, lambda i: (0,) * a.ndim, pipeline_mode=pl.Buffered(1))
    row = lambda w: pl.BlockSpec((tm, w), lambda i: (i, 0))
    return pl.pallas_call(
        _tail_kernel,
        grid=(s // tm,),
        in_specs=[row(D_MODEL), row(GLA_V), pl.BlockSpec((FOX_HEADS // 2, tm, LANES), lambda i: (0, i, 0)),
                  row(MEM_W), row(3 * D_MODEL),
                  full(wg), full(wf), full(wm), full(wo), full(gffn), full(w1), full(w2), full(gfin)],
        out_specs=row(D_MODEL),
        out_shape=jax.ShapeDtypeStruct((s, D_MODEL), F32),
        compiler_params=pltpu.CompilerParams(dimension_semantics=("arbitrary",), vmem_limit_bytes=VMEM_LIMIT),
        name="tail",
    )(x, ogla, ofox, omem, gate, wg, wf, wm, wo, gffn, w1, w2, gfin)


def kernel(x, mem, g_mix, w_in, w_alpha_up, b_alpha, b_forget, g_gla_head, g_mem, w_mem_kv,
           w_gla_o, w_fox_o, w_mem_o, w_out, g_ffn, w_ff1, w_ff2, g_final):
    assert x.shape[0] == 1 and g_mix.shape[0] == 1, "single batch, single layer"
    s = x.shape[1]
    assert s % ROW_TILE == 0 and s % TAIL_TILE == 0 and s % FOX_TQ == 0
    assert FOX_TQ % FOX_TK == 0 and FOX_TQ % ROW_TILE == 0
    xs = x[0]
    w = w_in[0]

    w_gla = w[:, :OFF_GA].astype(BF16)
    w_fk = w[:, OFF_FK:OFF_FV].astype(BF16)
    w_qvt = jnp.concatenate([w[:, OFF_FQ:OFF_FK], w[:, OFF_FV:OFF_FF]], axis=1).astype(BF16)
    w_mq = w[:, OFF_MQ:OFF_GATE].astype(BF16)
    w_gate = w[:, OFF_GATE:].astype(BF16)
    w_small = jnp.concatenate(
        [w[:, OFF_GA:OFF_FQ], w[:, OFF_FF:OFF_MQ], jnp.zeros((D_MODEL, LANES - GLA_LOWRANK - FOX_HEADS), F32)], axis=1)
    w_small_hi = w_small.astype(BF16)
    w_small = jnp.concatenate([w_small_hi, (w_small - w_small_hi.astype(F32)).astype(BF16)], axis=1)
    w_up_hi = w_alpha_up[0].astype(BF16)
    w_up = jnp.concatenate([w_up_hi, (w_alpha_up[0] - w_up_hi.astype(F32)).astype(BF16)], axis=1)
    b_f = jnp.zeros((1, LANES), F32).at[0, GLA_LOWRANK:GLA_LOWRANK + FOX_HEADS].set(b_forget[0])

    mk, mv = _memkv(mem[0], g_mem, w_mem_kv[0].astype(BF16))
    gla_qkvg, loga, qt, kaug, vt, omem, gate, stats, statq = _proj(
        xs, g_mix, w_gla, w_fk, w_qvt, w_mq, w_gate, w_small, w_up, b_alpha, b_f, mk, mv)

    ogla = _gla(gla_qkvg, loga, g_gla_head.reshape(1, GLA_V))

    per_blk = lambda a: a.reshape(s // FOX_TQ, FOX_TQ // ROW_TILE, FOX_HEADS)
    qn = jnp.sqrt(jnp.max(per_blk(statq[:, :, 0]), axis=1))
    kn = jnp.sqrt(jnp.max(stats[:, 0, :FOX_HEADS], axis=0))
    fmax = jnp.max(per_blk(stats[:, 1, :FOX_HEADS]), axis=1)
    fmin = jnp.min(per_blk(stats[:, 2, :FOX_HEADS]), axis=1)
    shift = (1.02 * (FOX_DH ** -0.5)) * qn * kn[None, :]
    thr = 2.0 * shift + fmax + PRUNE_LOGIT_GAP
    ofox = _fox(thr.T, fmin.T, shift.T, qt, kaug, vt)

    out = _tail(xs, ogla, ofox, omem, gate, w_gla_o[0].astype(BF16), w_fox_o[0].astype(BF16),
                w_mem_o[0].astype(BF16), w_out[0].astype(BF16), g_ffn, w_ff1[0].astype(BF16),
                w_ff2[0].astype(BF16), g_final.reshape(1, D_MODEL))
    return out[None]
```

```python
import jax
import jax.numpy as jnp
import numpy as np
from jax import lax
from jax.experimental import pallas as pl
from jax.experimental.pallas import tpu as pltpu

D_MODEL = 1024
CHUNK = 64
EPS = 1e-6
GLA_HEADS = 4
GLA_DK = 64
GLA_DV = 128
GLA_LOWRANK = 16
GLA_TAU = 16.0
FOX_HEADS = 8
FOX_DH = 64
MEM_HEADS = 4
MEM_DH = 128
D_FF = 4 * D_MODEL
GLA_K = GLA_HEADS * GLA_DK
GLA_V = GLA_HEADS * GLA_DV
FOX_W = FOX_HEADS * FOX_DH
MEM_W = MEM_HEADS * MEM_DH
OFF_GA = 2 * GLA_K + 2 * GLA_V
OFF_FQ = OFF_GA + GLA_LOWRANK
OFF_FK = OFF_FQ + FOX_W
OFF_FV = OFF_FK + FOX_W
OFF_FF = OFF_FV + FOX_W
OFF_MQ = OFF_FF + FOX_HEADS
OFF_GATE = OFF_MQ + MEM_W

LANES = 128
BF16_SUBLANES = 16
FOX_ONES_ROWS = 16
FOX_AUG_PER_HEAD = 6
ROW_TILE = 256
GLA_TILE = 512
GLA_GROUP = 256
FOX_TQ = 512
FOX_TK = 256
FOX_STEPS = 8
TAIL_TILE = 512
TAIL_GROUP = 256
FF_CHUNK = 1024
VMEM_LIMIT = 56 * 1024 * 1024
NEG_BIG = -1e30
PRUNE_LOGIT_GAP = 104.0
FOX_MAX_SHIFT_GAP = 50.0

F32 = jnp.float32
BF16 = jnp.bfloat16


def _rms(xf, g):
    r = lax.rsqrt(jnp.mean(xf * xf, axis=-1, keepdims=True) + EPS)
    return (xf * r) * g


def _log_sigmoid(x):
    return jnp.minimum(x, 0.0) - jnp.log1p(jnp.exp(-jnp.abs(x)))


def _sigmoid(x):
    return 1.0 / (1.0 + jnp.exp(-x))


def _dot(a, b):
    return jnp.dot(a, b, preferred_element_type=F32)


def _dot_nt(a, b):
    return lax.dot_general(a, b, (((1,), (1,)), ((), ())), preferred_element_type=F32)


def _dot_tn(a, b):
    return lax.dot_general(a, b, (((0,), (0,)), ((), ())), preferred_element_type=F32)


def _memkv_kernel(mem_ref, g_ref, w_ref, mk_ref, mv_ref):
    mn = _rms(mem_ref[...], g_ref[...]).astype(BF16)
    kv = _dot(mn, w_ref[...])
    mk_ref[...] = kv[:, :MEM_W].astype(BF16)
    mv_ref[...] = kv[:, MEM_W:].astype(BF16)


def _memkv(mem, g_mem, w_mem_kv):
    m = mem.shape[0]
    return pl.pallas_call(
        _memkv_kernel,
        out_shape=(jax.ShapeDtypeStruct((m, MEM_W), BF16), jax.ShapeDtypeStruct((m, MEM_W), BF16)),
        name="memkv",
    )(mem, g_mem, w_mem_kv)


def _proj_kernel(x_ref, gmix_ref, wgla_ref, wfk_ref, wqv_ref, wmq_ref, wgate_ref, wsmall_ref, wup_ref,
                 balpha_ref, bforget_ref, mk_ref, mv_ref, pqt_ref, pk_ref,
                 gla_ref, loga_ref, qt_ref, kaug_ref, vt_ref, omem_ref, gate_ref, stat_ref, statq_ref,
                 carry_ref, wqvt_ref):
    tm = x_ref.shape[0]

    @pl.when(pl.program_id(0) == 0)
    def _():
        carry_ref[...] = jnp.zeros_like(carry_ref)
        wqvt_ref[...] = wqv_ref[...].T

    u = _rms(x_ref[...], gmix_ref[...])
    ub = u.astype(BF16)

    u_lo = (u - ub.astype(F32)).astype(BF16)
    parts = _dot(jnp.concatenate([ub, u_lo], axis=0), wsmall_ref[...])
    small = (parts[:tm, :LANES] + parts[:tm, LANES:]) + (parts[tm:, :LANES] + parts[tm:, LANES:])

    mq = _dot(ub, wmq_ref[...]).astype(BF16)
    scale = MEM_DH ** -0.5
    heads = [slice(h * MEM_DH, (h + 1) * MEM_DH) for h in range(MEM_HEADS)]
    scores = [_dot_nt(mq[:, sl], mk_ref[:, sl]) * scale for sl in heads]

    gla_ref[...] = _dot(ub, wgla_ref[...]).astype(BF16)
    fkb = _dot(ub, wfk_ref[...]).astype(BF16)

    ga = small[:, :GLA_LOWRANK]
    ga_hi = ga.astype(BF16)
    ga_lo = (ga - ga_hi.astype(F32)).astype(BF16)
    up = _dot(jnp.concatenate([ga_hi, ga_lo], axis=0), wup_ref[...])
    alpha_pre = ((up[:tm, :GLA_K] + up[:tm, GLA_K:]) + (up[tm:, :GLA_K] + up[tm:, GLA_K:])) + balpha_ref[...]
    loga_ref[...] = _log_sigmoid(alpha_pre) * (1.0 / GLA_TAU)

    lane = lax.broadcasted_iota(jnp.int32, (tm, LANES), 1)
    ff_valid = (lane >= GLA_LOWRANK) & (lane < GLA_LOWRANK + FOX_HEADS)
    logf = jnp.where(ff_valid, _log_sigmoid(small + bforget_ref[...]), 0.0)
    logf = pltpu.roll(logf, LANES - GLA_LOWRANK, 1)
    r = lax.broadcasted_iota(jnp.int32, (tm, tm), 0)
    c = lax.broadcasted_iota(jnp.int32, (tm, tm), 1)
    tri = jnp.where(r >= c, 1.0, 0.0).astype(BF16)
    lf_hi = logf.astype(BF16)
    lf_r = logf - lf_hi.astype(F32)
    lf_mid = lf_r.astype(BF16)
    lf_lo = (lf_r - lf_mid.astype(F32)).astype(BF16)
    c3 = _dot(tri, jnp.concatenate([lf_hi, lf_mid, lf_lo], axis=1))
    fcum = (c3[:, :LANES] + c3[:, LANES:2 * LANES] + c3[:, 2 * LANES:]) + carry_ref[...]
    carry_ref[...] = fcum[tm - 1:tm, :]

    f_hi = fcum.astype(BF16)
    rem = fcum - f_hi.astype(F32)
    f_mid = rem.astype(BF16)
    f_lo = (rem - f_mid.astype(F32)).astype(BF16)
    f3 = jnp.concatenate([f_hi, f_mid, f_lo], axis=1)
    def aug_slot(idx):
        a = idx % LANES
        return a % FOX_AUG_PER_HEAD, a < 2 * FOX_AUG_PER_HEAD

    slot_q, in_q = aug_slot(lax.broadcasted_iota(jnp.int32, (FOX_W, 1), 0))
    slot_k, in_k = aug_slot(lax.broadcasted_iota(jnp.int32, (1, FOX_W), 1))
    augqt = _dot_nt(pqt_ref[...], f3) + jnp.where(in_q & (slot_q < 3), -1.0, 0.0)
    augk = _dot(f3, pk_ref[...]) + jnp.where(in_k & (slot_k >= 3), 1.0, 0.0)
    gate_ref[...] = _sigmoid(_dot(ub, wgate_ref[...])).astype(BF16)

    probs = []
    for s in scores:
        m = jnp.max(s, axis=-1, keepdims=True)
        e = jnp.exp(s - m)
        probs.append((e / jnp.sum(e, axis=-1, keepdims=True)).astype(BF16))
    omem_ref[...] = jnp.concatenate([_dot(pr, mv_ref[:, sl]) for pr, sl in zip(probs, heads)], axis=-1).astype(BF16)

    qvt = _dot_nt(wqvt_ref[...], ub)
    fqt = (qvt[:FOX_W] * (FOX_DH ** -0.5)).astype(BF16)
    vt = qvt[FOX_W:].astype(BF16)
    ones = jnp.ones((FOX_ONES_ROWS, tm), BF16)
    vt_parts = []
    for h in range(FOX_HEADS):
        vt_parts += [vt[h * FOX_DH:(h + 1) * FOX_DH, :], ones]
    vt_ref[...] = jnp.concatenate(vt_parts, axis=0)

    for p in range(FOX_HEADS // 2):
        src = slice(p * LANES, (p + 1) * LANES)
        qt_ref[2 * p * LANES:(2 * p + 1) * LANES, :] = fqt[src]
        qt_ref[(2 * p + 1) * LANES:(2 * p + 2) * LANES, :] = augqt[src].astype(BF16)
        kaug_ref[p, :, :LANES] = fkb[:, src]
        kaug_ref[p, :, LANES:] = augk[:, src].astype(BF16)

    fq32 = fqt.astype(F32) * (FOX_DH ** 0.5)
    nq2 = jnp.sum((fq32 * fq32).reshape(FOX_HEADS, FOX_DH, tm), axis=1)
    statq_ref[0] = jnp.broadcast_to(jnp.max(nq2, axis=1, keepdims=True), (FOX_HEADS, LANES))
    gi = lax.broadcasted_iota(jnp.int32, (FOX_W, LANES), 0) // FOX_DH
    gj = lax.broadcasted_iota(jnp.int32, (FOX_W, LANES), 1)
    group = jnp.where(gi == gj, 1.0, 0.0).astype(BF16)
    fk = fkb.astype(F32)
    nk2 = jnp.max(_dot((fk * fk).astype(BF16), group), axis=0, keepdims=True)
    fmax = jnp.max(fcum, axis=0, keepdims=True)
    fmin = jnp.min(fcum, axis=0, keepdims=True)
    stat_ref[0] = jnp.concatenate([nk2, fmax, fmin, jnp.zeros((5, LANES), F32)], axis=0)


def _aug_placement():
    pq = np.zeros((3 * LANES, FOX_W), np.float32)
    pk = np.zeros((3 * LANES, FOX_W), np.float32)
    for h in range(FOX_HEADS):
        base = (h // 2) * LANES + (h % 2) * FOX_AUG_PER_HEAD
        for c in range(3):
            pq[c * LANES + h, base + 3 + c] = 1.0
            pk[c * LANES + h, base + c] = 1.0
    return jnp.asarray(pq.T, BF16), jnp.asarray(pk, BF16)


def _proj(x, g_mix, w_gla, w_fk, w_qvt, w_mq, w_gate, w_small, w_up, b_alpha, b_forget, mk, mv):
    s = x.shape[0]
    tm = ROW_TILE
    nt = s // tm
    pqt, pk = _aug_placement()
    full = lambda shape: pl.BlockSpec(shape, lambda i: (0,) * len(shape))
    row = lambda w: pl.BlockSpec((tm, w), lambda i: (i, 0))
    col = lambda r: pl.BlockSpec((r, tm), lambda i: (0, i))
    stat = pl.BlockSpec((1, 8, LANES), lambda i: (i, 0, 0))
    vt_rows = FOX_HEADS * (FOX_DH + FOX_ONES_ROWS)
    return pl.pallas_call(
        _proj_kernel,
        grid=(nt,),
        in_specs=[row(D_MODEL), full((1, D_MODEL)), full(w_gla.shape), full(w_fk.shape), full(w_qvt.shape),
                  full(w_mq.shape), full(w_gate.shape), full(w_small.shape), full(w_up.shape),
                  full(b_alpha.shape), full(b_forget.shape), full(mk.shape), full(mv.shape),
                  full(pqt.shape), full(pk.shape)],
        out_specs=[row(w_gla.shape[1]), row(GLA_K), col(2 * FOX_W),
                   pl.BlockSpec((FOX_HEADS // 2, tm, 2 * LANES), lambda i: (0, i, 0)), col(vt_rows), row(MEM_W),
                   row(w_gate.shape[1]), stat, stat],
        out_shape=[jax.ShapeDtypeStruct((s, w_gla.shape[1]), BF16),
                   jax.ShapeDtypeStruct((s, GLA_K), F32),
                   jax.ShapeDtypeStruct((2 * FOX_W, s), BF16),
                   jax.ShapeDtypeStruct((FOX_HEADS // 2, s, 2 * LANES), BF16),
                   jax.ShapeDtypeStruct((vt_rows, s), BF16),
                   jax.ShapeDtypeStruct((s, MEM_W), BF16),
                   jax.ShapeDtypeStruct((s, w_gate.shape[1]), BF16),
                   jax.ShapeDtypeStruct((nt, 8, LANES), F32),
                   jax.ShapeDtypeStruct((nt, 8, LANES), F32)],
        scratch_shapes=[pltpu.VMEM((1, LANES), F32), pltpu.VMEM((w_qvt.shape[1], w_qvt.shape[0]), BF16)],
        compiler_params=pltpu.CompilerParams(dimension_semantics=("arbitrary",), vmem_limit_bytes=VMEM_LIMIT),
        name="proj",
    )(x, g_mix, w_gla, w_fk, w_qvt, w_mq, w_gate, w_small, w_up, b_alpha, b_forget, mk, mv, pqt, pk)


def _gla_kernel(qkvg_ref, loga_ref, ghead_ref, *rest):
    n_w = (len(rest) - 2) // 2
    w_in_refs, o_ref, w_out_refs, state_ref = rest[:n_w], rest[n_w], rest[n_w + 1:2 * n_w + 1], rest[-1]
    t = GLA_GROUP
    nc = t // CHUNK

    @pl.when(pl.program_id(0) == 0)
    def _():
        state_ref[...] = jnp.zeros_like(state_ref)

    for w_src, w_dst in zip(w_in_refs, w_out_refs):
        w_dst[...] = w_src[...].astype(BF16)

    r = lax.broadcasted_iota(jnp.int32, (t, t), 0)
    c = lax.broadcasted_iota(jnp.int32, (t, t), 1)
    same = (r // CHUNK) == (c // CHUNK)
    lower = r >= c
    causal = same & lower
    anti = same & (r < c)
    cum = jnp.where(causal, 1.0, 0.0).astype(BF16)
    lane = lax.broadcasted_iota(jnp.int32, (1, LANES), 1)

    chunks = [slice(ci * CHUNK, (ci + 1) * CHUNK) for ci in range(nc)]
    pairs = [slice((h // 2) * LANES, (h // 2 + 1) * LANES) for h in range(GLA_HEADS)]
    in_head = [(lane // GLA_DK) == (h % 2) for h in range(GLA_HEADS)]
    heads = range(GLA_HEADS)
    groups = [slice(g0, g0 + t) for g0 in range(0, qkvg_ref.shape[0], t)]

    pre = []
    for rg in groups:
        la = loga_ref[rg, :]
        la_hi = la.astype(BF16)
        la_r = la - la_hi.astype(F32)
        la_mid = la_r.astype(BF16)
        la_lo = (la_r - la_mid.astype(F32)).astype(BF16)
        b3 = _dot(cum, jnp.concatenate([la_hi, la_mid, la_lo], axis=1))
        b = b3[:, :GLA_K] + b3[:, GLA_K:2 * GLA_K] + b3[:, 2 * GLA_K:]
        b_last = jnp.concatenate(
            [jnp.broadcast_to(b[(ci + 1) * CHUNK - 1:(ci + 1) * CHUNK, :], (CHUNK, GLA_K)) for ci in range(nc)],
            axis=0)
        e_pos = jnp.exp(b)
        e_neg = jnp.exp(-b)
        q = qkvg_ref[rg, 0:GLA_K].astype(F32) * (GLA_DK ** -0.5)
        k = qkvg_ref[rg, GLA_K:2 * GLA_K].astype(F32)
        q_pos = (q * e_pos).astype(BF16)
        q_neg = (q * e_neg).astype(BF16)
        k_pos = (k * e_pos).astype(BF16)
        k_neg = (k * e_neg).astype(BF16)
        k_dec = (k * jnp.exp(b_last - b)).astype(BF16)

        vs = [qkvg_ref[rg, 2 * GLA_K + h * GLA_DV:2 * GLA_K + (h + 1) * GLA_DV] for h in heads]
        qps = [jnp.where(in_head[h], q_pos[:, pairs[h]], jnp.zeros((), BF16)) for h in heads]
        kvs = [[jnp.where(in_head[h], _dot_tn(vs[h][rows], k_dec[rows, pairs[h]]), 0.0) for rows in chunks]
               for h in heads]
        pre.append((e_pos, q_neg, k_pos, k_neg, vs, qps, kvs))

    sts = [state_ref[h] for h in heads]
    inters = []
    for e_pos, _, _, _, _, qps, kvs in pre:
        inter = [[] for _ in heads]
        for ci, rows in enumerate(chunks):
            for h in heads:
                inter[h].append(_dot_nt(qps[h][rows], sts[h].astype(BF16)))
                dec = e_pos[(ci + 1) * CHUNK - 1:(ci + 1) * CHUNK, pairs[h]]
                sts[h] = sts[h] * dec + kvs[h][ci]
        inters.append(inter)
    for h in heads:
        state_ref[h] = sts[h]

    attns = []
    for _, q_neg, k_pos, k_neg, _, qps, _ in pre:
        grp = []
        for h in heads:
            qn = jnp.where(in_head[h], q_neg[:, pairs[h]], jnp.zeros((), BF16))
            a_c = _dot_nt(qps[h], k_neg[:, pairs[h]])
            a_a = _dot_nt(qn, k_pos[:, pairs[h]])
            grp.append(jnp.where(causal, a_c, jnp.where(anti, a_a, 0.0)).astype(BF16))
        attns.append(grp)

    for rg, grp, inter, (_, _, _, _, vs, _, _) in zip(groups, attns, inters, pre):
        outs = [_dot(grp[h], vs[h]) + jnp.concatenate(inter[h], axis=0) for h in heads]
        for h in heads:
            gh = ghead_ref[:, h * GLA_DV:(h + 1) * GLA_DV]
            on = _rms(outs[h], gh)
            gsl = slice(2 * GLA_K + GLA_V + h * GLA_DV, 2 * GLA_K + GLA_V + (h + 1) * GLA_DV)
            gg = qkvg_ref[rg, gsl].astype(F32)
            o_ref[rg, h * GLA_DV:(h + 1) * GLA_DV] = (on * (gg * _sigmoid(gg))).astype(BF16)


def _gla(qkvg, loga, ghead, later_weights):
    s = qkvg.shape[0]
    t = GLA_TILE
    assert s % t == 0 and t % GLA_GROUP == 0
    steps = s // t
    w_specs = [pl.BlockSpec((w.shape[0] // steps, w.shape[1]), lambda i: (i, 0)) for w in later_weights]
    assert all(w.shape[0] % (steps * BF16_SUBLANES) == 0 for w in later_weights)
    outs = pl.pallas_call(
        _gla_kernel,
        grid=(steps,),
        in_specs=[pl.BlockSpec((t, qkvg.shape[1]), lambda i: (i, 0)),
                  pl.BlockSpec((t, GLA_K), lambda i: (i, 0)),
                  pl.BlockSpec((1, GLA_V), lambda i: (0, 0))] + w_specs,
        out_specs=[pl.BlockSpec((t, GLA_V), lambda i: (i, 0))] + w_specs,
        out_shape=[jax.ShapeDtypeStruct((s, GLA_V), BF16)]
        + [jax.ShapeDtypeStruct(w.shape, BF16) for w in later_weights],
        scratch_shapes=[pltpu.VMEM((GLA_HEADS, GLA_DV, LANES), F32)],
        compiler_params=pltpu.CompilerParams(dimension_semantics=("arbitrary",), vmem_limit_bytes=VMEM_LIMIT),
        name="gla",
    )(qkvg, loga, ghead, *later_weights)
    return outs[0], outs[1:]


def _fox_kernel(thr_ref, fmin_ref, shift_ref, qt_ref, k_ref, vt_ref, o_ref,
                sa_ref, sb_ref, pa_ref, pb_ref, m_ref, acc_ref, si_ref, sj_ref):
    p = pl.program_id(0)
    tq = FOX_TQ
    tk = FOX_TK
    nsub = tq // tk
    nq = k_ref.shape[0] // tq
    hrows = FOX_DH + FOX_ONES_ROWS
    row = lax.broadcasted_iota(jnp.int32, (2 * LANES, 1), 0)
    krow = lax.broadcasted_iota(jnp.int32, (tk, tq), 0)
    qcol = lax.broadcasted_iota(jnp.int32, (tk, tq), 1)
    m0 = jnp.full((1, tq), NEG_BIG, F32)
    acc0 = jnp.zeros((hrows, tq), F32)

    for hi in range(2):
        h = 2 * p + hi
        aug0 = LANES + hi * FOX_AUG_PER_HEAD
        mine = ((row // FOX_DH) == hi) | ((row >= aug0) & (row < aug0 + FOX_AUG_PER_HEAD))
        vrows = slice(hi * hrows, (hi + 1) * hrows)

        def score_tiles(qi, kj, masked):
            q0 = pl.multiple_of(qi * tq, tq)
            qm = jnp.where(mine, qt_ref[:, pl.ds(q0, tq)], jnp.zeros((), BF16))
            for u in range(nsub):
                k0 = pl.multiple_of(kj * tq + u * tk, tk)
                s_t = _dot(k_ref[pl.ds(k0, tk), :], qm)
                if masked:
                    s_t = jnp.where(krow + u * tk <= qcol, s_t, NEG_BIG)
                yield u, s_t

        def values(kj, u):
            return vt_ref[vrows, pl.ds(pl.multiple_of(kj * tq + u * tk, tk), tk)]

        def fast_produce(qi, kj, buf, masked=False):
            shift = shift_ref[h, qi]
            for u, s_t in score_tiles(qi, kj, masked):
                buf[u] = jnp.exp(s_t - shift).astype(BF16)

        def fast_consume(i, kj, buf, fresh=False):
            acc = acc0 if fresh else acc_ref[hi, i]
            for u in range(nsub):
                acc = acc + _dot(values(kj, u), buf[u])
            acc_ref[hi, i] = acc

        def slow_produce(qi, kj, buf, masked=False):
            for u, s_t in score_tiles(qi, kj, masked):
                buf[u] = s_t

        def slow_consume(i, kj, buf, fresh=False):
            m = m0 if fresh else m_ref[hi, i]
            acc = acc0 if fresh else acc_ref[hi, i]
            for u in range(nsub):
                s_t = buf[u]
                m_new = jnp.maximum(m, jnp.max(s_t, axis=0, keepdims=True))
                alpha = jnp.exp(m - m_new)
                pt = jnp.exp(s_t - m_new).astype(BF16)
                acc = alpha * acc + _dot(values(kj, u), pt)
                m = m_new
            m_ref[hi, i] = m
            acc_ref[hi, i] = acc

        def list_block(i, t):
            thr = thr_ref[h, i]
            for j in range(nq):
                si_ref[t] = i
                sj_ref[t] = j
                t = t + jnp.where((j < i) & (fmin_ref[h, j] <= thr), 1, 0)
            return t

        worst_shift = lax.fori_loop(0, nq, lambda i, w: jnp.maximum(w, shift_ref[h, i]), jnp.float32(0.0))

        def run_head(produce, consume, buf_a, buf_b):
            def diag_prefetch(i, buf):
                ic = jnp.minimum(i, nq - 1)
                produce(ic, ic, buf, masked=True)

            diag_prefetch(0, buf_a)

            def diag_step(tn, n):
                i = FOX_STEPS * tn
                for d in range(0, FOX_STEPS, 2):
                    diag_prefetch(i + d + 1, buf_b)
                    consume(i + d, i + d, buf_a, fresh=True)
                    diag_prefetch(i + d + 2, buf_a)
                    consume(i + d + 1, i + d + 1, buf_b, fresh=True)
                for d in range(FOX_STEPS):
                    n = list_block(i + d, n)
                return n

            n_tiles = lax.fori_loop(0, nq // FOX_STEPS, diag_step, 0)
            for pad in range(FOX_STEPS):
                si_ref[n_tiles + pad] = nq
                sj_ref[n_tiles + pad] = 0
            m_ref[hi, nq] = m0
            acc_ref[hi, nq] = acc0

            def prefetch(t, buf):
                produce(jnp.minimum(si_ref[t], nq - 1), sj_ref[t], buf)

            prefetch(0, buf_a)

            def multi_step(tn, carry):
                t = FOX_STEPS * tn
                for d in range(0, FOX_STEPS, 2):
                    prefetch(t + d + 1, buf_b)
                    consume(si_ref[t + d], sj_ref[t + d], buf_a)
                    prefetch(t + d + 2, buf_a)
                    consume(si_ref[t + d + 1], sj_ref[t + d + 1], buf_b)
                return carry

            lax.fori_loop(0, (n_tiles + FOX_STEPS - 1) // FOX_STEPS, multi_step, 0)

        no_running_max = 2.0 * worst_shift <= FOX_MAX_SHIFT_GAP

        @pl.when(no_running_max)
        def _():
            run_head(fast_produce, fast_consume, pa_ref, pb_ref)

        @pl.when(jnp.logical_not(no_running_max))
        def _():
            run_head(slow_produce, slow_consume, sa_ref, sb_ref)

    def finish(i, carry):
        a0 = acc_ref[0, i]
        a1 = acc_ref[1, i]
        ot = jnp.concatenate([a0[:FOX_DH] / a0[FOX_DH:FOX_DH + 1], a1[:FOX_DH] / a1[FOX_DH:FOX_DH + 1]], axis=0)
        o_ref[pl.ds(pl.multiple_of(i * tq, tq), tq), :] = ot.T.astype(BF16)
        return carry

    lax.fori_loop(0, nq, finish, 0)


def _fox(thr, fmin_blk, shift, qt, kaug, vt):
    s = kaug.shape[1]
    tq = FOX_TQ
    nq = s // tq
    npair = FOX_HEADS // 2
    prow = 2 * (FOX_DH + FOX_ONES_ROWS)
    assert nq % FOX_STEPS == 0 and FOX_STEPS % 2 == 0
    hrows = FOX_DH + FOX_ONES_ROWS
    max_tiles = nq * (nq - 1) // 2 + FOX_STEPS
    once = pl.Buffered(1)
    grid_spec = pltpu.PrefetchScalarGridSpec(
        num_scalar_prefetch=3,
        grid=(npair,),
        in_specs=[pl.BlockSpec((2 * LANES, s), lambda p, *_: (p, 0), pipeline_mode=once),
                  pl.BlockSpec((None, s, 2 * LANES), lambda p, *_: (p, 0, 0), pipeline_mode=once),
                  pl.BlockSpec((prow, s), lambda p, *_: (p, 0), pipeline_mode=once)],
        out_specs=pl.BlockSpec((None, s, LANES), lambda p, *_: (p, 0, 0)),
        scratch_shapes=[pltpu.VMEM((tq // FOX_TK, FOX_TK, tq), F32), pltpu.VMEM((tq // FOX_TK, FOX_TK, tq), F32),
                        pltpu.VMEM((tq // FOX_TK, FOX_TK, tq), BF16), pltpu.VMEM((tq // FOX_TK, FOX_TK, tq), BF16),
                        pltpu.VMEM((2, nq + 1, 1, tq), F32), pltpu.VMEM((2, nq + 1, hrows, tq), F32),
                        pltpu.SMEM((max_tiles,), jnp.int32), pltpu.SMEM((max_tiles,), jnp.int32)],
    )
    return pl.pallas_call(
        _fox_kernel,
        grid_spec=grid_spec,
        out_shape=jax.ShapeDtypeStruct((npair, s, LANES), BF16),
        compiler_params=pltpu.CompilerParams(dimension_semantics=("arbitrary",), vmem_limit_bytes=VMEM_LIMIT),
        name="fox",
    )(thr, fmin_blk, shift, qt, kaug, vt)


def _tail_kernel(x_ref, ogla_ref, ofox_ref, omem_ref, gate_ref, wg_ref, wf_ref, wm_ref, wo_ref,
                 gffn_ref, w1_ref, w2_ref, gfin_ref, out_ref):
    groups = [slice(r0, r0 + TAIL_GROUP) for r0 in range(0, x_ref.shape[0], TAIL_GROUP)]
    merged = []
    for rg in groups:
        ofox = jnp.concatenate([ofox_ref[p, rg, :] for p in range(FOX_HEADS // 2)], axis=1)
        merged.append((gate_ref[rg, 0:D_MODEL].astype(F32) * _dot(ogla_ref[rg, :], wg_ref[...])
                       + gate_ref[rg, D_MODEL:2 * D_MODEL].astype(F32) * _dot(ofox, wf_ref[...])
                       + gate_ref[rg, 2 * D_MODEL:3 * D_MODEL].astype(F32) * _dot(omem_ref[rg, :], wm_ref[...])
                       ).astype(BF16))
    hs = [x_ref[rg, :] + _dot(mg, wo_ref[...]) for rg, mg in zip(groups, merged)]
    u2s = [_rms(h, gffn_ref[...]).astype(BF16) for h in hs]
    accs = [jnp.zeros_like(h) for h in hs]
    for cidx in range(D_FF // FF_CHUNK):
        cs = slice(cidx * FF_CHUNK, (cidx + 1) * FF_CHUNK)
        for g, u2 in enumerate(u2s):
            a = jnp.maximum(_dot(u2, w1_ref[:, cs]), 0.0)
            accs[g] = accs[g] + _dot((a * a).astype(BF16), w2_ref[cs, :])
    for rg, h, acc in zip(groups, hs, accs):
        out_ref[rg, :] = _rms(h + acc, gfin_ref[...])


def _tail(x, ogla, ofox, omem, gate, wg, wf, wm, wo, gffn, w1, w2, gfin):
    s = x.shape[0]
    tm = TAIL_TILE
    full = lambda a: pl.BlockSpec(a.shape, lambda i: (0,) * a.ndim, pipeline_mode=pl.Buffered(1))
    row = lambda w: pl.BlockSpec((tm, w), lambda i: (i, 0))
    return pl.pallas_call(
        _tail_kernel,
        grid=(s // tm,),
        in_specs=[row(D_MODEL), row(GLA_V), pl.BlockSpec((FOX_HEADS // 2, tm, LANES), lambda i: (0, i, 0)),
                  row(MEM_W), row(3 * D_MODEL),
                  full(wg), full(wf), full(wm), full(wo), full(gffn), full(w1), full(w2), full(gfin)],
        out_specs=row(D_MODEL),
        out_shape=jax.ShapeDtypeStruct((s, D_MODEL), F32),
        compiler_params=pltpu.CompilerParams(dimension_semantics=("arbitrary",), vmem_limit_bytes=VMEM_LIMIT),
        name="tail",
    )(x, ogla, ofox, omem, gate, wg, wf, wm, wo, gffn, w1, w2, gfin)


def kernel(x, mem, g_mix, w_in, w_alpha_up, b_alpha, b_forget, g_gla_head, g_mem, w_mem_kv,
           w_gla_o, w_fox_o, w_mem_o, w_out, g_ffn, w_ff1, w_ff2, g_final):
    assert x.shape[0] == 1 and g_mix.shape[0] == 1, "single batch, single layer"
    s = x.shape[1]
    assert s % ROW_TILE == 0 and s % TAIL_TILE == 0 and s % FOX_TQ == 0
    assert FOX_TQ % FOX_TK == 0 and FOX_TQ % ROW_TILE == 0
    xs = x[0]
    w = w_in[0]

    w_gla = w[:, :OFF_GA].astype(BF16)
    w_fk = w[:, OFF_FK:OFF_FV].astype(BF16)
    w_qvt = jnp.concatenate([w[:, OFF_FQ:OFF_FK], w[:, OFF_FV:OFF_FF]], axis=1).astype(BF16)
    w_mq = w[:, OFF_MQ:OFF_GATE].astype(BF16)
    w_gate = w[:, OFF_GATE:].astype(BF16)
    w_small = jnp.concatenate(
        [w[:, OFF_GA:OFF_FQ], w[:, OFF_FF:OFF_MQ], jnp.zeros((D_MODEL, LANES - GLA_LOWRANK - FOX_HEADS), F32)], axis=1)
    w_small_hi = w_small.astype(BF16)
    w_small = jnp.concatenate([w_small_hi, (w_small - w_small_hi.astype(F32)).astype(BF16)], axis=1)
    w_up_hi = w_alpha_up[0].astype(BF16)
    w_up = jnp.concatenate([w_up_hi, (w_alpha_up[0] - w_up_hi.astype(F32)).astype(BF16)], axis=1)
    b_f = jnp.zeros((1, LANES), F32).at[0, GLA_LOWRANK:GLA_LOWRANK + FOX_HEADS].set(b_forget[0])

    mk, mv = _memkv(mem[0], g_mem, w_mem_kv[0].astype(BF16))
    gla_qkvg, loga, qt, kaug, vt, omem, gate, stats, statq = _proj(
        xs, g_mix, w_gla, w_fk, w_qvt, w_mq, w_gate, w_small, w_up, b_alpha, b_f, mk, mv)

    ogla, (wg, wf, wm, wo, w1, w2) = _gla(
        gla_qkvg, loga, g_gla_head.reshape(1, GLA_V),
        [w_gla_o[0], w_fox_o[0], w_mem_o[0], w_out[0], w_ff1[0], w_ff2[0]])

    per_blk = lambda a: a.reshape(s // FOX_TQ, FOX_TQ // ROW_TILE, FOX_HEADS)
    qn = jnp.sqrt(jnp.max(per_blk(statq[:, :, 0]), axis=1))
    kn = jnp.sqrt(jnp.max(stats[:, 0, :FOX_HEADS], axis=0))
    fmax = jnp.max(per_blk(stats[:, 1, :FOX_HEADS]), axis=1)
    fmin = jnp.min(per_blk(stats[:, 2, :FOX_HEADS]), axis=1)
    shift = (1.02 * (FOX_DH ** -0.5)) * qn * kn[None, :]
    thr = 2.0 * shift + fmax + PRUNE_LOGIT_GAP
    ofox = _fox(thr.T, fmin.T, shift.T, qt, kaug, vt)

    out = _tail(xs, ogla, ofox, omem, gate, wg, wf, wm, wo, g_ffn, w1, w2, g_final.reshape(1, D_MODEL))
    return out[None]
```

```python
import jax
import jax.numpy as jnp
import numpy as np
from jax import lax
from jax.experimental import pallas as pl
from jax.experimental.pallas import tpu as pltpu

D_MODEL = 1024
CHUNK = 64
EPS = 1e-6
GLA_HEADS = 4
GLA_DK = 64
GLA_DV = 128
GLA_LOWRANK = 16
GLA_TAU = 16.0
FOX_HEADS = 8
FOX_DH = 64
MEM_HEADS = 4
MEM_DH = 128
D_FF = 4 * D_MODEL
GLA_K = GLA_HEADS * GLA_DK
GLA_V = GLA_HEADS * GLA_DV
FOX_W = FOX_HEADS * FOX_DH
MEM_W = MEM_HEADS * MEM_DH
OFF_GA = 2 * GLA_K + 2 * GLA_V
OFF_FQ = OFF_GA + GLA_LOWRANK
OFF_FK = OFF_FQ + FOX_W
OFF_FV = OFF_FK + FOX_W
OFF_FF = OFF_FV + FOX_W
OFF_MQ = OFF_FF + FOX_HEADS
OFF_GATE = OFF_MQ + MEM_W

LANES = 128
BF16_SUBLANES = 16
FOX_ONES_ROWS = 16
FOX_AUG_PER_HEAD = 6
ROW_TILE = 512
PROJ_GROUP = 256
GLA_TILE = 512
GLA_GROUP = 256
FOX_TQ = 512
FOX_TK = 256
FOX_STEPS = 8
TAIL_TILE = 512
TAIL_GROUP = 256
FF_CHUNK = 1024
VMEM_LIMIT = 56 * 1024 * 1024
NEG_BIG = -1e30
PRUNE_LOGIT_GAP = 104.0
FOX_MAX_SHIFT_GAP = 50.0

F32 = jnp.float32
BF16 = jnp.bfloat16


def _rms(xf, g):
    r = lax.rsqrt(jnp.mean(xf * xf, axis=-1, keepdims=True) + EPS)
    return (xf * r) * g


def _log_sigmoid(x):
    return jnp.minimum(x, 0.0) - jnp.log1p(jnp.exp(-jnp.abs(x)))


def _sigmoid(x):
    return 1.0 / (1.0 + jnp.exp(-x))


def _dot(a, b):
    return jnp.dot(a, b, preferred_element_type=F32)


def _dot_nt(a, b):
    return lax.dot_general(a, b, (((1,), (1,)), ((), ())), preferred_element_type=F32)


def _dot_tn(a, b):
    return lax.dot_general(a, b, (((0,), (0,)), ((), ())), preferred_element_type=F32)


def _memkv_kernel(mem_ref, g_ref, w_ref, mk_ref, mv_ref):
    mn = _rms(mem_ref[...], g_ref[...]).astype(BF16)
    kv = _dot(mn, w_ref[...])
    mk_ref[...] = kv[:, :MEM_W].astype(BF16)
    mv_ref[...] = kv[:, MEM_W:].astype(BF16)


def _memkv(mem, g_mem, w_mem_kv):
    m = mem.shape[0]
    return pl.pallas_call(
        _memkv_kernel,
        out_shape=(jax.ShapeDtypeStruct((m, MEM_W), BF16), jax.ShapeDtypeStruct((m, MEM_W), BF16)),
        name="memkv",
    )(mem, g_mem, w_mem_kv)


def _proj_kernel(x_ref, gmix_ref, wgla_ref, wfk_ref, wqv_ref, wmq_ref, wgate_ref, wsmall_ref, wup_ref,
                 balpha_ref, bforget_ref, mk_ref, mv_ref, pqt_ref, pk_ref,
                 gla_ref, loga_ref, qt_ref, kaug_ref, vt_ref, omem_ref, gate_ref, stat_ref, statq_ref,
                 carry_ref, wqvt_ref):
    tm = x_ref.shape[0]

    @pl.when(pl.program_id(0) == 0)
    def _():
        carry_ref[...] = jnp.zeros_like(carry_ref)
        wqvt_ref[...] = wqv_ref[...].T

    tg = PROJ_GROUP
    groups = [slice(r0, r0 + tg) for r0 in range(0, tm, tg)]
    gid = range(len(groups))
    scale = MEM_DH ** -0.5
    heads = [slice(h * MEM_DH, (h + 1) * MEM_DH) for h in range(MEM_HEADS)]

    ubs, smalls, scores = [], [], []
    for rg in groups:
        u = _rms(x_ref[rg, :], gmix_ref[...])
        ub = u.astype(BF16)
        u_lo = (u - ub.astype(F32)).astype(BF16)
        parts = _dot(jnp.concatenate([ub, u_lo], axis=0), wsmall_ref[...])
        smalls.append((parts[:tg, :LANES] + parts[:tg, LANES:]) + (parts[tg:, :LANES] + parts[tg:, LANES:]))
        mq = _dot(ub, wmq_ref[...]).astype(BF16)
        scores.append([_dot_nt(mq[:, sl], mk_ref[:, sl]) * scale for sl in heads])
        ubs.append(ub)

    fkbs = []
    for rg, ub in zip(groups, ubs):
        gla_ref[rg, :] = _dot(ub, wgla_ref[...]).astype(BF16)
        fkbs.append(_dot(ub, wfk_ref[...]).astype(BF16))

    lane = lax.broadcasted_iota(jnp.int32, (tg, LANES), 1)
    ff_valid = (lane >= GLA_LOWRANK) & (lane < GLA_LOWRANK + FOX_HEADS)
    r = lax.broadcasted_iota(jnp.int32, (tg, tg), 0)
    c = lax.broadcasted_iota(jnp.int32, (tg, tg), 1)
    tri = jnp.where(r >= c, 1.0, 0.0).astype(BF16)

    def aug_slot(idx):
        a = idx % LANES
        return a % FOX_AUG_PER_HEAD, a < 2 * FOX_AUG_PER_HEAD

    slot_q, in_q = aug_slot(lax.broadcasted_iota(jnp.int32, (FOX_W, 1), 0))
    slot_k, in_k = aug_slot(lax.broadcasted_iota(jnp.int32, (1, FOX_W), 1))
    carry = carry_ref[...]
    fcums, augqts, augks = [], [], []
    for rg, small in zip(groups, smalls):
        ga = small[:, :GLA_LOWRANK]
        ga_hi = ga.astype(BF16)
        ga_lo = (ga - ga_hi.astype(F32)).astype(BF16)
        up = _dot(jnp.concatenate([ga_hi, ga_lo], axis=0), wup_ref[...])
        alpha_pre = ((up[:tg, :GLA_K] + up[:tg, GLA_K:]) + (up[tg:, :GLA_K] + up[tg:, GLA_K:])) + balpha_ref[...]
        loga_ref[rg, :] = _log_sigmoid(alpha_pre) * (1.0 / GLA_TAU)

        logf = jnp.where(ff_valid, _log_sigmoid(small + bforget_ref[...]), 0.0)
        logf = pltpu.roll(logf, LANES - GLA_LOWRANK, 1)
        lf_hi = logf.astype(BF16)
        lf_r = logf - lf_hi.astype(F32)
        lf_mid = lf_r.astype(BF16)
        lf_lo = (lf_r - lf_mid.astype(F32)).astype(BF16)
        c3 = _dot(tri, jnp.concatenate([lf_hi, lf_mid, lf_lo], axis=1))
        fcum = (c3[:, :LANES] + c3[:, LANES:2 * LANES] + c3[:, 2 * LANES:]) + carry
        carry = fcum[tg - 1:tg, :]

        f_hi = fcum.astype(BF16)
        rem = fcum - f_hi.astype(F32)
        f_mid = rem.astype(BF16)
        f_lo = (rem - f_mid.astype(F32)).astype(BF16)
        f3 = jnp.concatenate([f_hi, f_mid, f_lo], axis=1)
        augqts.append(_dot_nt(pqt_ref[...], f3) + jnp.where(in_q & (slot_q < 3), -1.0, 0.0))
        augks.append(_dot(f3, pk_ref[...]) + jnp.where(in_k & (slot_k >= 3), 1.0, 0.0))
        fcums.append(fcum)
    carry_ref[...] = carry

    for rg, ub in zip(groups, ubs):
        gate_ref[rg, :] = _sigmoid(_dot(ub, wgate_ref[...])).astype(BF16)

    for rg, sc in zip(groups, scores):
        probs = []
        for s in sc:
            m = jnp.max(s, axis=-1, keepdims=True)
            e = jnp.exp(s - m)
            probs.append((e / jnp.sum(e, axis=-1, keepdims=True)).astype(BF16))
        omem_ref[rg, :] = jnp.concatenate([_dot(pr, mv_ref[:, sl]) for pr, sl in zip(probs, heads)],
                                          axis=-1).astype(BF16)

    ones = jnp.ones((FOX_ONES_ROWS, tg), BF16)
    gi = lax.broadcasted_iota(jnp.int32, (FOX_W, LANES), 0) // FOX_DH
    gj = lax.broadcasted_iota(jnp.int32, (FOX_W, LANES), 1)
    group = jnp.where(gi == gj, 1.0, 0.0).astype(BF16)
    for g, rg, ub, fkb, fcum, augqt, augk in zip(gid, groups, ubs, fkbs, fcums, augqts, augks):
        qvt = _dot_nt(wqvt_ref[...], ub)
        fqt = (qvt[:FOX_W] * (FOX_DH ** -0.5)).astype(BF16)
        vt = qvt[FOX_W:].astype(BF16)
        vt_parts = []
        for h in range(FOX_HEADS):
            vt_parts += [vt[h * FOX_DH:(h + 1) * FOX_DH, :], ones]
        vt_ref[:, rg] = jnp.concatenate(vt_parts, axis=0)

        for p in range(FOX_HEADS // 2):
            src = slice(p * LANES, (p + 1) * LANES)
            qt_ref[2 * p * LANES:(2 * p + 1) * LANES, rg] = fqt[src]
            qt_ref[(2 * p + 1) * LANES:(2 * p + 2) * LANES, rg] = augqt[src].astype(BF16)
            kaug_ref[p, rg, :LANES] = fkb[:, src]
            kaug_ref[p, rg, LANES:] = augk[:, src].astype(BF16)

        fq32 = fqt.astype(F32) * (FOX_DH ** 0.5)
        nq2 = jnp.sum((fq32 * fq32).reshape(FOX_HEADS, FOX_DH, tg), axis=1)
        statq_ref[g] = jnp.broadcast_to(jnp.max(nq2, axis=1, keepdims=True), (FOX_HEADS, LANES))
        fk = fkb.astype(F32)
        nk2 = jnp.max(_dot((fk * fk).astype(BF16), group), axis=0, keepdims=True)
        fmax = jnp.max(fcum, axis=0, keepdims=True)
        fmin = jnp.min(fcum, axis=0, keepdims=True)
        stat_ref[g] = jnp.concatenate([nk2, fmax, fmin, jnp.zeros((5, LANES), F32)], axis=0)


def _aug_placement():
    pq = np.zeros((3 * LANES, FOX_W), np.float32)
    pk = np.zeros((3 * LANES, FOX_W), np.float32)
    for h in range(FOX_HEADS):
        base = (h // 2) * LANES + (h % 2) * FOX_AUG_PER_HEAD
        for c in range(3):
            pq[c * LANES + h, base + 3 + c] = 1.0
            pk[c * LANES + h, base + c] = 1.0
    return jnp.asarray(pq.T, BF16), jnp.asarray(pk, BF16)


def _proj(x, g_mix, w_gla, w_fk, w_qvt, w_mq, w_gate, w_small, w_up, b_alpha, b_forget, mk, mv):
    s = x.shape[0]
    tm = ROW_TILE
    nt = s // tm
    pqt, pk = _aug_placement()
    full = lambda shape: pl.BlockSpec(shape, lambda i: (0,) * len(shape), pipeline_mode=pl.Buffered(1))
    row = lambda w: pl.BlockSpec((tm, w), lambda i: (i, 0))
    col = lambda r: pl.BlockSpec((r, tm), lambda i: (0, i))
    ng = tm // PROJ_GROUP
    stat = pl.BlockSpec((ng, 8, LANES), lambda i: (i, 0, 0))
    vt_rows = FOX_HEADS * (FOX_DH + FOX_ONES_ROWS)
    return pl.pallas_call(
        _proj_kernel,
        grid=(nt,),
        in_specs=[row(D_MODEL), full((1, D_MODEL)), full(w_gla.shape), full(w_fk.shape), full(w_qvt.shape),
                  full(w_mq.shape), full(w_gate.shape), full(w_small.shape), full(w_up.shape),
                  full(b_alpha.shape), full(b_forget.shape), full(mk.shape), full(mv.shape),
                  full(pqt.shape), full(pk.shape)],
        out_specs=[row(w_gla.shape[1]), row(GLA_K), col(2 * FOX_W),
                   pl.BlockSpec((FOX_HEADS // 2, tm, 2 * LANES), lambda i: (0, i, 0)), col(vt_rows), row(MEM_W),
                   row(w_gate.shape[1]), stat, stat],
        out_shape=[jax.ShapeDtypeStruct((s, w_gla.shape[1]), BF16),
                   jax.ShapeDtypeStruct((s, GLA_K), F32),
                   jax.ShapeDtypeStruct((2 * FOX_W, s), BF16),
                   jax.ShapeDtypeStruct((FOX_HEADS // 2, s, 2 * LANES), BF16),
                   jax.ShapeDtypeStruct((vt_rows, s), BF16),
                   jax.ShapeDtypeStruct((s, MEM_W), BF16),
                   jax.ShapeDtypeStruct((s, w_gate.shape[1]), BF16),
                   jax.ShapeDtypeStruct((nt * ng, 8, LANES), F32),
                   jax.ShapeDtypeStruct((nt * ng, 8, LANES), F32)],
        scratch_shapes=[pltpu.VMEM((1, LANES), F32), pltpu.VMEM((w_qvt.shape[1], w_qvt.shape[0]), BF16)],
        compiler_params=pltpu.CompilerParams(dimension_semantics=("arbitrary",), vmem_limit_bytes=VMEM_LIMIT),
        name="proj",
    )(x, g_mix, w_gla, w_fk, w_qvt, w_mq, w_gate, w_small, w_up, b_alpha, b_forget, mk, mv, pqt, pk)


def _gla_kernel(qkvg_ref, loga_ref, ghead_ref, *rest):
    n_w = (len(rest) - 2) // 2
    w_in_refs, o_ref, w_out_refs, state_ref = rest[:n_w], rest[n_w], rest[n_w + 1:2 * n_w + 1], rest[-1]
    t = GLA_GROUP
    nc = t // CHUNK

    @pl.when(pl.program_id(0) == 0)
    def _():
        state_ref[...] = jnp.zeros_like(state_ref)

    for w_src, w_dst in zip(w_in_refs, w_out_refs):
        w_dst[...] = w_src[...].astype(BF16)

    r = lax.broadcasted_iota(jnp.int32, (t, t), 0)
    c = lax.broadcasted_iota(jnp.int32, (t, t), 1)
    same = (r // CHUNK) == (c // CHUNK)
    lower = r >= c
    causal = same & lower
    anti = same & (r < c)
    cum = jnp.where(causal, 1.0, 0.0).astype(BF16)
    lane = lax.broadcasted_iota(jnp.int32, (1, LANES), 1)

    chunks = [slice(ci * CHUNK, (ci + 1) * CHUNK) for ci in range(nc)]
    pairs = [slice((h // 2) * LANES, (h // 2 + 1) * LANES) for h in range(GLA_HEADS)]
    in_head = [(lane // GLA_DK) == (h % 2) for h in range(GLA_HEADS)]
    heads = range(GLA_HEADS)
    groups = [slice(g0, g0 + t) for g0 in range(0, qkvg_ref.shape[0], t)]

    pre = []
    for rg in groups:
        la = loga_ref[rg, :]
        la_hi = la.astype(BF16)
        la_r = la - la_hi.astype(F32)
        la_mid = la_r.astype(BF16)
        la_lo = (la_r - la_mid.astype(F32)).astype(BF16)
        b3 = _dot(cum, jnp.concatenate([la_hi, la_mid, la_lo], axis=1))
        b = b3[:, :GLA_K] + b3[:, GLA_K:2 * GLA_K] + b3[:, 2 * GLA_K:]
        b_last = jnp.concatenate(
            [jnp.broadcast_to(b[(ci + 1) * CHUNK - 1:(ci + 1) * CHUNK, :], (CHUNK, GLA_K)) for ci in range(nc)],
            axis=0)
        e_pos = jnp.exp(b)
        e_neg = jnp.exp(-b)
        q = qkvg_ref[rg, 0:GLA_K].astype(F32) * (GLA_DK ** -0.5)
        k = qkvg_ref[rg, GLA_K:2 * GLA_K].astype(F32)
        q_pos = (q * e_pos).astype(BF16)
        q_neg = (q * e_neg).astype(BF16)
        k_pos = (k * e_pos).astype(BF16)
        k_neg = (k * e_neg).astype(BF16)
        k_dec = (k * jnp.exp(b_last - b)).astype(BF16)

        vs = [qkvg_ref[rg, 2 * GLA_K + h * GLA_DV:2 * GLA_K + (h + 1) * GLA_DV] for h in heads]
        qps = [jnp.where(in_head[h], q_pos[:, pairs[h]], jnp.zeros((), BF16)) for h in heads]
        kvs = [[jnp.where(in_head[h], _dot_tn(vs[h][rows], k_dec[rows, pairs[h]]), 0.0) for rows in chunks]
               for h in heads]
        pre.append((e_pos, q_neg, k_pos, k_neg, vs, qps, kvs))

    sts = [state_ref[h] for h in heads]
    inters = []
    for e_pos, _, _, _, _, qps, kvs in pre:
        inter = [[] for _ in heads]
        for ci, rows in enumerate(chunks):
            for h in heads:
                inter[h].append(_dot_nt(qps[h][rows], sts[h].astype(BF16)))
                dec = e_pos[(ci + 1) * CHUNK - 1:(ci + 1) * CHUNK, pairs[h]]
                sts[h] = sts[h] * dec + kvs[h][ci]
        inters.append(inter)
    for h in heads:
        state_ref[h] = sts[h]

    attns = []
    for _, q_neg, k_pos, k_neg, _, qps, _ in pre:
        grp = []
        for h in heads:
            qn = jnp.where(in_head[h], q_neg[:, pairs[h]], jnp.zeros((), BF16))
            a_c = _dot_nt(qps[h], k_neg[:, pairs[h]])
            a_a = _dot_nt(qn, k_pos[:, pairs[h]])
            grp.append(jnp.where(causal, a_c, jnp.where(anti, a_a, 0.0)).astype(BF16))
        attns.append(grp)

    for rg, grp, inter, (_, _, _, _, vs, _, _) in zip(groups, attns, inters, pre):
        outs = [_dot(grp[h], vs[h]) + jnp.concatenate(inter[h], axis=0) for h in heads]
        for h in heads:
            gh = ghead_ref[:, h * GLA_DV:(h + 1) * GLA_DV]
            on = _rms(outs[h], gh)
            gsl = slice(2 * GLA_K + GLA_V + h * GLA_DV, 2 * GLA_K + GLA_V + (h + 1) * GLA_DV)
            gg = qkvg_ref[rg, gsl].astype(F32)
            o_ref[rg, h * GLA_DV:(h + 1) * GLA_DV] = (on * (gg * _sigmoid(gg))).astype(BF16)


def _gla(qkvg, loga, ghead, later_weights):
    s = qkvg.shape[0]
    t = GLA_TILE
    assert s % t == 0 and t % GLA_GROUP == 0
    steps = s // t
    w_specs = [pl.BlockSpec((w.shape[0] // steps, w.shape[1]), lambda i: (i, 0)) for w in later_weights]
    assert all(w.shape[0] % (steps * BF16_SUBLANES) == 0 for w in later_weights)
    outs = pl.pallas_call(
        _gla_kernel,
        grid=(steps,),
        in_specs=[pl.BlockSpec((t, qkvg.shape[1]), lambda i: (i, 0)),
                  pl.BlockSpec((t, GLA_K), lambda i: (i, 0)),
                  pl.BlockSpec((1, GLA_V), lambda i: (0, 0))] + w_specs,
        out_specs=[pl.BlockSpec((t, GLA_V), lambda i: (i, 0))] + w_specs,
        out_shape=[jax.ShapeDtypeStruct((s, GLA_V), BF16)]
        + [jax.ShapeDtypeStruct(w.shape, BF16) for w in later_weights],
        scratch_shapes=[pltpu.VMEM((GLA_HEADS, GLA_DV, LANES), F32)],
        compiler_params=pltpu.CompilerParams(dimension_semantics=("arbitrary",), vmem_limit_bytes=VMEM_LIMIT),
        name="gla",
    )(qkvg, loga, ghead, *later_weights)
    return outs[0], outs[1:]


def _fox_kernel(thr_ref, fmin_ref, shift_ref, qt_ref, k_ref, vt_ref, o_ref,
                sa_ref, sb_ref, pa_ref, pb_ref, m_ref, acc_ref, si_ref, sj_ref):
    p = pl.program_id(0)
    tq = FOX_TQ
    tk = FOX_TK
    nsub = tq // tk
    nq = k_ref.shape[0] // tq
    hrows = FOX_DH + FOX_ONES_ROWS
    row = lax.broadcasted_iota(jnp.int32, (2 * LANES, 1), 0)
    krow = lax.broadcasted_iota(jnp.int32, (tk, tq), 0)
    qcol = lax.broadcasted_iota(jnp.int32, (tk, tq), 1)
    m0 = jnp.full((1, tq), NEG_BIG, F32)
    acc0 = jnp.zeros((hrows, tq), F32)

    for hi in range(2):
        h = 2 * p + hi
        aug0 = LANES + hi * FOX_AUG_PER_HEAD
        mine = ((row // FOX_DH) == hi) | ((row >= aug0) & (row < aug0 + FOX_AUG_PER_HEAD))
        vrows = slice(hi * hrows, (hi + 1) * hrows)

        def score_tiles(qi, kj, masked):
            q0 = pl.multiple_of(qi * tq, tq)
            qm = jnp.where(mine, qt_ref[:, pl.ds(q0, tq)], jnp.zeros((), BF16))
            for u in range(nsub):
                k0 = pl.multiple_of(kj * tq + u * tk, tk)
                s_t = _dot(k_ref[pl.ds(k0, tk), :], qm)
                if masked:
                    s_t = jnp.where(krow + u * tk <= qcol, s_t, NEG_BIG)
                yield u, s_t

        def values(kj, u):
            return vt_ref[vrows, pl.ds(pl.multiple_of(kj * tq + u * tk, tk), tk)]

        def fast_produce(qi, kj, buf, masked=False):
            shift = shift_ref[h, qi]
            for u, s_t in score_tiles(qi, kj, masked):
                buf[u] = jnp.exp(s_t - shift).astype(BF16)

        def fast_consume(i, kj, buf, fresh=False):
            acc = acc0 if fresh else acc_ref[hi, i]
            for u in range(nsub):
                acc = acc + _dot(values(kj, u), buf[u])
            acc_ref[hi, i] = acc

        def slow_produce(qi, kj, buf, masked=False):
            for u, s_t in score_tiles(qi, kj, masked):
                buf[u] = s_t

        def slow_consume(i, kj, buf, fresh=False):
            m = m0 if fresh else m_ref[hi, i]
            acc = acc0 if fresh else acc_ref[hi, i]
            for u in range(nsub):
                s_t = buf[u]
                m_new = jnp.maximum(m, jnp.max(s_t, axis=0, keepdims=True))
                alpha = jnp.exp(m - m_new)
                pt = jnp.exp(s_t - m_new).astype(BF16)
                acc = alpha * acc + _dot(values(kj, u), pt)
                m = m_new
            m_ref[hi, i] = m
            acc_ref[hi, i] = acc

        def list_block(i, t):
            thr = thr_ref[h, i]
            for j in range(nq):
                si_ref[t] = i
                sj_ref[t] = j
                t = t + jnp.where((j < i) & (fmin_ref[h, j] <= thr), 1, 0)
            return t

        worst_shift = lax.fori_loop(0, nq, lambda i, w: jnp.maximum(w, shift_ref[h, i]), jnp.float32(0.0))

        def run_head(produce, consume, buf_a, buf_b):
            def diag_prefetch(i, buf):
                ic = jnp.minimum(i, nq - 1)
                produce(ic, ic, buf, masked=True)

            diag_prefetch(0, buf_a)

            def diag_step(tn, n):
                i = FOX_STEPS * tn
                for d in range(0, FOX_STEPS, 2):
                    diag_prefetch(i + d + 1, buf_b)
                    consume(i + d, i + d, buf_a, fresh=True)
                    diag_prefetch(i + d + 2, buf_a)
                    consume(i + d + 1, i + d + 1, buf_b, fresh=True)
                for d in range(FOX_STEPS):
                    n = list_block(i + d, n)
                return n

            n_tiles = lax.fori_loop(0, nq // FOX_STEPS, diag_step, 0)
            for pad in range(FOX_STEPS):
                si_ref[n_tiles + pad] = nq
                sj_ref[n_tiles + pad] = 0
            m_ref[hi, nq] = m0
            acc_ref[hi, nq] = acc0

            def prefetch(t, buf):
                produce(jnp.minimum(si_ref[t], nq - 1), sj_ref[t], buf)

            prefetch(0, buf_a)

            def multi_step(tn, carry):
                t = FOX_STEPS * tn
                for d in range(0, FOX_STEPS, 2):
                    prefetch(t + d + 1, buf_b)
                    consume(si_ref[t + d], sj_ref[t + d], buf_a)
                    prefetch(t + d + 2, buf_a)
                    consume(si_ref[t + d + 1], sj_ref[t + d + 1], buf_b)
                return carry

            lax.fori_loop(0, (n_tiles + FOX_STEPS - 1) // FOX_STEPS, multi_step, 0)

        no_running_max = 2.0 * worst_shift <= FOX_MAX_SHIFT_GAP

        @pl.when(no_running_max)
        def _():
            run_head(fast_produce, fast_consume, pa_ref, pb_ref)

        @pl.when(jnp.logical_not(no_running_max))
        def _():
            run_head(slow_produce, slow_consume, sa_ref, sb_ref)

    def finish(i, carry):
        a0 = acc_ref[0, i]
        a1 = acc_ref[1, i]
        ot = jnp.concatenate([a0[:FOX_DH] / a0[FOX_DH:FOX_DH + 1], a1[:FOX_DH] / a1[FOX_DH:FOX_DH + 1]], axis=0)
        o_ref[pl.ds(pl.multiple_of(i * tq, tq), tq), :] = ot.T.astype(BF16)
        return carry

    lax.fori_loop(0, nq, finish, 0)


def _fox(thr, fmin_blk, shift, qt, kaug, vt):
    s = kaug.shape[1]
    tq = FOX_TQ
    nq = s // tq
    npair = FOX_HEADS // 2
    prow = 2 * (FOX_DH + FOX_ONES_ROWS)
    assert nq % FOX_STEPS == 0 and FOX_STEPS % 2 == 0
    hrows = FOX_DH + FOX_ONES_ROWS
    max_tiles = nq * (nq - 1) // 2 + FOX_STEPS
    once = pl.Buffered(1)
    grid_spec = pltpu.PrefetchScalarGridSpec(
        num_scalar_prefetch=3,
        grid=(npair,),
        in_specs=[pl.BlockSpec((2 * LANES, s), lambda p, *_: (p, 0), pipeline_mode=once),
                  pl.BlockSpec((None, s, 2 * LANES), lambda p, *_: (p, 0, 0), pipeline_mode=once),
                  pl.BlockSpec((prow, s), lambda p, *_: (p, 0), pipeline_mode=once)],
        out_specs=pl.BlockSpec((None, s, LANES), lambda p, *_: (p, 0, 0)),
        scratch_shapes=[pltpu.VMEM((tq // FOX_TK, FOX_TK, tq), F32), pltpu.VMEM((tq // FOX_TK, FOX_TK, tq), F32),
                        pltpu.VMEM((tq // FOX_TK, FOX_TK, tq), BF16), pltpu.VMEM((tq // FOX_TK, FOX_TK, tq), BF16),
                        pltpu.VMEM((2, nq + 1, 1, tq), F32), pltpu.VMEM((2, nq + 1, hrows, tq), F32),
                        pltpu.SMEM((max_tiles,), jnp.int32), pltpu.SMEM((max_tiles,), jnp.int32)],
    )
    return pl.pallas_call(
        _fox_kernel,
        grid_spec=grid_spec,
        out_shape=jax.ShapeDtypeStruct((npair, s, LANES), BF16),
        compiler_params=pltpu.CompilerParams(dimension_semantics=("arbitrary",), vmem_limit_bytes=VMEM_LIMIT),
        name="fox",
    )(thr, fmin_blk, shift, qt, kaug, vt)


def _tail_kernel(x_ref, ogla_ref, ofox_ref, omem_ref, gate_ref, wg_ref, wf_ref, wm_ref, wo_ref,
                 gffn_ref, w1_ref, w2_ref, gfin_ref, out_ref):
    groups = [slice(r0, r0 + TAIL_GROUP) for r0 in range(0, x_ref.shape[0], TAIL_GROUP)]
    merged = []
    for rg in groups:
        ofox = jnp.concatenate([ofox_ref[p, rg, :] for p in range(FOX_HEADS // 2)], axis=1)
        merged.append((gate_ref[rg, 0:D_MODEL].astype(F32) * _dot(ogla_ref[rg, :], wg_ref[...])
                       + gate_ref[rg, D_MODEL:2 * D_MODEL].astype(F32) * _dot(ofox, wf_ref[...])
                       + gate_ref[rg, 2 * D_MODEL:3 * D_MODEL].astype(F32) * _dot(omem_ref[rg, :], wm_ref[...])
                       ).astype(BF16))
    hs = [x_ref[rg, :] + _dot(mg, wo_ref[...]) for rg, mg in zip(groups, merged)]
    u2s = [_rms(h, gffn_ref[...]).astype(BF16) for h in hs]
    accs = [jnp.zeros_like(h) for h in hs]
    for cidx in range(D_FF // FF_CHUNK):
        cs = slice(cidx * FF_CHUNK, (cidx + 1) * FF_CHUNK)
        for g, u2 in enumerate(u2s):
            a = jnp.maximum(_dot(u2, w1_ref[:, cs]), 0.0)
            accs[g] = accs[g] + _dot((a * a).astype(BF16), w2_ref[cs, :])
    for rg, h, acc in zip(groups, hs, accs):
        out_ref[rg, :] = _rms(h + acc, gfin_ref[...])


def _tail(x, ogla, ofox, omem, gate, wg, wf, wm, wo, gffn, w1, w2, gfin):
    s = x.shape[0]
    tm = TAIL_TILE
    full = lambda a: pl.BlockSpec(a.shape, lambda i: (0,) * a.ndim, pipeline_mode=pl.Buffered(1))
    row = lambda w: pl.BlockSpec((tm, w), lambda i: (i, 0))
    return pl.pallas_call(
        _tail_kernel,
        grid=(s // tm,),
        in_specs=[row(D_MODEL), row(GLA_V), pl.BlockSpec((FOX_HEADS // 2, tm, LANES), lambda i: (0, i, 0)),
                  row(MEM_W), row(3 * D_MODEL),
                  full(wg), full(wf), full(wm), full(wo), full(gffn), full(w1), full(w2), full(gfin)],
        out_specs=row(D_MODEL),
        out_shape=jax.ShapeDtypeStruct((s, D_MODEL), F32),
        compiler_params=pltpu.CompilerParams(dimension_semantics=("arbitrary",), vmem_limit_bytes=VMEM_LIMIT),
        name="tail",
    )(x, ogla, ofox, omem, gate, wg, wf, wm, wo, gffn, w1, w2, gfin)


def kernel(x, mem, g_mix, w_in, w_alpha_up, b_alpha, b_forget, g_gla_head, g_mem, w_mem_kv,
           w_gla_o, w_fox_o, w_mem_o, w_out, g_ffn, w_ff1, w_ff2, g_final):
    assert x.shape[0] == 1 and g_mix.shape[0] == 1, "single batch, single layer"
    s = x.shape[1]
    assert s % ROW_TILE == 0 and s % TAIL_TILE == 0 and s % FOX_TQ == 0
    assert FOX_TQ % FOX_TK == 0 and FOX_TQ % PROJ_GROUP == 0 and ROW_TILE % PROJ_GROUP == 0
    xs = x[0]
    w = w_in[0]

    w_gla = w[:, :OFF_GA].astype(BF16)
    w_fk = w[:, OFF_FK:OFF_FV].astype(BF16)
    w_qvt = jnp.concatenate([w[:, OFF_FQ:OFF_FK], w[:, OFF_FV:OFF_FF]], axis=1).astype(BF16)
    w_mq = w[:, OFF_MQ:OFF_GATE].astype(BF16)
    w_gate = w[:, OFF_GATE:].astype(BF16)
    w_small = jnp.concatenate(
        [w[:, OFF_GA:OFF_FQ], w[:, OFF_FF:OFF_MQ], jnp.zeros((D_MODEL, LANES - GLA_LOWRANK - FOX_HEADS), F32)], axis=1)
    w_small_hi = w_small.astype(BF16)
    w_small = jnp.concatenate([w_small_hi, (w_small - w_small_hi.astype(F32)).astype(BF16)], axis=1)
    w_up_hi = w_alpha_up[0].astype(BF16)
    w_up = jnp.concatenate([w_up_hi, (w_alpha_up[0] - w_up_hi.astype(F32)).astype(BF16)], axis=1)
    b_f = jnp.zeros((1, LANES), F32).at[0, GLA_LOWRANK:GLA_LOWRANK + FOX_HEADS].set(b_forget[0])

    mk, mv = _memkv(mem[0], g_mem, w_mem_kv[0].astype(BF16))
    gla_qkvg, loga, qt, kaug, vt, omem, gate, stats, statq = _proj(
        xs, g_mix, w_gla, w_fk, w_qvt, w_mq, w_gate, w_small, w_up, b_alpha, b_f, mk, mv)

    ogla, (wg, wf, wm, wo, w1, w2) = _gla(
        gla_qkvg, loga, g_gla_head.reshape(1, GLA_V),
        [w_gla_o[0], w_fox_o[0], w_mem_o[0], w_out[0], w_ff1[0], w_ff2[0]])

    per_blk = lambda a: a.reshape(s // FOX_TQ, FOX_TQ // PROJ_GROUP, FOX_HEADS)
    qn = jnp.sqrt(jnp.max(per_blk(statq[:, :, 0]), axis=1))
    kn = jnp.sqrt(jnp.max(stats[:, 0, :FOX_HEADS], axis=0))
    fmax = jnp.max(per_blk(stats[:, 1, :FOX_HEADS]), axis=1)
    fmin = jnp.min(per_blk(stats[:, 2, :FOX_HEADS]), axis=1)
    shift = (1.02 * (FOX_DH ** -0.5)) * qn * kn[None, :]
    thr = 2.0 * shift + fmax + PRUNE_LOGIT_GAP
    ofox = _fox(thr.T, fmin.T, shift.T, qt, kaug, vt)

    out = _tail(xs, ogla, ofox, omem, gate, wg, wf, wm, wo, g_ffn, w1, w2, g_final.reshape(1, D_MODEL))
    return out[None]
```

```python
import jax
import jax.numpy as jnp
import numpy as np
from jax import lax
from jax.experimental import pallas as pl
from jax.experimental.pallas import tpu as pltpu

D_MODEL = 1024
CHUNK = 64
EPS = 1e-6
GLA_HEADS = 4
GLA_DK = 64
GLA_DV = 128
GLA_LOWRANK = 16
GLA_TAU = 16.0
FOX_HEADS = 8
FOX_DH = 64
MEM_HEADS = 4
MEM_DH = 128
D_FF = 4 * D_MODEL
GLA_K = GLA_HEADS * GLA_DK
GLA_V = GLA_HEADS * GLA_DV
FOX_W = FOX_HEADS * FOX_DH
MEM_W = MEM_HEADS * MEM_DH
OFF_GA = 2 * GLA_K + 2 * GLA_V
OFF_FQ = OFF_GA + GLA_LOWRANK
OFF_FK = OFF_FQ + FOX_W
OFF_FV = OFF_FK + FOX_W
OFF_FF = OFF_FV + FOX_W
OFF_MQ = OFF_FF + FOX_HEADS
OFF_GATE = OFF_MQ + MEM_W

LANES = 128
BF16_SUBLANES = 16
FOX_ONES_ROWS = 16
FOX_AUG_PER_HEAD = 6
ROW_TILE = 512
PROJ_GROUP = 256
GLA_TILE = 512
GLA_GROUP = 256
FOX_TQ = 512
FOX_TK = 256
FOX_STEPS = 8
TAIL_TILE = 512
TAIL_GROUP = 256
FF_CHUNK = 1024
VMEM_LIMIT = 56 * 1024 * 1024
NEG_BIG = -1e30
PRUNE_LOGIT_GAP = 104.0
FOX_MAX_SHIFT_GAP = 50.0

F32 = jnp.float32
BF16 = jnp.bfloat16


def _rms(xf, g):
    r = lax.rsqrt(jnp.mean(xf * xf, axis=-1, keepdims=True) + EPS)
    return (xf * r) * g


def _log_sigmoid(x):
    return jnp.minimum(x, 0.0) - jnp.log1p(jnp.exp(-jnp.abs(x)))


def _sigmoid(x):
    return 1.0 / (1.0 + jnp.exp(-x))


def _dot(a, b):
    return jnp.dot(a, b, preferred_element_type=F32)


def _dot_nt(a, b):
    return lax.dot_general(a, b, (((1,), (1,)), ((), ())), preferred_element_type=F32)


def _dot_tn(a, b):
    return lax.dot_general(a, b, (((0,), (0,)), ((), ())), preferred_element_type=F32)


def _memkv_kernel(mem_ref, g_ref, w_ref, mk_ref, mv_ref):
    mn = _rms(mem_ref[...], g_ref[...]).astype(BF16)
    kv = _dot(mn, w_ref[...])
    mk_ref[...] = kv[:, :MEM_W].astype(BF16)
    mv_ref[...] = kv[:, MEM_W:].astype(BF16)


def _memkv(mem, g_mem, w_mem_kv):
    m = mem.shape[0]
    return pl.pallas_call(
        _memkv_kernel,
        out_shape=(jax.ShapeDtypeStruct((m, MEM_W), BF16), jax.ShapeDtypeStruct((m, MEM_W), BF16)),
        name="memkv",
    )(mem, g_mem, w_mem_kv)


def _proj_kernel(x_ref, gmix_ref, wgla_ref, wfk_ref, wqv_ref, wmq_ref, wgate_ref, wsmall_ref, wup_ref,
                 balpha_ref, bforget_ref, mk_ref, mv_ref, pqt_ref, pk_ref,
                 gla_ref, loga_ref, qt_ref, kaug_ref, vt_ref, omem_ref, gate_ref, stat_ref, statq_ref,
                 carry_ref, wqvt_ref):
    tm = x_ref.shape[0]

    @pl.when(pl.program_id(0) == 0)
    def _():
        carry_ref[...] = jnp.zeros_like(carry_ref)
        wqvt_ref[...] = wqv_ref[...].T

    tg = PROJ_GROUP
    groups = [slice(r0, r0 + tg) for r0 in range(0, tm, tg)]
    gid = range(len(groups))
    scale = MEM_DH ** -0.5
    heads = [slice(h * MEM_DH, (h + 1) * MEM_DH) for h in range(MEM_HEADS)]

    ubs, smalls, scores = [], [], []
    for rg in groups:
        u = _rms(x_ref[rg, :], gmix_ref[...])
        ub = u.astype(BF16)
        u_lo = (u - ub.astype(F32)).astype(BF16)
        parts = _dot(jnp.concatenate([ub, u_lo], axis=0), wsmall_ref[...])
        smalls.append((parts[:tg, :LANES] + parts[:tg, LANES:]) + (parts[tg:, :LANES] + parts[tg:, LANES:]))
        mq = _dot(ub, wmq_ref[...]).astype(BF16)
        scores.append([_dot_nt(mq[:, sl], mk_ref[:, sl]) * scale for sl in heads])
        ubs.append(ub)

    fkbs = []
    for rg, ub in zip(groups, ubs):
        gla_ref[rg, :] = _dot(ub, wgla_ref[...]).astype(BF16)
        fkbs.append(_dot(ub, wfk_ref[...]).astype(BF16))

    lane = lax.broadcasted_iota(jnp.int32, (tg, LANES), 1)
    ff_valid = (lane >= GLA_LOWRANK) & (lane < GLA_LOWRANK + FOX_HEADS)
    r = lax.broadcasted_iota(jnp.int32, (tg, tg), 0)
    c = lax.broadcasted_iota(jnp.int32, (tg, tg), 1)
    tri = jnp.where(r >= c, 1.0, 0.0).astype(BF16)

    def aug_slot(idx):
        a = idx % LANES
        return a % FOX_AUG_PER_HEAD, a < 2 * FOX_AUG_PER_HEAD

    slot_q, in_q = aug_slot(lax.broadcasted_iota(jnp.int32, (FOX_W, 1), 0))
    slot_k, in_k = aug_slot(lax.broadcasted_iota(jnp.int32, (1, FOX_W), 1))
    carry = carry_ref[...]
    fcums, augqts, augks = [], [], []
    for rg, small in zip(groups, smalls):
        ga = small[:, :GLA_LOWRANK]
        ga_hi = ga.astype(BF16)
        ga_lo = (ga - ga_hi.astype(F32)).astype(BF16)
        up = _dot(jnp.concatenate([ga_hi, ga_lo], axis=0), wup_ref[...])
        alpha_pre = ((up[:tg, :GLA_K] + up[:tg, GLA_K:]) + (up[tg:, :GLA_K] + up[tg:, GLA_K:])) + balpha_ref[...]
        loga_ref[rg, :] = _log_sigmoid(alpha_pre) * (1.0 / GLA_TAU)

        logf = jnp.where(ff_valid, _log_sigmoid(small + bforget_ref[...]), 0.0)
        logf = pltpu.roll(logf, LANES - GLA_LOWRANK, 1)
        lf_hi = logf.astype(BF16)
        lf_r = logf - lf_hi.astype(F32)
        lf_mid = lf_r.astype(BF16)
        lf_lo = (lf_r - lf_mid.astype(F32)).astype(BF16)
        c3 = _dot(tri, jnp.concatenate([lf_hi, lf_mid, lf_lo], axis=1))
        fcum = (c3[:, :LANES] + c3[:, LANES:2 * LANES] + c3[:, 2 * LANES:]) + carry
        carry = fcum[tg - 1:tg, :]

        f_hi = fcum.astype(BF16)
        rem = fcum - f_hi.astype(F32)
        f_mid = rem.astype(BF16)
        f_lo = (rem - f_mid.astype(F32)).astype(BF16)
        f3 = jnp.concatenate([f_hi, f_mid, f_lo], axis=1)
        augqts.append(_dot_nt(pqt_ref[...], f3) + jnp.where(in_q & (slot_q < 3), -1.0, 0.0))
        augks.append(_dot(f3, pk_ref[...]) + jnp.where(in_k & (slot_k >= 3), 1.0, 0.0))
        fcums.append(fcum)
    carry_ref[...] = carry

    for rg, ub in zip(groups, ubs):
        gate_ref[rg, :] = _sigmoid(_dot(ub, wgate_ref[...])).astype(BF16)

    for rg, sc in zip(groups, scores):
        probs = []
        for s in sc:
            m = jnp.max(s, axis=-1, keepdims=True)
            e = jnp.exp(s - m)
            probs.append((e / jnp.sum(e, axis=-1, keepdims=True)).astype(BF16))
        omem_ref[rg, :] = jnp.concatenate([_dot(pr, mv_ref[:, sl]) for pr, sl in zip(probs, heads)],
                                          axis=-1).astype(BF16)

    ones = jnp.ones((FOX_ONES_ROWS, tg), BF16)
    gi = lax.broadcasted_iota(jnp.int32, (FOX_W, LANES), 0) // FOX_DH
    gj = lax.broadcasted_iota(jnp.int32, (FOX_W, LANES), 1)
    group = jnp.where(gi == gj, 1.0, 0.0).astype(BF16)
    for g, rg, ub, fkb, fcum, augqt, augk in zip(gid, groups, ubs, fkbs, fcums, augqts, augks):
        qvt = _dot_nt(wqvt_ref[...], ub)
        fqt = (qvt[:FOX_W] * (FOX_DH ** -0.5)).astype(BF16)
        vt = qvt[FOX_W:].astype(BF16)
        vt_parts = []
        for h in range(FOX_HEADS):
            vt_parts += [vt[h * FOX_DH:(h + 1) * FOX_DH, :], ones]
        vt_ref[:, rg] = jnp.concatenate(vt_parts, axis=0)

        for p in range(FOX_HEADS // 2):
            src = slice(p * LANES, (p + 1) * LANES)
            qt_ref[2 * p * LANES:(2 * p + 1) * LANES, rg] = fqt[src]
            qt_ref[(2 * p + 1) * LANES:(2 * p + 2) * LANES, rg] = augqt[src].astype(BF16)
            kaug_ref[p, rg, :LANES] = fkb[:, src]
            kaug_ref[p, rg, LANES:] = augk[:, src].astype(BF16)

        fq32 = fqt.astype(F32) * (FOX_DH ** 0.5)
        nq2 = jnp.sum((fq32 * fq32).reshape(FOX_HEADS, FOX_DH, tg), axis=1)
        statq_ref[g] = jnp.broadcast_to(jnp.max(nq2, axis=1, keepdims=True), (FOX_HEADS, LANES))
        fk = fkb.astype(F32)
        nk2 = jnp.max(_dot((fk * fk).astype(BF16), group), axis=0, keepdims=True)
        fmax = jnp.max(fcum, axis=0, keepdims=True)
        fmin = jnp.min(fcum, axis=0, keepdims=True)
        stat_ref[g] = jnp.concatenate([nk2, fmax, fmin, jnp.zeros((5, LANES), F32)], axis=0)


def _aug_placement():
    pq = np.zeros((3 * LANES, FOX_W), np.float32)
    pk = np.zeros((3 * LANES, FOX_W), np.float32)
    for h in range(FOX_HEADS):
        base = (h // 2) * LANES + (h % 2) * FOX_AUG_PER_HEAD
        for c in range(3):
            pq[c * LANES + h, base + 3 + c] = 1.0
            pk[c * LANES + h, base + c] = 1.0
    return jnp.asarray(pq.T, BF16), jnp.asarray(pk, BF16)


def _proj(x, g_mix, w_gla, w_fk, w_qvt, w_mq, w_gate, w_small, w_up, b_alpha, b_forget, mk, mv):
    s = x.shape[0]
    tm = ROW_TILE
    nt = s // tm
    pqt, pk = _aug_placement()
    full = lambda shape: pl.BlockSpec(shape, lambda i: (0,) * len(shape), pipeline_mode=pl.Buffered(1))
    row = lambda w: pl.BlockSpec((tm, w), lambda i: (i, 0))
    col = lambda r: pl.BlockSpec((r, tm), lambda i: (0, i))
    ng = tm // PROJ_GROUP
    stat = pl.BlockSpec((ng, 8, LANES), lambda i: (i, 0, 0))
    vt_rows = FOX_HEADS * (FOX_DH + FOX_ONES_ROWS)
    return pl.pallas_call(
        _proj_kernel,
        grid=(nt,),
        in_specs=[row(D_MODEL), full((1, D_MODEL)), full(w_gla.shape), full(w_fk.shape), full(w_qvt.shape),
                  full(w_mq.shape), full(w_gate.shape), full(w_small.shape), full(w_up.shape),
                  full(b_alpha.shape), full(b_forget.shape), full(mk.shape), full(mv.shape),
                  full(pqt.shape), full(pk.shape)],
        out_specs=[row(w_gla.shape[1]), row(GLA_K), col(2 * FOX_W),
                   pl.BlockSpec((FOX_HEADS // 2, tm, 2 * LANES), lambda i: (0, i, 0)), col(vt_rows), row(MEM_W),
                   row(w_gate.shape[1]), stat, stat],
        out_shape=[jax.ShapeDtypeStruct((s, w_gla.shape[1]), BF16),
                   jax.ShapeDtypeStruct((s, GLA_K), F32),
                   jax.ShapeDtypeStruct((2 * FOX_W, s), BF16),
                   jax.ShapeDtypeStruct((FOX_HEADS // 2, s, 2 * LANES), BF16),
                   jax.ShapeDtypeStruct((vt_rows, s), BF16),
                   jax.ShapeDtypeStruct((s, MEM_W), BF16),
                   jax.ShapeDtypeStruct((s, w_gate.shape[1]), BF16),
                   jax.ShapeDtypeStruct((nt * ng, 8, LANES), F32),
                   jax.ShapeDtypeStruct((nt * ng, 8, LANES), F32)],
        scratch_shapes=[pltpu.VMEM((1, LANES), F32), pltpu.VMEM((w_qvt.shape[1], w_qvt.shape[0]), BF16)],
        compiler_params=pltpu.CompilerParams(dimension_semantics=("arbitrary",), vmem_limit_bytes=VMEM_LIMIT),
        name="proj",
    )(x, g_mix, w_gla, w_fk, w_qvt, w_mq, w_gate, w_small, w_up, b_alpha, b_forget, mk, mv, pqt, pk)


def _gla_kernel(qkvg_ref, loga_ref, ghead_ref, *rest):
    n_w = (len(rest) - 2) // 2
    w_in_refs, o_ref, w_out_refs, state_ref = rest[:n_w], rest[n_w], rest[n_w + 1:2 * n_w + 1], rest[-1]
    t = GLA_GROUP
    nc = t // CHUNK

    @pl.when(pl.program_id(0) == 0)
    def _():
        state_ref[...] = jnp.zeros_like(state_ref)

    for w_src, w_dst in zip(w_in_refs, w_out_refs):
        w_dst[...] = w_src[...].astype(BF16)

    r = lax.broadcasted_iota(jnp.int32, (t, t), 0)
    c = lax.broadcasted_iota(jnp.int32, (t, t), 1)
    same = (r // CHUNK) == (c // CHUNK)
    lower = r >= c
    causal = same & lower
    anti = same & (r < c)
    cum = jnp.where(causal, 1.0, 0.0).astype(BF16)
    lane = lax.broadcasted_iota(jnp.int32, (1, LANES), 1)

    chunks = [slice(ci * CHUNK, (ci + 1) * CHUNK) for ci in range(nc)]
    pairs = [slice((h // 2) * LANES, (h // 2 + 1) * LANES) for h in range(GLA_HEADS)]
    in_head = [(lane // GLA_DK) == (h % 2) for h in range(GLA_HEADS)]
    heads = range(GLA_HEADS)
    groups = [slice(g0, g0 + t) for g0 in range(0, qkvg_ref.shape[0], t)]

    pre = []
    for rg in groups:
        la = loga_ref[rg, :]
        la_hi = la.astype(BF16)
        la_r = la - la_hi.astype(F32)
        la_mid = la_r.astype(BF16)
        la_lo = (la_r - la_mid.astype(F32)).astype(BF16)
        b3 = _dot(cum, jnp.concatenate([la_hi, la_mid, la_lo], axis=1))
        b = b3[:, :GLA_K] + b3[:, GLA_K:2 * GLA_K] + b3[:, 2 * GLA_K:]
        b_last = jnp.concatenate(
            [jnp.broadcast_to(b[(ci + 1) * CHUNK - 1:(ci + 1) * CHUNK, :], (CHUNK, GLA_K)) for ci in range(nc)],
            axis=0)
        e_pos = jnp.exp(b)
        e_neg = jnp.exp(-b)
        q = qkvg_ref[rg, 0:GLA_K].astype(F32) * (GLA_DK ** -0.5)
        k = qkvg_ref[rg, GLA_K:2 * GLA_K].astype(F32)
        q_pos = (q * e_pos).astype(BF16)
        q_neg = (q * e_neg).astype(BF16)
        k_pos = (k * e_pos).astype(BF16)
        k_neg = (k * e_neg).astype(BF16)
        k_dec = (k * jnp.exp(b_last - b)).astype(BF16)

        vs = [qkvg_ref[rg, 2 * GLA_K + h * GLA_DV:2 * GLA_K + (h + 1) * GLA_DV] for h in heads]
        qps = [jnp.where(in_head[h], q_pos[:, pairs[h]], jnp.zeros((), BF16)) for h in heads]
        kvs = [[jnp.where(in_head[h], _dot_tn(vs[h][rows], k_dec[rows, pairs[h]]), 0.0) for rows in chunks]
               for h in heads]
        pre.append((e_pos, q_neg, k_pos, k_neg, vs, qps, kvs))

    sts = [state_ref[h] for h in heads]
    inters = []
    for e_pos, _, _, _, _, qps, kvs in pre:
        inter = [[] for _ in heads]
        for ci, rows in enumerate(chunks):
            for h in heads:
                inter[h].append(_dot_nt(qps[h][rows], sts[h].astype(BF16)))
                dec = e_pos[(ci + 1) * CHUNK - 1:(ci + 1) * CHUNK, pairs[h]]
                sts[h] = sts[h] * dec + kvs[h][ci]
        inters.append(inter)
    for h in heads:
        state_ref[h] = sts[h]

    attns = []
    for _, q_neg, k_pos, k_neg, _, qps, _ in pre:
        grp = []
        for h in heads:
            qn = jnp.where(in_head[h], q_neg[:, pairs[h]], jnp.zeros((), BF16))
            a_c = _dot_nt(qps[h], k_neg[:, pairs[h]])
            a_a = _dot_nt(qn, k_pos[:, pairs[h]])
            grp.append(jnp.where(causal, a_c, jnp.where(anti, a_a, 0.0)).astype(BF16))
        attns.append(grp)

    for rg, grp, inter, (_, _, _, _, vs, _, _) in zip(groups, attns, inters, pre):
        outs = [_dot(grp[h], vs[h]) + jnp.concatenate(inter[h], axis=0) for h in heads]
        for h in heads:
            gh = ghead_ref[:, h * GLA_DV:(h + 1) * GLA_DV]
            on = _rms(outs[h], gh)
            gsl = slice(2 * GLA_K + GLA_V + h * GLA_DV, 2 * GLA_K + GLA_V + (h + 1) * GLA_DV)
            gg = qkvg_ref[rg, gsl].astype(F32)
            o_ref[rg, h * GLA_DV:(h + 1) * GLA_DV] = (on * (gg * _sigmoid(gg))).astype(BF16)


def _gla(qkvg, loga, ghead, later_weights):
    s = qkvg.shape[0]
    t = GLA_TILE
    assert s % t == 0 and t % GLA_GROUP == 0
    steps = s // t
    w_specs = [pl.BlockSpec((w.shape[0] // steps, w.shape[1]), lambda i: (i, 0)) for w in later_weights]
    assert all(w.shape[0] % (steps * BF16_SUBLANES) == 0 for w in later_weights)
    outs = pl.pallas_call(
        _gla_kernel,
        grid=(steps,),
        in_specs=[pl.BlockSpec((t, qkvg.shape[1]), lambda i: (i, 0)),
                  pl.BlockSpec((t, GLA_K), lambda i: (i, 0)),
                  pl.BlockSpec((1, GLA_V), lambda i: (0, 0))] + w_specs,
        out_specs=[pl.BlockSpec((t, GLA_V), lambda i: (i, 0))] + w_specs,
        out_shape=[jax.ShapeDtypeStruct((s, GLA_V), BF16)]
        + [jax.ShapeDtypeStruct(w.shape, BF16) for w in later_weights],
        scratch_shapes=[pltpu.VMEM((GLA_HEADS, GLA_DV, LANES), F32)],
        compiler_params=pltpu.CompilerParams(dimension_semantics=("arbitrary",), vmem_limit_bytes=VMEM_LIMIT),
        name="gla",
    )(qkvg, loga, ghead, *later_weights)
    return outs[0], outs[1:]


def _fox_kernel(thr_ref, fmin_ref, shift_ref, qt_hbm, k_hbm, vt_hbm, o_ref,
                sa_ref, sb_ref, pa_ref, pb_ref, m_ref, acc_ref, si_ref, sj_ref,
                qt_ref, k_ref, vt_ref, dma_sem):
    p = pl.program_id(0)
    tq = FOX_TQ
    tk = FOX_TK
    nsub = tq // tk
    nq = k_ref.shape[0] // tq
    hrows = FOX_DH + FOX_ONES_ROWS
    row = lax.broadcasted_iota(jnp.int32, (2 * LANES, 1), 0)
    krow = lax.broadcasted_iota(jnp.int32, (tk, tq), 0)
    qcol = lax.broadcasted_iota(jnp.int32, (tk, tq), 1)
    m0 = jnp.full((1, tq), NEG_BIG, F32)
    acc0 = jnp.zeros((hrows, tq), F32)

    n_chunk = nq // FOX_STEPS
    chunk = FOX_STEPS * tq

    def chunk_copies(c):
        at = pl.ds(pl.multiple_of(c * chunk, chunk), chunk)
        return (pltpu.make_async_copy(qt_hbm.at[pl.ds(pl.multiple_of(p * 2 * LANES, 2 * LANES), 2 * LANES), at],
                                      qt_ref.at[:, at], dma_sem.at[0, c]),
                pltpu.make_async_copy(k_hbm.at[p, at, :], k_ref.at[at, :], dma_sem.at[1, c]),
                pltpu.make_async_copy(vt_hbm.at[pl.ds(pl.multiple_of(p * 2 * hrows, 2 * hrows), 2 * hrows), at],
                                      vt_ref.at[:, at], dma_sem.at[2, c]))

    def wait_chunk(c):
        for cp in chunk_copies(c):
            cp.wait()

    for c in range(n_chunk):
        for cp in chunk_copies(c):
            cp.start()

    for hi in range(2):
        h = 2 * p + hi
        aug0 = LANES + hi * FOX_AUG_PER_HEAD
        mine = ((row // FOX_DH) == hi) | ((row >= aug0) & (row < aug0 + FOX_AUG_PER_HEAD))
        vrows = slice(hi * hrows, (hi + 1) * hrows)

        def score_tiles(qi, kj, masked):
            q0 = pl.multiple_of(qi * tq, tq)
            qm = jnp.where(mine, qt_ref[:, pl.ds(q0, tq)], jnp.zeros((), BF16))
            for u in range(nsub):
                k0 = pl.multiple_of(kj * tq + u * tk, tk)
                s_t = _dot(k_ref[pl.ds(k0, tk), :], qm)
                if masked:
                    s_t = jnp.where(krow + u * tk <= qcol, s_t, NEG_BIG)
                yield u, s_t

        def values(kj, u):
            return vt_ref[vrows, pl.ds(pl.multiple_of(kj * tq + u * tk, tk), tk)]

        def fast_produce(qi, kj, buf, masked=False):
            shift = shift_ref[h, qi]
            for u, s_t in score_tiles(qi, kj, masked):
                buf[u] = jnp.exp(s_t - shift).astype(BF16)

        def fast_consume(i, kj, buf, fresh=False):
            acc = acc0 if fresh else acc_ref[hi, i]
            for u in range(nsub):
                acc = acc + _dot(values(kj, u), buf[u])
            acc_ref[hi, i] = acc

        def slow_produce(qi, kj, buf, masked=False):
            for u, s_t in score_tiles(qi, kj, masked):
                buf[u] = s_t

        def slow_consume(i, kj, buf, fresh=False):
            m = m0 if fresh else m_ref[hi, i]
            acc = acc0 if fresh else acc_ref[hi, i]
            for u in range(nsub):
                s_t = buf[u]
                m_new = jnp.maximum(m, jnp.max(s_t, axis=0, keepdims=True))
                alpha = jnp.exp(m - m_new)
                pt = jnp.exp(s_t - m_new).astype(BF16)
                acc = alpha * acc + _dot(values(kj, u), pt)
                m = m_new
            m_ref[hi, i] = m
            acc_ref[hi, i] = acc

        def list_block(i, t):
            thr = thr_ref[h, i]
            for j in range(nq):
                si_ref[t] = i
                sj_ref[t] = j
                t = t + jnp.where((j < i) & (fmin_ref[h, j] <= thr), 1, 0)
            return t

        worst_shift = lax.fori_loop(0, nq, lambda i, w: jnp.maximum(w, shift_ref[h, i]), jnp.float32(0.0))

        def run_head(produce, consume, buf_a, buf_b):
            def diag_prefetch(i, buf):
                ic = jnp.minimum(i, nq - 1)
                produce(ic, ic, buf, masked=True)

            if hi == 0:
                wait_chunk(0)
            diag_prefetch(0, buf_a)

            def diag_step(tn, n):
                i = FOX_STEPS * tn
                if hi == 0:
                    @pl.when(tn + 1 < n_chunk)
                    def _():
                        wait_chunk(tn + 1)

                for d in range(0, FOX_STEPS, 2):
                    diag_prefetch(i + d + 1, buf_b)
                    consume(i + d, i + d, buf_a, fresh=True)
                    diag_prefetch(i + d + 2, buf_a)
                    consume(i + d + 1, i + d + 1, buf_b, fresh=True)
                for d in range(FOX_STEPS):
                    n = list_block(i + d, n)
                return n

            n_tiles = lax.fori_loop(0, nq // FOX_STEPS, diag_step, 0)
            for pad in range(FOX_STEPS):
                si_ref[n_tiles + pad] = nq
                sj_ref[n_tiles + pad] = 0
            m_ref[hi, nq] = m0
            acc_ref[hi, nq] = acc0

            def prefetch(t, buf):
                produce(jnp.minimum(si_ref[t], nq - 1), sj_ref[t], buf)

            prefetch(0, buf_a)

            def multi_step(tn, carry):
                t = FOX_STEPS * tn
                for d in range(0, FOX_STEPS, 2):
                    prefetch(t + d + 1, buf_b)
                    consume(si_ref[t + d], sj_ref[t + d], buf_a)
                    prefetch(t + d + 2, buf_a)
                    consume(si_ref[t + d + 1], sj_ref[t + d + 1], buf_b)
                return carry

            lax.fori_loop(0, (n_tiles + FOX_STEPS - 1) // FOX_STEPS, multi_step, 0)

        no_running_max = 2.0 * worst_shift <= FOX_MAX_SHIFT_GAP

        @pl.when(no_running_max)
        def _():
            run_head(fast_produce, fast_consume, pa_ref, pb_ref)

        @pl.when(jnp.logical_not(no_running_max))
        def _():
            run_head(slow_produce, slow_consume, sa_ref, sb_ref)

    def finish(i, carry):
        a0 = acc_ref[0, i]
        a1 = acc_ref[1, i]
        ot = jnp.concatenate([a0[:FOX_DH] / a0[FOX_DH:FOX_DH + 1], a1[:FOX_DH] / a1[FOX_DH:FOX_DH + 1]], axis=0)
        o_ref[pl.ds(pl.multiple_of(i * tq, tq), tq), :] = ot.T.astype(BF16)
        return carry

    lax.fori_loop(0, nq, finish, 0)


def _fox(thr, fmin_blk, shift, qt, kaug, vt):
    s = kaug.shape[1]
    tq = FOX_TQ
    nq = s // tq
    npair = FOX_HEADS // 2
    prow = 2 * (FOX_DH + FOX_ONES_ROWS)
    assert nq % FOX_STEPS == 0 and FOX_STEPS % 2 == 0
    hrows = FOX_DH + FOX_ONES_ROWS
    max_tiles = nq * (nq - 1) // 2 + FOX_STEPS
    hbm = pl.BlockSpec(memory_space=pl.ANY)
    grid_spec = pltpu.PrefetchScalarGridSpec(
        num_scalar_prefetch=3,
        grid=(npair,),
        in_specs=[hbm, hbm, hbm],
        out_specs=pl.BlockSpec((None, s, LANES), lambda p, *_: (p, 0, 0)),
        scratch_shapes=[pltpu.VMEM((tq // FOX_TK, FOX_TK, tq), F32), pltpu.VMEM((tq // FOX_TK, FOX_TK, tq), F32),
                        pltpu.VMEM((tq // FOX_TK, FOX_TK, tq), BF16), pltpu.VMEM((tq // FOX_TK, FOX_TK, tq), BF16),
                        pltpu.VMEM((2, nq + 1, 1, tq), F32), pltpu.VMEM((2, nq + 1, hrows, tq), F32),
                        pltpu.SMEM((max_tiles,), jnp.int32), pltpu.SMEM((max_tiles,), jnp.int32),
                        pltpu.VMEM((2 * LANES, s), BF16), pltpu.VMEM((s, 2 * LANES), BF16),
                        pltpu.VMEM((prow, s), BF16), pltpu.SemaphoreType.DMA((3, nq // FOX_STEPS))],
    )
    return pl.pallas_call(
        _fox_kernel,
        grid_spec=grid_spec,
        out_shape=jax.ShapeDtypeStruct((npair, s, LANES), BF16),
        compiler_params=pltpu.CompilerParams(dimension_semantics=("arbitrary",), vmem_limit_bytes=VMEM_LIMIT),
        name="fox",
    )(thr, fmin_blk, shift, qt, kaug, vt)


def _tail_kernel(x_ref, ogla_ref, ofox_ref, omem_ref, gate_ref, wg_ref, wf_ref, wm_ref, wo_ref,
                 gffn_ref, w1_ref, w2_ref, gfin_ref, out_ref):
    groups = [slice(r0, r0 + TAIL_GROUP) for r0 in range(0, x_ref.shape[0], TAIL_GROUP)]
    merged = []
    for rg in groups:
        ofox = jnp.concatenate([ofox_ref[p, rg, :] for p in range(FOX_HEADS // 2)], axis=1)
        merged.append((gate_ref[rg, 0:D_MODEL].astype(F32) * _dot(ogla_ref[rg, :], wg_ref[...])
                       + gate_ref[rg, D_MODEL:2 * D_MODEL].astype(F32) * _dot(ofox, wf_ref[...])
                       + gate_ref[rg, 2 * D_MODEL:3 * D_MODEL].astype(F32) * _dot(omem_ref[rg, :], wm_ref[...])
                       ).astype(BF16))
    hs = [x_ref[rg, :] + _dot(mg, wo_ref[...]) for rg, mg in zip(groups, merged)]
    u2s = [_rms(h, gffn_ref[...]).astype(BF16) for h in hs]
    accs = [jnp.zeros_like(h) for h in hs]
    for cidx in range(D_FF // FF_CHUNK):
        cs = slice(cidx * FF_CHUNK, (cidx + 1) * FF_CHUNK)
        for g, u2 in enumerate(u2s):
            a = jnp.maximum(_dot(u2, w1_ref[:, cs]), 0.0)
            accs[g] = accs[g] + _dot((a * a).astype(BF16), w2_ref[cs, :])
    for rg, h, acc in zip(groups, hs, accs):
        out_ref[rg, :] = _rms(h + acc, gfin_ref[...])


def _tail(x, ogla, ofox, omem, gate, wg, wf, wm, wo, gffn, w1, w2, gfin):
    s = x.shape[0]
    tm = TAIL_TILE
    full = lambda a: pl.BlockSpec(a.shape, lambda i: (0,) * a.ndim, pipeline_mode=pl.Buffered(1))
    row = lambda w: pl.BlockSpec((tm, w), lambda i: (i, 0))
    return pl.pallas_call(
        _tail_kernel,
        grid=(s // tm,),
        in_specs=[row(D_MODEL), row(GLA_V), pl.BlockSpec((FOX_HEADS // 2, tm, LANES), lambda i: (0, i, 0)),
                  row(MEM_W), row(3 * D_MODEL),
                  full(wg), full(wf), full(wm), full(wo), full(gffn), full(w1), full(w2), full(gfin)],
        out_specs=row(D_MODEL),
        out_shape=jax.ShapeDtypeStruct((s, D_MODEL), F32),
        compiler_params=pltpu.CompilerParams(dimension_semantics=("arbitrary",), vmem_limit_bytes=VMEM_LIMIT),
        name="tail",
    )(x, ogla, ofox, omem, gate, wg, wf, wm, wo, gffn, w1, w2, gfin)


def kernel(x, mem, g_mix, w_in, w_alpha_up, b_alpha, b_forget, g_gla_head, g_mem, w_mem_kv,
           w_gla_o, w_fox_o, w_mem_o, w_out, g_ffn, w_ff1, w_ff2, g_final):
    assert x.shape[0] == 1 and g_mix.shape[0] == 1, "single batch, single layer"
    s = x.shape[1]
    assert s % ROW_TILE == 0 and s % TAIL_TILE == 0 and s % FOX_TQ == 0
    assert FOX_TQ % FOX_TK == 0 and FOX_TQ % PROJ_GROUP == 0 and ROW_TILE % PROJ_GROUP == 0
    xs = x[0]
    w = w_in[0]

    w_gla = w[:, :OFF_GA].astype(BF16)
    w_fk = w[:, OFF_FK:OFF_FV].astype(BF16)
    w_qvt = jnp.concatenate([w[:, OFF_FQ:OFF_FK], w[:, OFF_FV:OFF_FF]], axis=1).astype(BF16)
    w_mq = w[:, OFF_MQ:OFF_GATE].astype(BF16)
    w_gate = w[:, OFF_GATE:].astype(BF16)
    w_small = jnp.concatenate(
        [w[:, OFF_GA:OFF_FQ], w[:, OFF_FF:OFF_MQ], jnp.zeros((D_MODEL, LANES - GLA_LOWRANK - FOX_HEADS), F32)], axis=1)
    w_small_hi = w_small.astype(BF16)
    w_small = jnp.concatenate([w_small_hi, (w_small - w_small_hi.astype(F32)).astype(BF16)], axis=1)
    w_up_hi = w_alpha_up[0].astype(BF16)
    w_up = jnp.concatenate([w_up_hi, (w_alpha_up[0] - w_up_hi.astype(F32)).astype(BF16)], axis=1)
    b_f = jnp.zeros((1, LANES), F32).at[0, GLA_LOWRANK:GLA_LOWRANK + FOX_HEADS].set(b_forget[0])

    mk, mv = _memkv(mem[0], g_mem, w_mem_kv[0].astype(BF16))
    gla_qkvg, loga, qt, kaug, vt, omem, gate, stats, statq = _proj(
        xs, g_mix, w_gla, w_fk, w_qvt, w_mq, w_gate, w_small, w_up, b_alpha, b_f, mk, mv)

    ogla, (wg, wf, wm, wo, w1, w2) = _gla(
        gla_qkvg, loga, g_gla_head.reshape(1, GLA_V),
        [w_gla_o[0], w_fox_o[0], w_mem_o[0], w_out[0], w_ff1[0], w_ff2[0]])

    per_blk = lambda a: a.reshape(s // FOX_TQ, FOX_TQ // PROJ_GROUP, FOX_HEADS)
    qn = jnp.sqrt(jnp.max(per_blk(statq[:, :, 0]), axis=1))
    kn = jnp.sqrt(jnp.max(stats[:, 0, :FOX_HEADS], axis=0))
    fmax = jnp.max(per_blk(stats[:, 1, :FOX_HEADS]), axis=1)
    fmin = jnp.min(per_blk(stats[:, 2, :FOX_HEADS]), axis=1)
    shift = (1.02 * (FOX_DH ** -0.5)) * qn * kn[None, :]
    thr = 2.0 * shift + fmax + PRUNE_LOGIT_GAP
    ofox = _fox(thr.T, fmin.T, shift.T, qt, kaug, vt)

    out = _tail(xs, ogla, ofox, omem, gate, wg, wf, wm, wo, g_ffn, w1, w2, g_final.reshape(1, D_MODEL))
    return out[None]
```

```python
import jax
import jax.numpy as jnp
import numpy as np
from jax import lax
from jax.experimental import pallas as pl
from jax.experimental.pallas import tpu as pltpu

D_MODEL = 1024
CHUNK = 64
EPS = 1e-6
GLA_HEADS = 4
GLA_DK = 64
GLA_DV = 128
GLA_LOWRANK = 16
GLA_TAU = 16.0
FOX_HEADS = 8
FOX_DH = 64
MEM_HEADS = 4
MEM_DH = 128
D_FF = 4 * D_MODEL
GLA_K = GLA_HEADS * GLA_DK
GLA_V = GLA_HEADS * GLA_DV
FOX_W = FOX_HEADS * FOX_DH
MEM_W = MEM_HEADS * MEM_DH
OFF_GA = 2 * GLA_K + 2 * GLA_V
OFF_FQ = OFF_GA + GLA_LOWRANK
OFF_FK = OFF_FQ + FOX_W
OFF_FV = OFF_FK + FOX_W
OFF_FF = OFF_FV + FOX_W
OFF_MQ = OFF_FF + FOX_HEADS
OFF_GATE = OFF_MQ + MEM_W

LANES = 128
BF16_SUBLANES = 16
FOX_ONES_ROWS = 16
FOX_AUG_PER_HEAD = 6
WSPLIT_ROWS = 128
ROW_TILE = 512
PROJ_GROUP = 256
GLA_TILE = 512
GLA_GROUP = 256
FOX_TQ = 512
FOX_TK = 256
FOX_STEPS = 8
TAIL_TILE = 512
TAIL_GROUP = 256
FF_CHUNK = 1024
VMEM_LIMIT = 56 * 1024 * 1024
NEG_BIG = -1e30
PRUNE_LOGIT_GAP = 104.0
FOX_MAX_SHIFT_GAP = 50.0

F32 = jnp.float32
BF16 = jnp.bfloat16


def _rms(xf, g):
    r = lax.rsqrt(jnp.mean(xf * xf, axis=-1, keepdims=True) + EPS)
    return (xf * r) * g


def _log_sigmoid(x):
    return jnp.minimum(x, 0.0) - jnp.log1p(jnp.exp(-jnp.abs(x)))


def _sigmoid(x):
    return 1.0 / (1.0 + jnp.exp(-x))


def _dot(a, b):
    return jnp.dot(a, b, preferred_element_type=F32)


def _dot_nt(a, b):
    return lax.dot_general(a, b, (((1,), (1,)), ((), ())), preferred_element_type=F32)


def _dot_tn(a, b):
    return lax.dot_general(a, b, (((0,), (0,)), ((), ())), preferred_element_type=F32)


def _memkv_kernel(mem_ref, g_ref, w_ref, mk_ref, mv_ref):
    mn = _rms(mem_ref[...], g_ref[...]).astype(BF16)
    kv = _dot(mn, w_ref[...])
    mk_ref[...] = kv[:, :MEM_W].astype(BF16)
    mv_ref[...] = kv[:, MEM_W:].astype(BF16)


def _memkv(mem, g_mem, w_mem_kv):
    m = mem.shape[0]
    return pl.pallas_call(
        _memkv_kernel,
        out_shape=(jax.ShapeDtypeStruct((m, MEM_W), BF16), jax.ShapeDtypeStruct((m, MEM_W), BF16)),
        name="memkv",
    )(mem, g_mem, w_mem_kv)


def _wsplit_kernel(w_ref, gla_ref, fk_ref, qv_ref, mq_ref, gate_ref, small_ref):
    rb = w_ref.shape[0]
    gla_ref[...] = w_ref[:, :OFF_GA].astype(BF16)
    fk_ref[...] = w_ref[:, OFF_FK:OFF_FV].astype(BF16)
    qv_ref[:, :FOX_W] = w_ref[:, OFF_FQ:OFF_FK].astype(BF16)
    qv_ref[:, FOX_W:] = w_ref[:, OFF_FV:OFF_FF].astype(BF16)
    mq_ref[...] = w_ref[:, OFF_MQ:OFF_GATE].astype(BF16)
    gate_ref[...] = w_ref[:, OFF_GATE:].astype(BF16)
    win_ga = w_ref[:, OFF_GA:OFF_GA + LANES]
    win_ff = w_ref[:, OFF_FF - GLA_LOWRANK:OFF_FF - GLA_LOWRANK + LANES]
    lane = lax.broadcasted_iota(jnp.int32, (rb, LANES), 1)
    small = jnp.where(lane < GLA_LOWRANK, win_ga, jnp.where(lane < GLA_LOWRANK + FOX_HEADS, win_ff, 0.0))
    small_hi = small.astype(BF16)
    small_ref[:, :LANES] = small_hi
    small_ref[:, LANES:] = (small - small_hi.astype(F32)).astype(BF16)


def _wsplit(w):
    assert OFF_GA % LANES == 0 and OFF_FF % LANES == GLA_LOWRANK and w.shape[0] % WSPLIT_ROWS == 0
    rb = WSPLIT_ROWS
    widths = (OFF_GA, FOX_W, 2 * FOX_W, MEM_W, 3 * D_MODEL, 2 * LANES)
    return pl.pallas_call(
        _wsplit_kernel,
        grid=(w.shape[0] // rb,),
        in_specs=[pl.BlockSpec((rb, w.shape[1]), lambda i: (i, 0))],
        out_specs=[pl.BlockSpec((rb, n), lambda i: (i, 0)) for n in widths],
        out_shape=[jax.ShapeDtypeStruct((w.shape[0], n), BF16) for n in widths],
        compiler_params=pltpu.CompilerParams(dimension_semantics=("arbitrary",), vmem_limit_bytes=VMEM_LIMIT),
        name="wsplit",
    )(w)


def _proj_kernel(x_ref, gmix_ref, wgla_ref, wfk_ref, wqv_ref, wmq_ref, wgate_ref, wsmall_ref, wup_ref,
                 balpha_ref, bforget_ref, mk_ref, mv_ref, pqt_ref, pk_ref,
                 gla_ref, loga_ref, qt_ref, kaug_ref, vt_ref, omem_ref, gate_ref, stat_ref, statq_ref,
                 carry_ref, wqvt_ref):
    tm = x_ref.shape[0]

    @pl.when(pl.program_id(0) == 0)
    def _():
        carry_ref[...] = jnp.zeros_like(carry_ref)
        wqvt_ref[...] = wqv_ref[...].T

    tg = PROJ_GROUP
    groups = [slice(r0, r0 + tg) for r0 in range(0, tm, tg)]
    gid = range(len(groups))
    scale = MEM_DH ** -0.5
    heads = [slice(h * MEM_DH, (h + 1) * MEM_DH) for h in range(MEM_HEADS)]

    ubs, smalls, scores = [], [], []
    for rg in groups:
        u = _rms(x_ref[rg, :], gmix_ref[...])
        ub = u.astype(BF16)
        u_lo = (u - ub.astype(F32)).astype(BF16)
        parts = _dot(jnp.concatenate([ub, u_lo], axis=0), wsmall_ref[...])
        smalls.append((parts[:tg, :LANES] + parts[:tg, LANES:]) + (parts[tg:, :LANES] + parts[tg:, LANES:]))
        mq = _dot(ub, wmq_ref[...]).astype(BF16)
        scores.append([_dot_nt(mq[:, sl], mk_ref[:, sl]) * scale for sl in heads])
        ubs.append(ub)

    fkbs = []
    for rg, ub in zip(groups, ubs):
        gla_ref[rg, :] = _dot(ub, wgla_ref[...]).astype(BF16)
        fkbs.append(_dot(ub, wfk_ref[...]).astype(BF16))

    lane = lax.broadcasted_iota(jnp.int32, (tg, LANES), 1)
    ff_valid = (lane >= GLA_LOWRANK) & (lane < GLA_LOWRANK + FOX_HEADS)
    r = lax.broadcasted_iota(jnp.int32, (tg, tg), 0)
    c = lax.broadcasted_iota(jnp.int32, (tg, tg), 1)
    tri = jnp.where(r >= c, 1.0, 0.0).astype(BF16)

    def aug_slot(idx):
        a = idx % LANES
        return a % FOX_AUG_PER_HEAD, a < 2 * FOX_AUG_PER_HEAD

    slot_q, in_q = aug_slot(lax.broadcasted_iota(jnp.int32, (FOX_W, 1), 0))
    slot_k, in_k = aug_slot(lax.broadcasted_iota(jnp.int32, (1, FOX_W), 1))
    carry = carry_ref[...]
    fcums, augqts, augks = [], [], []
    for rg, small in zip(groups, smalls):
        ga = small[:, :GLA_LOWRANK]
        ga_hi = ga.astype(BF16)
        ga_lo = (ga - ga_hi.astype(F32)).astype(BF16)
        up = _dot(jnp.concatenate([ga_hi, ga_lo], axis=0), wup_ref[...])
        alpha_pre = ((up[:tg, :GLA_K] + up[:tg, GLA_K:]) + (up[tg:, :GLA_K] + up[tg:, GLA_K:])) + balpha_ref[...]
        loga_ref[rg, :] = _log_sigmoid(alpha_pre) * (1.0 / GLA_TAU)

        logf = jnp.where(ff_valid, _log_sigmoid(small + bforget_ref[...]), 0.0)
        logf = pltpu.roll(logf, LANES - GLA_LOWRANK, 1)
        lf_hi = logf.astype(BF16)
        lf_r = logf - lf_hi.astype(F32)
        lf_mid = lf_r.astype(BF16)
        lf_lo = (lf_r - lf_mid.astype(F32)).astype(BF16)
        c3 = _dot(tri, jnp.concatenate([lf_hi, lf_mid, lf_lo], axis=1))
        fcum = (c3[:, :LANES] + c3[:, LANES:2 * LANES] + c3[:, 2 * LANES:]) + carry
        carry = fcum[tg - 1:tg, :]

        f_hi = fcum.astype(BF16)
        rem = fcum - f_hi.astype(F32)
        f_mid = rem.astype(BF16)
        f_lo = (rem - f_mid.astype(F32)).astype(BF16)
        f3 = jnp.concatenate([f_hi, f_mid, f_lo], axis=1)
        augqts.append(_dot_nt(pqt_ref[...], f3) + jnp.where(in_q & (slot_q < 3), -1.0, 0.0))
        augks.append(_dot(f3, pk_ref[...]) + jnp.where(in_k & (slot_k >= 3), 1.0, 0.0))
        fcums.append(fcum)
    carry_ref[...] = carry

    for rg, ub in zip(groups, ubs):
        gate_ref[rg, :] = _sigmoid(_dot(ub, wgate_ref[...])).astype(BF16)

    for rg, sc in zip(groups, scores):
        probs = []
        for s in sc:
            m = jnp.max(s, axis=-1, keepdims=True)
            e = jnp.exp(s - m)
            probs.append((e / jnp.sum(e, axis=-1, keepdims=True)).astype(BF16))
        omem_ref[rg, :] = jnp.concatenate([_dot(pr, mv_ref[:, sl]) for pr, sl in zip(probs, heads)],
                                          axis=-1).astype(BF16)

    ones = jnp.ones((FOX_ONES_ROWS, tg), BF16)
    gi = lax.broadcasted_iota(jnp.int32, (FOX_W, LANES), 0) // FOX_DH
    gj = lax.broadcasted_iota(jnp.int32, (FOX_W, LANES), 1)
    group = jnp.where(gi == gj, 1.0, 0.0).astype(BF16)
    for g, rg, ub, fkb, fcum, augqt, augk in zip(gid, groups, ubs, fkbs, fcums, augqts, augks):
        qvt = _dot_nt(wqvt_ref[...], ub)
        fqt = (qvt[:FOX_W] * (FOX_DH ** -0.5)).astype(BF16)
        vt = qvt[FOX_W:].astype(BF16)
        vt_parts = []
        for h in range(FOX_HEADS):
            vt_parts += [vt[h * FOX_DH:(h + 1) * FOX_DH, :], ones]
        vt_ref[:, rg] = jnp.concatenate(vt_parts, axis=0)

        for p in range(FOX_HEADS // 2):
            src = slice(p * LANES, (p + 1) * LANES)
            qt_ref[2 * p * LANES:(2 * p + 1) * LANES, rg] = fqt[src]
            qt_ref[(2 * p + 1) * LANES:(2 * p + 2) * LANES, rg] = augqt[src].astype(BF16)
            kaug_ref[p, rg, :LANES] = fkb[:, src]
            kaug_ref[p, rg, LANES:] = augk[:, src].astype(BF16)

        fq32 = fqt.astype(F32) * (FOX_DH ** 0.5)
        nq2 = jnp.sum((fq32 * fq32).reshape(FOX_HEADS, FOX_DH, tg), axis=1)
        statq_ref[g] = jnp.broadcast_to(jnp.max(nq2, axis=1, keepdims=True), (FOX_HEADS, LANES))
        fk = fkb.astype(F32)
        nk2 = jnp.max(_dot((fk * fk).astype(BF16), group), axis=0, keepdims=True)
        fmax = jnp.max(fcum, axis=0, keepdims=True)
        fmin = jnp.min(fcum, axis=0, keepdims=True)
        stat_ref[g] = jnp.concatenate([nk2, fmax, fmin, jnp.zeros((5, LANES), F32)], axis=0)


def _aug_placement():
    pq = np.zeros((3 * LANES, FOX_W), np.float32)
    pk = np.zeros((3 * LANES, FOX_W), np.float32)
    for h in range(FOX_HEADS):
        base = (h // 2) * LANES + (h % 2) * FOX_AUG_PER_HEAD
        for c in range(3):
            pq[c * LANES + h, base + 3 + c] = 1.0
            pk[c * LANES + h, base + c] = 1.0
    return jnp.asarray(pq.T, BF16), jnp.asarray(pk, BF16)


def _proj(x, g_mix, w_gla, w_fk, w_qvt, w_mq, w_gate, w_small, w_up, b_alpha, b_forget, mk, mv):
    s = x.shape[0]
    tm = ROW_TILE
    nt = s // tm
    pqt, pk = _aug_placement()
    full = lambda shape: pl.BlockSpec(shape, lambda i: (0,) * len(shape), pipeline_mode=pl.Buffered(1))
    row = lambda w: pl.BlockSpec((tm, w), lambda i: (i, 0))
    col = lambda r: pl.BlockSpec((r, tm), lambda i: (0, i))
    ng = tm // PROJ_GROUP
    stat = pl.BlockSpec((ng, 8, LANES), lambda i: (i, 0, 0))
    vt_rows = FOX_HEADS * (FOX_DH + FOX_ONES_ROWS)
    return pl.pallas_call(
        _proj_kernel,
        grid=(nt,),
        in_specs=[row(D_MODEL), full((1, D_MODEL)), full(w_gla.shape), full(w_fk.shape), full(w_qvt.shape),
                  full(w_mq.shape), full(w_gate.shape), full(w_small.shape), full(w_up.shape),
                  full(b_alpha.shape), full(b_forget.shape), full(mk.shape), full(mv.shape),
                  full(pqt.shape), full(pk.shape)],
        out_specs=[row(w_gla.shape[1]), row(GLA_K), col(2 * FOX_W),
                   pl.BlockSpec((FOX_HEADS // 2, tm, 2 * LANES), lambda i: (0, i, 0)), col(vt_rows), row(MEM_W),
                   row(w_gate.shape[1]), stat, stat],
        out_shape=[jax.ShapeDtypeStruct((s, w_gla.shape[1]), BF16),
                   jax.ShapeDtypeStruct((s, GLA_K), F32),
                   jax.ShapeDtypeStruct((2 * FOX_W, s), BF16),
                   jax.ShapeDtypeStruct((FOX_HEADS // 2, s, 2 * LANES), BF16),
                   jax.ShapeDtypeStruct((vt_rows, s), BF16),
                   jax.ShapeDtypeStruct((s, MEM_W), BF16),
                   jax.ShapeDtypeStruct((s, w_gate.shape[1]), BF16),
                   jax.ShapeDtypeStruct((nt * ng, 8, LANES), F32),
                   jax.ShapeDtypeStruct((nt * ng, 8, LANES), F32)],
        scratch_shapes=[pltpu.VMEM((1, LANES), F32), pltpu.VMEM((w_qvt.shape[1], w_qvt.shape[0]), BF16)],
        compiler_params=pltpu.CompilerParams(dimension_semantics=("arbitrary",), vmem_limit_bytes=VMEM_LIMIT),
        name="proj",
    )(x, g_mix, w_gla, w_fk, w_qvt, w_mq, w_gate, w_small, w_up, b_alpha, b_forget, mk, mv, pqt, pk)


def _gla_kernel(qkvg_ref, loga_ref, ghead_ref, *rest):
    n_w = (len(rest) - 2) // 2
    w_in_refs, o_ref, w_out_refs, state_ref = rest[:n_w], rest[n_w], rest[n_w + 1:2 * n_w + 1], rest[-1]
    t = GLA_GROUP
    nc = t // CHUNK

    @pl.when(pl.program_id(0) == 0)
    def _():
        state_ref[...] = jnp.zeros_like(state_ref)

    for w_src, w_dst in zip(w_in_refs, w_out_refs):
        w_dst[...] = w_src[...].astype(BF16)

    r = lax.broadcasted_iota(jnp.int32, (t, t), 0)
    c = lax.broadcasted_iota(jnp.int32, (t, t), 1)
    same = (r // CHUNK) == (c // CHUNK)
    lower = r >= c
    causal = same & lower
    anti = same & (r < c)
    cum = jnp.where(causal, 1.0, 0.0).astype(BF16)
    lane = lax.broadcasted_iota(jnp.int32, (1, LANES), 1)

    chunks = [slice(ci * CHUNK, (ci + 1) * CHUNK) for ci in range(nc)]
    pairs = [slice((h // 2) * LANES, (h // 2 + 1) * LANES) for h in range(GLA_HEADS)]
    in_head = [(lane // GLA_DK) == (h % 2) for h in range(GLA_HEADS)]
    heads = range(GLA_HEADS)
    groups = [slice(g0, g0 + t) for g0 in range(0, qkvg_ref.shape[0], t)]

    pre = []
    for rg in groups:
        la = loga_ref[rg, :]
        la_hi = la.astype(BF16)
        la_r = la - la_hi.astype(F32)
        la_mid = la_r.astype(BF16)
        la_lo = (la_r - la_mid.astype(F32)).astype(BF16)
        b3 = _dot(cum, jnp.concatenate([la_hi, la_mid, la_lo], axis=1))
        b = b3[:, :GLA_K] + b3[:, GLA_K:2 * GLA_K] + b3[:, 2 * GLA_K:]
        b_last = jnp.concatenate(
            [jnp.broadcast_to(b[(ci + 1) * CHUNK - 1:(ci + 1) * CHUNK, :], (CHUNK, GLA_K)) for ci in range(nc)],
            axis=0)
        e_pos = jnp.exp(b)
        e_neg = jnp.exp(-b)
        q = qkvg_ref[rg, 0:GLA_K].astype(F32) * (GLA_DK ** -0.5)
        k = qkvg_ref[rg, GLA_K:2 * GLA_K].astype(F32)
        q_pos = (q * e_pos).astype(BF16)
        q_neg = (q * e_neg).astype(BF16)
        k_pos = (k * e_pos).astype(BF16)
        k_neg = (k * e_neg).astype(BF16)
        k_dec = (k * jnp.exp(b_last - b)).astype(BF16)

        vs = [qkvg_ref[rg, 2 * GLA_K + h * GLA_DV:2 * GLA_K + (h + 1) * GLA_DV] for h in heads]
        qps = [jnp.where(in_head[h], q_pos[:, pairs[h]], jnp.zeros((), BF16)) for h in heads]
        kvs = [[jnp.where(in_head[h], _dot_tn(vs[h][rows], k_dec[rows, pairs[h]]), 0.0) for rows in chunks]
               for h in heads]
        pre.append((e_pos, q_neg, k_pos, k_neg, vs, qps, kvs))

    sts = [state_ref[h] for h in heads]
    inters = []
    for e_pos, _, _, _, _, qps, kvs in pre:
        inter = [[] for _ in heads]
        for ci, rows in enumerate(chunks):
            for h in heads:
                inter[h].append(_dot_nt(qps[h][rows], sts[h].astype(BF16)))
                dec = e_pos[(ci + 1) * CHUNK - 1:(ci + 1) * CHUNK, pairs[h]]
                sts[h] = sts[h] * dec + kvs[h][ci]
        inters.append(inter)
    for h in heads:
        state_ref[h] = sts[h]

    attns = []
    for _, q_neg, k_pos, k_neg, _, qps, _ in pre:
        grp = []
        for h in heads:
            qn = jnp.where(in_head[h], q_neg[:, pairs[h]], jnp.zeros((), BF16))
            a_c = _dot_nt(qps[h], k_neg[:, pairs[h]])
            a_a = _dot_nt(qn, k_pos[:, pairs[h]])
            grp.append(jnp.where(causal, a_c, jnp.where(anti, a_a, 0.0)).astype(BF16))
        attns.append(grp)

    for rg, grp, inter, (_, _, _, _, vs, _, _) in zip(groups, attns, inters, pre):
        outs = [_dot(grp[h], vs[h]) + jnp.concatenate(inter[h], axis=0) for h in heads]
        for h in heads:
            gh = ghead_ref[:, h * GLA_DV:(h + 1) * GLA_DV]
            on = _rms(outs[h], gh)
            gsl = slice(2 * GLA_K + GLA_V + h * GLA_DV, 2 * GLA_K + GLA_V + (h + 1) * GLA_DV)
            gg = qkvg_ref[rg, gsl].astype(F32)
            o_ref[rg, h * GLA_DV:(h + 1) * GLA_DV] = (on * (gg * _sigmoid(gg))).astype(BF16)


def _gla(qkvg, loga, ghead, later_weights):
    s = qkvg.shape[0]
    t = GLA_TILE
    assert s % t == 0 and t % GLA_GROUP == 0
    steps = s // t
    w_specs = [pl.BlockSpec((w.shape[0] // steps, w.shape[1]), lambda i: (i, 0)) for w in later_weights]
    assert all(w.shape[0] % (steps * BF16_SUBLANES) == 0 for w in later_weights)
    outs = pl.pallas_call(
        _gla_kernel,
        grid=(steps,),
        in_specs=[pl.BlockSpec((t, qkvg.shape[1]), lambda i: (i, 0)),
                  pl.BlockSpec((t, GLA_K), lambda i: (i, 0)),
                  pl.BlockSpec((1, GLA_V), lambda i: (0, 0))] + w_specs,
        out_specs=[pl.BlockSpec((t, GLA_V), lambda i: (i, 0))] + w_specs,
        out_shape=[jax.ShapeDtypeStruct((s, GLA_V), BF16)]
        + [jax.ShapeDtypeStruct(w.shape, BF16) for w in later_weights],
        scratch_shapes=[pltpu.VMEM((GLA_HEADS, GLA_DV, LANES), F32)],
        compiler_params=pltpu.CompilerParams(dimension_semantics=("arbitrary",), vmem_limit_bytes=VMEM_LIMIT),
        name="gla",
    )(qkvg, loga, ghead, *later_weights)
    return outs[0], outs[1:]


def _fox_kernel(thr_ref, fmin_ref, shift_ref, qt_hbm, k_hbm, vt_hbm, o_ref,
                sa_ref, sb_ref, pa_ref, pb_ref, m_ref, acc_ref, si_ref, sj_ref,
                qt_ref, k_ref, vt_ref, dma_sem):
    p = pl.program_id(0)
    tq = FOX_TQ
    tk = FOX_TK
    nsub = tq // tk
    nq = k_ref.shape[0] // tq
    hrows = FOX_DH + FOX_ONES_ROWS
    row = lax.broadcasted_iota(jnp.int32, (2 * LANES, 1), 0)
    krow = lax.broadcasted_iota(jnp.int32, (tk, tq), 0)
    qcol = lax.broadcasted_iota(jnp.int32, (tk, tq), 1)
    m0 = jnp.full((1, tq), NEG_BIG, F32)
    acc0 = jnp.zeros((hrows, tq), F32)

    n_chunk = nq // FOX_STEPS
    chunk = FOX_STEPS * tq

    def chunk_copies(c):
        at = pl.ds(pl.multiple_of(c * chunk, chunk), chunk)
        return (pltpu.make_async_copy(qt_hbm.at[pl.ds(pl.multiple_of(p * 2 * LANES, 2 * LANES), 2 * LANES), at],
                                      qt_ref.at[:, at], dma_sem.at[0, c]),
                pltpu.make_async_copy(k_hbm.at[p, at, :], k_ref.at[at, :], dma_sem.at[1, c]),
                pltpu.make_async_copy(vt_hbm.at[pl.ds(pl.multiple_of(p * 2 * hrows, 2 * hrows), 2 * hrows), at],
                                      vt_ref.at[:, at], dma_sem.at[2, c]))

    def wait_chunk(c):
        for cp in chunk_copies(c):
            cp.wait()

    for c in range(n_chunk):
        for cp in chunk_copies(c):
            cp.start()

    for hi in range(2):
        h = 2 * p + hi
        aug0 = LANES + hi * FOX_AUG_PER_HEAD
        mine = ((row // FOX_DH) == hi) | ((row >= aug0) & (row < aug0 + FOX_AUG_PER_HEAD))
        vrows = slice(hi * hrows, (hi + 1) * hrows)

        def score_tiles(qi, kj, masked):
            q0 = pl.multiple_of(qi * tq, tq)
            qm = jnp.where(mine, qt_ref[:, pl.ds(q0, tq)], jnp.zeros((), BF16))
            for u in range(nsub):
                k0 = pl.multiple_of(kj * tq + u * tk, tk)
                s_t = _dot(k_ref[pl.ds(k0, tk), :], qm)
                if masked:
                    s_t = jnp.where(krow + u * tk <= qcol, s_t, NEG_BIG)
                yield u, s_t

        def values(kj, u):
            return vt_ref[vrows, pl.ds(pl.multiple_of(kj * tq + u * tk, tk), tk)]

        def fast_produce(qi, kj, buf, masked=False):
            shift = shift_ref[h, qi]
            for u, s_t in score_tiles(qi, kj, masked):
                buf[u] = jnp.exp(s_t - shift).astype(BF16)

        def fast_consume(i, kj, buf, fresh=False):
            acc = acc0 if fresh else acc_ref[hi, i]
            for u in range(nsub):
                acc = acc + _dot(values(kj, u), buf[u])
            acc_ref[hi, i] = acc

        def slow_produce(qi, kj, buf, masked=False):
            for u, s_t in score_tiles(qi, kj, masked):
                buf[u] = s_t

        def slow_consume(i, kj, buf, fresh=False):
            m = m0 if fresh else m_ref[hi, i]
            acc = acc0 if fresh else acc_ref[hi, i]
            for u in range(nsub):
                s_t = buf[u]
                m_new = jnp.maximum(m, jnp.max(s_t, axis=0, keepdims=True))
                alpha = jnp.exp(m - m_new)
                pt = jnp.exp(s_t - m_new).astype(BF16)
                acc = alpha * acc + _dot(values(kj, u), pt)
                m = m_new
            m_ref[hi, i] = m
            acc_ref[hi, i] = acc

        def list_block(i, t):
            thr = thr_ref[h, i]
            for j in range(nq):
                si_ref[t] = i
                sj_ref[t] = j
                t = t + jnp.where((j < i) & (fmin_ref[h, j] <= thr), 1, 0)
            return t

        worst_shift = lax.fori_loop(0, nq, lambda i, w: jnp.maximum(w, shift_ref[h, i]), jnp.float32(0.0))

        def run_head(produce, consume, buf_a, buf_b):
            def diag_prefetch(i, buf):
                ic = jnp.minimum(i, nq - 1)
                produce(ic, ic, buf, masked=True)

            if hi == 0:
                wait_chunk(0)
            diag_prefetch(0, buf_a)

            def diag_step(tn, n):
                i = FOX_STEPS * tn
                if hi == 0:
                    @pl.when(tn + 1 < n_chunk)
                    def _():
                        wait_chunk(tn + 1)

                for d in range(0, FOX_STEPS, 2):
                    diag_prefetch(i + d + 1, buf_b)
                    consume(i + d, i + d, buf_a, fresh=True)
                    diag_prefetch(i + d + 2, buf_a)
                    consume(i + d + 1, i + d + 1, buf_b, fresh=True)
                for d in range(FOX_STEPS):
                    n = list_block(i + d, n)
                return n

            n_tiles = lax.fori_loop(0, nq // FOX_STEPS, diag_step, 0)
            for pad in range(FOX_STEPS):
                si_ref[n_tiles + pad] = nq
                sj_ref[n_tiles + pad] = 0
            m_ref[hi, nq] = m0
            acc_ref[hi, nq] = acc0

            def prefetch(t, buf):
                produce(jnp.minimum(si_ref[t], nq - 1), sj_ref[t], buf)

            prefetch(0, buf_a)

            def multi_step(tn, carry):
                t = FOX_STEPS * tn
                for d in range(0, FOX_STEPS, 2):
                    prefetch(t + d + 1, buf_b)
                    consume(si_ref[t + d], sj_ref[t + d], buf_a)
                    prefetch(t + d + 2, buf_a)
                    consume(si_ref[t + d + 1], sj_ref[t + d + 1], buf_b)
                return carry

            lax.fori_loop(0, (n_tiles + FOX_STEPS - 1) // FOX_STEPS, multi_step, 0)

        no_running_max = 2.0 * worst_shift <= FOX_MAX_SHIFT_GAP

        @pl.when(no_running_max)
        def _():
            run_head(fast_produce, fast_consume, pa_ref, pb_ref)

        @pl.when(jnp.logical_not(no_running_max))
        def _():
            run_head(slow_produce, slow_consume, sa_ref, sb_ref)

    def finish(i, carry):
        a0 = acc_ref[0, i]
        a1 = acc_ref[1, i]
        ot = jnp.concatenate([a0[:FOX_DH] / a0[FOX_DH:FOX_DH + 1], a1[:FOX_DH] / a1[FOX_DH:FOX_DH + 1]], axis=0)
        o_ref[pl.ds(pl.multiple_of(i * tq, tq), tq), :] = ot.T.astype(BF16)
        return carry

    lax.fori_loop(0, nq, finish, 0)


def _fox(thr, fmin_blk, shift, qt, kaug, vt):
    s = kaug.shape[1]
    tq = FOX_TQ
    nq = s // tq
    npair = FOX_HEADS // 2
    prow = 2 * (FOX_DH + FOX_ONES_ROWS)
    assert nq % FOX_STEPS == 0 and FOX_STEPS % 2 == 0
    hrows = FOX_DH + FOX_ONES_ROWS
    max_tiles = nq * (nq - 1) // 2 + FOX_STEPS
    hbm = pl.BlockSpec(memory_space=pl.ANY)
    grid_spec = pltpu.PrefetchScalarGridSpec(
        num_scalar_prefetch=3,
        grid=(npair,),
        in_specs=[hbm, hbm, hbm],
        out_specs=pl.BlockSpec((None, s, LANES), lambda p, *_: (p, 0, 0)),
        scratch_shapes=[pltpu.VMEM((tq // FOX_TK, FOX_TK, tq), F32), pltpu.VMEM((tq // FOX_TK, FOX_TK, tq), F32),
                        pltpu.VMEM((tq // FOX_TK, FOX_TK, tq), BF16), pltpu.VMEM((tq // FOX_TK, FOX_TK, tq), BF16),
                        pltpu.VMEM((2, nq + 1, 1, tq), F32), pltpu.VMEM((2, nq + 1, hrows, tq), F32),
                        pltpu.SMEM((max_tiles,), jnp.int32), pltpu.SMEM((max_tiles,), jnp.int32),
                        pltpu.VMEM((2 * LANES, s), BF16), pltpu.VMEM((s, 2 * LANES), BF16),
                        pltpu.VMEM((prow, s), BF16), pltpu.SemaphoreType.DMA((3, nq // FOX_STEPS))],
    )
    return pl.pallas_call(
        _fox_kernel,
        grid_spec=grid_spec,
        out_shape=jax.ShapeDtypeStruct((npair, s, LANES), BF16),
        compiler_params=pltpu.CompilerParams(dimension_semantics=("arbitrary",), vmem_limit_bytes=VMEM_LIMIT),
        name="fox",
    )(thr, fmin_blk, shift, qt, kaug, vt)


def _tail_kernel(x_ref, ogla_ref, ofox_ref, omem_ref, gate_ref, wg_ref, wf_ref, wm_ref, wo_ref,
                 gffn_ref, w1_ref, w2_ref, gfin_ref, out_ref):
    groups = [slice(r0, r0 + TAIL_GROUP) for r0 in range(0, x_ref.shape[0], TAIL_GROUP)]
    merged = []
    for rg in groups:
        ofox = jnp.concatenate([ofox_ref[p, rg, :] for p in range(FOX_HEADS // 2)], axis=1)
        merged.append((gate_ref[rg, 0:D_MODEL].astype(F32) * _dot(ogla_ref[rg, :], wg_ref[...])
                       + gate_ref[rg, D_MODEL:2 * D_MODEL].astype(F32) * _dot(ofox, wf_ref[...])
                       + gate_ref[rg, 2 * D_MODEL:3 * D_MODEL].astype(F32) * _dot(omem_ref[rg, :], wm_ref[...])
                       ).astype(BF16))
    hs = [x_ref[rg, :] + _dot(mg, wo_ref[...]) for rg, mg in zip(groups, merged)]
    u2s = [_rms(h, gffn_ref[...]).astype(BF16) for h in hs]
    accs = [jnp.zeros_like(h) for h in hs]
    for cidx in range(D_FF // FF_CHUNK):
        cs = slice(cidx * FF_CHUNK, (cidx + 1) * FF_CHUNK)
        for g, u2 in enumerate(u2s):
            a = jnp.maximum(_dot(u2, w1_ref[:, cs]), 0.0)
            accs[g] = accs[g] + _dot((a * a).astype(BF16), w2_ref[cs, :])
    for rg, h, acc in zip(groups, hs, accs):
        out_ref[rg, :] = _rms(h + acc, gfin_ref[...])


def _tail(x, ogla, ofox, omem, gate, wg, wf, wm, wo, gffn, w1, w2, gfin):
    s = x.shape[0]
    tm = TAIL_TILE
    full = lambda a: pl.BlockSpec(a.shape, lambda i: (0,) * a.ndim, pipeline_mode=pl.Buffered(1))
    row = lambda w: pl.BlockSpec((tm, w), lambda i: (i, 0))
    return pl.pallas_call(
        _tail_kernel,
        grid=(s // tm,),
        in_specs=[row(D_MODEL), row(GLA_V), pl.BlockSpec((FOX_HEADS // 2, tm, LANES), lambda i: (0, i, 0)),
                  row(MEM_W), row(3 * D_MODEL),
                  full(wg), full(wf), full(wm), full(wo), full(gffn), full(w1), full(w2), full(gfin)],
        out_specs=row(D_MODEL),
        out_shape=jax.ShapeDtypeStruct((s, D_MODEL), F32),
        compiler_params=pltpu.CompilerParams(dimension_semantics=("arbitrary",), vmem_limit_bytes=VMEM_LIMIT),
        name="tail",
    )(x, ogla, ofox, omem, gate, wg, wf, wm, wo, gffn, w1, w2, gfin)


def kernel(x, mem, g_mix, w_in, w_alpha_up, b_alpha, b_forget, g_gla_head, g_mem, w_mem_kv,
           w_gla_o, w_fox_o, w_mem_o, w_out, g_ffn, w_ff1, w_ff2, g_final):
    assert x.shape[0] == 1 and g_mix.shape[0] == 1, "single batch, single layer"
    s = x.shape[1]
    assert s % ROW_TILE == 0 and s % TAIL_TILE == 0 and s % FOX_TQ == 0
    assert FOX_TQ % FOX_TK == 0 and FOX_TQ % PROJ_GROUP == 0 and ROW_TILE % PROJ_GROUP == 0
    xs = x[0]
    w = w_in[0]

    w_gla, w_fk, w_qvt, w_mq, w_gate, w_small = _wsplit(w)
    w_up_hi = w_alpha_up[0].astype(BF16)
    w_up = jnp.concatenate([w_up_hi, (w_alpha_up[0] - w_up_hi.astype(F32)).astype(BF16)], axis=1)
    b_f = jnp.zeros((1, LANES), F32).at[0, GLA_LOWRANK:GLA_LOWRANK + FOX_HEADS].set(b_forget[0])

    mk, mv = _memkv(mem[0], g_mem, w_mem_kv[0].astype(BF16))
    gla_qkvg, loga, qt, kaug, vt, omem, gate, stats, statq = _proj(
        xs, g_mix, w_gla, w_fk, w_qvt, w_mq, w_gate, w_small, w_up, b_alpha, b_f, mk, mv)

    ogla, (wg, wf, wm, wo, w1, w2) = _gla(
        gla_qkvg, loga, g_gla_head.reshape(1, GLA_V),
        [w_gla_o[0], w_fox_o[0], w_mem_o[0], w_out[0], w_ff1[0], w_ff2[0]])

    per_blk = lambda a: a.reshape(s // FOX_TQ, FOX_TQ // PROJ_GROUP, FOX_HEADS)
    qn = jnp.sqrt(jnp.max(per_blk(statq[:, :, 0]), axis=1))
    kn = jnp.sqrt(jnp.max(stats[:, 0, :FOX_HEADS], axis=0))
    fmax = jnp.max(per_blk(stats[:, 1, :FOX_HEADS]), axis=1)
    fmin = jnp.min(per_blk(stats[:, 2, :FOX_HEADS]), axis=1)
    shift = (1.02 * (FOX_DH ** -0.5)) * qn * kn[None, :]
    thr = 2.0 * shift + fmax + PRUNE_LOGIT_GAP
    ofox = _fox(thr.T, fmin.T, shift.T, qt, kaug, vt)

    out = _tail(xs, ogla, ofox, omem, gate, wg, wf, wm, wo, g_ffn, w1, w2, g_final.reshape(1, D_MODEL))
    return out[None]
```

```python
import jax
import jax.numpy as jnp
import numpy as np
from jax import lax
from jax.experimental import pallas as pl
from jax.experimental.pallas import tpu as pltpu

D_MODEL = 1024
CHUNK = 64
EPS = 1e-6
GLA_HEADS = 4
GLA_DK = 64
GLA_DV = 128
GLA_LOWRANK = 16
GLA_TAU = 16.0
FOX_HEADS = 8
FOX_DH = 64
MEM_HEADS = 4
MEM_DH = 128
D_FF = 4 * D_MODEL
GLA_K = GLA_HEADS * GLA_DK
GLA_V = GLA_HEADS * GLA_DV
FOX_W = FOX_HEADS * FOX_DH
MEM_W = MEM_HEADS * MEM_DH
OFF_GA = 2 * GLA_K + 2 * GLA_V
OFF_FQ = OFF_GA + GLA_LOWRANK
OFF_FK = OFF_FQ + FOX_W
OFF_FV = OFF_FK + FOX_W
OFF_FF = OFF_FV + FOX_W
OFF_MQ = OFF_FF + FOX_HEADS
OFF_GATE = OFF_MQ + MEM_W

LANES = 128
BF16_SUBLANES = 16
FOX_ONES_ROWS = 16
FOX_AUG_PER_HEAD = 6
ROW_TILE = 512
PROJ_GROUP = 256
GLA_TILE = 512
GLA_GROUP = 256
FOX_TQ = 512
FOX_TK = 256
FOX_STEPS = 8
TAIL_TILE = 512
TAIL_GROUP = 256
FF_CHUNK = 1024
VMEM_LIMIT = 56 * 1024 * 1024
NEG_BIG = -1e30
PRUNE_LOGIT_GAP = 104.0
FOX_MAX_SHIFT_GAP = 50.0

F32 = jnp.float32
BF16 = jnp.bfloat16


def _rms(xf, g):
    r = lax.rsqrt(jnp.mean(xf * xf, axis=-1, keepdims=True) + EPS)
    return (xf * r) * g


def _log_sigmoid(x):
    return jnp.minimum(x, 0.0) - jnp.log1p(jnp.exp(-jnp.abs(x)))


def _sigmoid(x):
    return 1.0 / (1.0 + jnp.exp(-x))


def _dot(a, b):
    return jnp.dot(a, b, preferred_element_type=F32)


def _dot_nt(a, b):
    return lax.dot_general(a, b, (((1,), (1,)), ((), ())), preferred_element_type=F32)


def _dot_tn(a, b):
    return lax.dot_general(a, b, (((0,), (0,)), ((), ())), preferred_element_type=F32)


def _memkv_kernel(mem_ref, g_ref, w_ref, mk_ref, mv_ref):
    mn = _rms(mem_ref[...], g_ref[...]).astype(BF16)
    kv = _dot(mn, w_ref[...])
    mk_ref[...] = kv[:, :MEM_W].astype(BF16)
    mv_ref[...] = kv[:, MEM_W:].astype(BF16)


def _memkv(mem, g_mem, w_mem_kv):
    m = mem.shape[0]
    return pl.pallas_call(
        _memkv_kernel,
        out_shape=(jax.ShapeDtypeStruct((m, MEM_W), BF16), jax.ShapeDtypeStruct((m, MEM_W), BF16)),
        name="memkv",
    )(mem, g_mem, w_mem_kv)


def _wsplit_kernel(wt_ref, gla_ref, fk_ref, qvt_ref, mq_ref, gate_ref, small_ref):
    step = 4 * LANES

    def put(dst_ref, lo, n):
        for c0 in range(0, n, step):
            dst_ref[:, c0:c0 + step] = wt_ref[lo + c0:lo + c0 + step, :].T.astype(BF16)

    put(gla_ref, 0, OFF_GA)
    put(fk_ref, OFF_FK, FOX_W)
    put(mq_ref, OFF_MQ, MEM_W)
    put(gate_ref, OFF_GATE, 3 * D_MODEL)
    qvt_ref[:FOX_W, :] = wt_ref[OFF_FQ:OFF_FK, :].astype(BF16)
    qvt_ref[FOX_W:, :] = wt_ref[OFF_FV:OFF_FF, :].astype(BF16)
    small_t = jnp.concatenate([wt_ref[OFF_GA:OFF_FQ, :], wt_ref[OFF_FF:OFF_MQ, :],
                               jnp.zeros((LANES - GLA_LOWRANK - FOX_HEADS, D_MODEL), F32)], axis=0)
    small = small_t.T
    small_hi = small.astype(BF16)
    small_ref[:, :LANES] = small_hi
    small_ref[:, LANES:] = (small - small_hi.astype(F32)).astype(BF16)


def _wsplit(wt):
    d = wt.shape[1]
    shapes = ((d, OFF_GA), (d, FOX_W), (2 * FOX_W, d), (d, MEM_W), (d, 3 * D_MODEL), (d, 2 * LANES))
    return pl.pallas_call(
        _wsplit_kernel,
        out_shape=[jax.ShapeDtypeStruct(sh, BF16) for sh in shapes],
        compiler_params=pltpu.CompilerParams(vmem_limit_bytes=VMEM_LIMIT),
        name="wsplit",
    )(wt)


def _proj_kernel(x_ref, gmix_ref, wgla_ref, wfk_ref, wqvt_ref, wmq_ref, wgate_ref, wsmall_ref, wup_ref,
                 balpha_ref, bforget_ref, mk_ref, mv_ref, pqt_ref, pk_ref,
                 gla_ref, loga_ref, qt_ref, kaug_ref, vt_ref, omem_ref, gate_ref, stat_ref, statq_ref,
                 carry_ref):
    tm = x_ref.shape[0]

    @pl.when(pl.program_id(0) == 0)
    def _():
        carry_ref[...] = jnp.zeros_like(carry_ref)

    tg = PROJ_GROUP
    groups = [slice(r0, r0 + tg) for r0 in range(0, tm, tg)]
    gid = range(len(groups))
    scale = MEM_DH ** -0.5
    heads = [slice(h * MEM_DH, (h + 1) * MEM_DH) for h in range(MEM_HEADS)]

    ubs, smalls, scores = [], [], []
    for rg in groups:
        u = _rms(x_ref[rg, :], gmix_ref[...])
        ub = u.astype(BF16)
        u_lo = (u - ub.astype(F32)).astype(BF16)
        parts = _dot(jnp.concatenate([ub, u_lo], axis=0), wsmall_ref[...])
        smalls.append((parts[:tg, :LANES] + parts[:tg, LANES:]) + (parts[tg:, :LANES] + parts[tg:, LANES:]))
        mq = _dot(ub, wmq_ref[...]).astype(BF16)
        scores.append([_dot_nt(mq[:, sl], mk_ref[:, sl]) * scale for sl in heads])
        ubs.append(ub)

    fkbs = []
    for rg, ub in zip(groups, ubs):
        gla_ref[rg, :] = _dot(ub, wgla_ref[...]).astype(BF16)
        fkbs.append(_dot(ub, wfk_ref[...]).astype(BF16))

    lane = lax.broadcasted_iota(jnp.int32, (tg, LANES), 1)
    ff_valid = (lane >= GLA_LOWRANK) & (lane < GLA_LOWRANK + FOX_HEADS)
    r = lax.broadcasted_iota(jnp.int32, (tg, tg), 0)
    c = lax.broadcasted_iota(jnp.int32, (tg, tg), 1)
    tri = jnp.where(r >= c, 1.0, 0.0).astype(BF16)

    def aug_slot(idx):
        a = idx % LANES
        return a % FOX_AUG_PER_HEAD, a < 2 * FOX_AUG_PER_HEAD

    slot_q, in_q = aug_slot(lax.broadcasted_iota(jnp.int32, (FOX_W, 1), 0))
    slot_k, in_k = aug_slot(lax.broadcasted_iota(jnp.int32, (1, FOX_W), 1))
    carry = carry_ref[...]
    fcums, augqts, augks = [], [], []
    for rg, small in zip(groups, smalls):
        ga = small[:, :GLA_LOWRANK]
        ga_hi = ga.astype(BF16)
        ga_lo = (ga - ga_hi.astype(F32)).astype(BF16)
        up = _dot(jnp.concatenate([ga_hi, ga_lo], axis=0), wup_ref[...])
        alpha_pre = ((up[:tg, :GLA_K] + up[:tg, GLA_K:]) + (up[tg:, :GLA_K] + up[tg:, GLA_K:])) + balpha_ref[...]
        loga_ref[rg, :] = _log_sigmoid(alpha_pre) * (1.0 / GLA_TAU)

        logf = jnp.where(ff_valid, _log_sigmoid(small + bforget_ref[...]), 0.0)
        logf = pltpu.roll(logf, LANES - GLA_LOWRANK, 1)
        lf_hi = logf.astype(BF16)
        lf_r = logf - lf_hi.astype(F32)
        lf_mid = lf_r.astype(BF16)
        lf_lo = (lf_r - lf_mid.astype(F32)).astype(BF16)
        c3 = _dot(tri, jnp.concatenate([lf_hi, lf_mid, lf_lo], axis=1))
        fcum = (c3[:, :LANES] + c3[:, LANES:2 * LANES] + c3[:, 2 * LANES:]) + carry
        carry = fcum[tg - 1:tg, :]

        f_hi = fcum.astype(BF16)
        rem = fcum - f_hi.astype(F32)
        f_mid = rem.astype(BF16)
        f_lo = (rem - f_mid.astype(F32)).astype(BF16)
        f3 = jnp.concatenate([f_hi, f_mid, f_lo], axis=1)
        augqts.append(_dot_nt(pqt_ref[...], f3) + jnp.where(in_q & (slot_q < 3), -1.0, 0.0))
        augks.append(_dot(f3, pk_ref[...]) + jnp.where(in_k & (slot_k >= 3), 1.0, 0.0))
        fcums.append(fcum)
    carry_ref[...] = carry

    for rg, ub in zip(groups, ubs):
        gate_ref[rg, :] = _sigmoid(_dot(ub, wgate_ref[...])).astype(BF16)

    for rg, sc in zip(groups, scores):
        probs = []
        for s in sc:
            m = jnp.max(s, axis=-1, keepdims=True)
            e = jnp.exp(s - m)
            probs.append((e / jnp.sum(e, axis=-1, keepdims=True)).astype(BF16))
        omem_ref[rg, :] = jnp.concatenate([_dot(pr, mv_ref[:, sl]) for pr, sl in zip(probs, heads)],
                                          axis=-1).astype(BF16)

    ones = jnp.ones((FOX_ONES_ROWS, tg), BF16)
    gi = lax.broadcasted_iota(jnp.int32, (FOX_W, LANES), 0) // FOX_DH
    gj = lax.broadcasted_iota(jnp.int32, (FOX_W, LANES), 1)
    group = jnp.where(gi == gj, 1.0, 0.0).astype(BF16)
    for g, rg, ub, fkb, fcum, augqt, augk in zip(gid, groups, ubs, fkbs, fcums, augqts, augks):
        qvt = _dot_nt(wqvt_ref[...], ub)
        fqt = (qvt[:FOX_W] * (FOX_DH ** -0.5)).astype(BF16)
        vt = qvt[FOX_W:].astype(BF16)
        vt_parts = []
        for h in range(FOX_HEADS):
            vt_parts += [vt[h * FOX_DH:(h + 1) * FOX_DH, :], ones]
        vt_ref[:, rg] = jnp.concatenate(vt_parts, axis=0)

        for p in range(FOX_HEADS // 2):
            src = slice(p * LANES, (p + 1) * LANES)
            qt_ref[2 * p * LANES:(2 * p + 1) * LANES, rg] = fqt[src]
            qt_ref[(2 * p + 1) * LANES:(2 * p + 2) * LANES, rg] = augqt[src].astype(BF16)
            kaug_ref[p, rg, :LANES] = fkb[:, src]
            kaug_ref[p, rg, LANES:] = augk[:, src].astype(BF16)

        fq32 = fqt.astype(F32) * (FOX_DH ** 0.5)
        nq2 = jnp.sum((fq32 * fq32).reshape(FOX_HEADS, FOX_DH, tg), axis=1)
        statq_ref[g] = jnp.broadcast_to(jnp.max(nq2, axis=1, keepdims=True), (FOX_HEADS, LANES))
        fk = fkb.astype(F32)
        nk2 = jnp.max(_dot((fk * fk).astype(BF16), group), axis=0, keepdims=True)
        fmax = jnp.max(fcum, axis=0, keepdims=True)
        fmin = jnp.min(fcum, axis=0, keepdims=True)
        stat_ref[g] = jnp.concatenate([nk2, fmax, fmin, jnp.zeros((5, LANES), F32)], axis=0)


def _aug_placement():
    pq = np.zeros((3 * LANES, FOX_W), np.float32)
    pk = np.zeros((3 * LANES, FOX_W), np.float32)
    for h in range(FOX_HEADS):
        base = (h // 2) * LANES + (h % 2) * FOX_AUG_PER_HEAD
        for c in range(3):
            pq[c * LANES + h, base + 3 + c] = 1.0
            pk[c * LANES + h, base + c] = 1.0
    return jnp.asarray(pq.T, BF16), jnp.asarray(pk, BF16)


def _proj(x, g_mix, w_gla, w_fk, w_qvt, w_mq, w_gate, w_small, w_up, b_alpha, b_forget, mk, mv):
    s = x.shape[0]
    tm = ROW_TILE
    nt = s // tm
    pqt, pk = _aug_placement()
    full = lambda shape: pl.BlockSpec(shape, lambda i: (0,) * len(shape), pipeline_mode=pl.Buffered(1))
    row = lambda w: pl.BlockSpec((tm, w), lambda i: (i, 0))
    col = lambda r: pl.BlockSpec((r, tm), lambda i: (0, i))
    ng = tm // PROJ_GROUP
    stat = pl.BlockSpec((ng, 8, LANES), lambda i: (i, 0, 0))
    vt_rows = FOX_HEADS * (FOX_DH + FOX_ONES_ROWS)
    return pl.pallas_call(
        _proj_kernel,
        grid=(nt,),
        in_specs=[row(D_MODEL), full((1, D_MODEL)), full(w_gla.shape), full(w_fk.shape), full(w_qvt.shape),
                  full(w_mq.shape), full(w_gate.shape), full(w_small.shape), full(w_up.shape),
                  full(b_alpha.shape), full(b_forget.shape), full(mk.shape), full(mv.shape),
                  full(pqt.shape), full(pk.shape)],
        out_specs=[row(w_gla.shape[1]), row(GLA_K), col(2 * FOX_W),
                   pl.BlockSpec((FOX_HEADS // 2, tm, 2 * LANES), lambda i: (0, i, 0)), col(vt_rows), row(MEM_W),
                   row(w_gate.shape[1]), stat, stat],
        out_shape=[jax.ShapeDtypeStruct((s, w_gla.shape[1]), BF16),
                   jax.ShapeDtypeStruct((s, GLA_K), F32),
                   jax.ShapeDtypeStruct((2 * FOX_W, s), BF16),
                   jax.ShapeDtypeStruct((FOX_HEADS // 2, s, 2 * LANES), BF16),
                   jax.ShapeDtypeStruct((vt_rows, s), BF16),
                   jax.ShapeDtypeStruct((s, MEM_W), BF16),
                   jax.ShapeDtypeStruct((s, w_gate.shape[1]), BF16),
                   jax.ShapeDtypeStruct((nt * ng, 8, LANES), F32),
                   jax.ShapeDtypeStruct((nt * ng, 8, LANES), F32)],
        scratch_shapes=[pltpu.VMEM((1, LANES), F32)],
        compiler_params=pltpu.CompilerParams(dimension_semantics=("arbitrary",), vmem_limit_bytes=VMEM_LIMIT),
        name="proj",
    )(x, g_mix, w_gla, w_fk, w_qvt, w_mq, w_gate, w_small, w_up, b_alpha, b_forget, mk, mv, pqt, pk)


def _gla_kernel(qkvg_ref, loga_ref, ghead_ref, *rest):
    n_w = (len(rest) - 2) // 2
    w_in_refs, o_ref, w_out_refs, state_ref = rest[:n_w], rest[n_w], rest[n_w + 1:2 * n_w + 1], rest[-1]
    t = GLA_GROUP
    nc = t // CHUNK

    @pl.when(pl.program_id(0) == 0)
    def _():
        state_ref[...] = jnp.zeros_like(state_ref)

    for w_src, w_dst in zip(w_in_refs, w_out_refs):
        w_dst[...] = w_src[...].astype(BF16)

    r = lax.broadcasted_iota(jnp.int32, (t, t), 0)
    c = lax.broadcasted_iota(jnp.int32, (t, t), 1)
    same = (r // CHUNK) == (c // CHUNK)
    lower = r >= c
    causal = same & lower
    anti = same & (r < c)
    cum = jnp.where(causal, 1.0, 0.0).astype(BF16)
    lane = lax.broadcasted_iota(jnp.int32, (1, LANES), 1)

    chunks = [slice(ci * CHUNK, (ci + 1) * CHUNK) for ci in range(nc)]
    pairs = [slice((h // 2) * LANES, (h // 2 + 1) * LANES) for h in range(GLA_HEADS)]
    in_head = [(lane // GLA_DK) == (h % 2) for h in range(GLA_HEADS)]
    heads = range(GLA_HEADS)
    groups = [slice(g0, g0 + t) for g0 in range(0, qkvg_ref.shape[0], t)]

    pre = []
    for rg in groups:
        la = loga_ref[rg, :]
        la_hi = la.astype(BF16)
        la_r = la - la_hi.astype(F32)
        la_mid = la_r.astype(BF16)
        la_lo = (la_r - la_mid.astype(F32)).astype(BF16)
        b3 = _dot(cum, jnp.concatenate([la_hi, la_mid, la_lo], axis=1))
        b = b3[:, :GLA_K] + b3[:, GLA_K:2 * GLA_K] + b3[:, 2 * GLA_K:]
        b_last = jnp.concatenate(
            [jnp.broadcast_to(b[(ci + 1) * CHUNK - 1:(ci + 1) * CHUNK, :], (CHUNK, GLA_K)) for ci in range(nc)],
            axis=0)
        e_pos = jnp.exp(b)
        e_neg = jnp.exp(-b)
        q = qkvg_ref[rg, 0:GLA_K].astype(F32) * (GLA_DK ** -0.5)
        k = qkvg_ref[rg, GLA_K:2 * GLA_K].astype(F32)
        q_pos = (q * e_pos).astype(BF16)
        q_neg = (q * e_neg).astype(BF16)
        k_pos = (k * e_pos).astype(BF16)
        k_neg = (k * e_neg).astype(BF16)
        k_dec = (k * jnp.exp(b_last - b)).astype(BF16)

        vs = [qkvg_ref[rg, 2 * GLA_K + h * GLA_DV:2 * GLA_K + (h + 1) * GLA_DV] for h in heads]
        qps = [jnp.where(in_head[h], q_pos[:, pairs[h]], jnp.zeros((), BF16)) for h in heads]
        kvs = [[jnp.where(in_head[h], _dot_tn(vs[h][rows], k_dec[rows, pairs[h]]), 0.0) for rows in chunks]
               for h in heads]
        pre.append((e_pos, q_neg, k_pos, k_neg, vs, qps, kvs))

    sts = [state_ref[h] for h in heads]
    inters = []
    for e_pos, _, _, _, _, qps, kvs in pre:
        inter = [[] for _ in heads]
        for ci, rows in enumerate(chunks):
            for h in heads:
                inter[h].append(_dot_nt(qps[h][rows], sts[h].astype(BF16)))
                dec = e_pos[(ci + 1) * CHUNK - 1:(ci + 1) * CHUNK, pairs[h]]
                sts[h] = sts[h] * dec + kvs[h][ci]
        inters.append(inter)
    for h in heads:
        state_ref[h] = sts[h]

    attns = []
    for _, q_neg, k_pos, k_neg, _, qps, _ in pre:
        grp = []
        for h in heads:
            qn = jnp.where(in_head[h], q_neg[:, pairs[h]], jnp.zeros((), BF16))
            a_c = _dot_nt(qps[h], k_neg[:, pairs[h]])
            a_a = _dot_nt(qn, k_pos[:, pairs[h]])
            grp.append(jnp.where(causal, a_c, jnp.where(anti, a_a, 0.0)).astype(BF16))
        attns.append(grp)

    for rg, grp, inter, (_, _, _, _, vs, _, _) in zip(groups, attns, inters, pre):
        outs = [_dot(grp[h], vs[h]) + jnp.concatenate(inter[h], axis=0) for h in heads]
        for h in heads:
            gh = ghead_ref[:, h * GLA_DV:(h + 1) * GLA_DV]
            on = _rms(outs[h], gh)
            gsl = slice(2 * GLA_K + GLA_V + h * GLA_DV, 2 * GLA_K + GLA_V + (h + 1) * GLA_DV)
            gg = qkvg_ref[rg, gsl].astype(F32)
            o_ref[rg, h * GLA_DV:(h + 1) * GLA_DV] = (on * (gg * _sigmoid(gg))).astype(BF16)


def _gla(qkvg, loga, ghead, later_weights):
    s = qkvg.shape[0]
    t = GLA_TILE
    assert s % t == 0 and t % GLA_GROUP == 0
    steps = s // t
    w_specs = [pl.BlockSpec((w.shape[0] // steps, w.shape[1]), lambda i: (i, 0)) for w in later_weights]
    assert all(w.shape[0] % (steps * BF16_SUBLANES) == 0 for w in later_weights)
    outs = pl.pallas_call(
        _gla_kernel,
        grid=(steps,),
        in_specs=[pl.BlockSpec((t, qkvg.shape[1]), lambda i: (i, 0)),
                  pl.BlockSpec((t, GLA_K), lambda i: (i, 0)),
                  pl.BlockSpec((1, GLA_V), lambda i: (0, 0))] + w_specs,
        out_specs=[pl.BlockSpec((t, GLA_V), lambda i: (i, 0))] + w_specs,
        out_shape=[jax.ShapeDtypeStruct((s, GLA_V), BF16)]
        + [jax.ShapeDtypeStruct(w.shape, BF16) for w in later_weights],
        scratch_shapes=[pltpu.VMEM((GLA_HEADS, GLA_DV, LANES), F32)],
        compiler_params=pltpu.CompilerParams(dimension_semantics=("arbitrary",), vmem_limit_bytes=VMEM_LIMIT),
        name="gla",
    )(qkvg, loga, ghead, *later_weights)
    return outs[0], outs[1:]


def _fox_kernel(thr_ref, fmin_ref, shift_ref, qt_hbm, k_hbm, vt_hbm, o_ref,
                sa_ref, sb_ref, pa_ref, pb_ref, m_ref, acc_ref, si_ref, sj_ref,
                qt_ref, k_ref, vt_ref, dma_sem):
    p = pl.program_id(0)
    tq = FOX_TQ
    tk = FOX_TK
    nsub = tq // tk
    nq = k_ref.shape[0] // tq
    hrows = FOX_DH + FOX_ONES_ROWS
    row = lax.broadcasted_iota(jnp.int32, (2 * LANES, 1), 0)
    krow = lax.broadcasted_iota(jnp.int32, (tk, tq), 0)
    qcol = lax.broadcasted_iota(jnp.int32, (tk, tq), 1)
    m0 = jnp.full((1, tq), NEG_BIG, F32)
    acc0 = jnp.zeros((hrows, tq), F32)

    n_chunk = nq // FOX_STEPS
    chunk = FOX_STEPS * tq

    def chunk_copies(c):
        at = pl.ds(pl.multiple_of(c * chunk, chunk), chunk)
        return (pltpu.make_async_copy(qt_hbm.at[pl.ds(pl.multiple_of(p * 2 * LANES, 2 * LANES), 2 * LANES), at],
                                      qt_ref.at[:, at], dma_sem.at[0, c]),
                pltpu.make_async_copy(k_hbm.at[p, at, :], k_ref.at[at, :], dma_sem.at[1, c]),
                pltpu.make_async_copy(vt_hbm.at[pl.ds(pl.multiple_of(p * 2 * hrows, 2 * hrows), 2 * hrows), at],
                                      vt_ref.at[:, at], dma_sem.at[2, c]))

    def wait_chunk(c):
        for cp in chunk_copies(c):
            cp.wait()

    for c in range(n_chunk):
        for cp in chunk_copies(c):
            cp.start()

    for hi in range(2):
        h = 2 * p + hi
        aug0 = LANES + hi * FOX_AUG_PER_HEAD
        mine = ((row // FOX_DH) == hi) | ((row >= aug0) & (row < aug0 + FOX_AUG_PER_HEAD))
        vrows = slice(hi * hrows, (hi + 1) * hrows)

        def score_tiles(qi, kj, masked):
            q0 = pl.multiple_of(qi * tq, tq)
            qm = jnp.where(mine, qt_ref[:, pl.ds(q0, tq)], jnp.zeros((), BF16))
            for u in range(nsub):
                k0 = pl.multiple_of(kj * tq + u * tk, tk)
                s_t = _dot(k_ref[pl.ds(k0, tk), :], qm)
                if masked:
                    s_t = jnp.where(krow + u * tk <= qcol, s_t, NEG_BIG)
                yield u, s_t

        def values(kj, u):
            return vt_ref[vrows, pl.ds(pl.multiple_of(kj * tq + u * tk, tk), tk)]

        def fast_produce(qi, kj, buf, masked=False):
            shift = shift_ref[h, qi]
            for u, s_t in score_tiles(qi, kj, masked):
                buf[u] = jnp.exp(s_t - shift).astype(BF16)

        def fast_consume(i, kj, buf, fresh=False):
            acc = acc0 if fresh else acc_ref[hi, i]
            for u in range(nsub):
                acc = acc + _dot(values(kj, u), buf[u])
            acc_ref[hi, i] = acc

        def slow_produce(qi, kj, buf, masked=False):
            for u, s_t in score_tiles(qi, kj, masked):
                buf[u] = s_t

        def slow_consume(i, kj, buf, fresh=False):
            m = m0 if fresh else m_ref[hi, i]
            acc = acc0 if fresh else acc_ref[hi, i]
            for u in range(nsub):
                s_t = buf[u]
                m_new = jnp.maximum(m, jnp.max(s_t, axis=0, keepdims=True))
                alpha = jnp.exp(m - m_new)
                pt = jnp.exp(s_t - m_new).astype(BF16)
                acc = alpha * acc + _dot(values(kj, u), pt)
                m = m_new
            m_ref[hi, i] = m
            acc_ref[hi, i] = acc

        def list_block(i, t):
            thr = thr_ref[h, i]
            for j in range(nq):
                si_ref[t] = i
                sj_ref[t] = j
                t = t + jnp.where((j < i) & (fmin_ref[h, j] <= thr), 1, 0)
            return t

        worst_shift = lax.fori_loop(0, nq, lambda i, w: jnp.maximum(w, shift_ref[h, i]), jnp.float32(0.0))

        def run_head(produce, consume, buf_a, buf_b):
            def diag_prefetch(i, buf):
                ic = jnp.minimum(i, nq - 1)
                produce(ic, ic, buf, masked=True)

            if hi == 0:
                wait_chunk(0)
            diag_prefetch(0, buf_a)

            def diag_step(tn, n):
                i = FOX_STEPS * tn
                if hi == 0:
                    @pl.when(tn + 1 < n_chunk)
                    def _():
                        wait_chunk(tn + 1)

                for d in range(0, FOX_STEPS, 2):
                    diag_prefetch(i + d + 1, buf_b)
                    consume(i + d, i + d, buf_a, fresh=True)
                    diag_prefetch(i + d + 2, buf_a)
                    consume(i + d + 1, i + d + 1, buf_b, fresh=True)
                for d in range(FOX_STEPS):
                    n = list_block(i + d, n)
                return n

            n_tiles = lax.fori_loop(0, nq // FOX_STEPS, diag_step, 0)
            for pad in range(FOX_STEPS):
                si_ref[n_tiles + pad] = nq
                sj_ref[n_tiles + pad] = 0
            m_ref[hi, nq] = m0
            acc_ref[hi, nq] = acc0

            def prefetch(t, buf):
                produce(jnp.minimum(si_ref[t], nq - 1), sj_ref[t], buf)

            prefetch(0, buf_a)

            def multi_step(tn, carry):
                t = FOX_STEPS * tn
                for d in range(0, FOX_STEPS, 2):
                    prefetch(t + d + 1, buf_b)
                    consume(si_ref[t + d], sj_ref[t + d], buf_a)
                    prefetch(t + d + 2, buf_a)
                    consume(si_ref[t + d + 1], sj_ref[t + d + 1], buf_b)
                return carry

            lax.fori_loop(0, (n_tiles + FOX_STEPS - 1) // FOX_STEPS, multi_step, 0)

        no_running_max = 2.0 * worst_shift <= FOX_MAX_SHIFT_GAP

        @pl.when(no_running_max)
        def _():
            run_head(fast_produce, fast_consume, pa_ref, pb_ref)

        @pl.when(jnp.logical_not(no_running_max))
        def _():
            run_head(slow_produce, slow_consume, sa_ref, sb_ref)

    def finish(i, carry):
        a0 = acc_ref[0, i]
        a1 = acc_ref[1, i]
        ot = jnp.concatenate([a0[:FOX_DH] / a0[FOX_DH:FOX_DH + 1], a1[:FOX_DH] / a1[FOX_DH:FOX_DH + 1]], axis=0)
        o_ref[pl.ds(pl.multiple_of(i * tq, tq), tq), :] = ot.T.astype(BF16)
        return carry

    lax.fori_loop(0, nq, finish, 0)


def _fox(thr, fmin_blk, shift, qt, kaug, vt):
    s = kaug.shape[1]
    tq = FOX_TQ
    nq = s // tq
    npair = FOX_HEADS // 2
    prow = 2 * (FOX_DH + FOX_ONES_ROWS)
    assert nq % FOX_STEPS == 0 and FOX_STEPS % 2 == 0
    hrows = FOX_DH + FOX_ONES_ROWS
    max_tiles = nq * (nq - 1) // 2 + FOX_STEPS
    hbm = pl.BlockSpec(memory_space=pl.ANY)
    grid_spec = pltpu.PrefetchScalarGridSpec(
        num_scalar_prefetch=3,
        grid=(npair,),
        in_specs=[hbm, hbm, hbm],
        out_specs=pl.BlockSpec((None, s, LANES), lambda p, *_: (p, 0, 0)),
        scratch_shapes=[pltpu.VMEM((tq // FOX_TK, FOX_TK, tq), F32), pltpu.VMEM((tq // FOX_TK, FOX_TK, tq), F32),
                        pltpu.VMEM((tq // FOX_TK, FOX_TK, tq), BF16), pltpu.VMEM((tq // FOX_TK, FOX_TK, tq), BF16),
                        pltpu.VMEM((2, nq + 1, 1, tq), F32), pltpu.VMEM((2, nq + 1, hrows, tq), F32),
                        pltpu.SMEM((max_tiles,), jnp.int32), pltpu.SMEM((max_tiles,), jnp.int32),
                        pltpu.VMEM((2 * LANES, s), BF16), pltpu.VMEM((s, 2 * LANES), BF16),
                        pltpu.VMEM((prow, s), BF16), pltpu.SemaphoreType.DMA((3, nq // FOX_STEPS))],
    )
    return pl.pallas_call(
        _fox_kernel,
        grid_spec=grid_spec,
        out_shape=jax.ShapeDtypeStruct((npair, s, LANES), BF16),
        compiler_params=pltpu.CompilerParams(dimension_semantics=("arbitrary",), vmem_limit_bytes=VMEM_LIMIT),
        name="fox",
    )(thr, fmin_blk, shift, qt, kaug, vt)


def _tail_kernel(x_ref, ogla_ref, ofox_ref, omem_ref, gate_ref, wg_ref, wf_ref, wm_ref, wo_ref,
                 gffn_ref, w1_ref, w2_ref, gfin_ref, out_ref):
    groups = [slice(r0, r0 + TAIL_GROUP) for r0 in range(0, x_ref.shape[0], TAIL_GROUP)]
    merged = []
    for rg in groups:
        ofox = jnp.concatenate([ofox_ref[p, rg, :] for p in range(FOX_HEADS // 2)], axis=1)
        merged.append((gate_ref[rg, 0:D_MODEL].astype(F32) * _dot(ogla_ref[rg, :], wg_ref[...])
                       + gate_ref[rg, D_MODEL:2 * D_MODEL].astype(F32) * _dot(ofox, wf_ref[...])
                       + gate_ref[rg, 2 * D_MODEL:3 * D_MODEL].astype(F32) * _dot(omem_ref[rg, :], wm_ref[...])
                       ).astype(BF16))
    hs = [x_ref[rg, :] + _dot(mg, wo_ref[...]) for rg, mg in zip(groups, merged)]
    u2s = [_rms(h, gffn_ref[...]).astype(BF16) for h in hs]
    accs = [jnp.zeros_like(h) for h in hs]
    for cidx in range(D_FF // FF_CHUNK):
        cs = slice(cidx * FF_CHUNK, (cidx + 1) * FF_CHUNK)
        for g, u2 in enumerate(u2s):
            a = jnp.maximum(_dot(u2, w1_ref[:, cs]), 0.0)
            accs[g] = accs[g] + _dot((a * a).astype(BF16), w2_ref[cs, :])
    for rg, h, acc in zip(groups, hs, accs):
        out_ref[rg, :] = _rms(h + acc, gfin_ref[...])


def _tail(x, ogla, ofox, omem, gate, wg, wf, wm, wo, gffn, w1, w2, gfin):
    s = x.shape[0]
    tm = TAIL_TILE
    full = lambda a: pl.BlockSpec(a.shape, lambda i: (0,) * a.ndim, pipeline_mode=pl.Buffered(1))
    row = lambda w: pl.BlockSpec((tm, w), lambda i: (i, 0))
    return pl.pallas_call(
        _tail_kernel,
        grid=(s // tm,),
        in_specs=[row(D_MODEL), row(GLA_V), pl.BlockSpec((FOX_HEADS // 2, tm, LANES), lambda i: (0, i, 0)),
                  row(MEM_W), row(3 * D_MODEL),
                  full(wg), full(wf), full(wm), full(wo), full(gffn), full(w1), full(w2), full(gfin)],
        out_specs=row(D_MODEL),
        out_shape=jax.ShapeDtypeStruct((s, D_MODEL), F32),
        compiler_params=pltpu.CompilerParams(dimension_semantics=("arbitrary",), vmem_limit_bytes=VMEM_LIMIT),
        name="tail",
    )(x, ogla, ofox, omem, gate, wg, wf, wm, wo, gffn, w1, w2, gfin)


def kernel(x, mem, g_mix, w_in, w_alpha_up, b_alpha, b_forget, g_gla_head, g_mem, w_mem_kv,
           w_gla_o, w_fox_o, w_mem_o, w_out, g_ffn, w_ff1, w_ff2, g_final):
    assert x.shape[0] == 1 and g_mix.shape[0] == 1, "single batch, single layer"
    s = x.shape[1]
    assert s % ROW_TILE == 0 and s % TAIL_TILE == 0 and s % FOX_TQ == 0
    assert FOX_TQ % FOX_TK == 0 and FOX_TQ % PROJ_GROUP == 0 and ROW_TILE % PROJ_GROUP == 0
    xs = x[0]
    w = w_in[0]

    w_gla, w_fk, w_qvt, w_mq, w_gate, w_small = _wsplit(jnp.swapaxes(w, 0, 1))
    w_up_hi = w_alpha_up[0].astype(BF16)
    w_up = jnp.concatenate([w_up_hi, (w_alpha_up[0] - w_up_hi.astype(F32)).astype(BF16)], axis=1)
    b_f = jnp.zeros((1, LANES), F32).at[0, GLA_LOWRANK:GLA_LOWRANK + FOX_HEADS].set(b_forget[0])

    mk, mv = _memkv(mem[0], g_mem, w_mem_kv[0].astype(BF16))
    gla_qkvg, loga, qt, kaug, vt, omem, gate, stats, statq = _proj(
        xs, g_mix, w_gla, w_fk, w_qvt, w_mq, w_gate, w_small, w_up, b_alpha, b_f, mk, mv)

    ogla, (wg, wf, wm, wo, w1, w2) = _gla(
        gla_qkvg, loga, g_gla_head.reshape(1, GLA_V),
        [w_gla_o[0], w_fox_o[0], w_mem_o[0], w_out[0], w_ff1[0], w_ff2[0]])

    per_blk = lambda a: a.reshape(s // FOX_TQ, FOX_TQ // PROJ_GROUP, FOX_HEADS)
    qn = jnp.sqrt(jnp.max(per_blk(statq[:, :, 0]), axis=1))
    kn = jnp.sqrt(jnp.max(stats[:, 0, :FOX_HEADS], axis=0))
    fmax = jnp.max(per_blk(stats[:, 1, :FOX_HEADS]), axis=1)
    fmin = jnp.min(per_blk(stats[:, 2, :FOX_HEADS]), axis=1)
    shift = (1.02 * (FOX_DH ** -0.5)) * qn * kn[None, :]
    thr = 2.0 * shift + fmax + PRUNE_LOGIT_GAP
    ofox = _fox(thr.T, fmin.T, shift.T, qt, kaug, vt)

    out = _tail(xs, ogla, ofox, omem, gate, wg, wf, wm, wo, g_ffn, w1, w2, g_final.reshape(1, D_MODEL))
    return out[None]
```

```python
import jax
import jax.numpy as jnp
import numpy as np
from jax import lax
from jax.experimental import pallas as pl
from jax.experimental.pallas import tpu as pltpu

D_MODEL = 1024
CHUNK = 64
EPS = 1e-6
GLA_HEADS = 4
GLA_DK = 64
GLA_DV = 128
GLA_LOWRANK = 16
GLA_TAU = 16.0
FOX_HEADS = 8
FOX_DH = 64
MEM_HEADS = 4
MEM_DH = 128
D_FF = 4 * D_MODEL
GLA_K = GLA_HEADS * GLA_DK
GLA_V = GLA_HEADS * GLA_DV
FOX_W = FOX_HEADS * FOX_DH
MEM_W = MEM_HEADS * MEM_DH
OFF_GA = 2 * GLA_K + 2 * GLA_V
OFF_FQ = OFF_GA + GLA_LOWRANK
OFF_FK = OFF_FQ + FOX_W
OFF_FV = OFF_FK + FOX_W
OFF_FF = OFF_FV + FOX_W
OFF_MQ = OFF_FF + FOX_HEADS
OFF_GATE = OFF_MQ + MEM_W

LANES = 128
BF16_SUBLANES = 16
FOX_ONES_ROWS = 16
FOX_AUG_PER_HEAD = 6
ROW_TILE = 512
PROJ_GROUP = 256
GLA_TILE = 512
GLA_GROUP = 256
FOX_TQ = 512
FOX_TK = 256
FOX_STEPS = 8
TAIL_TILE = 512
TAIL_GROUP = 256
FF_CHUNK = 1024
VMEM_LIMIT = 56 * 1024 * 1024
NEG_BIG = -1e30
PRUNE_LOGIT_GAP = 104.0
FOX_MAX_SHIFT_GAP = 50.0

F32 = jnp.float32
BF16 = jnp.bfloat16


def _rms(xf, g):
    r = lax.rsqrt(jnp.mean(xf * xf, axis=-1, keepdims=True) + EPS)
    return (xf * r) * g


def _log_sigmoid(x):
    return jnp.minimum(x, 0.0) - jnp.log1p(jnp.exp(-jnp.abs(x)))


def _sigmoid(x):
    return 1.0 / (1.0 + jnp.exp(-x))


def _dot(a, b):
    return jnp.dot(a, b, preferred_element_type=F32)


def _dot_nt(a, b):
    return lax.dot_general(a, b, (((1,), (1,)), ((), ())), preferred_element_type=F32)


def _dot_tn(a, b):
    return lax.dot_general(a, b, (((0,), (0,)), ((), ())), preferred_element_type=F32)


PREP_CHUNKS = ((0, OFF_GA), (OFF_GA, OFF_GATE), (OFF_GATE, OFF_GATE + OFF_GA), (OFF_GATE + OFF_GA, OFF_GATE + 3 * D_MODEL))


def _prep_kernel(wt_hbm, mem_ref, gmem_ref, wkv_ref,
                 gla_ref, fk_ref, qvt_ref, mq_ref, gate_ref, small_ref, mk_ref, mv_ref, wt_ref, dma_sem):
    copies = [pltpu.make_async_copy(wt_hbm.at[lo:hi, :], wt_ref.at[lo:hi, :], dma_sem.at[c])
              for c, (lo, hi) in enumerate(PREP_CHUNKS)]
    for cp in copies:
        cp.start()

    mn = _rms(mem_ref[...], gmem_ref[...]).astype(BF16)
    kv = _dot(mn, wkv_ref[...].astype(BF16))
    mk_ref[...] = kv[:, :MEM_W].astype(BF16)
    mv_ref[...] = kv[:, MEM_W:].astype(BF16)

    step = 4 * LANES

    def put(dst_ref, lo, n, dst0=0):
        for c0 in range(0, n, step):
            dst_ref[:, dst0 + c0:dst0 + c0 + step] = wt_ref[lo + c0:lo + c0 + step, :].T.astype(BF16)

    copies[0].wait()
    put(gla_ref, 0, OFF_GA)
    copies[1].wait()
    put(fk_ref, OFF_FK, FOX_W)
    put(mq_ref, OFF_MQ, MEM_W)
    copies[2].wait()
    put(gate_ref, OFF_GATE, OFF_GA)
    copies[3].wait()
    put(gate_ref, OFF_GATE + OFF_GA, 3 * D_MODEL - OFF_GA, dst0=OFF_GA)
    qvt_ref[:FOX_W, :] = wt_ref[OFF_FQ:OFF_FK, :].astype(BF16)
    qvt_ref[FOX_W:, :] = wt_ref[OFF_FV:OFF_FF, :].astype(BF16)
    small_t = jnp.concatenate([wt_ref[OFF_GA:OFF_FQ, :], wt_ref[OFF_FF:OFF_MQ, :],
                               jnp.zeros((LANES - GLA_LOWRANK - FOX_HEADS, D_MODEL), F32)], axis=0)
    small = small_t.T
    small_hi = small.astype(BF16)
    small_ref[:, :LANES] = small_hi
    small_ref[:, LANES:] = (small - small_hi.astype(F32)).astype(BF16)


def _prep(wt, mem, g_mem, w_mem_kv):
    d = wt.shape[1]
    m = mem.shape[0]
    assert PREP_CHUNKS[-1][1] == wt.shape[0]
    shapes = ((d, OFF_GA), (d, FOX_W), (2 * FOX_W, d), (d, MEM_W), (d, 3 * D_MODEL), (d, 2 * LANES),
              (m, MEM_W), (m, MEM_W))
    vmem = pl.BlockSpec(memory_space=pltpu.VMEM)
    return pl.pallas_call(
        _prep_kernel,
        in_specs=[pl.BlockSpec(memory_space=pl.ANY), vmem, vmem, vmem],
        out_specs=[vmem] * len(shapes),
        out_shape=[jax.ShapeDtypeStruct(sh, BF16) for sh in shapes],
        scratch_shapes=[pltpu.VMEM(wt.shape, F32), pltpu.SemaphoreType.DMA((len(PREP_CHUNKS),))],
        compiler_params=pltpu.CompilerParams(vmem_limit_bytes=VMEM_LIMIT),
        name="prep",
    )(wt, mem, g_mem, w_mem_kv)


def _proj_kernel(x_ref, gmix_ref, wgla_ref, wfk_ref, wqvt_ref, wmq_ref, wgate_ref, wsmall_ref, wup_ref,
                 balpha_ref, bforget_ref, mk_ref, mv_ref, pqt_ref, pk_ref,
                 gla_ref, loga_ref, qt_ref, kaug_ref, vt_ref, omem_ref, gate_ref, stat_ref, statq_ref,
                 carry_ref):
    tm = x_ref.shape[0]

    @pl.when(pl.program_id(0) == 0)
    def _():
        carry_ref[...] = jnp.zeros_like(carry_ref)

    tg = PROJ_GROUP
    groups = [slice(r0, r0 + tg) for r0 in range(0, tm, tg)]
    gid = range(len(groups))
    scale = MEM_DH ** -0.5
    heads = [slice(h * MEM_DH, (h + 1) * MEM_DH) for h in range(MEM_HEADS)]

    ubs, smalls, scores = [], [], []
    for rg in groups:
        u = _rms(x_ref[rg, :], gmix_ref[...])
        ub = u.astype(BF16)
        u_lo = (u - ub.astype(F32)).astype(BF16)
        parts = _dot(jnp.concatenate([ub, u_lo], axis=0), wsmall_ref[...])
        smalls.append((parts[:tg, :LANES] + parts[:tg, LANES:]) + (parts[tg:, :LANES] + parts[tg:, LANES:]))
        mq = _dot(ub, wmq_ref[...]).astype(BF16)
        scores.append([_dot_nt(mq[:, sl], mk_ref[:, sl]) * scale for sl in heads])
        ubs.append(ub)

    fkbs = []
    for rg, ub in zip(groups, ubs):
        gla_ref[rg, :] = _dot(ub, wgla_ref[...]).astype(BF16)
        fkbs.append(_dot(ub, wfk_ref[...]).astype(BF16))

    lane = lax.broadcasted_iota(jnp.int32, (tg, LANES), 1)
    ff_valid = (lane >= GLA_LOWRANK) & (lane < GLA_LOWRANK + FOX_HEADS)
    r = lax.broadcasted_iota(jnp.int32, (tg, tg), 0)
    c = lax.broadcasted_iota(jnp.int32, (tg, tg), 1)
    tri = jnp.where(r >= c, 1.0, 0.0).astype(BF16)

    def aug_slot(idx):
        a = idx % LANES
        return a % FOX_AUG_PER_HEAD, a < 2 * FOX_AUG_PER_HEAD

    slot_q, in_q = aug_slot(lax.broadcasted_iota(jnp.int32, (FOX_W, 1), 0))
    slot_k, in_k = aug_slot(lax.broadcasted_iota(jnp.int32, (1, FOX_W), 1))
    carry = carry_ref[...]
    fcums, augqts, augks = [], [], []
    for rg, small in zip(groups, smalls):
        ga = small[:, :GLA_LOWRANK]
        ga_hi = ga.astype(BF16)
        ga_lo = (ga - ga_hi.astype(F32)).astype(BF16)
        up = _dot(jnp.concatenate([ga_hi, ga_lo], axis=0), wup_ref[...])
        alpha_pre = ((up[:tg, :GLA_K] + up[:tg, GLA_K:]) + (up[tg:, :GLA_K] + up[tg:, GLA_K:])) + balpha_ref[...]
        loga_ref[rg, :] = _log_sigmoid(alpha_pre) * (1.0 / GLA_TAU)

        logf = jnp.where(ff_valid, _log_sigmoid(small + bforget_ref[...]), 0.0)
        logf = pltpu.roll(logf, LANES - GLA_LOWRANK, 1)
        lf_hi = logf.astype(BF16)
        lf_r = logf - lf_hi.astype(F32)
        lf_mid = lf_r.astype(BF16)
        lf_lo = (lf_r - lf_mid.astype(F32)).astype(BF16)
        c3 = _dot(tri, jnp.concatenate([lf_hi, lf_mid, lf_lo], axis=1))
        fcum = (c3[:, :LANES] + c3[:, LANES:2 * LANES] + c3[:, 2 * LANES:]) + carry
        carry = fcum[tg - 1:tg, :]

        f_hi = fcum.astype(BF16)
        rem = fcum - f_hi.astype(F32)
        f_mid = rem.astype(BF16)
        f_lo = (rem - f_mid.astype(F32)).astype(BF16)
        f3 = jnp.concatenate([f_hi, f_mid, f_lo], axis=1)
        augqts.append(_dot_nt(pqt_ref[...], f3) + jnp.where(in_q & (slot_q < 3), -1.0, 0.0))
        augks.append(_dot(f3, pk_ref[...]) + jnp.where(in_k & (slot_k >= 3), 1.0, 0.0))
        fcums.append(fcum)
    carry_ref[...] = carry

    for rg, ub in zip(groups, ubs):
        gate_ref[rg, :] = _sigmoid(_dot(ub, wgate_ref[...])).astype(BF16)

    for rg, sc in zip(groups, scores):
        probs = []
        for s in sc:
            m = jnp.max(s, axis=-1, keepdims=True)
            e = jnp.exp(s - m)
            probs.append((e / jnp.sum(e, axis=-1, keepdims=True)).astype(BF16))
        omem_ref[rg, :] = jnp.concatenate([_dot(pr, mv_ref[:, sl]) for pr, sl in zip(probs, heads)],
                                          axis=-1).astype(BF16)

    ones = jnp.ones((FOX_ONES_ROWS, tg), BF16)
    gi = lax.broadcasted_iota(jnp.int32, (FOX_W, LANES), 0) // FOX_DH
    gj = lax.broadcasted_iota(jnp.int32, (FOX_W, LANES), 1)
    group = jnp.where(gi == gj, 1.0, 0.0).astype(BF16)
    for g, rg, ub, fkb, fcum, augqt, augk in zip(gid, groups, ubs, fkbs, fcums, augqts, augks):
        qvt = _dot_nt(wqvt_ref[...], ub)
        fqt = (qvt[:FOX_W] * (FOX_DH ** -0.5)).astype(BF16)
        vt = qvt[FOX_W:].astype(BF16)
        vt_parts = []
        for h in range(FOX_HEADS):
            vt_parts += [vt[h * FOX_DH:(h + 1) * FOX_DH, :], ones]
        vt_ref[:, rg] = jnp.concatenate(vt_parts, axis=0)

        for p in range(FOX_HEADS // 2):
            src = slice(p * LANES, (p + 1) * LANES)
            qt_ref[2 * p * LANES:(2 * p + 1) * LANES, rg] = fqt[src]
            qt_ref[(2 * p + 1) * LANES:(2 * p + 2) * LANES, rg] = augqt[src].astype(BF16)
            kaug_ref[p, rg, :LANES] = fkb[:, src]
            kaug_ref[p, rg, LANES:] = augk[:, src].astype(BF16)

        fq32 = fqt.astype(F32) * (FOX_DH ** 0.5)
        nq2 = jnp.sum((fq32 * fq32).reshape(FOX_HEADS, FOX_DH, tg), axis=1)
        statq_ref[g] = jnp.broadcast_to(jnp.max(nq2, axis=1, keepdims=True), (FOX_HEADS, LANES))
        fk = fkb.astype(F32)
        nk2 = jnp.max(_dot((fk * fk).astype(BF16), group), axis=0, keepdims=True)
        fmax = jnp.max(fcum, axis=0, keepdims=True)
        fmin = jnp.min(fcum, axis=0, keepdims=True)
        stat_ref[g] = jnp.concatenate([nk2, fmax, fmin, jnp.zeros((5, LANES), F32)], axis=0)


def _aug_placement():
    pq = np.zeros((3 * LANES, FOX_W), np.float32)
    pk = np.zeros((3 * LANES, FOX_W), np.float32)
    for h in range(FOX_HEADS):
        base = (h // 2) * LANES + (h % 2) * FOX_AUG_PER_HEAD
        for c in range(3):
            pq[c * LANES + h, base + 3 + c] = 1.0
            pk[c * LANES + h, base + c] = 1.0
    return jnp.asarray(pq.T, BF16), jnp.asarray(pk, BF16)


def _proj(x, g_mix, w_gla, w_fk, w_qvt, w_mq, w_gate, w_small, w_up, b_alpha, b_forget, mk, mv):
    s = x.shape[0]
    tm = ROW_TILE
    nt = s // tm
    pqt, pk = _aug_placement()
    full = lambda shape: pl.BlockSpec(shape, lambda i: (0,) * len(shape), pipeline_mode=pl.Buffered(1))
    row = lambda w: pl.BlockSpec((tm, w), lambda i: (i, 0))
    col = lambda r: pl.BlockSpec((r, tm), lambda i: (0, i))
    ng = tm // PROJ_GROUP
    stat = pl.BlockSpec((ng, 8, LANES), lambda i: (i, 0, 0))
    vt_rows = FOX_HEADS * (FOX_DH + FOX_ONES_ROWS)
    return pl.pallas_call(
        _proj_kernel,
        grid=(nt,),
        in_specs=[row(D_MODEL), full((1, D_MODEL)), full(w_gla.shape), full(w_fk.shape), full(w_qvt.shape),
                  full(w_mq.shape), full(w_gate.shape), full(w_small.shape), full(w_up.shape),
                  full(b_alpha.shape), full(b_forget.shape), full(mk.shape), full(mv.shape),
                  full(pqt.shape), full(pk.shape)],
        out_specs=[row(w_gla.shape[1]), row(GLA_K), col(2 * FOX_W),
                   pl.BlockSpec((FOX_HEADS // 2, tm, 2 * LANES), lambda i: (0, i, 0)), col(vt_rows), row(MEM_W),
                   row(w_gate.shape[1]), stat, stat],
        out_shape=[jax.ShapeDtypeStruct((s, w_gla.shape[1]), BF16),
                   jax.ShapeDtypeStruct((s, GLA_K), F32),
                   jax.ShapeDtypeStruct((2 * FOX_W, s), BF16),
                   jax.ShapeDtypeStruct((FOX_HEADS // 2, s, 2 * LANES), BF16),
                   jax.ShapeDtypeStruct((vt_rows, s), BF16),
                   jax.ShapeDtypeStruct((s, MEM_W), BF16),
                   jax.ShapeDtypeStruct((s, w_gate.shape[1]), BF16),
                   jax.ShapeDtypeStruct((nt * ng, 8, LANES), F32),
                   jax.ShapeDtypeStruct((nt * ng, 8, LANES), F32)],
        scratch_shapes=[pltpu.VMEM((1, LANES), F32)],
        compiler_params=pltpu.CompilerParams(dimension_semantics=("arbitrary",), vmem_limit_bytes=VMEM_LIMIT),
        name="proj",
    )(x, g_mix, w_gla, w_fk, w_qvt, w_mq, w_gate, w_small, w_up, b_alpha, b_forget, mk, mv, pqt, pk)


def _gla_kernel(qkvg_ref, loga_ref, ghead_ref, *rest):
    n_w = (len(rest) - 2) // 2
    w_in_refs, o_ref, w_out_refs, state_ref = rest[:n_w], rest[n_w], rest[n_w + 1:2 * n_w + 1], rest[-1]
    t = GLA_GROUP
    nc = t // CHUNK

    @pl.when(pl.program_id(0) == 0)
    def _():
        state_ref[...] = jnp.zeros_like(state_ref)

    for w_src, w_dst in zip(w_in_refs, w_out_refs):
        w_dst[...] = w_src[...].astype(BF16)

    r = lax.broadcasted_iota(jnp.int32, (t, t), 0)
    c = lax.broadcasted_iota(jnp.int32, (t, t), 1)
    same = (r // CHUNK) == (c // CHUNK)
    lower = r >= c
    causal = same & lower
    anti = same & (r < c)
    cum = jnp.where(causal, 1.0, 0.0).astype(BF16)
    lane = lax.broadcasted_iota(jnp.int32, (1, LANES), 1)

    chunks = [slice(ci * CHUNK, (ci + 1) * CHUNK) for ci in range(nc)]
    pairs = [slice((h // 2) * LANES, (h // 2 + 1) * LANES) for h in range(GLA_HEADS)]
    in_head = [(lane // GLA_DK) == (h % 2) for h in range(GLA_HEADS)]
    heads = range(GLA_HEADS)
    groups = [slice(g0, g0 + t) for g0 in range(0, qkvg_ref.shape[0], t)]

    pre = []
    for rg in groups:
        la = loga_ref[rg, :]
        la_hi = la.astype(BF16)
        la_r = la - la_hi.astype(F32)
        la_mid = la_r.astype(BF16)
        la_lo = (la_r - la_mid.astype(F32)).astype(BF16)
        b3 = _dot(cum, jnp.concatenate([la_hi, la_mid, la_lo], axis=1))
        b = b3[:, :GLA_K] + b3[:, GLA_K:2 * GLA_K] + b3[:, 2 * GLA_K:]
        b_last = jnp.concatenate(
            [jnp.broadcast_to(b[(ci + 1) * CHUNK - 1:(ci + 1) * CHUNK, :], (CHUNK, GLA_K)) for ci in range(nc)],
            axis=0)
        e_pos = jnp.exp(b)
        e_neg = jnp.exp(-b)
        q = qkvg_ref[rg, 0:GLA_K].astype(F32) * (GLA_DK ** -0.5)
        k = qkvg_ref[rg, GLA_K:2 * GLA_K].astype(F32)
        q_pos = (q * e_pos).astype(BF16)
        q_neg = (q * e_neg).astype(BF16)
        k_pos = (k * e_pos).astype(BF16)
        k_neg = (k * e_neg).astype(BF16)
        k_dec = (k * jnp.exp(b_last - b)).astype(BF16)

        vs = [qkvg_ref[rg, 2 * GLA_K + h * GLA_DV:2 * GLA_K + (h + 1) * GLA_DV] for h in heads]
        qps = [jnp.where(in_head[h], q_pos[:, pairs[h]], jnp.zeros((), BF16)) for h in heads]
        kvs = [[jnp.where(in_head[h], _dot_tn(vs[h][rows], k_dec[rows, pairs[h]]), 0.0) for rows in chunks]
               for h in heads]
        pre.append((e_pos, q_neg, k_pos, k_neg, vs, qps, kvs))

    sts = [state_ref[h] for h in heads]
    inters = []
    for e_pos, _, _, _, _, qps, kvs in pre:
        inter = [[] for _ in heads]
        for ci, rows in enumerate(chunks):
            for h in heads:
                inter[h].append(_dot_nt(qps[h][rows], sts[h].astype(BF16)))
                dec = e_pos[(ci + 1) * CHUNK - 1:(ci + 1) * CHUNK, pairs[h]]
                sts[h] = sts[h] * dec + kvs[h][ci]
        inters.append(inter)
    for h in heads:
        state_ref[h] = sts[h]

    attns = []
    for _, q_neg, k_pos, k_neg, _, qps, _ in pre:
        grp = []
        for h in heads:
            qn = jnp.where(in_head[h], q_neg[:, pairs[h]], jnp.zeros((), BF16))
            a_c = _dot_nt(qps[h], k_neg[:, pairs[h]])
            a_a = _dot_nt(qn, k_pos[:, pairs[h]])
            grp.append(jnp.where(causal, a_c, jnp.where(anti, a_a, 0.0)).astype(BF16))
        attns.append(grp)

    for rg, grp, inter, (_, _, _, _, vs, _, _) in zip(groups, attns, inters, pre):
        outs = [_dot(grp[h], vs[h]) + jnp.concatenate(inter[h], axis=0) for h in heads]
        for h in heads:
            gh = ghead_ref[:, h * GLA_DV:(h + 1) * GLA_DV]
            on = _rms(outs[h], gh)
            gsl = slice(2 * GLA_K + GLA_V + h * GLA_DV, 2 * GLA_K + GLA_V + (h + 1) * GLA_DV)
            gg = qkvg_ref[rg, gsl].astype(F32)
            o_ref[rg, h * GLA_DV:(h + 1) * GLA_DV] = (on * (gg * _sigmoid(gg))).astype(BF16)


def _gla(qkvg, loga, ghead, later_weights):
    s = qkvg.shape[0]
    t = GLA_TILE
    assert s % t == 0 and t % GLA_GROUP == 0
    steps = s // t
    w_specs = [pl.BlockSpec((w.shape[0] // steps, w.shape[1]), lambda i: (i, 0)) for w in later_weights]
    assert all(w.shape[0] % (steps * BF16_SUBLANES) == 0 for w in later_weights)
    outs = pl.pallas_call(
        _gla_kernel,
        grid=(steps,),
        in_specs=[pl.BlockSpec((t, qkvg.shape[1]), lambda i: (i, 0)),
                  pl.BlockSpec((t, GLA_K), lambda i: (i, 0)),
                  pl.BlockSpec((1, GLA_V), lambda i: (0, 0))] + w_specs,
        out_specs=[pl.BlockSpec((t, GLA_V), lambda i: (i, 0))] + w_specs,
        out_shape=[jax.ShapeDtypeStruct((s, GLA_V), BF16)]
        + [jax.ShapeDtypeStruct(w.shape, BF16) for w in later_weights],
        scratch_shapes=[pltpu.VMEM((GLA_HEADS, GLA_DV, LANES), F32)],
        compiler_params=pltpu.CompilerParams(dimension_semantics=("arbitrary",), vmem_limit_bytes=VMEM_LIMIT),
        name="gla",
    )(qkvg, loga, ghead, *later_weights)
    return outs[0], outs[1:]


def _fox_kernel(thr_ref, fmin_ref, shift_ref, qt_hbm, k_hbm, vt_hbm, o_ref,
                sa_ref, sb_ref, pa_ref, pb_ref, m_ref, acc_ref, si_ref, sj_ref,
                qt_ref, k_ref, vt_ref, dma_sem):
    p = pl.program_id(0)
    tq = FOX_TQ
    tk = FOX_TK
    nsub = tq // tk
    nq = k_ref.shape[0] // tq
    hrows = FOX_DH + FOX_ONES_ROWS
    row = lax.broadcasted_iota(jnp.int32, (2 * LANES, 1), 0)
    krow = lax.broadcasted_iota(jnp.int32, (tk, tq), 0)
    qcol = lax.broadcasted_iota(jnp.int32, (tk, tq), 1)
    m0 = jnp.full((1, tq), NEG_BIG, F32)
    acc0 = jnp.zeros((hrows, tq), F32)

    n_chunk = nq // FOX_STEPS
    chunk = FOX_STEPS * tq

    def chunk_copies(c):
        at = pl.ds(pl.multiple_of(c * chunk, chunk), chunk)
        return (pltpu.make_async_copy(qt_hbm.at[pl.ds(pl.multiple_of(p * 2 * LANES, 2 * LANES), 2 * LANES), at],
                                      qt_ref.at[:, at], dma_sem.at[0, c]),
                pltpu.make_async_copy(k_hbm.at[p, at, :], k_ref.at[at, :], dma_sem.at[1, c]),
                pltpu.make_async_copy(vt_hbm.at[pl.ds(pl.multiple_of(p * 2 * hrows, 2 * hrows), 2 * hrows), at],
                                      vt_ref.at[:, at], dma_sem.at[2, c]))

    def wait_chunk(c):
        for cp in chunk_copies(c):
            cp.wait()

    for c in range(n_chunk):
        for cp in chunk_copies(c):
            cp.start()

    for hi in range(2):
        h = 2 * p + hi
        aug0 = LANES + hi * FOX_AUG_PER_HEAD
        mine = ((row // FOX_DH) == hi) | ((row >= aug0) & (row < aug0 + FOX_AUG_PER_HEAD))
        vrows = slice(hi * hrows, (hi + 1) * hrows)

        def score_tiles(qi, kj, masked):
            q0 = pl.multiple_of(qi * tq, tq)
            qm = jnp.where(mine, qt_ref[:, pl.ds(q0, tq)], jnp.zeros((), BF16))
            for u in range(nsub):
                k0 = pl.multiple_of(kj * tq + u * tk, tk)
                s_t = _dot(k_ref[pl.ds(k0, tk), :], qm)
                if masked:
                    s_t = jnp.where(krow + u * tk <= qcol, s_t, NEG_BIG)
                yield u, s_t

        def values(kj, u):
            return vt_ref[vrows, pl.ds(pl.multiple_of(kj * tq + u * tk, tk), tk)]

        def fast_produce(qi, kj, buf, masked=False):
            shift = shift_ref[h, qi]
            for u, s_t in score_tiles(qi, kj, masked):
                buf[u] = jnp.exp(s_t - shift).astype(BF16)

        def fast_consume(i, kj, buf, fresh=False):
            acc = acc0 if fresh else acc_ref[hi, i]
            for u in range(nsub):
                acc = acc + _dot(values(kj, u), buf[u])
            acc_ref[hi, i] = acc

        def slow_produce(qi, kj, buf, masked=False):
            for u, s_t in score_tiles(qi, kj, masked):
                buf[u] = s_t

        def slow_consume(i, kj, buf, fresh=False):
            m = m0 if fresh else m_ref[hi, i]
            acc = acc0 if fresh else acc_ref[hi, i]
            for u in range(nsub):
                s_t = buf[u]
                m_new = jnp.maximum(m, jnp.max(s_t, axis=0, keepdims=True))
                alpha = jnp.exp(m - m_new)
                pt = jnp.exp(s_t - m_new).astype(BF16)
                acc = alpha * acc + _dot(values(kj, u), pt)
                m = m_new
            m_ref[hi, i] = m
            acc_ref[hi, i] = acc

        def list_block(i, t):
            thr = thr_ref[h, i]
            for j in range(nq):
                si_ref[t] = i
                sj_ref[t] = j
                t = t + jnp.where((j < i) & (fmin_ref[h, j] <= thr), 1, 0)
            return t

        worst_shift = lax.fori_loop(0, nq, lambda i, w: jnp.maximum(w, shift_ref[h, i]), jnp.float32(0.0))

        def run_head(produce, consume, buf_a, buf_b):
            def diag_prefetch(i, buf):
                ic = jnp.minimum(i, nq - 1)
                produce(ic, ic, buf, masked=True)

            if hi == 0:
                wait_chunk(0)
            diag_prefetch(0, buf_a)

            def diag_step(tn, n):
                i = FOX_STEPS * tn
                if hi == 0:
                    @pl.when(tn + 1 < n_chunk)
                    def _():
                        wait_chunk(tn + 1)

                for d in range(0, FOX_STEPS, 2):
                    diag_prefetch(i + d + 1, buf_b)
                    consume(i + d, i + d, buf_a, fresh=True)
                    diag_prefetch(i + d + 2, buf_a)
                    consume(i + d + 1, i + d + 1, buf_b, fresh=True)
                for d in range(FOX_STEPS):
                    n = list_block(i + d, n)
                return n

            n_tiles = lax.fori_loop(0, nq // FOX_STEPS, diag_step, 0)
            for pad in range(FOX_STEPS):
                si_ref[n_tiles + pad] = nq
                sj_ref[n_tiles + pad] = 0
            m_ref[hi, nq] = m0
            acc_ref[hi, nq] = acc0

            def prefetch(t, buf):
                produce(jnp.minimum(si_ref[t], nq - 1), sj_ref[t], buf)

            prefetch(0, buf_a)

            def multi_step(tn, carry):
                t = FOX_STEPS * tn
                for d in range(0, FOX_STEPS, 2):
                    prefetch(t + d + 1, buf_b)
                    consume(si_ref[t + d], sj_ref[t + d], buf_a)
                    prefetch(t + d + 2, buf_a)
                    consume(si_ref[t + d + 1], sj_ref[t + d + 1], buf_b)
                return carry

            lax.fori_loop(0, (n_tiles + FOX_STEPS - 1) // FOX_STEPS, multi_step, 0)

        no_running_max = 2.0 * worst_shift <= FOX_MAX_SHIFT_GAP

        @pl.when(no_running_max)
        def _():
            run_head(fast_produce, fast_consume, pa_ref, pb_ref)

        @pl.when(jnp.logical_not(no_running_max))
        def _():
            run_head(slow_produce, slow_consume, sa_ref, sb_ref)

    def finish(i, carry):
        a0 = acc_ref[0, i]
        a1 = acc_ref[1, i]
        ot = jnp.concatenate([a0[:FOX_DH] / a0[FOX_DH:FOX_DH + 1], a1[:FOX_DH] / a1[FOX_DH:FOX_DH + 1]], axis=0)
        o_ref[pl.ds(pl.multiple_of(i * tq, tq), tq), :] = ot.T.astype(BF16)
        return carry

    lax.fori_loop(0, nq, finish, 0)


def _fox(thr, fmin_blk, shift, qt, kaug, vt):
    s = kaug.shape[1]
    tq = FOX_TQ
    nq = s // tq
    npair = FOX_HEADS // 2
    prow = 2 * (FOX_DH + FOX_ONES_ROWS)
    assert nq % FOX_STEPS == 0 and FOX_STEPS % 2 == 0
    hrows = FOX_DH + FOX_ONES_ROWS
    max_tiles = nq * (nq - 1) // 2 + FOX_STEPS
    hbm = pl.BlockSpec(memory_space=pl.ANY)
    grid_spec = pltpu.PrefetchScalarGridSpec(
        num_scalar_prefetch=3,
        grid=(npair,),
        in_specs=[hbm, hbm, hbm],
        out_specs=pl.BlockSpec((None, s, LANES), lambda p, *_: (p, 0, 0)),
        scratch_shapes=[pltpu.VMEM((tq // FOX_TK, FOX_TK, tq), F32), pltpu.VMEM((tq // FOX_TK, FOX_TK, tq), F32),
                        pltpu.VMEM((tq // FOX_TK, FOX_TK, tq), BF16), pltpu.VMEM((tq // FOX_TK, FOX_TK, tq), BF16),
                        pltpu.VMEM((2, nq + 1, 1, tq), F32), pltpu.VMEM((2, nq + 1, hrows, tq), F32),
                        pltpu.SMEM((max_tiles,), jnp.int32), pltpu.SMEM((max_tiles,), jnp.int32),
                        pltpu.VMEM((2 * LANES, s), BF16), pltpu.VMEM((s, 2 * LANES), BF16),
                        pltpu.VMEM((prow, s), BF16), pltpu.SemaphoreType.DMA((3, nq // FOX_STEPS))],
    )
    return pl.pallas_call(
        _fox_kernel,
        grid_spec=grid_spec,
        out_shape=jax.ShapeDtypeStruct((npair, s, LANES), BF16),
        compiler_params=pltpu.CompilerParams(dimension_semantics=("arbitrary",), vmem_limit_bytes=VMEM_LIMIT),
        name="fox",
    )(thr, fmin_blk, shift, qt, kaug, vt)


def _tail_kernel(x_ref, ogla_ref, ofox_ref, omem_ref, gate_ref, wg_ref, wf_ref, wm_ref, wo_ref,
                 gffn_ref, w1_ref, w2_ref, gfin_ref, out_ref):
    groups = [slice(r0, r0 + TAIL_GROUP) for r0 in range(0, x_ref.shape[0], TAIL_GROUP)]
    merged = []
    for rg in groups:
        ofox = jnp.concatenate([ofox_ref[p, rg, :] for p in range(FOX_HEADS // 2)], axis=1)
        merged.append((gate_ref[rg, 0:D_MODEL].astype(F32) * _dot(ogla_ref[rg, :], wg_ref[...])
                       + gate_ref[rg, D_MODEL:2 * D_MODEL].astype(F32) * _dot(ofox, wf_ref[...])
                       + gate_ref[rg, 2 * D_MODEL:3 * D_MODEL].astype(F32) * _dot(omem_ref[rg, :], wm_ref[...])
                       ).astype(BF16))
    hs = [x_ref[rg, :] + _dot(mg, wo_ref[...]) for rg, mg in zip(groups, merged)]
    u2s = [_rms(h, gffn_ref[...]).astype(BF16) for h in hs]
    accs = [jnp.zeros_like(h) for h in hs]
    for cidx in range(D_FF // FF_CHUNK):
        cs = slice(cidx * FF_CHUNK, (cidx + 1) * FF_CHUNK)
        for g, u2 in enumerate(u2s):
            a = jnp.maximum(_dot(u2, w1_ref[:, cs]), 0.0)
            accs[g] = accs[g] + _dot((a * a).astype(BF16), w2_ref[cs, :])
    for rg, h, acc in zip(groups, hs, accs):
        out_ref[rg, :] = _rms(h + acc, gfin_ref[...])


def _tail(x, ogla, ofox, omem, gate, wg, wf, wm, wo, gffn, w1, w2, gfin):
    s = x.shape[0]
    tm = TAIL_TILE
    full = lambda a: pl.BlockSpec(a.shape, lambda i: (0,) * a.ndim, pipeline_mode=pl.Buffered(1))
    row = lambda w: pl.BlockSpec((tm, w), lambda i: (i, 0))
    return pl.pallas_call(
        _tail_kernel,
        grid=(s // tm,),
        in_specs=[row(D_MODEL), row(GLA_V), pl.BlockSpec((FOX_HEADS // 2, tm, LANES), lambda i: (0, i, 0)),
                  row(MEM_W), row(3 * D_MODEL),
                  full(wg), full(wf), full(wm), full(wo), full(gffn), full(w1), full(w2), full(gfin)],
        out_specs=row(D_MODEL),
        out_shape=jax.ShapeDtypeStruct((s, D_MODEL), F32),
        compiler_params=pltpu.CompilerParams(dimension_semantics=("arbitrary",), vmem_limit_bytes=VMEM_LIMIT),
        name="tail",
    )(x, ogla, ofox, omem, gate, wg, wf, wm, wo, gffn, w1, w2, gfin)


def kernel(x, mem, g_mix, w_in, w_alpha_up, b_alpha, b_forget, g_gla_head, g_mem, w_mem_kv,
           w_gla_o, w_fox_o, w_mem_o, w_out, g_ffn, w_ff1, w_ff2, g_final):
    assert x.shape[0] == 1 and g_mix.shape[0] == 1, "single batch, single layer"
    s = x.shape[1]
    assert s % ROW_TILE == 0 and s % TAIL_TILE == 0 and s % FOX_TQ == 0
    assert FOX_TQ % FOX_TK == 0 and FOX_TQ % PROJ_GROUP == 0 and ROW_TILE % PROJ_GROUP == 0
    xs = x[0]
    w = w_in[0]

    w_gla, w_fk, w_qvt, w_mq, w_gate, w_small, mk, mv = _prep(jnp.swapaxes(w, 0, 1), mem[0], g_mem, w_mem_kv[0])
    w_up_hi = w_alpha_up[0].astype(BF16)
    w_up = jnp.concatenate([w_up_hi, (w_alpha_up[0] - w_up_hi.astype(F32)).astype(BF16)], axis=1)
    b_f = jnp.zeros((1, LANES), F32).at[0, GLA_LOWRANK:GLA_LOWRANK + FOX_HEADS].set(b_forget[0])

    gla_qkvg, loga, qt, kaug, vt, omem, gate, stats, statq = _proj(
        xs, g_mix, w_gla, w_fk, w_qvt, w_mq, w_gate, w_small, w_up, b_alpha, b_f, mk, mv)

    ogla, (wg, wf, wm, wo, w1, w2) = _gla(
        gla_qkvg, loga, g_gla_head.reshape(1, GLA_V),
        [w_gla_o[0], w_fox_o[0], w_mem_o[0], w_out[0], w_ff1[0], w_ff2[0]])

    per_blk = lambda a: a.reshape(s // FOX_TQ, FOX_TQ // PROJ_GROUP, FOX_HEADS)
    qn = jnp.sqrt(jnp.max(per_blk(statq[:, :, 0]), axis=1))
    kn = jnp.sqrt(jnp.max(stats[:, 0, :FOX_HEADS], axis=0))
    fmax = jnp.max(per_blk(stats[:, 1, :FOX_HEADS]), axis=1)
    fmin = jnp.min(per_blk(stats[:, 2, :FOX_HEADS]), axis=1)
    shift = (1.02 * (FOX_DH ** -0.5)) * qn * kn[None, :]
    thr = 2.0 * shift + fmax + PRUNE_LOGIT_GAP
    ofox = _fox(thr.T, fmin.T, shift.T, qt, kaug, vt)

    out = _tail(xs, ogla, ofox, omem, gate, wg, wf, wm, wo, g_ffn, w1, w2, g_final.reshape(1, D_MODEL))
    return out[None]
```
